```python
import math
import jax
import jax.numpy as jnp
from jax import lax
import numpy as np

D_MODEL = 1024
BATCH = 8
SEQ = 4096
DEPTH = 2

GRID_W = 64
CTX_LEN = 256
EPS = 1e-6
D_RNN = 1024
RNN_BLOCKS = 8
RNN_BW = D_RNN // RNN_BLOCKS
CONV_W = 4
CONV_LEFT = 2
LRU_C = 8.0
N_HEADS = 8
HEAD_DIM = 64
V_DIM = 2 * HEAD_DIM
ATTN_QK = N_HEADS * 2 * HEAD_DIM
ATTN_V = N_HEADS * V_DIM
Q_BLOCK = 128
ROPE_BASE = 10000.0
N_EXPERTS = 16
N_GROUPS = 4
EXPERTS_PER_GROUP = N_EXPERTS // N_GROUPS
TOP_K = 2
D_EXPERT = 1024
MOE_BLOCK = 256
COL_SPLITS = (D_RNN, ATTN_QK, ATTN_V, ATTN_QK, D_RNN, D_MODEL, D_MODEL)
D_IN = sum(COL_SPLITS)
CTX_LAST_PARTS = 3

kernel_name = 'hybrid_rglru_diffattn_groupmoe_dit'


def _rmsnorm(x, g):
    xf = x.astype(jnp.float32)
    y = xf * lax.rsqrt(jnp.mean(xf * xf, axis=-1, keepdims=True) + EPS)
    return (y * g.astype(jnp.float32)).astype(x.dtype)


def _adaln(cond, w_mod, b_mod):
    m = jax.nn.silu(cond) @ w_mod + b_mod
    return jnp.split(m, 6, axis=-1)


def _modulate(h, shift, scale):
    return h * (1 + scale) + shift


def _split_proj(p, n_parts):
    idx, acc = [], 0
    for s in COL_SPLITS[:n_parts - 1]:
        acc += s
        idx.append(acc)
    return jnp.split(p, idx, axis=-1)


def _axial_rope_tables(rows, dtype):
    n_pairs = HEAD_DIM // 4
    inv = ROPE_BASE ** (-jnp.arange(n_pairs, dtype=jnp.float32) / n_pairs)
    r = jnp.repeat(jnp.arange(rows, dtype=jnp.float32), GRID_W)
    col = jnp.tile(jnp.arange(GRID_W, dtype=jnp.float32), rows)
    ang = jnp.concatenate([r[:, None] * inv, col[:, None] * inv], axis=-1)
    return jnp.cos(ang).astype(dtype), jnp.sin(ang).astype(dtype)


def _apply_rope(t, cos, sin):
    tp = t.reshape(t.shape[:-1] + (HEAD_DIM // 2, 2))
    t1, t2 = tp[..., 0], tp[..., 1]
    cs = cos[None, :, None, None, :]
    sn = sin[None, :, None, None, :]
    return jnp.stack([t1 * cs - t2 * sn, t1 * sn + t2 * cs], axis=-1).reshape(t.shape)


def _centred_dwconv(u, w, b):
    L = u.shape[1]
    up = jnp.pad(u, ((0, 0), (CONV_LEFT, CONV_W - 1 - CONV_LEFT), (0, 0)))
    out = b
    for j in range(CONV_W):
        out = out + up[:, j:j + L] * w[j]
    return out


def _block_diag(u, w, b):
    ub = u.reshape(u.shape[:-1] + (RNN_BLOCKS, RNN_BW))
    return jnp.einsum('blni,nio->blno', ub, w).reshape(u.shape) + b


def _lin_combine(left, right):
    a_l, b_l = left
    a_r, b_r = right
    return a_l * a_r, a_r * b_l + b_r


def _rglru_scan(u, h0, w_a, b_a, w_x, b_x, lam):
    r = jax.nn.sigmoid(_block_diag(u, w_a, b_a).astype(jnp.float32))
    i = jax.nn.sigmoid(_block_diag(u, w_x, b_x).astype(jnp.float32))
    log_a = -LRU_C * r * jax.nn.softplus(-lam.astype(jnp.float32))
    a = jnp.exp(log_a)
    b = jnp.sqrt(-jnp.expm1(2.0 * log_a)) * i * u.astype(jnp.float32)
    b = b.at[:, 0].add(a[:, 0] * h0)
    _, h = lax.associative_scan(_lin_combine, (a, b), axis=1)
    return h


def _bidir_rglru(u_c, u_l, w_a, b_a, w_x, b_x, lam, with_ctx):
    h0 = jnp.zeros((u_l.shape[0], D_RNN), jnp.float32)
    hc_f = _rglru_scan(u_c, h0, w_a[0], b_a[0], w_x[0], b_x[0], lam[0])
    hl_f = _rglru_scan(u_l, hc_f[:, -1], w_a[0], b_a[0], w_x[0], b_x[0], lam[0])
    hc_b = _rglru_scan(jnp.flip(u_c, 1), h0, w_a[1], b_a[1], w_x[1], b_x[1], lam[1])
    hl_b = _rglru_scan(jnp.flip(u_l, 1), hc_b[:, -1], w_a[1], b_a[1], w_x[1], b_x[1], lam[1])
    y_l = (hl_f + jnp.flip(hl_b, 1)).astype(u_l.dtype)
    y_c = (hc_f + jnp.flip(hc_b, 1)).astype(u_c.dtype) if with_ctx else None
    return y_l, y_c


def _diff_attend(q, k, v, lam):
    s = jnp.einsum('bqhcd,bkhcd->bhcqk', q, k, preferred_element_type=jnp.float32) * (HEAD_DIM ** -0.5)
    p = jax.nn.softmax(s, axis=-1)
    p = p[:, :, 0] - lam * p[:, :, 1]
    return jnp.einsum('bhqk,bkhe->bqhe', p.astype(v.dtype), v)


def _latent_diff_attention(q, k_all, v_all, lam):
    B, S = q.shape[:2]
    nb = S // Q_BLOCK
    qb = jnp.moveaxis(q.reshape((B, nb, Q_BLOCK) + q.shape[2:]), 1, 0)
    ob = lax.map(lambda qi: _diff_attend(qi, k_all, v_all, lam), qb)
    return jnp.moveaxis(ob, 0, 1).reshape(B, S, N_HEADS, V_DIM)


def _moe(h, w_router, b_router, w1, w3, w2):
    T, D = h.shape
    logits = jnp.matmul(h, w_router).astype(jnp.float32) + b_router.astype(jnp.float32)
    probs = jax.nn.softmax(logits, axis=-1).reshape(T, N_GROUPS, EXPERTS_PER_GROUP)
    group = jnp.argmax(probs.max(axis=-1), axis=-1)
    in_group = jnp.take_along_axis(probs, group[:, None, None], axis=1)[:, 0]
    top_p, top_local = lax.top_k(in_group, TOP_K)
    expert = (group[:, None] * EXPERTS_PER_GROUP + top_local).reshape(-1).astype(jnp.int32)
    gate = (top_p / jnp.sum(top_p, axis=-1, keepdims=True)).reshape(-1)
    A = T * TOP_K
    order = jnp.argsort(expert)
    sorted_e = expert[order]
    sizes = jnp.bincount(expert, length=N_EXPERTS)
    padded = (sizes + MOE_BLOCK - 1) // MOE_BLOCK * MOE_BLOCK
    starts = jnp.cumsum(sizes) - sizes
    pend = jnp.cumsum(padded)
    pstarts = pend - padded
    dest = pstarts[sorted_e] + jnp.arange(A) - starts[sorted_e]
    n_blocks = -(-A // MOE_BLOCK) + N_EXPERTS
    n_slots = n_blocks * MOE_BLOCK
    slot_tok = jnp.full((n_slots,), T, jnp.int32).at[dest].set((order // TOP_K).astype(jnp.int32))
    slot_gate = jnp.zeros((n_slots,), jnp.float32).at[dest].set(gate[order])
    block_e = jnp.minimum(jnp.searchsorted(pend, jnp.arange(n_blocks) * MOE_BLOCK, side='right'), N_EXPERTS - 1)
    h_pad = jnp.concatenate([h, jnp.zeros((1, D), h.dtype)], axis=0)
    xb = h_pad[slot_tok].reshape(n_blocks, MOE_BLOCK, D)

    def expert_block(args):
        xe, e = args
        return (jax.nn.silu(xe @ w1[e]) * (xe @ w3[e])) @ w2[e]

    yb = lax.map(expert_block, (xb, block_e)).reshape(n_slots, D)
    y = yb * slot_gate[:, None].astype(yb.dtype)
    return jnp.zeros((T + 1, D), h.dtype).at[slot_tok].add(y)[:T]


def setup_inputs(seed: int = 0) -> dict:
    key = jax.random.key(seed)
    ks = jax.random.split(key, 32)
    f32 = jnp.float32
    D = D_MODEL

    def nrm(k, shape, s):
        return jax.random.normal(k, shape, f32) * s

    a8 = jax.random.uniform(ks[15], (DEPTH, 2, D_RNN), f32, 0.9, 0.999)
    a = a8 ** (1.0 / LRU_C)
    return {
        'x': nrm(ks[0], (BATCH, SEQ, D), 1.0),
        'c': nrm(ks[1], (BATCH, D), 1.0),
        'ctx': nrm(ks[2], (BATCH, CTX_LEN, D), 1.0),
        'c_ctx': nrm(ks[3], (D,), 1.0),
        'w_mod': nrm(ks[4], (DEPTH, D, 6 * D), 0.5 * D ** -0.5),
        'b_mod': nrm(ks[5], (DEPTH, 6 * D), 0.02),
        'g_norm1': 1.0 + nrm(ks[6], (DEPTH, D), 0.02),
        'g_norm2': 1.0 + nrm(ks[7], (DEPTH, D), 0.02),
        'w_in': nrm(ks[8], (DEPTH, D, D_IN), D ** -0.5),
        'conv_w': nrm(ks[9], (DEPTH, CONV_W, D_RNN), CONV_W ** -0.5),
        'conv_b': nrm(ks[10], (DEPTH, D_RNN), 0.02),
        'lru_wa': nrm(ks[11], (DEPTH, 2, RNN_BLOCKS, RNN_BW, RNN_BW), RNN_BW ** -0.5),
        'lru_ba': nrm(ks[12], (DEPTH, 2, D_RNN), 0.02),
        'lru_wx': nrm(ks[13], (DEPTH, 2, RNN_BLOCKS, RNN_BW, RNN_BW), RNN_BW ** -0.5),
        'lru_bx': nrm(ks[14], (DEPTH, 2, D_RNN), 0.02),
        'lru_lambda': jnp.log(a) - jnp.log1p(-a),
        'diff_lambda': nrm(ks[16], (DEPTH, 4, HEAD_DIM), 0.1),
        'g_subln': 1.0 + nrm(ks[17], (DEPTH, V_DIM), 0.02),
        'w_rnn_proj': nrm(ks[18], (DEPTH, D_RNN, D), D_RNN ** -0.5),
        'w_attn_proj': nrm(ks[19], (DEPTH, ATTN_V, D), ATTN_V ** -0.5),
        'w_o': nrm(ks[20], (DEPTH, D, D), D ** -0.5),
        'w_router': nrm(ks[21], (D, N_EXPERTS), D ** -0.5),
        'b_router': nrm(ks[22], (N_EXPERTS,), 0.01),
        'w_e1': nrm(ks[23], (DEPTH, N_EXPERTS, D, D_EXPERT), D ** -0.5),
        'w_e3': nrm(ks[24], (DEPTH, N_EXPERTS, D, D_EXPERT), D ** -0.5),
        'w_e2': nrm(ks[25], (DEPTH, N_EXPERTS, D_EXPERT, D), D_EXPERT ** -0.5),
        'g_final': 1.0 + nrm(ks[26], (D,), 0.02),
    }


def reference(x, c, ctx, c_ctx, w_mod, b_mod, g_norm1, g_norm2, w_in, conv_w, conv_b,
              lru_wa, lru_ba, lru_wx, lru_bx, lru_lambda, diff_lambda, g_subln,
              w_rnn_proj, w_attn_proj, w_o, w_router, b_router, w_e1, w_e3, w_e2, g_final):
    B, S, D = x.shape
    Lc = ctx.shape[1]
    rows = S // GRID_W
    cos, sin = _axial_rope_tables(rows, x.dtype)

    def head5(t):
        return t.reshape(t.shape[:2] + (N_HEADS, 2, HEAD_DIM))

    def headv(t):
        return t.reshape(t.shape[:2] + (N_HEADS, V_DIM))

    xc = ctx
    for l in range(DEPTH):
        last = l == DEPTH - 1
        lam_init = 0.8 - 0.6 * math.exp(-0.3 * l)
        sh1, sc1, gt1, sh2, sc2, gt2 = [m[:, None, :] for m in _adaln(c, w_mod[l], b_mod[l])]
        csh1, csc1, cgt1, csh2, csc2, cgt2 = _adaln(c_ctx, w_mod[l], b_mod[l])

        h = _modulate(_rmsnorm(x, g_norm1[l]), sh1, sc1)
        hc = _modulate(_rmsnorm(xc, g_norm1[l]), csh1, csc1)
        rx, k, v, q, rg, g_r, g_a = _split_proj(h @ w_in[l], 7)
        n_ctx_parts = CTX_LAST_PARTS if last else 7
        ctx_parts = _split_proj(hc @ w_in[l][:, :sum(COL_SPLITS[:n_ctx_parts])], n_ctx_parts)
        rx_c, k_c, v_c = ctx_parts[0], ctx_parts[1], ctx_parts[2]

        lq1, lk1, lq2, lk2 = diff_lambda[l].astype(jnp.float32)
        lam = jnp.exp(jnp.sum(lq1 * lk1)) - jnp.exp(jnp.sum(lq2 * lk2)) + lam_init
        q_l = _apply_rope(head5(q), cos, sin)
        k_l = _apply_rope(head5(k), cos, sin)
        k_c5 = head5(k_c)
        v_c4 = headv(v_c)
        k_all = jnp.concatenate([k_c5, k_l], axis=1)
        v_all = jnp.concatenate([v_c4, headv(v)], axis=1)
        o_a = _latent_diff_attention(q_l, k_all, v_all, lam)
        o_a = (_rmsnorm(o_a, g_subln[l]) * (1.0 - lam_init)).reshape(B, S, ATTN_V) @ w_attn_proj[l]

        u_l = _centred_dwconv(rx, conv_w[l], conv_b[l])
        u_c = _centred_dwconv(rx_c, conv_w[l], conv_b[l])
        y_l, y_c = _bidir_rglru(u_c, u_l, lru_wa[l], lru_ba[l], lru_wx[l], lru_bx[l], lru_lambda[l], not last)
        o_r = (y_l * jax.nn.gelu(rg)) @ w_rnn_proj[l]

        x = x + gt1 * ((jax.nn.sigmoid(g_r) * o_r + jax.nn.sigmoid(g_a) * o_a) @ w_o[l])
        if not last:
            q_c, rg_c, g_r_c, g_a_c = ctx_parts[3], ctx_parts[4], ctx_parts[5], ctx_parts[6]
            o_ac = _diff_attend(head5(q_c), k_c5, v_c4, lam)
            o_ac = (_rmsnorm(o_ac, g_subln[l]) * (1.0 - lam_init)).reshape(B, Lc, ATTN_V) @ w_attn_proj[l]
            o_rc = (y_c * jax.nn.gelu(rg_c)) @ w_rnn_proj[l]
            xc = xc + cgt1 * ((jax.nn.sigmoid(g_r_c) * o_rc + jax.nn.sigmoid(g_a_c) * o_ac) @ w_o[l])

        h2 = _modulate(_rmsnorm(x, g_norm2[l]), sh2, sc2).reshape(B * S, D)
        if not last:
            h2c = _modulate(_rmsnorm(xc, g_norm2[l]), csh2, csc2).reshape(B * Lc, D)
            y = _moe(jnp.concatenate([h2, h2c], axis=0), w_router, b_router, w_e1[l], w_e3[l], w_e2[l])
            xc = xc + cgt2 * y[B * S:].reshape(B, Lc, D)
            y_lat = y[:B * S]
        else:
            y_lat = _moe(h2, w_router, b_router, w_e1[l], w_e3[l], w_e2[l])
        x = x + gt2 * y_lat.reshape(B, S, D)

    return _rmsnorm(x, g_final)
```

```python
import functools
import math

import jax
import jax.numpy as jnp
from jax import lax
from jax.experimental import pallas as pl
from jax.experimental.pallas import tpu as pltpu

F32 = jnp.float32
BF16 = jnp.bfloat16

N_HEADS = 8
HEAD_DIM = 64
V_DIM = 2 * HEAD_DIM
GRID_W = 64
EPS = 1e-6
RNN_BLOCKS = 8
RNN_BW = 128
CONV_W = 4
CONV_LEFT = 2
LRU_C = 8.0
ROPE_BASE = 10000.0
N_EXPERTS = 16
N_GROUPS = 4
EXPERTS_PER_GROUP = N_EXPERTS // N_GROUPS
TOP_K = 2

LANES = 128
SUBLANES = 8
ROW_TILE = 256
SCAN_CHUNK = 128
ATTN_TQ = 256
ATTN_TK = 256
MOE_ROWS = 512
MOD_ROWS = 16
NEG_BIG = -1e30


def _cparams(sem, vmem_mib):
    return pltpu.CompilerParams(dimension_semantics=sem, vmem_limit_bytes=vmem_mib * 1024 * 1024)


def _sigmoid(x):
    return 1.0 / (1.0 + jnp.exp(-x))


def _gelu_tanh(x):
    return 0.5 * x * (1.0 + jnp.tanh(math.sqrt(2.0 / math.pi) * (x + 0.044715 * (x * x * x))))


def _rms(x, g):
    return x * lax.rsqrt(jnp.mean(x * x, axis=-1, keepdims=True) + EPS) * g


def _adaln_kernel(c_ref, w_ref, b_ref, o_ref):
    c = c_ref[...]
    a = c * _sigmoid(c)
    o_ref[...] = jnp.dot(a, w_ref[...], preferred_element_type=F32,
                         precision=lax.Precision.HIGHEST) + b_ref[...]


def _adaln(cond, w_mod, b_mod):
    depth, d, d6 = w_mod.shape
    n = d6 // d
    return pl.pallas_call(
        _adaln_kernel,
        grid=(depth, n),
        in_specs=[pl.BlockSpec((MOD_ROWS, d), lambda l, j: (0, 0)),
                  pl.BlockSpec((None, d, d), lambda l, j: (l, 0, j)),
                  pl.BlockSpec((None, 1, d), lambda l, j: (l, 0, j))],
        out_specs=pl.BlockSpec((None, MOD_ROWS, d), lambda l, j: (l, 0, j)),
        out_shape=jax.ShapeDtypeStruct((depth, MOD_ROWS, d6), F32),
        compiler_params=_cparams(("arbitrary", "arbitrary"), 32),
        name="adaln",
    )(cond, w_mod, b_mod.reshape(depth, 1, d6))


def _inproj_kernel(x_ref, sh_ref, sc_ref, g_ref, w_ref, cos_ref, sin_ref,
                   k_ref, v_ref, q_ref, rx_ref, rg_ref, gr_ref, ga_ref, *, d):
    x = x_ref[...]
    h = (_rms(x, g_ref[...]) * (1.0 + sc_ref[...]) + sh_ref[...]).astype(BF16)
    cosv = cos_ref[...]
    sinv = sin_ref[...]

    def proj(p):
        return jnp.dot(h, w_ref[:, p * d:(p + 1) * d], preferred_element_type=F32)

    def rope_store(res, ref, scale):
        for hh in range(N_HEADS):
            xs = res[:, hh * LANES:(hh + 1) * LANES]
            out = xs * cosv + pltpu.roll(xs, LANES // 2, 1) * sinv
            ref[:, hh * LANES:(hh + 1) * LANES] = (out * scale).astype(BF16)

    rope_store(proj(0), k_ref, 1.0)
    v_ref[...] = proj(1).astype(BF16)
    rope_store(proj(2), q_ref, HEAD_DIM ** -0.5)
    rx_ref[...] = proj(3)
    rg_ref[...] = proj(4)
    gr_ref[...] = proj(5)
    ga_ref[...] = proj(6)


def _inproj(xs, sh, sc, g, w, cos_t, sin_t, n_ctx_tiles):
    b, lt, d = xs.shape
    nt = lt // ROW_TILE
    mod_map = lambda bi, i: (jnp.where(i < n_ctx_tiles, b, bi), 0, 0)
    tok = pl.BlockSpec((None, ROW_TILE, d), lambda bi, i: (bi, i, 0))
    out_bf = jax.ShapeDtypeStruct((b, lt, d), BF16)
    out_f = jax.ShapeDtypeStruct((b, lt, d), F32)
    return pl.pallas_call(
        functools.partial(_inproj_kernel, d=d),
        grid=(b, nt),
        in_specs=[tok,
                  pl.BlockSpec((None, 1, d), mod_map),
                  pl.BlockSpec((None, 1, d), mod_map),
                  pl.BlockSpec((1, d), lambda bi, i: (0, 0)),
                  pl.BlockSpec(w.shape, lambda bi, i: (0, 0)),
                  pl.BlockSpec((ROW_TILE, LANES), lambda bi, i: (i, 0)),
                  pl.BlockSpec((ROW_TILE, LANES), lambda bi, i: (i, 0))],
        out_specs=[tok] * 7,
        out_shape=[out_bf, out_bf, out_bf, out_f, out_f, out_f, out_f],
        compiler_params=_cparams(("parallel", "arbitrary"), 56),
        name="inproj",
    )(xs, sh, sc, g, w, cos_t, sin_t)


def _scan_chunk(i, n_ctx, n_chunks, reverse):
    if not reverse:
        return i
    return jnp.where(i < n_ctx, n_ctx - 1 - i, n_chunks - 1 - (i - n_ctx))


def _scan_kernel(*refs, reverse, tt, n_ctx, n_chunks):
    if reverse:
        (rx_ref, prev_ref, next_ref, cw_ref, cb_ref, wg_ref, bg_ref, c_ref, hf_ref,
         out_ref, ext_scr, a_scr, b_scr, h_scr) = refs
    else:
        (rx_ref, prev_ref, next_ref, cw_ref, cb_ref, wg_ref, bg_ref, c_ref,
         out_ref, ext_scr, a_scr, b_scr, h_scr) = refs
        hf_ref = None
    i = pl.program_id(0)
    c = _scan_chunk(i, n_ctx, n_chunks, reverse)
    seq_start = jnp.logical_or(c == 0, c == n_ctx)
    seq_end = jnp.logical_or(c == n_ctx - 1, c == n_chunks - 1)

    ext_scr[0:CONV_LEFT] = jnp.where(seq_start, 0.0, prev_ref[...])
    ext_scr[CONV_LEFT:CONV_LEFT + tt] = rx_ref[...]
    ext_scr[CONV_LEFT + tt:CONV_LEFT + tt + 1] = jnp.where(seq_end, 0.0, next_ref[...])

    rows = tt * SUBLANES
    for n in range(RNN_BLOCKS):
        sl = slice(n * RNN_BW, (n + 1) * RNN_BW)
        u = cb_ref[:, sl] + cw_ref[0:1, sl] * ext_scr[0:tt, :, sl]
        for j in range(1, CONV_W):
            u = u + cw_ref[j:j + 1, sl] * ext_scr[j:j + tt, :, sl]
        u2 = u.reshape(rows, RNN_BW)
        z = jnp.dot(u2.astype(BF16), wg_ref[n], preferred_element_type=F32) + bg_ref[n]
        r = _sigmoid(z[:, :RNN_BW])
        ig = _sigmoid(z[:, RNN_BW:])
        a = jnp.exp(-c_ref[:, sl] * r)
        bb = jnp.sqrt(1.0 - a * a) * ig * u2
        a_scr[:, :, sl] = a.reshape(tt, SUBLANES, RNN_BW)
        b_scr[:, :, sl] = bb.reshape(tt, SUBLANES, RNN_BW)

    @pl.when(i == 0)
    def _():
        h_scr[...] = jnp.zeros_like(h_scr)

    def step(s, h):
        t = tt - 1 - s if reverse else s
        h = a_scr[t] * h + b_scr[t]
        if reverse:
            out_ref[t] = hf_ref[t] + h
        else:
            out_ref[t] = h
        return h

    h_scr[...] = lax.fori_loop(0, tt, step, h_scr[...], unroll=8)


def _scan(rx_t, cw, cb, wg, bg, cdec, hf, n_ctx, reverse):
    lt, b, d = rx_t.shape
    tt = SCAN_CHUNK
    n_chunks = lt // tt
    chunk = lambda i: _scan_chunk(i, n_ctx, n_chunks, reverse)
    blk = pl.BlockSpec((tt, b, d), lambda i: (chunk(i), 0, 0))
    const2 = lambda i: (0, 0)
    const3 = lambda i: (0, 0, 0)
    in_specs = [blk,
                pl.BlockSpec((CONV_LEFT, b, d), lambda i: (jnp.maximum(chunk(i) * (tt // CONV_LEFT) - 1, 0), 0, 0)),
                pl.BlockSpec((1, b, d), lambda i: (jnp.minimum((chunk(i) + 1) * tt, lt - 1), 0, 0)),
                pl.BlockSpec(cw.shape, const2),
                pl.BlockSpec(cb.shape, const2),
                pl.BlockSpec(wg.shape, const3),
                pl.BlockSpec(bg.shape, const3),
                pl.BlockSpec(cdec.shape, const2)]
    args = [rx_t, rx_t, rx_t, cw, cb, wg, bg, cdec]
    if reverse:
        in_specs.append(blk)
        args.append(hf)
    return pl.pallas_call(
        functools.partial(_scan_kernel, reverse=reverse, tt=tt, n_ctx=n_ctx, n_chunks=n_chunks),
        grid=(n_chunks,),
        in_specs=in_specs,
        out_specs=blk,
        out_shape=jax.ShapeDtypeStruct((lt, b, d), F32),
        scratch_shapes=[pltpu.VMEM((tt + CONV_W - 1, b, d), F32),
                        pltpu.VMEM((tt, b, d), F32),
                        pltpu.VMEM((tt, b, d), F32),
                        pltpu.VMEM((b, d), F32)],
        compiler_params=_cparams(("arbitrary",), 56),
        name="scan_bwd" if reverse else "scan_fwd",
    )(*args)


def _attn_kernel(lam_ref, q_ref, k_ref, v_ref, g_ref, o_ref, *, tq, tk, ctx_q_tiles, ctx_chunks, all_chunks,
                 out_scale):
    qi = pl.program_id(2)
    q = q_ref[...]
    lane = lax.broadcasted_iota(jnp.int32, q.shape, 1)
    first = (lane // (HEAD_DIM // 2)) % 2 == 0
    zero = jnp.zeros_like(q)
    q_a = jnp.where(first, q, zero)
    q_b = jnp.where(first, zero, q)
    n_kv = jnp.where(qi < ctx_q_tiles, ctx_chunks, all_chunks)

    def one(qm, kc, vc, m, l, acc):
        s = lax.dot_general(qm, kc, (((1,), (1,)), ((), ())), preferred_element_type=F32)
        mn = jnp.maximum(m, jnp.max(s, axis=-1, keepdims=True))
        alpha = jnp.exp(m - mn)
        p = jnp.exp(s - mn)
        l = alpha * l + jnp.sum(p, axis=-1, keepdims=True)
        acc = alpha * acc + jnp.dot(p.astype(BF16), vc, preferred_element_type=F32)
        return mn, l, acc

    def body(c, carry):
        m1, l1, a1, m2, l2, a2 = carry
        off = pl.multiple_of(c * tk, tk)
        kc = k_ref[pl.ds(off, tk), :]
        vc = v_ref[pl.ds(off, tk), :]
        m1, l1, a1 = one(q_a, kc, vc, m1, l1, a1)
        m2, l2, a2 = one(q_b, kc, vc, m2, l2, a2)
        return m1, l1, a1, m2, l2, a2

    m0 = jnp.full((tq, 1), NEG_BIG, F32)
    l0 = jnp.zeros((tq, 1), F32)
    a0 = jnp.zeros((tq, V_DIM), F32)
    m1, l1, a1, m2, l2, a2 = lax.fori_loop(0, n_kv, body, (m0, l0, a0, m0, l0, a0))
    o = a1 / l1 - lam_ref[0] * (a2 / l2)
    o_ref[...] = (_rms(o, g_ref[...]) * out_scale).astype(BF16)


def _attention(lam, q, k, v, g_sub, n_ctx_rows, with_ctx_queries, out_scale):
    b, lt, d = q.shape
    tq, tk = ATTN_TQ, ATTN_TK
    ctx_q_tiles = n_ctx_rows // tq
    q_off = 0 if with_ctx_queries else ctx_q_tiles
    lq = lt - q_off * tq
    kv = pl.BlockSpec((None, lt, V_DIM), lambda bi, h, qi: (bi, 0, h))
    return pl.pallas_call(
        functools.partial(_attn_kernel, tq=tq, tk=tk, ctx_q_tiles=ctx_q_tiles if with_ctx_queries else 0,
                          ctx_chunks=n_ctx_rows // tk, all_chunks=lt // tk, out_scale=out_scale),
        grid=(b, N_HEADS, lq // tq),
        in_specs=[pl.BlockSpec(memory_space=pltpu.SMEM),
                  pl.BlockSpec((None, tq, V_DIM), lambda bi, h, qi: (bi, qi + q_off, h)),
                  kv, kv,
                  pl.BlockSpec((1, V_DIM), lambda bi, h, qi: (0, 0))],
        out_specs=pl.BlockSpec((None, tq, V_DIM), lambda bi, h, qi: (bi, qi, h)),
        out_shape=jax.ShapeDtypeStruct((b, lq, d), BF16),
        compiler_params=_cparams(("parallel", "parallel", "arbitrary"), 40),
        name="diff_attn",
    )(lam, q, k, v, g_sub)


def _postmix_kernel(x_ref, at_ref, y_ref, rg_ref, gr_ref, ga_ref, gt1_ref, sh2_ref, sc2_ref, g2_ref,
                    wr_ref, wa_ref, wo_ref, wrt_ref, brt_ref, x1_ref, h2_ref, route_ref):
    yg = (y_ref[...].astype(F32) * _gelu_tanh(rg_ref[...])).astype(BF16)
    o_r = jnp.dot(yg, wr_ref[...], preferred_element_type=F32)
    o_a = jnp.dot(at_ref[...], wa_ref[...], preferred_element_type=F32)
    mix = (_sigmoid(gr_ref[...]) * o_r + _sigmoid(ga_ref[...]) * o_a).astype(BF16)
    x1 = x_ref[...] + gt1_ref[...] * jnp.dot(mix, wo_ref[...], preferred_element_type=F32)
    x1_ref[...] = x1
    h2 = _rms(x1, g2_ref[...]) * (1.0 + sc2_ref[...]) + sh2_ref[...]
    h2_ref[...] = h2

    logits = jnp.dot(h2, wrt_ref[...], preferred_element_type=F32,
                     precision=lax.Precision.HIGHEST) + brt_ref[...]
    lane = lax.broadcasted_iota(jnp.int32, logits.shape, 1)
    lane_f = lane.astype(F32)
    valid = lane < N_EXPERTS
    lg = jnp.where(valid, logits, NEG_BIG)
    e = jnp.exp(lg - jnp.max(lg, axis=-1, keepdims=True))
    e = jnp.where(valid, e, 0.0)
    p = e / jnp.sum(e, axis=-1, keepdims=True)
    p = jnp.where(valid, p, -1.0)
    grp = lane // EXPERTS_PER_GROUP
    best = jnp.max(jnp.where(grp == 0, p, -1.0), axis=-1, keepdims=True)
    best_g = jnp.zeros_like(best, dtype=jnp.int32)
    for gi in range(1, N_GROUPS):
        gm = jnp.max(jnp.where(grp == gi, p, -1.0), axis=-1, keepdims=True)
        better = gm > best
        best_g = jnp.where(better, gi, best_g)
        best = jnp.where(better, gm, best)
    pg = jnp.where(grp == best_g, p, -1.0)
    big = float(LANES)
    v1 = jnp.max(pg, axis=-1, keepdims=True)
    i1 = jnp.min(jnp.where(pg == v1, lane_f, big), axis=-1, keepdims=True)
    pg2 = jnp.where(lane_f == i1, -1.0, pg)
    v2 = jnp.max(pg2, axis=-1, keepdims=True)
    i2 = jnp.min(jnp.where(pg2 == v2, lane_f, big), axis=-1, keepdims=True)
    den = v1 + v2
    route = jnp.where(lane == 0, i1, jnp.where(lane == 1, i2, jnp.where(lane == 2, v1 / den, v2 / den)))
    route_ref[...] = route


def _postmix(xs, attn, y, rg, gr, ga, gt1, sh2, sc2, g2, wr, wa, wo, wrt, brt, n_ctx_tiles, with_ctx):
    b, lt, d = xs.shape
    off = 0 if with_ctx else n_ctx_tiles
    lq = lt - off * ROW_TILE
    full = pl.BlockSpec((None, ROW_TILE, d), lambda bi, i: (bi, i + off, 0))
    part = pl.BlockSpec((None, ROW_TILE, d), lambda bi, i: (bi, i, 0))
    mod_map = lambda bi, i: (jnp.where(i + off < n_ctx_tiles, b, bi), 0, 0)
    mod = pl.BlockSpec((None, 1, d), mod_map)
    const = lambda bi, i: (0, 0)
    return pl.pallas_call(
        _postmix_kernel,
        grid=(b, lq // ROW_TILE),
        in_specs=[full, part, full, full, full, full, mod, mod, mod,
                  pl.BlockSpec((1, d), const),
                  pl.BlockSpec((d, d), const), pl.BlockSpec((d, d), const), pl.BlockSpec((d, d), const),
                  pl.BlockSpec((d, LANES), const), pl.BlockSpec((1, LANES), const)],
        out_specs=[part, part, pl.BlockSpec((None, ROW_TILE, LANES), lambda bi, i: (bi, i, 0))],
        out_shape=[jax.ShapeDtypeStruct((b, lq, d), F32),
                   jax.ShapeDtypeStruct((b, lq, d), F32),
                   jax.ShapeDtypeStruct((b, lq, LANES), F32)],
        compiler_params=_cparams(("parallel", "arbitrary"), 48),
        name="postmix",
    )(xs, attn, y, rg, gr, ga, gt1, sh2, sc2, g2, wr, wa, wo, wrt, brt)


def _expert_kernel(be_ref, src_ref, dst_ref, nv_ref, h_hbm, w1_ref, w3_ref, w2_ref, y_hbm,
                   xbuf, ybuf, gsem, ssem):
    i = pl.program_id(0)

    @pl.when(i == 0)
    def _():
        ybuf[...] = jnp.zeros_like(ybuf)
        tail = y_hbm.at[pl.ds(y_hbm.shape[0] - MOE_ROWS, MOE_ROWS)]
        pltpu.make_async_copy(ybuf, tail, ssem).start()
        pltpu.make_async_copy(ybuf, tail, ssem).wait()

    @pl.when(i < nv_ref[0])
    def _():
        base = i * MOE_ROWS

        def gather(r, carry):
            pltpu.make_async_copy(h_hbm.at[pl.ds(src_ref[base + r], 1)], xbuf.at[pl.ds(r, 1)], gsem).start()
            return carry

        lax.fori_loop(0, MOE_ROWS, gather, 0)
        pltpu.make_async_copy(h_hbm.at[pl.ds(0, MOE_ROWS)], xbuf, gsem).wait()

        xb = xbuf[...].astype(BF16)
        a1 = jnp.dot(xb, w1_ref[...], preferred_element_type=F32)
        a3 = jnp.dot(xb, w3_ref[...], preferred_element_type=F32)
        mid = (a1 * _sigmoid(a1) * a3).astype(BF16)
        ybuf[...] = jnp.dot(mid, w2_ref[...], preferred_element_type=F32)

        def scatter(r, carry):
            pltpu.make_async_copy(ybuf.at[pl.ds(r, 1)], y_hbm.at[pl.ds(dst_ref[base + r], 1)], ssem).start()
            return carry

        lax.fori_loop(0, MOE_ROWS, scatter, 0)
        pltpu.make_async_copy(ybuf, y_hbm.at[pl.ds(0, MOE_ROWS)], ssem).wait()


def _experts(block_e, src, dst, n_valid, h2, w1, w3, w2, n_out_rows):
    t, d = h2.shape
    de = w1.shape[-1]
    n_blocks = block_e.shape[0]
    wspec = lambda shp: pl.BlockSpec((None,) + shp, lambda i, be, s, dd, nv: (be[i], 0, 0))
    return pl.pallas_call(
        _expert_kernel,
        grid_spec=pltpu.PrefetchScalarGridSpec(
            num_scalar_prefetch=4,
            grid=(n_blocks,),
            in_specs=[pl.BlockSpec(memory_space=pl.ANY), wspec((d, de)), wspec((d, de)), wspec((de, d))],
            out_specs=pl.BlockSpec(memory_space=pl.ANY),
            scratch_shapes=[pltpu.VMEM((MOE_ROWS, d), F32), pltpu.VMEM((MOE_ROWS, d), F32),
                            pltpu.SemaphoreType.DMA(()), pltpu.SemaphoreType.DMA(())]),
        out_shape=jax.ShapeDtypeStruct((n_out_rows, d), F32),
        compiler_params=_cparams(("arbitrary",), 48),
        name="experts",
    )(block_e, src, dst, n_valid, h2, w1, w3, w2)


def _route_tables(route, n_tok):
    expert = route[:, :TOP_K].astype(jnp.int32).reshape(-1)
    a = expert.shape[0]
    onehot = (expert[:, None] == jnp.arange(N_EXPERTS, dtype=jnp.int32)[None, :]).astype(jnp.int32)
    csum = jnp.cumsum(onehot, axis=0)
    rank = jnp.sum((csum - onehot) * onehot, axis=-1)
    sizes = csum[-1]
    padded = (sizes + MOE_ROWS - 1) // MOE_ROWS * MOE_ROWS
    pend = jnp.cumsum(padded)
    pstarts = pend - padded
    dest = pstarts[expert] + rank
    n_blocks = -(-a // MOE_ROWS) + N_EXPERTS
    n_slots = n_blocks * MOE_ROWS
    flat = jnp.arange(a, dtype=jnp.int32)
    row_in_block = jnp.arange(n_slots, dtype=jnp.int32) % MOE_ROWS
    src = jnp.zeros((n_slots,), jnp.int32).at[dest].set(flat // TOP_K)
    dst = (TOP_K * n_tok + row_in_block).at[dest].set((flat % TOP_K) * n_tok + flat // TOP_K)
    block_e = jnp.minimum(jnp.searchsorted(pend, jnp.arange(n_blocks, dtype=jnp.int32) * MOE_ROWS, side='right'),
                          N_EXPERTS - 1).astype(jnp.int32)
    n_valid = (pend[-1] // MOE_ROWS).astype(jnp.int32).reshape(1)
    return block_e, src, dst, n_valid


def _combine_kernel(x_ref, y0_ref, y1_ref, route_ref, gt2_ref, g_ref, o_ref, *, final):
    rt = route_ref[...]
    g0 = rt[:, 2:3]
    g1 = rt[:, 3:4]
    x2 = x_ref[...] + gt2_ref[...] * (g0 * y0_ref[...] + g1 * y1_ref[...])
    o_ref[...] = _rms(x2, g_ref[...]) if final else x2


def _combine(x1, ybuf, route, gt2, g_final, n_ctx_tiles, has_ctx, final):
    b, lq, d = x1.shape
    nt = lq // ROW_TILE
    n_tok = b * lq
    tiles_per_k = n_tok // ROW_TILE
    mod_map = lambda bi, i: (jnp.where(i < (n_ctx_tiles if has_ctx else 0), b, bi), 0, 0)
    tok = pl.BlockSpec((None, ROW_TILE, d), lambda bi, i: (bi, i, 0))
    return pl.pallas_call(
        functools.partial(_combine_kernel, final=final),
        grid=(b, nt),
        in_specs=[tok,
                  pl.BlockSpec((ROW_TILE, d), lambda bi, i: (bi * nt + i, 0)),
                  pl.BlockSpec((ROW_TILE, d), lambda bi, i: (tiles_per_k + bi * nt + i, 0)),
                  pl.BlockSpec((None, ROW_TILE, LANES), lambda bi, i: (bi, i, 0)),
                  pl.BlockSpec((None, 1, d), mod_map),
                  pl.BlockSpec((1, d), lambda bi, i: (0, 0))],
        out_specs=tok,
        out_shape=jax.ShapeDtypeStruct((b, lq, d), F32),
        compiler_params=_cparams(("parallel", "arbitrary"), 32),
        name="combine",
    )(x1, ybuf, ybuf, route, gt2, g_final)


def _rope_tables(n_ctx, seq):
    n_pairs = HEAD_DIM // 4
    inv = ROPE_BASE ** (-jnp.arange(n_pairs, dtype=F32) / n_pairs)
    rows = seq // GRID_W
    r = jnp.repeat(jnp.arange(rows, dtype=F32), GRID_W)
    col = jnp.tile(jnp.arange(GRID_W, dtype=F32), rows)
    ang = jnp.concatenate([r[:, None] * inv, col[:, None] * inv], axis=-1)
    cos = jnp.concatenate([jnp.ones((n_ctx, HEAD_DIM // 2), F32), jnp.cos(ang)], axis=0)
    sin = jnp.concatenate([jnp.zeros((n_ctx, HEAD_DIM // 2), F32), jnp.sin(ang)], axis=0)
    return jnp.tile(cos, (1, 4)), jnp.concatenate([-sin, -sin, sin, sin], axis=-1)


def _qk_perm():
    lane = jnp.arange(LANES)
    g, i = lane // 32, lane % 32
    within = (g % 2) * HEAD_DIM + 2 * i + g // 2
    return (jnp.arange(N_HEADS)[:, None] * LANES + within[None, :]).reshape(-1)


def kernel(x, c, ctx, c_ctx, w_mod, b_mod, g_norm1, g_norm2, w_in, conv_w, conv_b, lru_wa, lru_ba, lru_wx,
           lru_bx, lru_lambda, diff_lambda, g_subln, w_rnn_proj, w_attn_proj, w_o, w_router, b_router,
           w_e1, w_e3, w_e2, g_final):
    b, seq, d = x.shape
    n_ctx = ctx.shape[1]
    depth = w_mod.shape[0]
    n_ctx_tiles = n_ctx // ROW_TILE
    assert b == SUBLANES and d == N_HEADS * V_DIM and n_ctx % ROW_TILE == 0 and seq % ROW_TILE == 0

    cond = jnp.zeros((MOD_ROWS, d), F32).at[:b].set(c).at[b].set(c_ctx)
    mods = _adaln(cond, w_mod, b_mod).reshape(depth, MOD_ROWS, 6, 1, d)
    cos_t, sin_t = _rope_tables(n_ctx, seq)
    perm = _qk_perm()
    wrt = jnp.zeros((d, LANES), F32).at[:, :N_EXPERTS].set(w_router)
    brt = jnp.zeros((1, LANES), F32).at[0, :N_EXPERTS].set(b_router)

    xs = jnp.concatenate([ctx, x], axis=1)
    out = None
    for l in range(depth):
        last = l == depth - 1
        lam_init = 0.8 - 0.6 * math.exp(-0.3 * l)
        sh1, sc1, gt1, sh2, sc2, gt2 = [mods[l, :, j] for j in range(6)]
        wl = w_in[l]
        w_perm = jnp.concatenate(
            [wl[:, d:2 * d][:, perm], wl[:, 2 * d:3 * d], wl[:, 3 * d:4 * d][:, perm],
             wl[:, 0:d], wl[:, 4 * d:]], axis=1).astype(BF16)
        k, v, q, rx, rg, gr, ga = _inproj(xs, sh1, sc1, g_norm1[l][None], w_perm, cos_t, sin_t, n_ctx_tiles)

        lq1, lk1, lq2, lk2 = diff_lambda[l].astype(F32)
        lam = (jnp.exp(jnp.sum(lq1 * lk1)) - jnp.exp(jnp.sum(lq2 * lk2)) + lam_init).reshape(1)
        attn = _attention(lam, q, k, v, g_subln[l][None], n_ctx, not last, 1.0 - lam_init)

        rx_t = jnp.transpose(rx, (1, 0, 2))
        cdec = LRU_C * jax.nn.softplus(-lru_lambda[l].astype(F32))
        n_ctx_chunks = n_ctx // SCAN_CHUNK
        hf = None
        for direction in range(2):
            wg = jnp.concatenate([lru_wa[l, direction], lru_wx[l, direction]], axis=-1).astype(BF16)
            bg = jnp.concatenate([lru_ba[l, direction].reshape(RNN_BLOCKS, 1, RNN_BW),
                                  lru_bx[l, direction].reshape(RNN_BLOCKS, 1, RNN_BW)], axis=-1)
            hf = _scan(rx_t, conv_w[l], conv_b[l][None], wg, bg, cdec[direction][None], hf,
                       n_ctx_chunks, reverse=direction == 1)
        y = jnp.transpose(hf, (1, 0, 2)).astype(BF16)

        x1, h2, route = _postmix(xs, attn, y, rg, gr, ga, gt1, sh2, sc2, g_norm2[l][None],
                                 w_rnn_proj[l].astype(BF16), w_attn_proj[l].astype(BF16), w_o[l].astype(BF16),
                                 wrt, brt, n_ctx_tiles, not last)
        lq = x1.shape[1]
        n_tok = b * lq
        block_e, src, dst, n_valid = _route_tables(route.reshape(n_tok, LANES), n_tok)
        ybuf = _experts(block_e, src, dst, n_valid, h2.reshape(n_tok, d), w_e1[l].astype(BF16),
                        w_e3[l].astype(BF16), w_e2[l].astype(BF16), TOP_K * n_tok + MOE_ROWS)
        res = _combine(x1, ybuf, route, gt2, g_final[None], n_ctx_tiles, not last, last)
        if last:
            out = res
        else:
            xs = res
    return out
```

```python
import functools
import math

import jax
import jax.numpy as jnp
from jax import lax
from jax.experimental import pallas as pl
from jax.experimental.pallas import tpu as pltpu

F32 = jnp.float32
BF16 = jnp.bfloat16

N_HEADS = 8
HEAD_DIM = 64
V_DIM = 2 * HEAD_DIM
GRID_W = 64
EPS = 1e-6
RNN_BLOCKS = 8
RNN_BW = 128
CONV_W = 4
CONV_LEFT = 2
LRU_C = 8.0
ROPE_BASE = 10000.0
N_EXPERTS = 16
N_GROUPS = 4
EXPERTS_PER_GROUP = N_EXPERTS // N_GROUPS
TOP_K = 2

LANES = 128
SUBLANES = 8
ROW_TILE = 256
SCAN_CHUNK = 128
ATTN_TQ = 256
ATTN_TK = 512
MOE_ROWS = 512
MOD_ROWS = 16
NEG_BIG = -1e30
LOG2_E = math.log2(math.e)


def _cparams(sem, vmem_mib):
    return pltpu.CompilerParams(dimension_semantics=sem, vmem_limit_bytes=vmem_mib * 1024 * 1024)


def _sigmoid(x):
    return 1.0 / (1.0 + jnp.exp(-x))


def _gelu_tanh(x):
    return 0.5 * x * (1.0 + jnp.tanh(math.sqrt(2.0 / math.pi) * (x + 0.044715 * (x * x * x))))


def _rms(x, g):
    return x * lax.rsqrt(jnp.mean(x * x, axis=-1, keepdims=True) + EPS) * g


def _adaln_kernel(c_ref, w_ref, b_ref, o_ref):
    c = c_ref[...]
    a = c * _sigmoid(c)
    o_ref[...] = jnp.dot(a, w_ref[...], preferred_element_type=F32,
                         precision=lax.Precision.HIGHEST) + b_ref[...]


def _adaln(cond, w_mod, b_mod):
    depth, d, d6 = w_mod.shape
    n = d6 // d
    return pl.pallas_call(
        _adaln_kernel,
        grid=(depth, n),
        in_specs=[pl.BlockSpec((MOD_ROWS, d), lambda l, j: (0, 0)),
                  pl.BlockSpec((None, d, d), lambda l, j: (l, 0, j)),
                  pl.BlockSpec((None, 1, d), lambda l, j: (l, 0, j))],
        out_specs=pl.BlockSpec((None, MOD_ROWS, d), lambda l, j: (l, 0, j)),
        out_shape=jax.ShapeDtypeStruct((depth, MOD_ROWS, d6), F32),
        compiler_params=_cparams(("arbitrary", "arbitrary"), 32),
        name="adaln",
    )(cond, w_mod, b_mod.reshape(depth, 1, d6))


def _inproj_kernel(x_ref, sh_ref, sc_ref, g_ref, w_ref, cos_ref, sin_ref,
                   k_ref, v_ref, q_ref, rx_ref, rg_ref, gr_ref, ga_ref, *, d):
    x = x_ref[...]
    h = (_rms(x, g_ref[...]) * (1.0 + sc_ref[...]) + sh_ref[...]).astype(BF16)
    cosv = cos_ref[...]
    sinv = sin_ref[...]

    def proj(p):
        return jnp.dot(h, w_ref[:, p * d:(p + 1) * d], preferred_element_type=F32)

    def rope_store(res, ref, scale):
        for hh in range(N_HEADS):
            xs = res[:, hh * LANES:(hh + 1) * LANES]
            out = xs * cosv + pltpu.roll(xs, LANES // 2, 1) * sinv
            ref[:, hh * LANES:(hh + 1) * LANES] = (out * scale).astype(BF16)

    rope_store(proj(0), k_ref, 1.0)
    v_ref[...] = proj(1).astype(BF16)
    rope_store(proj(2), q_ref, HEAD_DIM ** -0.5 * LOG2_E)
    rx_ref[...] = proj(3)
    rg_ref[...] = proj(4)
    gr_ref[...] = proj(5)
    ga_ref[...] = proj(6)


def _inproj(xs, sh, sc, g, w, cos_t, sin_t, n_ctx_tiles):
    b, lt, d = xs.shape
    nt = lt // ROW_TILE
    mod_map = lambda bi, i: (jnp.where(i < n_ctx_tiles, b, bi), 0, 0)
    tok = pl.BlockSpec((None, ROW_TILE, d), lambda bi, i: (bi, i, 0))
    out_bf = jax.ShapeDtypeStruct((b, lt, d), BF16)
    out_f = jax.ShapeDtypeStruct((b, lt, d), F32)
    return pl.pallas_call(
        functools.partial(_inproj_kernel, d=d),
        grid=(b, nt),
        in_specs=[tok,
                  pl.BlockSpec((None, 1, d), mod_map),
                  pl.BlockSpec((None, 1, d), mod_map),
                  pl.BlockSpec((1, d), lambda bi, i: (0, 0)),
                  pl.BlockSpec(w.shape, lambda bi, i: (0, 0)),
                  pl.BlockSpec((ROW_TILE, LANES), lambda bi, i: (i, 0)),
                  pl.BlockSpec((ROW_TILE, LANES), lambda bi, i: (i, 0))],
        out_specs=[tok] * 7,
        out_shape=[out_bf, out_bf, out_bf, out_f, out_f, out_f, out_f],
        compiler_params=_cparams(("parallel", "arbitrary"), 56),
        name="inproj",
    )(xs, sh, sc, g, w, cos_t, sin_t)


def _scan_chunk(i, n_ctx, n_chunks, reverse):
    if not reverse:
        return i
    return jnp.where(i < n_ctx, n_ctx - 1 - i, n_chunks - 1 - (i - n_ctx))


def _scan_kernel(*refs, reverse, tt, n_ctx, n_chunks):
    if reverse:
        (rx_ref, prev_ref, next_ref, cw_ref, cb_ref, wg_ref, bg_ref, c_ref, hf_ref,
         out_ref, ext_scr, a_scr, b_scr, h_scr) = refs
    else:
        (rx_ref, prev_ref, next_ref, cw_ref, cb_ref, wg_ref, bg_ref, c_ref,
         out_ref, ext_scr, a_scr, b_scr, h_scr) = refs
        hf_ref = None
    i = pl.program_id(0)
    c = _scan_chunk(i, n_ctx, n_chunks, reverse)
    seq_start = jnp.logical_or(c == 0, c == n_ctx)
    seq_end = jnp.logical_or(c == n_ctx - 1, c == n_chunks - 1)

    ext_scr[0:CONV_LEFT] = jnp.where(seq_start, 0.0, prev_ref[...])
    ext_scr[CONV_LEFT:CONV_LEFT + tt] = rx_ref[...]
    ext_scr[CONV_LEFT + tt:CONV_LEFT + tt + 1] = jnp.where(seq_end, 0.0, next_ref[...])

    rows = tt * SUBLANES
    for n in range(RNN_BLOCKS):
        sl = slice(n * RNN_BW, (n + 1) * RNN_BW)
        u = cb_ref[:, sl] + cw_ref[0:1, sl] * ext_scr[0:tt, :, sl]
        for j in range(1, CONV_W):
            u = u + cw_ref[j:j + 1, sl] * ext_scr[j:j + tt, :, sl]
        u2 = u.reshape(rows, RNN_BW)
        z = jnp.dot(u2.astype(BF16), wg_ref[n], preferred_element_type=F32) + bg_ref[n]
        r = _sigmoid(z[:, :RNN_BW])
        ig = _sigmoid(z[:, RNN_BW:])
        a = jnp.exp(-c_ref[:, sl] * r)
        bb = jnp.sqrt(1.0 - a * a) * ig * u2
        a_scr[:, :, sl] = a.reshape(tt, SUBLANES, RNN_BW)
        b_scr[:, :, sl] = bb.reshape(tt, SUBLANES, RNN_BW)

    @pl.when(i == 0)
    def _():
        h_scr[...] = jnp.zeros_like(h_scr)

    def step(s, h):
        t = tt - 1 - s if reverse else s
        h = a_scr[t] * h + b_scr[t]
        if reverse:
            out_ref[t] = hf_ref[t] + h
        else:
            out_ref[t] = h
        return h

    h_scr[...] = lax.fori_loop(0, tt, step, h_scr[...], unroll=8)


def _scan(rx_t, cw, cb, wg, bg, cdec, hf, n_ctx, reverse):
    lt, b, d = rx_t.shape
    tt = SCAN_CHUNK
    n_chunks = lt // tt
    chunk = lambda i: _scan_chunk(i, n_ctx, n_chunks, reverse)
    blk = pl.BlockSpec((tt, b, d), lambda i: (chunk(i), 0, 0))
    const2 = lambda i: (0, 0)
    const3 = lambda i: (0, 0, 0)
    in_specs = [blk,
                pl.BlockSpec((CONV_LEFT, b, d), lambda i: (jnp.maximum(chunk(i) * (tt // CONV_LEFT) - 1, 0), 0, 0)),
                pl.BlockSpec((1, b, d), lambda i: (jnp.minimum((chunk(i) + 1) * tt, lt - 1), 0, 0)),
                pl.BlockSpec(cw.shape, const2),
                pl.BlockSpec(cb.shape, const2),
                pl.BlockSpec(wg.shape, const3),
                pl.BlockSpec(bg.shape, const3),
                pl.BlockSpec(cdec.shape, const2)]
    args = [rx_t, rx_t, rx_t, cw, cb, wg, bg, cdec]
    if reverse:
        in_specs.append(blk)
        args.append(hf)
    return pl.pallas_call(
        functools.partial(_scan_kernel, reverse=reverse, tt=tt, n_ctx=n_ctx, n_chunks=n_chunks),
        grid=(n_chunks,),
        in_specs=in_specs,
        out_specs=blk,
        out_shape=jax.ShapeDtypeStruct((lt, b, d), F32),
        scratch_shapes=[pltpu.VMEM((tt + CONV_W - 1, b, d), F32),
                        pltpu.VMEM((tt, b, d), F32),
                        pltpu.VMEM((tt, b, d), F32),
                        pltpu.VMEM((b, d), F32)],
        compiler_params=_cparams(("arbitrary",), 56),
        name="scan_bwd" if reverse else "scan_fwd",
    )(*args)


def _attn_tile(lam, q_ref, k_ref, v_ref, g_ref, o_ref, s_scr, *, tq, chunks, out_scale):
    q = q_ref[...]
    lane = lax.broadcasted_iota(jnp.int32, q.shape, 1)
    first = (lane // (HEAD_DIM // 2)) % 2 == 0
    zero = jnp.zeros_like(q)
    qs = jnp.concatenate([jnp.where(first, q, zero), jnp.where(first, zero, q)], axis=0)

    mrun = jnp.full((2 * tq, LANES), NEG_BIG, F32)
    for start, size in chunks:
        s = lax.dot_general(qs, k_ref[start:start + size, :], (((1,), (1,)), ((), ())),
                            preferred_element_type=F32)
        s_scr[:, start:start + size] = s
        for t in range(size // LANES):
            mrun = jnp.maximum(mrun, s[:, t * LANES:(t + 1) * LANES])
    mb = jnp.broadcast_to(jnp.max(mrun, axis=-1, keepdims=True), (2 * tq, LANES))

    lrun = jnp.zeros((2 * tq, LANES), F32)
    acc = jnp.zeros((2 * tq, V_DIM), F32)
    for start, size in chunks:
        ps = []
        for t in range(size // LANES):
            p = jnp.exp2(s_scr[:, start + t * LANES:start + (t + 1) * LANES] - mb)
            lrun = lrun + p
            ps.append(p.astype(BF16))
        acc = acc + jnp.dot(jnp.concatenate(ps, axis=1), v_ref[start:start + size, :],
                            preferred_element_type=F32)
    l = jnp.sum(lrun, axis=-1, keepdims=True)
    o = acc[:tq] / l[:tq] - lam * (acc[tq:] / l[tq:])
    o_ref[...] = (_rms(o, g_ref[...]) * out_scale).astype(BF16)


def _attn_kernel(lam_ref, q_ref, k_ref, v_ref, g_ref, o_ref, s_scr, *, tq, ctx_q_tiles, ctx_chunks, all_chunks,
                 out_scale):
    qi = pl.program_id(2)
    tile = functools.partial(_attn_tile, lam_ref[0], q_ref, k_ref, v_ref, g_ref, o_ref, s_scr, tq=tq,
                             out_scale=out_scale)
    if ctx_q_tiles:
        @pl.when(qi < ctx_q_tiles)
        def _():
            tile(chunks=ctx_chunks)

        @pl.when(qi >= ctx_q_tiles)
        def _():
            tile(chunks=all_chunks)
    else:
        tile(chunks=all_chunks)


def _key_chunks(n_ctx_rows, lt):
    ctx = [(s, min(ATTN_TK, n_ctx_rows - s)) for s in range(0, n_ctx_rows, ATTN_TK)]
    lat = [(s, min(ATTN_TK, lt - s)) for s in range(n_ctx_rows, lt, ATTN_TK)]
    return tuple(ctx), tuple(ctx + lat)


def _attention(lam, q, k, v, g_sub, n_ctx_rows, with_ctx_queries, out_scale):
    b, lt, d = q.shape
    tq = ATTN_TQ
    ctx_q_tiles = n_ctx_rows // tq
    q_off = 0 if with_ctx_queries else ctx_q_tiles
    lq = lt - q_off * tq
    ctx_chunks, all_chunks = _key_chunks(n_ctx_rows, lt)
    kv = pl.BlockSpec((None, lt, V_DIM), lambda bi, h, qi: (bi, 0, h))
    return pl.pallas_call(
        functools.partial(_attn_kernel, tq=tq, ctx_q_tiles=ctx_q_tiles if with_ctx_queries else 0,
                          ctx_chunks=ctx_chunks, all_chunks=all_chunks, out_scale=out_scale),
        grid=(b, N_HEADS, lq // tq),
        in_specs=[pl.BlockSpec(memory_space=pltpu.SMEM),
                  pl.BlockSpec((None, tq, V_DIM), lambda bi, h, qi: (bi, qi + q_off, h)),
                  kv, kv,
                  pl.BlockSpec((1, V_DIM), lambda bi, h, qi: (0, 0))],
        out_specs=pl.BlockSpec((None, tq, V_DIM), lambda bi, h, qi: (bi, qi, h)),
        out_shape=jax.ShapeDtypeStruct((b, lq, d), BF16),
        scratch_shapes=[pltpu.VMEM((2 * tq, lt), F32)],
        compiler_params=_cparams(("parallel", "parallel", "arbitrary"), 48),
        name="diff_attn",
    )(lam, q, k, v, g_sub)


def _postmix_kernel(x_ref, at_ref, y_ref, rg_ref, gr_ref, ga_ref, gt1_ref, sh2_ref, sc2_ref, g2_ref,
                    wr_ref, wa_ref, wo_ref, wrt_ref, brt_ref, x1_ref, h2_ref, route_ref):
    yg = (y_ref[...].astype(F32) * _gelu_tanh(rg_ref[...])).astype(BF16)
    o_r = jnp.dot(yg, wr_ref[...], preferred_element_type=F32)
    o_a = jnp.dot(at_ref[...], wa_ref[...], preferred_element_type=F32)
    mix = (_sigmoid(gr_ref[...]) * o_r + _sigmoid(ga_ref[...]) * o_a).astype(BF16)
    x1 = x_ref[...] + gt1_ref[...] * jnp.dot(mix, wo_ref[...], preferred_element_type=F32)
    x1_ref[...] = x1
    h2 = _rms(x1, g2_ref[...]) * (1.0 + sc2_ref[...]) + sh2_ref[...]
    h2_ref[...] = h2

    logits = jnp.dot(h2, wrt_ref[...], preferred_element_type=F32,
                     precision=lax.Precision.HIGHEST) + brt_ref[...]
    lane = lax.broadcasted_iota(jnp.int32, logits.shape, 1)
    lane_f = lane.astype(F32)
    valid = lane < N_EXPERTS
    lg = jnp.where(valid, logits, NEG_BIG)
    e = jnp.exp(lg - jnp.max(lg, axis=-1, keepdims=True))
    e = jnp.where(valid, e, 0.0)
    p = e / jnp.sum(e, axis=-1, keepdims=True)
    p = jnp.where(valid, p, -1.0)
    grp = lane // EXPERTS_PER_GROUP
    best = jnp.max(jnp.where(grp == 0, p, -1.0), axis=-1, keepdims=True)
    best_g = jnp.zeros_like(best, dtype=jnp.int32)
    for gi in range(1, N_GROUPS):
        gm = jnp.max(jnp.where(grp == gi, p, -1.0), axis=-1, keepdims=True)
        better = gm > best
        best_g = jnp.where(better, gi, best_g)
        best = jnp.where(better, gm, best)
    pg = jnp.where(grp == best_g, p, -1.0)
    big = float(LANES)
    v1 = jnp.max(pg, axis=-1, keepdims=True)
    i1 = jnp.min(jnp.where(pg == v1, lane_f, big), axis=-1, keepdims=True)
    pg2 = jnp.where(lane_f == i1, -1.0, pg)
    v2 = jnp.max(pg2, axis=-1, keepdims=True)
    i2 = jnp.min(jnp.where(pg2 == v2, lane_f, big), axis=-1, keepdims=True)
    den = v1 + v2
    route = jnp.where(lane == 0, i1, jnp.where(lane == 1, i2, jnp.where(lane == 2, v1 / den, v2 / den)))
    route_ref[...] = route


def _postmix(xs, attn, y, rg, gr, ga, gt1, sh2, sc2, g2, wr, wa, wo, wrt, brt, n_ctx_tiles, with_ctx):
    b, lt, d = xs.shape
    off = 0 if with_ctx else n_ctx_tiles
    lq = lt - off * ROW_TILE
    full = pl.BlockSpec((None, ROW_TILE, d), lambda bi, i: (bi, i + off, 0))
    part = pl.BlockSpec((None, ROW_TILE, d), lambda bi, i: (bi, i, 0))
    mod_map = lambda bi, i: (jnp.where(i + off < n_ctx_tiles, b, bi), 0, 0)
    mod = pl.BlockSpec((None, 1, d), mod_map)
    const = lambda bi, i: (0, 0)
    return pl.pallas_call(
        _postmix_kernel,
        grid=(b, lq // ROW_TILE),
        in_specs=[full, part, full, full, full, full, mod, mod, mod,
                  pl.BlockSpec((1, d), const),
                  pl.BlockSpec((d, d), const), pl.BlockSpec((d, d), const), pl.BlockSpec((d, d), const),
                  pl.BlockSpec((d, LANES), const), pl.BlockSpec((1, LANES), const)],
        out_specs=[part, part, pl.BlockSpec((None, ROW_TILE, LANES), lambda bi, i: (bi, i, 0))],
        out_shape=[jax.ShapeDtypeStruct((b, lq, d), F32),
                   jax.ShapeDtypeStruct((b, lq, d), F32),
                   jax.ShapeDtypeStruct((b, lq, LANES), F32)],
        compiler_params=_cparams(("parallel", "arbitrary"), 48),
        name="postmix",
    )(xs, attn, y, rg, gr, ga, gt1, sh2, sc2, g2, wr, wa, wo, wrt, brt)


def _expert_kernel(be_ref, src_ref, dst_ref, nv_ref, h_hbm, w1_ref, w3_ref, w2_ref, y_hbm,
                   xbuf, ybuf, gsem, ssem):
    i = pl.program_id(0)

    @pl.when(i == 0)
    def _():
        ybuf[...] = jnp.zeros_like(ybuf)
        tail = y_hbm.at[pl.ds(y_hbm.shape[0] - MOE_ROWS, MOE_ROWS)]
        pltpu.make_async_copy(ybuf, tail, ssem).start()
        pltpu.make_async_copy(ybuf, tail, ssem).wait()

    @pl.when(i < nv_ref[0])
    def _():
        base = i * MOE_ROWS

        def gather(r, carry):
            pltpu.make_async_copy(h_hbm.at[pl.ds(src_ref[base + r], 1)], xbuf.at[pl.ds(r, 1)], gsem).start()
            return carry

        lax.fori_loop(0, MOE_ROWS, gather, 0)
        pltpu.make_async_copy(h_hbm.at[pl.ds(0, MOE_ROWS)], xbuf, gsem).wait()

        xb = xbuf[...].astype(BF16)
        a1 = jnp.dot(xb, w1_ref[...], preferred_element_type=F32)
        a3 = jnp.dot(xb, w3_ref[...], preferred_element_type=F32)
        mid = (a1 * _sigmoid(a1) * a3).astype(BF16)
        ybuf[...] = jnp.dot(mid, w2_ref[...], preferred_element_type=F32)

        def scatter(r, carry):
            pltpu.make_async_copy(ybuf.at[pl.ds(r, 1)], y_hbm.at[pl.ds(dst_ref[base + r], 1)], ssem).start()
            return carry

        lax.fori_loop(0, MOE_ROWS, scatter, 0)
        pltpu.make_async_copy(ybuf, y_hbm.at[pl.ds(0, MOE_ROWS)], ssem).wait()


def _experts(block_e, src, dst, n_valid, h2, w1, w3, w2, n_out_rows):
    t, d = h2.shape
    de = w1.shape[-1]
    n_blocks = block_e.shape[0]
    wspec = lambda shp: pl.BlockSpec((None,) + shp, lambda i, be, s, dd, nv: (be[i], 0, 0))
    return pl.pallas_call(
        _expert_kernel,
        grid_spec=pltpu.PrefetchScalarGridSpec(
            num_scalar_prefetch=4,
            grid=(n_blocks,),
            in_specs=[pl.BlockSpec(memory_space=pl.ANY), wspec((d, de)), wspec((d, de)), wspec((de, d))],
            out_specs=pl.BlockSpec(memory_space=pl.ANY),
            scratch_shapes=[pltpu.VMEM((MOE_ROWS, d), F32), pltpu.VMEM((MOE_ROWS, d), F32),
                            pltpu.SemaphoreType.DMA(()), pltpu.SemaphoreType.DMA(())]),
        out_shape=jax.ShapeDtypeStruct((n_out_rows, d), F32),
        compiler_params=_cparams(("arbitrary",), 48),
        name="experts",
    )(block_e, src, dst, n_valid, h2, w1, w3, w2)


def _route_tables(route, n_tok):
    expert = route[:, :TOP_K].astype(jnp.int32).reshape(-1)
    a = expert.shape[0]
    onehot = (expert[:, None] == jnp.arange(N_EXPERTS, dtype=jnp.int32)[None, :]).astype(jnp.int32)
    csum = jnp.cumsum(onehot, axis=0)
    rank = jnp.sum((csum - onehot) * onehot, axis=-1)
    sizes = csum[-1]
    padded = (sizes + MOE_ROWS - 1) // MOE_ROWS * MOE_ROWS
    pend = jnp.cumsum(padded)
    pstarts = pend - padded
    dest = pstarts[expert] + rank
    n_blocks = -(-a // MOE_ROWS) + N_EXPERTS
    n_slots = n_blocks * MOE_ROWS
    flat = jnp.arange(a, dtype=jnp.int32)
    row_in_block = jnp.arange(n_slots, dtype=jnp.int32) % MOE_ROWS
    src = jnp.zeros((n_slots,), jnp.int32).at[dest].set(flat // TOP_K)
    dst = (TOP_K * n_tok + row_in_block).at[dest].set((flat % TOP_K) * n_tok + flat // TOP_K)
    block_e = jnp.minimum(jnp.searchsorted(pend, jnp.arange(n_blocks, dtype=jnp.int32) * MOE_ROWS, side='right'),
                          N_EXPERTS - 1).astype(jnp.int32)
    n_valid = (pend[-1] // MOE_ROWS).astype(jnp.int32).reshape(1)
    return block_e, src, dst, n_valid


def _combine_kernel(x_ref, y0_ref, y1_ref, route_ref, gt2_ref, g_ref, o_ref, *, final):
    rt = route_ref[...]
    g0 = rt[:, 2:3]
    g1 = rt[:, 3:4]
    x2 = x_ref[...] + gt2_ref[...] * (g0 * y0_ref[...] + g1 * y1_ref[...])
    o_ref[...] = _rms(x2, g_ref[...]) if final else x2


def _combine(x1, ybuf, route, gt2, g_final, n_ctx_tiles, has_ctx, final):
    b, lq, d = x1.shape
    nt = lq // ROW_TILE
    n_tok = b * lq
    tiles_per_k = n_tok // ROW_TILE
    mod_map = lambda bi, i: (jnp.where(i < (n_ctx_tiles if has_ctx else 0), b, bi), 0, 0)
    tok = pl.BlockSpec((None, ROW_TILE, d), lambda bi, i: (bi, i, 0))
    return pl.pallas_call(
        functools.partial(_combine_kernel, final=final),
        grid=(b, nt),
        in_specs=[tok,
                  pl.BlockSpec((ROW_TILE, d), lambda bi, i: (bi * nt + i, 0)),
                  pl.BlockSpec((ROW_TILE, d), lambda bi, i: (tiles_per_k + bi * nt + i, 0)),
                  pl.BlockSpec((None, ROW_TILE, LANES), lambda bi, i: (bi, i, 0)),
                  pl.BlockSpec((None, 1, d), mod_map),
                  pl.BlockSpec((1, d), lambda bi, i: (0, 0))],
        out_specs=tok,
        out_shape=jax.ShapeDtypeStruct((b, lq, d), F32),
        compiler_params=_cparams(("parallel", "arbitrary"), 32),
        name="combine",
    )(x1, ybuf, ybuf, route, gt2, g_final)


def _rope_tables(n_ctx, seq):
    n_pairs = HEAD_DIM // 4
    inv = ROPE_BASE ** (-jnp.arange(n_pairs, dtype=F32) / n_pairs)
    rows = seq // GRID_W
    r = jnp.repeat(jnp.arange(rows, dtype=F32), GRID_W)
    col = jnp.tile(jnp.arange(GRID_W, dtype=F32), rows)
    ang = jnp.concatenate([r[:, None] * inv, col[:, None] * inv], axis=-1)
    cos = jnp.concatenate([jnp.ones((n_ctx, HEAD_DIM // 2), F32), jnp.cos(ang)], axis=0)
    sin = jnp.concatenate([jnp.zeros((n_ctx, HEAD_DIM // 2), F32), jnp.sin(ang)], axis=0)
    return jnp.tile(cos, (1, 4)), jnp.concatenate([-sin, -sin, sin, sin], axis=-1)


def _qk_perm():
    lane = jnp.arange(LANES)
    g, i = lane // 32, lane % 32
    within = (g % 2) * HEAD_DIM + 2 * i + g // 2
    return (jnp.arange(N_HEADS)[:, None] * LANES + within[None, :]).reshape(-1)


def kernel(x, c, ctx, c_ctx, w_mod, b_mod, g_norm1, g_norm2, w_in, conv_w, conv_b, lru_wa, lru_ba, lru_wx,
           lru_bx, lru_lambda, diff_lambda, g_subln, w_rnn_proj, w_attn_proj, w_o, w_router, b_router,
           w_e1, w_e3, w_e2, g_final):
    b, seq, d = x.shape
    n_ctx = ctx.shape[1]
    depth = w_mod.shape[0]
    n_ctx_tiles = n_ctx // ROW_TILE
    assert b == SUBLANES and d == N_HEADS * V_DIM and n_ctx % ROW_TILE == 0 and seq % ROW_TILE == 0

    cond = jnp.zeros((MOD_ROWS, d), F32).at[:b].set(c).at[b].set(c_ctx)
    mods = _adaln(cond, w_mod, b_mod).reshape(depth, MOD_ROWS, 6, 1, d)
    cos_t, sin_t = _rope_tables(n_ctx, seq)
    perm = _qk_perm()
    wrt = jnp.zeros((d, LANES), F32).at[:, :N_EXPERTS].set(w_router)
    brt = jnp.zeros((1, LANES), F32).at[0, :N_EXPERTS].set(b_router)

    xs = jnp.concatenate([ctx, x], axis=1)
    out = None
    for l in range(depth):
        last = l == depth - 1
        lam_init = 0.8 - 0.6 * math.exp(-0.3 * l)
        sh1, sc1, gt1, sh2, sc2, gt2 = [mods[l, :, j] for j in range(6)]
        wl = w_in[l]
        w_perm = jnp.concatenate(
            [wl[:, d:2 * d][:, perm], wl[:, 2 * d:3 * d], wl[:, 3 * d:4 * d][:, perm],
             wl[:, 0:d], wl[:, 4 * d:]], axis=1).astype(BF16)
        k, v, q, rx, rg, gr, ga = _inproj(xs, sh1, sc1, g_norm1[l][None], w_perm, cos_t, sin_t, n_ctx_tiles)

        lq1, lk1, lq2, lk2 = diff_lambda[l].astype(F32)
        lam = (jnp.exp(jnp.sum(lq1 * lk1)) - jnp.exp(jnp.sum(lq2 * lk2)) + lam_init).reshape(1)
        attn = _attention(lam, q, k, v, g_subln[l][None], n_ctx, not last, 1.0 - lam_init)

        rx_t = jnp.transpose(rx, (1, 0, 2))
        cdec = LRU_C * jax.nn.softplus(-lru_lambda[l].astype(F32))
        n_ctx_chunks = n_ctx // SCAN_CHUNK
        hf = None
        for direction in range(2):
            wg = jnp.concatenate([lru_wa[l, direction], lru_wx[l, direction]], axis=-1).astype(BF16)
            bg = jnp.concatenate([lru_ba[l, direction].reshape(RNN_BLOCKS, 1, RNN_BW),
                                  lru_bx[l, direction].reshape(RNN_BLOCKS, 1, RNN_BW)], axis=-1)
            hf = _scan(rx_t, conv_w[l], conv_b[l][None], wg, bg, cdec[direction][None], hf,
                       n_ctx_chunks, reverse=direction == 1)
        y = jnp.transpose(hf, (1, 0, 2)).astype(BF16)

        x1, h2, route = _postmix(xs, attn, y, rg, gr, ga, gt1, sh2, sc2, g_norm2[l][None],
                                 w_rnn_proj[l].astype(BF16), w_attn_proj[l].astype(BF16), w_o[l].astype(BF16),
                                 wrt, brt, n_ctx_tiles, not last)
        lq = x1.shape[1]
        n_tok = b * lq
        block_e, src, dst, n_valid = _route_tables(route.reshape(n_tok, LANES), n_tok)
        ybuf = _experts(block_e, src, dst, n_valid, h2.reshape(n_tok, d), w_e1[l].astype(BF16),
                        w_e3[l].astype(BF16), w_e2[l].astype(BF16), TOP_K * n_tok + MOE_ROWS)
        res = _combine(x1, ybuf, route, gt2, g_final[None], n_ctx_tiles, not last, last)
        if last:
            out = res
        else:
            xs = res
    return out
```

```python
import functools
import math

import jax
import jax.numpy as jnp
from jax import lax
from jax.experimental import pallas as pl
from jax.experimental.pallas import tpu as pltpu

F32 = jnp.float32
BF16 = jnp.bfloat16

N_HEADS = 8
HEAD_DIM = 64
V_DIM = 2 * HEAD_DIM
GRID_W = 64
EPS = 1e-6
RNN_BLOCKS = 8
RNN_BW = 128
CONV_W = 4
CONV_LEFT = 2
LRU_C = 8.0
ROPE_BASE = 10000.0
N_EXPERTS = 16
N_GROUPS = 4
EXPERTS_PER_GROUP = N_EXPERTS // N_GROUPS
TOP_K = 2

LANES = 128
SUBLANES = 8
ROW_TILE = 256
SCAN_CHUNK = 128
ATTN_TQ = 256
ATTN_TK = 512
MOE_ROWS = 512
MOD_ROWS = 16
NEG_BIG = -1e30
LOG2_E = math.log2(math.e)


def _cparams(sem, vmem_mib):
    return pltpu.CompilerParams(dimension_semantics=sem, vmem_limit_bytes=vmem_mib * 1024 * 1024)


def _sigmoid(x):
    return 1.0 / (1.0 + jnp.exp(-x))


def _gelu_tanh(x):
    return 0.5 * x * (1.0 + jnp.tanh(math.sqrt(2.0 / math.pi) * (x + 0.044715 * (x * x * x))))


def _rms(x, g):
    return x * lax.rsqrt(jnp.mean(x * x, axis=-1, keepdims=True) + EPS) * g


def _adaln_kernel(c_ref, w_ref, b_ref, o_ref):
    c = c_ref[...]
    a = c * _sigmoid(c)
    o_ref[...] = jnp.dot(a, w_ref[...], preferred_element_type=F32,
                         precision=lax.Precision.HIGHEST) + b_ref[...]


def _adaln(cond, w_mod, b_mod):
    depth, d, d6 = w_mod.shape
    n = d6 // d
    return pl.pallas_call(
        _adaln_kernel,
        grid=(depth, n),
        in_specs=[pl.BlockSpec((MOD_ROWS, d), lambda l, j: (0, 0)),
                  pl.BlockSpec((None, d, d), lambda l, j: (l, 0, j)),
                  pl.BlockSpec((None, 1, d), lambda l, j: (l, 0, j))],
        out_specs=pl.BlockSpec((None, MOD_ROWS, d), lambda l, j: (l, 0, j)),
        out_shape=jax.ShapeDtypeStruct((depth, MOD_ROWS, d6), F32),
        compiler_params=_cparams(("arbitrary", "arbitrary"), 32),
        name="adaln",
    )(cond, w_mod, b_mod.reshape(depth, 1, d6))


def _inproj_kernel(x_ref, sh_ref, sc_ref, g_ref, w_ref, cos_ref, sin_ref,
                   k_ref, v_ref, q_ref, rx_ref, rg_ref, gr_ref, ga_ref, *, d):
    x = x_ref[...]
    h = (_rms(x, g_ref[...]) * (1.0 + sc_ref[...]) + sh_ref[...]).astype(BF16)
    cosv = cos_ref[...]
    sinv = sin_ref[...]

    def proj(p):
        return jnp.dot(h, w_ref[:, p * d:(p + 1) * d], preferred_element_type=F32)

    def rope_store(res, ref, scale):
        for hh in range(N_HEADS):
            xs = res[:, hh * LANES:(hh + 1) * LANES]
            out = xs * cosv + pltpu.roll(xs, LANES // 2, 1) * sinv
            ref[:, hh * LANES:(hh + 1) * LANES] = (out * scale).astype(BF16)

    rope_store(proj(0), k_ref, 1.0)
    v_ref[...] = proj(1).T.astype(BF16)
    rope_store(proj(2), q_ref, HEAD_DIM ** -0.5 * LOG2_E)
    rx_ref[...] = proj(3)
    rg_ref[...] = proj(4)
    gr_ref[...] = proj(5)
    ga_ref[...] = proj(6)


def _inproj(xs, sh, sc, g, w, cos_t, sin_t, n_ctx_tiles):
    b, lt, d = xs.shape
    nt = lt // ROW_TILE
    mod_map = lambda bi, i: (jnp.where(i < n_ctx_tiles, b, bi), 0, 0)
    tok = pl.BlockSpec((None, ROW_TILE, d), lambda bi, i: (bi, i, 0))
    out_bf = jax.ShapeDtypeStruct((b, lt, d), BF16)
    out_f = jax.ShapeDtypeStruct((b, lt, d), F32)
    return pl.pallas_call(
        functools.partial(_inproj_kernel, d=d),
        grid=(b, nt),
        in_specs=[tok,
                  pl.BlockSpec((None, 1, d), mod_map),
                  pl.BlockSpec((None, 1, d), mod_map),
                  pl.BlockSpec((1, d), lambda bi, i: (0, 0)),
                  pl.BlockSpec(w.shape, lambda bi, i: (0, 0)),
                  pl.BlockSpec((ROW_TILE, LANES), lambda bi, i: (i, 0)),
                  pl.BlockSpec((ROW_TILE, LANES), lambda bi, i: (i, 0))],
        out_specs=[tok, pl.BlockSpec((None, d, ROW_TILE), lambda bi, i: (bi, 0, i))] + [tok] * 5,
        out_shape=[out_bf, jax.ShapeDtypeStruct((b, d, lt), BF16), out_bf, out_f, out_f, out_f, out_f],
        compiler_params=_cparams(("parallel", "arbitrary"), 56),
        name="inproj",
    )(xs, sh, sc, g, w, cos_t, sin_t)


def _scan_chunk(i, n_ctx, n_chunks, reverse):
    if not reverse:
        return i
    return jnp.where(i < n_ctx, n_ctx - 1 - i, n_chunks - 1 - (i - n_ctx))


def _scan_kernel(*refs, reverse, tt, n_ctx, n_chunks):
    if reverse:
        (rx_ref, prev_ref, next_ref, cw_ref, cb_ref, wg_ref, bg_ref, c_ref, hf_ref,
         out_ref, ext_scr, a_scr, b_scr, h_scr) = refs
    else:
        (rx_ref, prev_ref, next_ref, cw_ref, cb_ref, wg_ref, bg_ref, c_ref,
         out_ref, ext_scr, a_scr, b_scr, h_scr) = refs
        hf_ref = None
    i = pl.program_id(0)
    c = _scan_chunk(i, n_ctx, n_chunks, reverse)
    seq_start = jnp.logical_or(c == 0, c == n_ctx)
    seq_end = jnp.logical_or(c == n_ctx - 1, c == n_chunks - 1)

    ext_scr[0:CONV_LEFT] = jnp.where(seq_start, 0.0, prev_ref[...])
    ext_scr[CONV_LEFT:CONV_LEFT + tt] = rx_ref[...]
    ext_scr[CONV_LEFT + tt:CONV_LEFT + tt + 1] = jnp.where(seq_end, 0.0, next_ref[...])

    rows = tt * SUBLANES
    for n in range(RNN_BLOCKS):
        sl = slice(n * RNN_BW, (n + 1) * RNN_BW)
        u = cb_ref[:, sl] + cw_ref[0:1, sl] * ext_scr[0:tt, :, sl]
        for j in range(1, CONV_W):
            u = u + cw_ref[j:j + 1, sl] * ext_scr[j:j + tt, :, sl]
        u2 = u.reshape(rows, RNN_BW)
        z = jnp.dot(u2.astype(BF16), wg_ref[n], preferred_element_type=F32) + bg_ref[n]
        r = _sigmoid(z[:, :RNN_BW])
        ig = _sigmoid(z[:, RNN_BW:])
        a = jnp.exp(-c_ref[:, sl] * r)
        bb = jnp.sqrt(1.0 - a * a) * ig * u2
        a_scr[:, :, sl] = a.reshape(tt, SUBLANES, RNN_BW)
        b_scr[:, :, sl] = bb.reshape(tt, SUBLANES, RNN_BW)

    @pl.when(i == 0)
    def _():
        h_scr[...] = jnp.zeros_like(h_scr)

    def step(s, h):
        t = tt - 1 - s if reverse else s
        h = a_scr[t] * h + b_scr[t]
        if reverse:
            out_ref[t] = hf_ref[t] + h
        else:
            out_ref[t] = h
        return h

    h_scr[...] = lax.fori_loop(0, tt, step, h_scr[...], unroll=8)


def _scan(rx_t, cw, cb, wg, bg, cdec, hf, n_ctx, reverse):
    lt, b, d = rx_t.shape
    tt = SCAN_CHUNK
    n_chunks = lt // tt
    chunk = lambda i: _scan_chunk(i, n_ctx, n_chunks, reverse)
    blk = pl.BlockSpec((tt, b, d), lambda i: (chunk(i), 0, 0))
    const2 = lambda i: (0, 0)
    const3 = lambda i: (0, 0, 0)
    in_specs = [blk,
                pl.BlockSpec((CONV_LEFT, b, d), lambda i: (jnp.maximum(chunk(i) * (tt // CONV_LEFT) - 1, 0), 0, 0)),
                pl.BlockSpec((1, b, d), lambda i: (jnp.minimum((chunk(i) + 1) * tt, lt - 1), 0, 0)),
                pl.BlockSpec(cw.shape, const2),
                pl.BlockSpec(cb.shape, const2),
                pl.BlockSpec(wg.shape, const3),
                pl.BlockSpec(bg.shape, const3),
                pl.BlockSpec(cdec.shape, const2)]
    args = [rx_t, rx_t, rx_t, cw, cb, wg, bg, cdec]
    if reverse:
        in_specs.append(blk)
        args.append(hf)
    return pl.pallas_call(
        functools.partial(_scan_kernel, reverse=reverse, tt=tt, n_ctx=n_ctx, n_chunks=n_chunks),
        grid=(n_chunks,),
        in_specs=in_specs,
        out_specs=blk,
        out_shape=jax.ShapeDtypeStruct((lt, b, d), F32),
        scratch_shapes=[pltpu.VMEM((tt + CONV_W - 1, b, d), F32),
                        pltpu.VMEM((tt, b, d), F32),
                        pltpu.VMEM((tt, b, d), F32),
                        pltpu.VMEM((b, d), F32)],
        compiler_params=_cparams(("arbitrary",), 56),
        name="scan_bwd" if reverse else "scan_fwd",
    )(*args)


def _attn_tile(lam, q_ref, k_ref, vt_ref, g_ref, o_ref, s_scr, *, tq, chunks, out_scale):
    q = q_ref[...]
    lane = lax.broadcasted_iota(jnp.int32, q.shape, 1)
    first = (lane // (HEAD_DIM // 2)) % 2 == 0
    zero = jnp.zeros_like(q)
    qs = jnp.concatenate([jnp.where(first, q, zero), jnp.where(first, zero, q)], axis=0)

    m8 = jnp.full((SUBLANES, 2 * tq), NEG_BIG, F32)
    for start, size in chunks:
        st = lax.dot_general(k_ref[start:start + size, :], qs, (((1,), (1,)), ((), ())),
                             preferred_element_type=F32)
        s_scr[start:start + size, :] = st
        m8 = jnp.maximum(m8, jnp.max(st.reshape(size // SUBLANES, SUBLANES, 2 * tq), axis=0))
    m = jnp.max(m8, axis=0, keepdims=True)

    l8 = jnp.zeros((SUBLANES, 2 * tq), F32)
    acc = jnp.zeros((V_DIM, 2 * tq), F32)
    for start, size in chunks:
        p = jnp.exp2(s_scr[start:start + size, :] - m)
        l8 = l8 + jnp.sum(p.reshape(size // SUBLANES, SUBLANES, 2 * tq), axis=0)
        acc = acc + jnp.dot(vt_ref[:, start:start + size], p.astype(BF16), preferred_element_type=F32)
    l = jnp.sum(l8, axis=0, keepdims=True)
    on = acc * (1.0 / l)
    dt = on[:, :tq] - lam * on[:, tq:]
    yt = dt * lax.rsqrt(jnp.mean(dt * dt, axis=0, keepdims=True) + EPS)
    o_ref[...] = (yt.T * g_ref[...] * out_scale).astype(BF16)


def _attn_kernel(lam_ref, q_ref, k_ref, vt_ref, g_ref, o_ref, s_scr, *, tq, ctx_q_tiles, ctx_chunks, all_chunks,
                 out_scale):
    qi = pl.program_id(2)
    tile = functools.partial(_attn_tile, lam_ref[0], q_ref, k_ref, vt_ref, g_ref, o_ref, s_scr, tq=tq,
                             out_scale=out_scale)
    if ctx_q_tiles:
        @pl.when(qi < ctx_q_tiles)
        def _():
            tile(chunks=ctx_chunks)

        @pl.when(qi >= ctx_q_tiles)
        def _():
            tile(chunks=all_chunks)
    else:
        tile(chunks=all_chunks)


def _key_chunks(n_ctx_rows, lt):
    ctx = [(s, min(ATTN_TK, n_ctx_rows - s)) for s in range(0, n_ctx_rows, ATTN_TK)]
    lat = [(s, min(ATTN_TK, lt - s)) for s in range(n_ctx_rows, lt, ATTN_TK)]
    return tuple(ctx), tuple(ctx + lat)


def _attention(lam, q, k, vt, g_sub, n_ctx_rows, with_ctx_queries, out_scale):
    b, lt, d = q.shape
    tq = ATTN_TQ
    ctx_q_tiles = n_ctx_rows // tq
    q_off = 0 if with_ctx_queries else ctx_q_tiles
    lq = lt - q_off * tq
    ctx_chunks, all_chunks = _key_chunks(n_ctx_rows, lt)
    return pl.pallas_call(
        functools.partial(_attn_kernel, tq=tq, ctx_q_tiles=ctx_q_tiles if with_ctx_queries else 0,
                          ctx_chunks=ctx_chunks, all_chunks=all_chunks, out_scale=out_scale),
        grid=(b, N_HEADS, lq // tq),
        in_specs=[pl.BlockSpec(memory_space=pltpu.SMEM),
                  pl.BlockSpec((None, tq, V_DIM), lambda bi, h, qi: (bi, qi + q_off, h)),
                  pl.BlockSpec((None, lt, V_DIM), lambda bi, h, qi: (bi, 0, h)),
                  pl.BlockSpec((None, V_DIM, lt), lambda bi, h, qi: (bi, h, 0)),
                  pl.BlockSpec((1, V_DIM), lambda bi, h, qi: (0, 0))],
        out_specs=pl.BlockSpec((None, tq, V_DIM), lambda bi, h, qi: (bi, qi, h)),
        out_shape=jax.ShapeDtypeStruct((b, lq, d), BF16),
        scratch_shapes=[pltpu.VMEM((lt, 2 * tq), F32)],
        compiler_params=_cparams(("parallel", "parallel", "arbitrary"), 48),
        name="diff_attn",
    )(lam, q, k, vt, g_sub)


def _postmix_kernel(x_ref, at_ref, y_ref, rg_ref, gr_ref, ga_ref, gt1_ref, sh2_ref, sc2_ref, g2_ref,
                    wr_ref, wa_ref, wo_ref, wrt_ref, brt_ref, x1_ref, h2_ref, route_ref):
    yg = (y_ref[...].astype(F32) * _gelu_tanh(rg_ref[...])).astype(BF16)
    o_r = jnp.dot(yg, wr_ref[...], preferred_element_type=F32)
    o_a = jnp.dot(at_ref[...], wa_ref[...], preferred_element_type=F32)
    mix = (_sigmoid(gr_ref[...]) * o_r + _sigmoid(ga_ref[...]) * o_a).astype(BF16)
    x1 = x_ref[...] + gt1_ref[...] * jnp.dot(mix, wo_ref[...], preferred_element_type=F32)
    x1_ref[...] = x1
    h2 = _rms(x1, g2_ref[...]) * (1.0 + sc2_ref[...]) + sh2_ref[...]
    h2_ref[...] = h2

    logits = jnp.dot(h2, wrt_ref[...], preferred_element_type=F32,
                     precision=lax.Precision.HIGHEST) + brt_ref[...]
    lane = lax.broadcasted_iota(jnp.int32, logits.shape, 1)
    lane_f = lane.astype(F32)
    valid = lane < N_EXPERTS
    lg = jnp.where(valid, logits, NEG_BIG)
    e = jnp.exp(lg - jnp.max(lg, axis=-1, keepdims=True))
    e = jnp.where(valid, e, 0.0)
    p = e / jnp.sum(e, axis=-1, keepdims=True)
    p = jnp.where(valid, p, -1.0)
    grp = lane // EXPERTS_PER_GROUP
    best = jnp.max(jnp.where(grp == 0, p, -1.0), axis=-1, keepdims=True)
    best_g = jnp.zeros_like(best, dtype=jnp.int32)
    for gi in range(1, N_GROUPS):
        gm = jnp.max(jnp.where(grp == gi, p, -1.0), axis=-1, keepdims=True)
        better = gm > best
        best_g = jnp.where(better, gi, best_g)
        best = jnp.where(better, gm, best)
    pg = jnp.where(grp == best_g, p, -1.0)
    big = float(LANES)
    v1 = jnp.max(pg, axis=-1, keepdims=True)
    i1 = jnp.min(jnp.where(pg == v1, lane_f, big), axis=-1, keepdims=True)
    pg2 = jnp.where(lane_f == i1, -1.0, pg)
    v2 = jnp.max(pg2, axis=-1, keepdims=True)
    i2 = jnp.min(jnp.where(pg2 == v2, lane_f, big), axis=-1, keepdims=True)
    den = v1 + v2
    route = jnp.where(lane == 0, i1, jnp.where(lane == 1, i2, jnp.where(lane == 2, v1 / den, v2 / den)))
    route_ref[...] = route


def _postmix(xs, attn, y, rg, gr, ga, gt1, sh2, sc2, g2, wr, wa, wo, wrt, brt, n_ctx_tiles, with_ctx):
    b, lt, d = xs.shape
    off = 0 if with_ctx else n_ctx_tiles
    lq = lt - off * ROW_TILE
    full = pl.BlockSpec((None, ROW_TILE, d), lambda bi, i: (bi, i + off, 0))
    part = pl.BlockSpec((None, ROW_TILE, d), lambda bi, i: (bi, i, 0))
    mod_map = lambda bi, i: (jnp.where(i + off < n_ctx_tiles, b, bi), 0, 0)
    mod = pl.BlockSpec((None, 1, d), mod_map)
    const = lambda bi, i: (0, 0)
    return pl.pallas_call(
        _postmix_kernel,
        grid=(b, lq // ROW_TILE),
        in_specs=[full, part, full, full, full, full, mod, mod, mod,
                  pl.BlockSpec((1, d), const),
                  pl.BlockSpec((d, d), const), pl.BlockSpec((d, d), const), pl.BlockSpec((d, d), const),
                  pl.BlockSpec((d, LANES), const), pl.BlockSpec((1, LANES), const)],
        out_specs=[part, part, pl.BlockSpec((None, ROW_TILE, LANES), lambda bi, i: (bi, i, 0))],
        out_shape=[jax.ShapeDtypeStruct((b, lq, d), F32),
                   jax.ShapeDtypeStruct((b, lq, d), F32),
                   jax.ShapeDtypeStruct((b, lq, LANES), F32)],
        compiler_params=_cparams(("parallel", "arbitrary"), 48),
        name="postmix",
    )(xs, attn, y, rg, gr, ga, gt1, sh2, sc2, g2, wr, wa, wo, wrt, brt)


def _expert_kernel(be_ref, src_ref, dst_ref, nv_ref, h_hbm, w1_ref, w3_ref, w2_ref, y_hbm,
                   xbuf, ybuf, gsem, ssem):
    i = pl.program_id(0)

    @pl.when(i == 0)
    def _():
        ybuf[...] = jnp.zeros_like(ybuf)
        tail = y_hbm.at[pl.ds(y_hbm.shape[0] - MOE_ROWS, MOE_ROWS)]
        pltpu.make_async_copy(ybuf, tail, ssem).start()
        pltpu.make_async_copy(ybuf, tail, ssem).wait()

    @pl.when(i < nv_ref[0])
    def _():
        base = i * MOE_ROWS

        def gather(r, carry):
            pltpu.make_async_copy(h_hbm.at[pl.ds(src_ref[base + r], 1)], xbuf.at[pl.ds(r, 1)], gsem).start()
            return carry

        lax.fori_loop(0, MOE_ROWS, gather, 0)
        pltpu.make_async_copy(h_hbm.at[pl.ds(0, MOE_ROWS)], xbuf, gsem).wait()

        xb = xbuf[...].astype(BF16)
        a1 = jnp.dot(xb, w1_ref[...], preferred_element_type=F32)
        a3 = jnp.dot(xb, w3_ref[...], preferred_element_type=F32)
        mid = (a1 * _sigmoid(a1) * a3).astype(BF16)
        ybuf[...] = jnp.dot(mid, w2_ref[...], preferred_element_type=F32)

        def scatter(r, carry):
            pltpu.make_async_copy(ybuf.at[pl.ds(r, 1)], y_hbm.at[pl.ds(dst_ref[base + r], 1)], ssem).start()
            return carry

        lax.fori_loop(0, MOE_ROWS, scatter, 0)
        pltpu.make_async_copy(ybuf, y_hbm.at[pl.ds(0, MOE_ROWS)], ssem).wait()


def _experts(block_e, src, dst, n_valid, h2, w1, w3, w2, n_out_rows):
    t, d = h2.shape
    de = w1.shape[-1]
    n_blocks = block_e.shape[0]
    wspec = lambda shp: pl.BlockSpec((None,) + shp, lambda i, be, s, dd, nv: (be[i], 0, 0))
    return pl.pallas_call(
        _expert_kernel,
        grid_spec=pltpu.PrefetchScalarGridSpec(
            num_scalar_prefetch=4,
            grid=(n_blocks,),
            in_specs=[pl.BlockSpec(memory_space=pl.ANY), wspec((d, de)), wspec((d, de)), wspec((de, d))],
            out_specs=pl.BlockSpec(memory_space=pl.ANY),
            scratch_shapes=[pltpu.VMEM((MOE_ROWS, d), F32), pltpu.VMEM((MOE_ROWS, d), F32),
                            pltpu.SemaphoreType.DMA(()), pltpu.SemaphoreType.DMA(())]),
        out_shape=jax.ShapeDtypeStruct((n_out_rows, d), F32),
        compiler_params=_cparams(("arbitrary",), 48),
        name="experts",
    )(block_e, src, dst, n_valid, h2, w1, w3, w2)


def _route_tables(route, n_tok):
    expert = route[:, :TOP_K].astype(jnp.int32).reshape(-1)
    a = expert.shape[0]
    onehot = (expert[:, None] == jnp.arange(N_EXPERTS, dtype=jnp.int32)[None, :]).astype(jnp.int32)
    csum = jnp.cumsum(onehot, axis=0)
    rank = jnp.sum((csum - onehot) * onehot, axis=-1)
    sizes = csum[-1]
    padded = (sizes + MOE_ROWS - 1) // MOE_ROWS * MOE_ROWS
    pend = jnp.cumsum(padded)
    pstarts = pend - padded
    dest = pstarts[expert] + rank
    n_blocks = -(-a // MOE_ROWS) + N_EXPERTS
    n_slots = n_blocks * MOE_ROWS
    flat = jnp.arange(a, dtype=jnp.int32)
    row_in_block = jnp.arange(n_slots, dtype=jnp.int32) % MOE_ROWS
    src = jnp.zeros((n_slots,), jnp.int32).at[dest].set(flat // TOP_K)
    dst = (TOP_K * n_tok + row_in_block).at[dest].set((flat % TOP_K) * n_tok + flat // TOP_K)
    block_e = jnp.minimum(jnp.searchsorted(pend, jnp.arange(n_blocks, dtype=jnp.int32) * MOE_ROWS, side='right'),
                          N_EXPERTS - 1).astype(jnp.int32)
    n_valid = (pend[-1] // MOE_ROWS).astype(jnp.int32).reshape(1)
    return block_e, src, dst, n_valid


def _combine_kernel(x_ref, y0_ref, y1_ref, route_ref, gt2_ref, g_ref, o_ref, *, final):
    rt = route_ref[...]
    g0 = rt[:, 2:3]
    g1 = rt[:, 3:4]
    x2 = x_ref[...] + gt2_ref[...] * (g0 * y0_ref[...] + g1 * y1_ref[...])
    o_ref[...] = _rms(x2, g_ref[...]) if final else x2


def _combine(x1, ybuf, route, gt2, g_final, n_ctx_tiles, has_ctx, final):
    b, lq, d = x1.shape
    nt = lq // ROW_TILE
    n_tok = b * lq
    tiles_per_k = n_tok // ROW_TILE
    mod_map = lambda bi, i: (jnp.where(i < (n_ctx_tiles if has_ctx else 0), b, bi), 0, 0)
    tok = pl.BlockSpec((None, ROW_TILE, d), lambda bi, i: (bi, i, 0))
    return pl.pallas_call(
        functools.partial(_combine_kernel, final=final),
        grid=(b, nt),
        in_specs=[tok,
                  pl.BlockSpec((ROW_TILE, d), lambda bi, i: (bi * nt + i, 0)),
                  pl.BlockSpec((ROW_TILE, d), lambda bi, i: (tiles_per_k + bi * nt + i, 0)),
                  pl.BlockSpec((None, ROW_TILE, LANES), lambda bi, i: (bi, i, 0)),
                  pl.BlockSpec((None, 1, d), mod_map),
                  pl.BlockSpec((1, d), lambda bi, i: (0, 0))],
        out_specs=tok,
        out_shape=jax.ShapeDtypeStruct((b, lq, d), F32),
        compiler_params=_cparams(("parallel", "arbitrary"), 32),
        name="combine",
    )(x1, ybuf, ybuf, route, gt2, g_final)


def _rope_tables(n_ctx, seq):
    n_pairs = HEAD_DIM // 4
    inv = ROPE_BASE ** (-jnp.arange(n_pairs, dtype=F32) / n_pairs)
    rows = seq // GRID_W
    r = jnp.repeat(jnp.arange(rows, dtype=F32), GRID_W)
    col = jnp.tile(jnp.arange(GRID_W, dtype=F32), rows)
    ang = jnp.concatenate([r[:, None] * inv, col[:, None] * inv], axis=-1)
    cos = jnp.concatenate([jnp.ones((n_ctx, HEAD_DIM // 2), F32), jnp.cos(ang)], axis=0)
    sin = jnp.concatenate([jnp.zeros((n_ctx, HEAD_DIM // 2), F32), jnp.sin(ang)], axis=0)
    return jnp.tile(cos, (1, 4)), jnp.concatenate([-sin, -sin, sin, sin], axis=-1)


def _qk_perm():
    lane = jnp.arange(LANES)
    g, i = lane // 32, lane % 32
    within = (g % 2) * HEAD_DIM + 2 * i + g // 2
    return (jnp.arange(N_HEADS)[:, None] * LANES + within[None, :]).reshape(-1)


def kernel(x, c, ctx, c_ctx, w_mod, b_mod, g_norm1, g_norm2, w_in, conv_w, conv_b, lru_wa, lru_ba, lru_wx,
           lru_bx, lru_lambda, diff_lambda, g_subln, w_rnn_proj, w_attn_proj, w_o, w_router, b_router,
           w_e1, w_e3, w_e2, g_final):
    b, seq, d = x.shape
    n_ctx = ctx.shape[1]
    depth = w_mod.shape[0]
    n_ctx_tiles = n_ctx // ROW_TILE
    assert b == SUBLANES and d == N_HEADS * V_DIM and n_ctx % ROW_TILE == 0 and seq % ROW_TILE == 0

    cond = jnp.zeros((MOD_ROWS, d), F32).at[:b].set(c).at[b].set(c_ctx)
    mods = _adaln(cond, w_mod, b_mod).reshape(depth, MOD_ROWS, 6, 1, d)
    cos_t, sin_t = _rope_tables(n_ctx, seq)
    perm = _qk_perm()
    wrt = jnp.zeros((d, LANES), F32).at[:, :N_EXPERTS].set(w_router)
    brt = jnp.zeros((1, LANES), F32).at[0, :N_EXPERTS].set(b_router)

    xs = jnp.concatenate([ctx, x], axis=1)
    out = None
    for l in range(depth):
        last = l == depth - 1
        lam_init = 0.8 - 0.6 * math.exp(-0.3 * l)
        sh1, sc1, gt1, sh2, sc2, gt2 = [mods[l, :, j] for j in range(6)]
        wl = w_in[l]
        w_perm = jnp.concatenate(
            [wl[:, d:2 * d][:, perm], wl[:, 2 * d:3 * d], wl[:, 3 * d:4 * d][:, perm],
             wl[:, 0:d], wl[:, 4 * d:]], axis=1).astype(BF16)
        k, v, q, rx, rg, gr, ga = _inproj(xs, sh1, sc1, g_norm1[l][None], w_perm, cos_t, sin_t, n_ctx_tiles)

        lq1, lk1, lq2, lk2 = diff_lambda[l].astype(F32)
        lam = (jnp.exp(jnp.sum(lq1 * lk1)) - jnp.exp(jnp.sum(lq2 * lk2)) + lam_init).reshape(1)
        attn = _attention(lam, q, k, v, g_subln[l][None], n_ctx, not last, 1.0 - lam_init)

        rx_t = jnp.transpose(rx, (1, 0, 2))
        cdec = LRU_C * jax.nn.softplus(-lru_lambda[l].astype(F32))
        n_ctx_chunks = n_ctx // SCAN_CHUNK
        hf = None
        for direction in range(2):
            wg = jnp.concatenate([lru_wa[l, direction], lru_wx[l, direction]], axis=-1).astype(BF16)
            bg = jnp.concatenate([lru_ba[l, direction].reshape(RNN_BLOCKS, 1, RNN_BW),
                                  lru_bx[l, direction].reshape(RNN_BLOCKS, 1, RNN_BW)], axis=-1)
            hf = _scan(rx_t, conv_w[l], conv_b[l][None], wg, bg, cdec[direction][None], hf,
                       n_ctx_chunks, reverse=direction == 1)
        y = jnp.transpose(hf, (1, 0, 2)).astype(BF16)

        x1, h2, route = _postmix(xs, attn, y, rg, gr, ga, gt1, sh2, sc2, g_norm2[l][None],
                                 w_rnn_proj[l].astype(BF16), w_attn_proj[l].astype(BF16), w_o[l].astype(BF16),
                                 wrt, brt, n_ctx_tiles, not last)
        lq = x1.shape[1]
        n_tok = b * lq
        block_e, src, dst, n_valid = _route_tables(route.reshape(n_tok, LANES), n_tok)
        ybuf = _experts(block_e, src, dst, n_valid, h2.reshape(n_tok, d), w_e1[l].astype(BF16),
                        w_e3[l].astype(BF16), w_e2[l].astype(BF16), TOP_K * n_tok + MOE_ROWS)
        res = _combine(x1, ybuf, route, gt2, g_final[None], n_ctx_tiles, not last, last)
        if last:
            out = res
        else:
            xs = res
    return out
```

```python
import functools
import math

import jax
import jax.numpy as jnp
from jax import lax
from jax.experimental import pallas as pl
from jax.experimental.pallas import tpu as pltpu

F32 = jnp.float32
BF16 = jnp.bfloat16

N_HEADS = 8
HEAD_DIM = 64
V_DIM = 2 * HEAD_DIM
GRID_W = 64
EPS = 1e-6
RNN_BLOCKS = 8
RNN_BW = 128
CONV_W = 4
CONV_LEFT = 2
LRU_C = 8.0
ROPE_BASE = 10000.0
N_EXPERTS = 16
N_GROUPS = 4
EXPERTS_PER_GROUP = N_EXPERTS // N_GROUPS
TOP_K = 2

LANES = 128
SUBLANES = 8
ROW_TILE = 256
SCAN_CHUNK = 128
ATTN_TQ = 256
ATTN_TK = 512
MOE_ROWS = 512
DISPATCH_ROWS = 512
MOD_ROWS = 16
NEG_BIG = -1e30
LOG2_E = math.log2(math.e)


def _cparams(sem, vmem_mib):
    return pltpu.CompilerParams(dimension_semantics=sem, vmem_limit_bytes=vmem_mib * 1024 * 1024)


def _sigmoid(x):
    return 1.0 / (1.0 + jnp.exp(-x))


def _gelu_tanh(x):
    return 0.5 * x * (1.0 + jnp.tanh(math.sqrt(2.0 / math.pi) * (x + 0.044715 * (x * x * x))))


def _rms(x, g):
    return x * lax.rsqrt(jnp.mean(x * x, axis=-1, keepdims=True) + EPS) * g


def _adaln_kernel(c_ref, w_ref, b_ref, o_ref):
    c = c_ref[...]
    a = c * _sigmoid(c)
    o_ref[...] = jnp.dot(a, w_ref[...], preferred_element_type=F32,
                         precision=lax.Precision.HIGHEST) + b_ref[...]


def _adaln(cond, w_mod, b_mod):
    depth, d, d6 = w_mod.shape
    n = d6 // d
    return pl.pallas_call(
        _adaln_kernel,
        grid=(depth, n),
        in_specs=[pl.BlockSpec((MOD_ROWS, d), lambda l, j: (0, 0)),
                  pl.BlockSpec((None, d, d), lambda l, j: (l, 0, j)),
                  pl.BlockSpec((None, 1, d), lambda l, j: (l, 0, j))],
        out_specs=pl.BlockSpec((None, MOD_ROWS, d), lambda l, j: (l, 0, j)),
        out_shape=jax.ShapeDtypeStruct((depth, MOD_ROWS, d6), F32),
        compiler_params=_cparams(("arbitrary", "arbitrary"), 32),
        name="adaln",
    )(cond, w_mod, b_mod.reshape(depth, 1, d6))


def _inproj_kernel(x_ref, sh_ref, sc_ref, g_ref, w_ref, cos_ref, sin_ref,
                   k_ref, v_ref, q_ref, rx_ref, rg_ref, gr_ref, ga_ref, *, d):
    x = x_ref[...]
    h = (_rms(x, g_ref[...]) * (1.0 + sc_ref[...]) + sh_ref[...]).astype(BF16)
    cosv = cos_ref[...]
    sinv = sin_ref[...]

    def proj(p):
        return jnp.dot(h, w_ref[:, p * d:(p + 1) * d], preferred_element_type=F32)

    def rope_store(res, ref, scale):
        for hh in range(N_HEADS):
            xs = res[:, hh * LANES:(hh + 1) * LANES]
            out = xs * cosv + pltpu.roll(xs, LANES // 2, 1) * sinv
            ref[:, hh * LANES:(hh + 1) * LANES] = (out * scale).astype(BF16)

    rope_store(proj(0), k_ref, 1.0)
    v_ref[...] = proj(1).T.astype(BF16)
    rope_store(proj(2), q_ref, HEAD_DIM ** -0.5 * LOG2_E)
    rx_ref[...] = proj(3)
    rg_ref[...] = proj(4)
    gr_ref[...] = proj(5)
    ga_ref[...] = proj(6)


def _inproj(xs, sh, sc, g, w, cos_t, sin_t, n_ctx_tiles):
    b, lt, d = xs.shape
    nt = lt // ROW_TILE
    mod_map = lambda bi, i: (jnp.where(i < n_ctx_tiles, b, bi), 0, 0)
    tok = pl.BlockSpec((None, ROW_TILE, d), lambda bi, i: (bi, i, 0))
    out_bf = jax.ShapeDtypeStruct((b, lt, d), BF16)
    out_f = jax.ShapeDtypeStruct((b, lt, d), F32)
    return pl.pallas_call(
        functools.partial(_inproj_kernel, d=d),
        grid=(b, nt),
        in_specs=[tok,
                  pl.BlockSpec((None, 1, d), mod_map),
                  pl.BlockSpec((None, 1, d), mod_map),
                  pl.BlockSpec((1, d), lambda bi, i: (0, 0)),
                  pl.BlockSpec(w.shape, lambda bi, i: (0, 0)),
                  pl.BlockSpec((ROW_TILE, LANES), lambda bi, i: (i, 0)),
                  pl.BlockSpec((ROW_TILE, LANES), lambda bi, i: (i, 0))],
        out_specs=[tok, pl.BlockSpec((None, d, ROW_TILE), lambda bi, i: (bi, 0, i))] + [tok] * 5,
        out_shape=[out_bf, jax.ShapeDtypeStruct((b, d, lt), BF16), out_bf, out_f, out_f, out_f, out_f],
        compiler_params=_cparams(("parallel", "arbitrary"), 56),
        name="inproj",
    )(xs, sh, sc, g, w, cos_t, sin_t)


def _scan_chunk(i, n_ctx, n_chunks, reverse):
    if not reverse:
        return i
    return jnp.where(i < n_ctx, n_ctx - 1 - i, n_chunks - 1 - (i - n_ctx))


def _scan_kernel(*refs, reverse, tt, n_ctx, n_chunks):
    if reverse:
        (rx_ref, prev_ref, next_ref, cw_ref, cb_ref, wg_ref, bg_ref, c_ref, hf_ref,
         out_ref, ext_scr, a_scr, b_scr, h_scr) = refs
    else:
        (rx_ref, prev_ref, next_ref, cw_ref, cb_ref, wg_ref, bg_ref, c_ref,
         out_ref, ext_scr, a_scr, b_scr, h_scr) = refs
        hf_ref = None
    i = pl.program_id(0)
    c = _scan_chunk(i, n_ctx, n_chunks, reverse)
    seq_start = jnp.logical_or(c == 0, c == n_ctx)
    seq_end = jnp.logical_or(c == n_ctx - 1, c == n_chunks - 1)

    ext_scr[0:CONV_LEFT] = jnp.where(seq_start, 0.0, prev_ref[...])
    ext_scr[CONV_LEFT:CONV_LEFT + tt] = rx_ref[...]
    ext_scr[CONV_LEFT + tt:CONV_LEFT + tt + 1] = jnp.where(seq_end, 0.0, next_ref[...])

    rows = tt * SUBLANES
    for n in range(RNN_BLOCKS):
        sl = slice(n * RNN_BW, (n + 1) * RNN_BW)
        u = cb_ref[:, sl] + cw_ref[0:1, sl] * ext_scr[0:tt, :, sl]
        for j in range(1, CONV_W):
            u = u + cw_ref[j:j + 1, sl] * ext_scr[j:j + tt, :, sl]
        u2 = u.reshape(rows, RNN_BW)
        z = jnp.dot(u2.astype(BF16), wg_ref[n], preferred_element_type=F32) + bg_ref[n]
        r = _sigmoid(z[:, :RNN_BW])
        ig = _sigmoid(z[:, RNN_BW:])
        a = jnp.exp(-c_ref[:, sl] * r)
        bb = jnp.sqrt(1.0 - a * a) * ig * u2
        a_scr[:, :, sl] = a.reshape(tt, SUBLANES, RNN_BW)
        b_scr[:, :, sl] = bb.reshape(tt, SUBLANES, RNN_BW)

    @pl.when(i == 0)
    def _():
        h_scr[...] = jnp.zeros_like(h_scr)

    def step(s, h):
        t = tt - 1 - s if reverse else s
        h = a_scr[t] * h + b_scr[t]
        if reverse:
            out_ref[t] = hf_ref[t] + h
        else:
            out_ref[t] = h
        return h

    h_scr[...] = lax.fori_loop(0, tt, step, h_scr[...], unroll=8)


def _scan(rx_t, cw, cb, wg, bg, cdec, hf, n_ctx, reverse):
    lt, b, d = rx_t.shape
    tt = SCAN_CHUNK
    n_chunks = lt // tt
    chunk = lambda i: _scan_chunk(i, n_ctx, n_chunks, reverse)
    blk = pl.BlockSpec((tt, b, d), lambda i: (chunk(i), 0, 0))
    const2 = lambda i: (0, 0)
    const3 = lambda i: (0, 0, 0)
    in_specs = [blk,
                pl.BlockSpec((CONV_LEFT, b, d), lambda i: (jnp.maximum(chunk(i) * (tt // CONV_LEFT) - 1, 0), 0, 0)),
                pl.BlockSpec((1, b, d), lambda i: (jnp.minimum((chunk(i) + 1) * tt, lt - 1), 0, 0)),
                pl.BlockSpec(cw.shape, const2),
                pl.BlockSpec(cb.shape, const2),
                pl.BlockSpec(wg.shape, const3),
                pl.BlockSpec(bg.shape, const3),
                pl.BlockSpec(cdec.shape, const2)]
    args = [rx_t, rx_t, rx_t, cw, cb, wg, bg, cdec]
    if reverse:
        in_specs.append(blk)
        args.append(hf)
    return pl.pallas_call(
        functools.partial(_scan_kernel, reverse=reverse, tt=tt, n_ctx=n_ctx, n_chunks=n_chunks),
        grid=(n_chunks,),
        in_specs=in_specs,
        out_specs=blk,
        out_shape=jax.ShapeDtypeStruct((lt, b, d), F32),
        scratch_shapes=[pltpu.VMEM((tt + CONV_W - 1, b, d), F32),
                        pltpu.VMEM((tt, b, d), F32),
                        pltpu.VMEM((tt, b, d), F32),
                        pltpu.VMEM((b, d), F32)],
        compiler_params=_cparams(("arbitrary",), 56),
        name="scan_bwd" if reverse else "scan_fwd",
    )(*args)


def _attn_tile(lam, q_ref, k_ref, vt_ref, g_ref, o_ref, s_scr, *, tq, chunks, out_scale):
    q = q_ref[...]
    lane = lax.broadcasted_iota(jnp.int32, q.shape, 1)
    first = (lane // (HEAD_DIM // 2)) % 2 == 0
    zero = jnp.zeros_like(q)
    qs = jnp.concatenate([jnp.where(first, q, zero), jnp.where(first, zero, q)], axis=0)

    m8 = jnp.full((SUBLANES, 2 * tq), NEG_BIG, F32)
    for start, size in chunks:
        st = lax.dot_general(k_ref[start:start + size, :], qs, (((1,), (1,)), ((), ())),
                             preferred_element_type=F32)
        s_scr[start:start + size, :] = st
        m8 = jnp.maximum(m8, jnp.max(st.reshape(size // SUBLANES, SUBLANES, 2 * tq), axis=0))
    m = jnp.max(m8, axis=0, keepdims=True)

    l8 = jnp.zeros((SUBLANES, 2 * tq), F32)
    acc = jnp.zeros((V_DIM, 2 * tq), F32)
    for start, size in chunks:
        p = jnp.exp2(s_scr[start:start + size, :] - m)
        l8 = l8 + jnp.sum(p.reshape(size // SUBLANES, SUBLANES, 2 * tq), axis=0)
        acc = acc + jnp.dot(vt_ref[:, start:start + size], p.astype(BF16), preferred_element_type=F32)
    l = jnp.sum(l8, axis=0, keepdims=True)
    on = acc * (1.0 / l)
    dt = on[:, :tq] - lam * on[:, tq:]
    yt = dt * lax.rsqrt(jnp.mean(dt * dt, axis=0, keepdims=True) + EPS)
    o_ref[...] = (yt.T * g_ref[...] * out_scale).astype(BF16)


def _attn_kernel(lam_ref, q_ref, k_ref, vt_ref, g_ref, o_ref, s_scr, *, tq, ctx_q_tiles, ctx_chunks, all_chunks,
                 out_scale):
    qi = pl.program_id(2)
    tile = functools.partial(_attn_tile, lam_ref[0], q_ref, k_ref, vt_ref, g_ref, o_ref, s_scr, tq=tq,
                             out_scale=out_scale)
    if ctx_q_tiles:
        @pl.when(qi < ctx_q_tiles)
        def _():
            tile(chunks=ctx_chunks)

        @pl.when(qi >= ctx_q_tiles)
        def _():
            tile(chunks=all_chunks)
    else:
        tile(chunks=all_chunks)


def _key_chunks(n_ctx_rows, lt):
    ctx = [(s, min(ATTN_TK, n_ctx_rows - s)) for s in range(0, n_ctx_rows, ATTN_TK)]
    lat = [(s, min(ATTN_TK, lt - s)) for s in range(n_ctx_rows, lt, ATTN_TK)]
    return tuple(ctx), tuple(ctx + lat)


def _attention(lam, q, k, vt, g_sub, n_ctx_rows, with_ctx_queries, out_scale):
    b, lt, d = q.shape
    tq = ATTN_TQ
    ctx_q_tiles = n_ctx_rows // tq
    q_off = 0 if with_ctx_queries else ctx_q_tiles
    lq = lt - q_off * tq
    ctx_chunks, all_chunks = _key_chunks(n_ctx_rows, lt)
    return pl.pallas_call(
        functools.partial(_attn_kernel, tq=tq, ctx_q_tiles=ctx_q_tiles if with_ctx_queries else 0,
                          ctx_chunks=ctx_chunks, all_chunks=all_chunks, out_scale=out_scale),
        grid=(b, N_HEADS, lq // tq),
        in_specs=[pl.BlockSpec(memory_space=pltpu.SMEM),
                  pl.BlockSpec((None, tq, V_DIM), lambda bi, h, qi: (bi, qi + q_off, h)),
                  pl.BlockSpec((None, lt, V_DIM), lambda bi, h, qi: (bi, 0, h)),
                  pl.BlockSpec((None, V_DIM, lt), lambda bi, h, qi: (bi, h, 0)),
                  pl.BlockSpec((1, V_DIM), lambda bi, h, qi: (0, 0))],
        out_specs=pl.BlockSpec((None, tq, V_DIM), lambda bi, h, qi: (bi, qi, h)),
        out_shape=jax.ShapeDtypeStruct((b, lq, d), BF16),
        scratch_shapes=[pltpu.VMEM((lt, 2 * tq), F32)],
        compiler_params=_cparams(("parallel", "parallel", "arbitrary"), 48),
        name="diff_attn",
    )(lam, q, k, vt, g_sub)


def _postmix_kernel(x_ref, at_ref, y_ref, rg_ref, gr_ref, ga_ref, gt1_ref, sh2_ref, sc2_ref, g2_ref,
                    wr_ref, wa_ref, wo_ref, wrt_ref, brt_ref, x1_ref, h2_ref, route_ref):
    yg = (y_ref[...].astype(F32) * _gelu_tanh(rg_ref[...])).astype(BF16)
    o_r = jnp.dot(yg, wr_ref[...], preferred_element_type=F32)
    o_a = jnp.dot(at_ref[...], wa_ref[...], preferred_element_type=F32)
    mix = (_sigmoid(gr_ref[...]) * o_r + _sigmoid(ga_ref[...]) * o_a).astype(BF16)
    x1 = x_ref[...] + gt1_ref[...] * jnp.dot(mix, wo_ref[...], preferred_element_type=F32)
    x1_ref[...] = x1
    h2 = _rms(x1, g2_ref[...]) * (1.0 + sc2_ref[...]) + sh2_ref[...]
    h2_ref[...] = h2

    logits = jnp.dot(h2, wrt_ref[...], preferred_element_type=F32,
                     precision=lax.Precision.HIGHEST) + brt_ref[...]
    lane = lax.broadcasted_iota(jnp.int32, logits.shape, 1)
    lane_f = lane.astype(F32)
    valid = lane < N_EXPERTS
    lg = jnp.where(valid, logits, NEG_BIG)
    e = jnp.exp(lg - jnp.max(lg, axis=-1, keepdims=True))
    e = jnp.where(valid, e, 0.0)
    p = e / jnp.sum(e, axis=-1, keepdims=True)
    p = jnp.where(valid, p, -1.0)
    grp = lane // EXPERTS_PER_GROUP
    best = jnp.max(jnp.where(grp == 0, p, -1.0), axis=-1, keepdims=True)
    best_g = jnp.zeros_like(best, dtype=jnp.int32)
    for gi in range(1, N_GROUPS):
        gm = jnp.max(jnp.where(grp == gi, p, -1.0), axis=-1, keepdims=True)
        better = gm > best
        best_g = jnp.where(better, gi, best_g)
        best = jnp.where(better, gm, best)
    pg = jnp.where(grp == best_g, p, -1.0)
    big = float(LANES)
    v1 = jnp.max(pg, axis=-1, keepdims=True)
    i1 = jnp.min(jnp.where(pg == v1, lane_f, big), axis=-1, keepdims=True)
    pg2 = jnp.where(lane_f == i1, -1.0, pg)
    v2 = jnp.max(pg2, axis=-1, keepdims=True)
    i2 = jnp.min(jnp.where(pg2 == v2, lane_f, big), axis=-1, keepdims=True)
    den = v1 + v2
    route = jnp.where(lane == 0, i1, jnp.where(lane == 1, i2, jnp.where(lane == 2, v1 / den, v2 / den)))
    route_ref[...] = route


def _postmix(xs, attn, y, rg, gr, ga, gt1, sh2, sc2, g2, wr, wa, wo, wrt, brt, n_ctx_tiles, with_ctx):
    b, lt, d = xs.shape
    off = 0 if with_ctx else n_ctx_tiles
    lq = lt - off * ROW_TILE
    full = pl.BlockSpec((None, ROW_TILE, d), lambda bi, i: (bi, i + off, 0))
    part = pl.BlockSpec((None, ROW_TILE, d), lambda bi, i: (bi, i, 0))
    mod_map = lambda bi, i: (jnp.where(i + off < n_ctx_tiles, b, bi), 0, 0)
    mod = pl.BlockSpec((None, 1, d), mod_map)
    const = lambda bi, i: (0, 0)
    return pl.pallas_call(
        _postmix_kernel,
        grid=(b, lq // ROW_TILE),
        in_specs=[full, part, full, full, full, full, mod, mod, mod,
                  pl.BlockSpec((1, d), const),
                  pl.BlockSpec((d, d), const), pl.BlockSpec((d, d), const), pl.BlockSpec((d, d), const),
                  pl.BlockSpec((d, LANES), const), pl.BlockSpec((1, LANES), const)],
        out_specs=[part, part, pl.BlockSpec((None, ROW_TILE, LANES), lambda bi, i: (bi, i, 0))],
        out_shape=[jax.ShapeDtypeStruct((b, lq, d), F32),
                   jax.ShapeDtypeStruct((b, lq, d), F32),
                   jax.ShapeDtypeStruct((b, lq, LANES), F32)],
        compiler_params=_cparams(("parallel", "arbitrary"), 48),
        name="postmix",
    )(xs, attn, y, rg, gr, ga, gt1, sh2, sc2, g2, wr, wa, wo, wrt, brt)


def _dispatch_kernel(pad_ref, dest_ref, h_ref, xs_hbm, zrows, sem, *, n_pad):
    i = pl.program_id(0)
    rows = h_ref.shape[0]

    @pl.when(i == 0)
    def _():
        zrows[...] = jnp.zeros_like(zrows)
        for c in range(n_pad // MOE_ROWS):
            def fill(r, carry):
                pltpu.make_async_copy(zrows.at[pl.ds(r, 1)],
                                      xs_hbm.at[pl.ds(pad_ref[c * MOE_ROWS + r], 1)], sem).start()
                return carry

            lax.fori_loop(0, MOE_ROWS, fill, 0, unroll=8)
            pltpu.make_async_copy(zrows, xs_hbm.at[pl.ds(0, MOE_ROWS)], sem).wait()

    def send(r, carry):
        for k in range(TOP_K):
            pltpu.make_async_copy(h_ref.at[pl.ds(r, 1)],
                                  xs_hbm.at[pl.ds(dest_ref[0, TOP_K * r + k], 1)], sem).start()
        return carry

    lax.fori_loop(0, rows, send, 0, unroll=8)
    for k in range(TOP_K):
        pltpu.make_async_copy(h_ref, xs_hbm.at[pl.ds(0, rows)], sem).wait()


def _dispatch(pad_slots, dest, h2, n_slots):
    t, d = h2.shape
    rows = DISPATCH_ROWS
    return pl.pallas_call(
        functools.partial(_dispatch_kernel, n_pad=pad_slots.shape[0]),
        grid_spec=pltpu.PrefetchScalarGridSpec(
            num_scalar_prefetch=1,
            grid=(t // rows,),
            in_specs=[pl.BlockSpec((None, 1, TOP_K * rows), lambda i, pad: (i, 0, 0), memory_space=pltpu.SMEM),
                      pl.BlockSpec((rows, d), lambda i, pad: (i, 0))],
            out_specs=pl.BlockSpec(memory_space=pl.ANY),
            scratch_shapes=[pltpu.VMEM((MOE_ROWS, d), F32), pltpu.SemaphoreType.DMA(())]),
        out_shape=jax.ShapeDtypeStruct((n_slots, d), F32),
        compiler_params=_cparams(("arbitrary",), 32),
        name="dispatch",
    )(pad_slots, dest.reshape(t // rows, 1, TOP_K * rows), h2)


def _expert_kernel(be_ref, nv_ref, x_ref, w1_ref, w3_ref, w2_ref, y_ref):
    i = pl.program_id(0)

    @pl.when(i < nv_ref[0])
    def _():
        xb = x_ref[...].astype(BF16)
        a1 = jnp.dot(xb, w1_ref[...], preferred_element_type=F32)
        a3 = jnp.dot(xb, w3_ref[...], preferred_element_type=F32)
        mid = (a1 * _sigmoid(a1) * a3).astype(BF16)
        y_ref[...] = jnp.dot(mid, w2_ref[...], preferred_element_type=F32)

    @pl.when(i >= nv_ref[0])
    def _():
        y_ref[...] = jnp.zeros_like(y_ref)


def _experts(block_e, n_valid, xs, w1, w3, w2):
    n_slots, d = xs.shape
    de = w1.shape[-1]
    wspec = lambda shp: pl.BlockSpec((None,) + shp, lambda i, be, nv: (be[i], 0, 0))
    return pl.pallas_call(
        _expert_kernel,
        grid_spec=pltpu.PrefetchScalarGridSpec(
            num_scalar_prefetch=2,
            grid=(n_slots // MOE_ROWS,),
            in_specs=[pl.BlockSpec((MOE_ROWS, d), lambda i, be, nv: (jnp.minimum(i, nv[0] - 1), 0)),
                      wspec((d, de)), wspec((d, de)), wspec((de, d))],
            out_specs=pl.BlockSpec((MOE_ROWS, d), lambda i, be, nv: (i, 0))),
        out_shape=jax.ShapeDtypeStruct((n_slots, d), F32),
        compiler_params=_cparams(("arbitrary",), 48),
        name="experts",
    )(block_e, n_valid, xs, w1, w3, w2)


def _route_tables(route, n_tok):
    expert = route[:, :TOP_K].astype(jnp.int32).reshape(-1)
    a = expert.shape[0]
    onehot = (expert[:, None] == jnp.arange(N_EXPERTS, dtype=jnp.int32)[None, :]).astype(jnp.int32)
    csum = jnp.cumsum(onehot, axis=0)
    rank = jnp.sum((csum - onehot) * onehot, axis=-1)
    sizes = csum[-1]
    padded = (sizes + MOE_ROWS - 1) // MOE_ROWS * MOE_ROWS
    pend = jnp.cumsum(padded)
    pstarts = pend - padded
    dest = (pstarts[expert] + rank).astype(jnp.int32)
    n_blocks = -(-a // MOE_ROWS) + N_EXPERTS
    n_slots = n_blocks * MOE_ROWS
    n_valid = (pend[-1] // MOE_ROWS).astype(jnp.int32)
    blk = jnp.minimum(jnp.arange(n_blocks, dtype=jnp.int32), n_valid - 1)
    block_e = jnp.minimum(jnp.searchsorted(pend, blk * MOE_ROWS, side='right'), N_EXPERTS - 1).astype(jnp.int32)
    gap_start = jnp.concatenate([pstarts + sizes, pend[-1:]]).astype(jnp.int32)
    gap_len = jnp.concatenate([padded - sizes, n_slots - pend[-1:]]).astype(jnp.int32)
    gap_end = jnp.cumsum(gap_len)
    j = jnp.arange(n_slots - a, dtype=jnp.int32)
    seg = jnp.searchsorted(gap_end, j, side='right')
    pad_slots = (gap_start[seg] + j - (gap_end - gap_len)[seg]).astype(jnp.int32)
    return dest, pad_slots, block_e, n_valid.reshape(1), n_slots


def _combine_kernel(dcur_ref, dnxt_ref, x_ref, route_ref, gt2_ref, g_ref, y_hbm, o_ref, ybuf, sems, *, final):
    g = pl.program_id(0)
    n = pl.num_programs(0)
    rows = x_ref.shape[0]
    slot = g % 2

    def fetch(tbl_ref, s):
        def body(r, carry):
            for k in range(TOP_K):
                pltpu.make_async_copy(y_hbm.at[pl.ds(tbl_ref[0, TOP_K * r + k], 1)],
                                      ybuf.at[s, k, pl.ds(r, 1)], sems.at[s]).start()
            return carry

        lax.fori_loop(0, rows, body, 0, unroll=8)

    @pl.when(g == 0)
    def _():
        fetch(dcur_ref, 0)

    @pl.when(g + 1 < n)
    def _():
        fetch(dnxt_ref, 1 - slot)

    for k in range(TOP_K):
        pltpu.make_async_copy(y_hbm.at[pl.ds(0, rows)], ybuf.at[slot, k], sems.at[slot]).wait()

    rt = route_ref[...]
    y = rt[:, 2:3] * ybuf[slot, 0] + rt[:, 3:4] * ybuf[slot, 1]
    x2 = x_ref[...] + gt2_ref[...] * y
    o_ref[...] = _rms(x2, g_ref[...]) if final else x2


def _combine(x1, y_slots, dest, route, gt2, g_final, n_ctx_tiles, has_ctx, final):
    b, lq, d = x1.shape
    nt = lq // ROW_TILE
    n_tiles = b * nt
    ctx_tiles = n_ctx_tiles if has_ctx else 0
    tok = lambda w: pl.BlockSpec((None, ROW_TILE, w), lambda g: (g // nt, g % nt, 0))
    tbl = lambda f: pl.BlockSpec((None, 1, TOP_K * ROW_TILE), lambda g: (f(g), 0, 0), memory_space=pltpu.SMEM)
    dest3 = dest.reshape(n_tiles, 1, TOP_K * ROW_TILE)
    return pl.pallas_call(
        functools.partial(_combine_kernel, final=final),
        grid=(n_tiles,),
        in_specs=[tbl(lambda g: g), tbl(lambda g: jnp.minimum(g + 1, n_tiles - 1)),
                  tok(d), tok(LANES),
                  pl.BlockSpec((None, 1, d), lambda g: (jnp.where(g % nt < ctx_tiles, b, g // nt), 0, 0)),
                  pl.BlockSpec((1, d), lambda g: (0, 0)),
                  pl.BlockSpec(memory_space=pl.ANY)],
        out_specs=tok(d),
        out_shape=jax.ShapeDtypeStruct((b, lq, d), F32),
        scratch_shapes=[pltpu.VMEM((2, TOP_K, ROW_TILE, d), F32), pltpu.SemaphoreType.DMA((2,))],
        compiler_params=_cparams(("arbitrary",), 32),
        name="combine",
    )(dest3, dest3, x1, route, gt2, g_final, y_slots)


def _rope_tables(n_ctx, seq):
    n_pairs = HEAD_DIM // 4
    inv = ROPE_BASE ** (-jnp.arange(n_pairs, dtype=F32) / n_pairs)
    rows = seq // GRID_W
    r = jnp.repeat(jnp.arange(rows, dtype=F32), GRID_W)
    col = jnp.tile(jnp.arange(GRID_W, dtype=F32), rows)
    ang = jnp.concatenate([r[:, None] * inv, col[:, None] * inv], axis=-1)
    cos = jnp.concatenate([jnp.ones((n_ctx, HEAD_DIM // 2), F32), jnp.cos(ang)], axis=0)
    sin = jnp.concatenate([jnp.zeros((n_ctx, HEAD_DIM // 2), F32), jnp.sin(ang)], axis=0)
    return jnp.tile(cos, (1, 4)), jnp.concatenate([-sin, -sin, sin, sin], axis=-1)


def _qk_perm():
    lane = jnp.arange(LANES)
    g, i = lane // 32, lane % 32
    within = (g % 2) * HEAD_DIM + 2 * i + g // 2
    return (jnp.arange(N_HEADS)[:, None] * LANES + within[None, :]).reshape(-1)


def kernel(x, c, ctx, c_ctx, w_mod, b_mod, g_norm1, g_norm2, w_in, conv_w, conv_b, lru_wa, lru_ba, lru_wx,
           lru_bx, lru_lambda, diff_lambda, g_subln, w_rnn_proj, w_attn_proj, w_o, w_router, b_router,
           w_e1, w_e3, w_e2, g_final):
    b, seq, d = x.shape
    n_ctx = ctx.shape[1]
    depth = w_mod.shape[0]
    n_ctx_tiles = n_ctx // ROW_TILE
    assert b == SUBLANES and d == N_HEADS * V_DIM and n_ctx % ROW_TILE == 0 and seq % ROW_TILE == 0

    cond = jnp.zeros((MOD_ROWS, d), F32).at[:b].set(c).at[b].set(c_ctx)
    mods = _adaln(cond, w_mod, b_mod).reshape(depth, MOD_ROWS, 6, 1, d)
    cos_t, sin_t = _rope_tables(n_ctx, seq)
    perm = _qk_perm()
    wrt = jnp.zeros((d, LANES), F32).at[:, :N_EXPERTS].set(w_router)
    brt = jnp.zeros((1, LANES), F32).at[0, :N_EXPERTS].set(b_router)

    xs = jnp.concatenate([ctx, x], axis=1)
    out = None
    for l in range(depth):
        last = l == depth - 1
        lam_init = 0.8 - 0.6 * math.exp(-0.3 * l)
        sh1, sc1, gt1, sh2, sc2, gt2 = [mods[l, :, j] for j in range(6)]
        wl = w_in[l]
        w_perm = jnp.concatenate(
            [wl[:, d:2 * d][:, perm], wl[:, 2 * d:3 * d], wl[:, 3 * d:4 * d][:, perm],
             wl[:, 0:d], wl[:, 4 * d:]], axis=1).astype(BF16)
        k, v, q, rx, rg, gr, ga = _inproj(xs, sh1, sc1, g_norm1[l][None], w_perm, cos_t, sin_t, n_ctx_tiles)

        lq1, lk1, lq2, lk2 = diff_lambda[l].astype(F32)
        lam = (jnp.exp(jnp.sum(lq1 * lk1)) - jnp.exp(jnp.sum(lq2 * lk2)) + lam_init).reshape(1)
        attn = _attention(lam, q, k, v, g_subln[l][None], n_ctx, not last, 1.0 - lam_init)

        rx_t = jnp.transpose(rx, (1, 0, 2))
        cdec = LRU_C * jax.nn.softplus(-lru_lambda[l].astype(F32))
        n_ctx_chunks = n_ctx // SCAN_CHUNK
        hf = None
        for direction in range(2):
            wg = jnp.concatenate([lru_wa[l, direction], lru_wx[l, direction]], axis=-1).astype(BF16)
            bg = jnp.concatenate([lru_ba[l, direction].reshape(RNN_BLOCKS, 1, RNN_BW),
                                  lru_bx[l, direction].reshape(RNN_BLOCKS, 1, RNN_BW)], axis=-1)
            hf = _scan(rx_t, conv_w[l], conv_b[l][None], wg, bg, cdec[direction][None], hf,
                       n_ctx_chunks, reverse=direction == 1)
        y = jnp.transpose(hf, (1, 0, 2)).astype(BF16)

        x1, h2, route = _postmix(xs, attn, y, rg, gr, ga, gt1, sh2, sc2, g_norm2[l][None],
                                 w_rnn_proj[l].astype(BF16), w_attn_proj[l].astype(BF16), w_o[l].astype(BF16),
                                 wrt, brt, n_ctx_tiles, not last)
        lq = x1.shape[1]
        n_tok = b * lq
        dest, pad_slots, block_e, n_valid, n_slots = _route_tables(route.reshape(n_tok, LANES), n_tok)
        xs_sorted = _dispatch(pad_slots, dest, h2.reshape(n_tok, d), n_slots)
        y_slots = _experts(block_e, n_valid, xs_sorted, w_e1[l].astype(BF16), w_e3[l].astype(BF16),
                           w_e2[l].astype(BF16))
        res = _combine(x1, y_slots, dest, route, gt2, g_final[None], n_ctx_tiles, not last, last)
        if last:
            out = res
        else:
            xs = res
    return out
```

```python
import functools
import math

import jax
import jax.numpy as jnp
from jax import lax
from jax.experimental import pallas as pl
from jax.experimental.pallas import tpu as pltpu

F32 = jnp.float32
BF16 = jnp.bfloat16

N_HEADS = 8
HEAD_DIM = 64
V_DIM = 2 * HEAD_DIM
GRID_W = 64
EPS = 1e-6
RNN_BLOCKS = 8
RNN_BW = 128
CONV_W = 4
CONV_LEFT = 2
LRU_C = 8.0
ROPE_BASE = 10000.0
N_EXPERTS = 16
N_GROUPS = 4
EXPERTS_PER_GROUP = N_EXPERTS // N_GROUPS
TOP_K = 2

LANES = 128
SUBLANES = 8
ROW_TILE = 256
SCAN_CHUNK = 128
ATTN_TQ = 256
ATTN_TK = 512
MOE_ROWS = 512
DISPATCH_ROWS = 512
MOD_ROWS = 16
NEG_BIG = -1e30
LOG2_E = math.log2(math.e)


def _cparams(sem, vmem_mib):
    return pltpu.CompilerParams(dimension_semantics=sem, vmem_limit_bytes=vmem_mib * 1024 * 1024)


def _sigmoid(x):
    return 1.0 / (1.0 + jnp.exp(-x))


def _gelu_tanh(x):
    return 0.5 * x * (1.0 + jnp.tanh(math.sqrt(2.0 / math.pi) * (x + 0.044715 * (x * x * x))))


def _rms(x, g):
    return x * lax.rsqrt(jnp.mean(x * x, axis=-1, keepdims=True) + EPS) * g


def _adaln_kernel(c_ref, w_ref, b_ref, o_ref):
    c = c_ref[...]
    a = c * _sigmoid(c)
    o_ref[...] = jnp.dot(a, w_ref[...], preferred_element_type=F32,
                         precision=lax.Precision.HIGHEST) + b_ref[...]


def _adaln(cond, w_mod, b_mod):
    depth, d, d6 = w_mod.shape
    n = d6 // d
    return pl.pallas_call(
        _adaln_kernel,
        grid=(depth, n),
        in_specs=[pl.BlockSpec((MOD_ROWS, d), lambda l, j: (0, 0)),
                  pl.BlockSpec((None, d, d), lambda l, j: (l, 0, j)),
                  pl.BlockSpec((None, 1, d), lambda l, j: (l, 0, j))],
        out_specs=pl.BlockSpec((None, MOD_ROWS, d), lambda l, j: (l, 0, j)),
        out_shape=jax.ShapeDtypeStruct((depth, MOD_ROWS, d6), F32),
        compiler_params=_cparams(("arbitrary", "arbitrary"), 32),
        name="adaln",
    )(cond, w_mod, b_mod.reshape(depth, 1, d6))


def _inproj_kernel(x_ref, sh_ref, sc_ref, g_ref, w_ref, cos_ref, sin_ref,
                   k_ref, v_ref, q_ref, rx_ref, rg_ref, gr_ref, ga_ref, *, d):
    x = x_ref[...]
    h = (_rms(x, g_ref[...]) * (1.0 + sc_ref[...]) + sh_ref[...]).astype(BF16)
    cosv = cos_ref[...]
    sinv = sin_ref[...]

    def proj(p):
        return jnp.dot(h, w_ref[:, p * d:(p + 1) * d], preferred_element_type=F32)

    def rope_store(res, ref, scale):
        for hh in range(N_HEADS):
            xs = res[:, hh * LANES:(hh + 1) * LANES]
            out = xs * cosv + pltpu.roll(xs, LANES // 2, 1) * sinv
            ref[:, hh * LANES:(hh + 1) * LANES] = (out * scale).astype(BF16)

    rope_store(proj(0), k_ref, 1.0)
    v_ref[...] = proj(1).T.astype(BF16)
    rope_store(proj(2), q_ref, HEAD_DIM ** -0.5 * LOG2_E)
    rx_ref[...] = proj(3)
    rg_ref[...] = proj(4)
    gr_ref[...] = proj(5)
    ga_ref[...] = proj(6)


def _inproj(xs, sh, sc, g, w, cos_t, sin_t, n_ctx_tiles):
    b, lt, d = xs.shape
    nt = lt // ROW_TILE
    mod_map = lambda bi, i: (jnp.where(i < n_ctx_tiles, b, bi), 0, 0)
    tok = pl.BlockSpec((None, ROW_TILE, d), lambda bi, i: (bi, i, 0))
    out_bf = jax.ShapeDtypeStruct((b, lt, d), BF16)
    out_f = jax.ShapeDtypeStruct((b, lt, d), F32)
    return pl.pallas_call(
        functools.partial(_inproj_kernel, d=d),
        grid=(b, nt),
        in_specs=[tok,
                  pl.BlockSpec((None, 1, d), mod_map),
                  pl.BlockSpec((None, 1, d), mod_map),
                  pl.BlockSpec((1, d), lambda bi, i: (0, 0)),
                  pl.BlockSpec(w.shape, lambda bi, i: (0, 0)),
                  pl.BlockSpec((ROW_TILE, LANES), lambda bi, i: (i, 0)),
                  pl.BlockSpec((ROW_TILE, LANES), lambda bi, i: (i, 0))],
        out_specs=[tok, pl.BlockSpec((None, d, ROW_TILE), lambda bi, i: (bi, 0, i))] + [tok] * 5,
        out_shape=[out_bf, jax.ShapeDtypeStruct((b, d, lt), BF16), out_bf, out_f, out_f, out_f, out_f],
        compiler_params=_cparams(("parallel", "arbitrary"), 56),
        name="inproj",
    )(xs, sh, sc, g, w, cos_t, sin_t)


def _scan_chunk(i, n_ctx, n_chunks, reverse):
    if not reverse:
        return i
    return jnp.where(i < n_ctx, n_ctx - 1 - i, n_chunks - 1 - (i - n_ctx))


def _scan_kernel(*refs, reverse, tt, n_ctx, n_chunks):
    if reverse:
        (rx_ref, prev_ref, next_ref, cw_ref, cb_ref, wg_ref, bg_ref, c_ref, hf_ref,
         out_ref, ext_scr, a_scr, b_scr, h_scr) = refs
    else:
        (rx_ref, prev_ref, next_ref, cw_ref, cb_ref, wg_ref, bg_ref, c_ref,
         out_ref, ext_scr, a_scr, b_scr, h_scr) = refs
        hf_ref = None
    i = pl.program_id(0)
    c = _scan_chunk(i, n_ctx, n_chunks, reverse)
    seq_start = jnp.logical_or(c == 0, c == n_ctx)
    seq_end = jnp.logical_or(c == n_ctx - 1, c == n_chunks - 1)

    ext_scr[0:CONV_LEFT] = jnp.where(seq_start, 0.0, prev_ref[...])
    ext_scr[CONV_LEFT:CONV_LEFT + tt] = rx_ref[...]
    ext_scr[CONV_LEFT + tt:CONV_LEFT + tt + 1] = jnp.where(seq_end, 0.0, next_ref[...])

    rows = tt * SUBLANES
    for n in range(RNN_BLOCKS):
        sl = slice(n * RNN_BW, (n + 1) * RNN_BW)
        u = cb_ref[:, sl] + cw_ref[0:1, sl] * ext_scr[0:tt, :, sl]
        for j in range(1, CONV_W):
            u = u + cw_ref[j:j + 1, sl] * ext_scr[j:j + tt, :, sl]
        u2 = u.reshape(rows, RNN_BW)
        z = jnp.dot(u2.astype(BF16), wg_ref[n], preferred_element_type=F32) + bg_ref[n]
        r = _sigmoid(z[:, :RNN_BW])
        ig = _sigmoid(z[:, RNN_BW:])
        a = jnp.exp(-c_ref[:, sl] * r)
        bb = jnp.sqrt(1.0 - a * a) * ig * u2
        a_scr[:, :, sl] = a.reshape(tt, SUBLANES, RNN_BW)
        b_scr[:, :, sl] = bb.reshape(tt, SUBLANES, RNN_BW)

    @pl.when(i == 0)
    def _():
        h_scr[...] = jnp.zeros_like(h_scr)

    def step(s, h):
        t = tt - 1 - s if reverse else s
        h = a_scr[t] * h + b_scr[t]
        if reverse:
            out_ref[t] = hf_ref[t] + h
        else:
            out_ref[t] = h
        return h

    h_scr[...] = lax.fori_loop(0, tt, step, h_scr[...], unroll=8)


def _scan(rx_t, cw, cb, wg, bg, cdec, hf, n_ctx, reverse):
    lt, b, d = rx_t.shape
    tt = SCAN_CHUNK
    n_chunks = lt // tt
    chunk = lambda i: _scan_chunk(i, n_ctx, n_chunks, reverse)
    blk = pl.BlockSpec((tt, b, d), lambda i: (chunk(i), 0, 0))
    const2 = lambda i: (0, 0)
    const3 = lambda i: (0, 0, 0)
    in_specs = [blk,
                pl.BlockSpec((CONV_LEFT, b, d), lambda i: (jnp.maximum(chunk(i) * (tt // CONV_LEFT) - 1, 0), 0, 0)),
                pl.BlockSpec((1, b, d), lambda i: (jnp.minimum((chunk(i) + 1) * tt, lt - 1), 0, 0)),
                pl.BlockSpec(cw.shape, const2),
                pl.BlockSpec(cb.shape, const2),
                pl.BlockSpec(wg.shape, const3),
                pl.BlockSpec(bg.shape, const3),
                pl.BlockSpec(cdec.shape, const2)]
    args = [rx_t, rx_t, rx_t, cw, cb, wg, bg, cdec]
    if reverse:
        in_specs.append(blk)
        args.append(hf)
    return pl.pallas_call(
        functools.partial(_scan_kernel, reverse=reverse, tt=tt, n_ctx=n_ctx, n_chunks=n_chunks),
        grid=(n_chunks,),
        in_specs=in_specs,
        out_specs=blk,
        out_shape=jax.ShapeDtypeStruct((lt, b, d), F32),
        scratch_shapes=[pltpu.VMEM((tt + CONV_W - 1, b, d), F32),
                        pltpu.VMEM((tt, b, d), F32),
                        pltpu.VMEM((tt, b, d), F32),
                        pltpu.VMEM((b, d), F32)],
        compiler_params=_cparams(("arbitrary",), 56),
        name="scan_bwd" if reverse else "scan_fwd",
    )(*args)


def _attn_passes(lam, q_ref, k_ref, vt_ref, g_ref, o_ref, *, write, read, tq, w_chunks, r_chunks, out_scale):
    n2 = 2 * tq
    if write is not None:
        s_w, m_w = write
        q = q_ref[...]
        lane = lax.broadcasted_iota(jnp.int32, q.shape, 1)
        first = (lane // (HEAD_DIM // 2)) % 2 == 0
        zero = jnp.zeros_like(q)
        qs = jnp.concatenate([jnp.where(first, q, zero), jnp.where(first, zero, q)], axis=0)
        m8 = jnp.full((SUBLANES, n2), NEG_BIG, F32)
    if read is not None:
        s_r, m_r = read
        m = m_r[...]
        l8 = jnp.zeros((SUBLANES, n2), F32)
        acc = jnp.zeros((V_DIM, n2), F32)
    for j in range(max(len(w_chunks) if write is not None else 0, len(r_chunks) if read is not None else 0)):
        if write is not None and j < len(w_chunks):
            start, size = w_chunks[j]
            st = lax.dot_general(k_ref[start:start + size, :], qs, (((1,), (1,)), ((), ())),
                                 preferred_element_type=F32)
            s_w[start:start + size, :] = st
            m8 = jnp.maximum(m8, jnp.max(st.reshape(size // SUBLANES, SUBLANES, n2), axis=0))
        if read is not None and j < len(r_chunks):
            start, size = r_chunks[j]
            p = jnp.exp2(s_r[start:start + size, :] - m)
            l8 = l8 + jnp.sum(p.reshape(size // SUBLANES, SUBLANES, n2), axis=0)
            acc = acc + jnp.dot(vt_ref[:, start:start + size], p.astype(BF16), preferred_element_type=F32)
    if write is not None:
        m_w[...] = jnp.max(m8, axis=0, keepdims=True)
    if read is not None:
        on = acc * (1.0 / jnp.sum(l8, axis=0, keepdims=True))
        dt = on[:, :tq] - lam * on[:, tq:]
        yt = dt * lax.rsqrt(jnp.mean(dt * dt, axis=0, keepdims=True) + EPS)
        o_ref[...] = (yt.T * g_ref[...] * out_scale).astype(BF16)


def _attn_pipe_kernel(lam_ref, q_ref, k_ref, vt_ref, g_ref, o_ref, s_a, s_b, m_a, m_b, *, tq, nq, chunks,
                      out_scale):
    s = pl.program_id(2)
    run = functools.partial(_attn_passes, lam_ref[0], q_ref, k_ref, vt_ref, g_ref, o_ref, tq=tq,
                            w_chunks=chunks, r_chunks=chunks, out_scale=out_scale)
    buf_a, buf_b = (s_a, m_a), (s_b, m_b)
    inner = jnp.logical_and(s > 0, s < nq)
    even = s % 2 == 0

    @pl.when(s == 0)
    def _():
        run(write=buf_a, read=None)

    @pl.when(jnp.logical_and(inner, even))
    def _():
        run(write=buf_a, read=buf_b)

    @pl.when(jnp.logical_and(inner, jnp.logical_not(even)))
    def _():
        run(write=buf_b, read=buf_a)

    @pl.when(s == nq)
    def _():
        run(write=None, read=buf_a if (nq - 1) % 2 == 0 else buf_b)


def _attn_single_kernel(lam_ref, q_ref, k_ref, vt_ref, g_ref, o_ref, s_a, m_a, *, tq, chunks, out_scale):
    run = functools.partial(_attn_passes, lam_ref[0], q_ref, k_ref, vt_ref, g_ref, o_ref, tq=tq,
                            w_chunks=chunks, r_chunks=chunks, out_scale=out_scale)
    run(write=(s_a, m_a), read=None)
    run(write=None, read=(s_a, m_a))


def _key_chunks(n_ctx_rows, lt):
    ctx = [(s, min(ATTN_TK, n_ctx_rows - s)) for s in range(0, n_ctx_rows, ATTN_TK)]
    lat = [(s, min(ATTN_TK, lt - s)) for s in range(n_ctx_rows, lt, ATTN_TK)]
    return tuple(ctx), tuple(ctx + lat)


def _attention(lam, q, k, vt, g_sub, n_ctx_rows, with_ctx_queries, out_scale):
    b, lt, d = q.shape
    tq = ATTN_TQ
    ctx_q_tiles = n_ctx_rows // tq
    nq = (lt - n_ctx_rows) // tq
    ctx_chunks, all_chunks = _key_chunks(n_ctx_rows, lt)
    n2 = 2 * tq
    lam_in = pl.BlockSpec(memory_space=pltpu.SMEM)
    g_in = pl.BlockSpec((1, V_DIM), lambda bi, h, s: (0, 0))
    lat = pl.pallas_call(
        functools.partial(_attn_pipe_kernel, tq=tq, nq=nq, chunks=all_chunks, out_scale=out_scale),
        grid=(b, N_HEADS, nq + 1),
        in_specs=[lam_in,
                  pl.BlockSpec((None, tq, V_DIM), lambda bi, h, s: (bi, jnp.minimum(s, nq - 1) + ctx_q_tiles, h)),
                  pl.BlockSpec((None, lt, V_DIM), lambda bi, h, s: (bi, 0, h)),
                  pl.BlockSpec((None, V_DIM, lt), lambda bi, h, s: (bi, h, 0)),
                  g_in],
        out_specs=pl.BlockSpec((None, tq, V_DIM), lambda bi, h, s: (bi, jnp.maximum(s - 1, 0), h)),
        out_shape=jax.ShapeDtypeStruct((b, nq * tq, d), BF16),
        scratch_shapes=[pltpu.VMEM((lt, n2), F32), pltpu.VMEM((lt, n2), F32),
                        pltpu.VMEM((1, n2), F32), pltpu.VMEM((1, n2), F32)],
        compiler_params=_cparams(("parallel", "parallel", "arbitrary"), 56),
        name="diff_attn",
    )(lam, q, k, vt, g_sub)
    if not with_ctx_queries:
        return lat, None
    ctx = pl.pallas_call(
        functools.partial(_attn_single_kernel, tq=tq, chunks=ctx_chunks, out_scale=out_scale),
        grid=(b, N_HEADS, ctx_q_tiles),
        in_specs=[lam_in,
                  pl.BlockSpec((None, tq, V_DIM), lambda bi, h, s: (bi, s, h)),
                  pl.BlockSpec((None, n_ctx_rows, V_DIM), lambda bi, h, s: (bi, 0, h)),
                  pl.BlockSpec((None, V_DIM, n_ctx_rows), lambda bi, h, s: (bi, h, 0)),
                  g_in],
        out_specs=pl.BlockSpec((None, tq, V_DIM), lambda bi, h, s: (bi, s, h)),
        out_shape=jax.ShapeDtypeStruct((b, n_ctx_rows, d), BF16),
        scratch_shapes=[pltpu.VMEM((n_ctx_rows, n2), F32), pltpu.VMEM((1, n2), F32)],
        compiler_params=_cparams(("parallel", "parallel", "arbitrary"), 32),
        name="diff_attn_ctx",
    )(lam, q, k, vt, g_sub)
    return lat, ctx


def _postmix_kernel(*refs, has_ctx, n_ctx_tiles):
    if has_ctx:
        (x_ref, at_ref, atc_ref, y_ref, rg_ref, gr_ref, ga_ref, gt1_ref, sh2_ref, sc2_ref, g2_ref,
         wr_ref, wa_ref, wo_ref, wrh_ref, wrl_ref, brt_ref, x1_ref, h2_ref, route_ref, cnt_ref, run_scr) = refs
        at = jnp.where(pl.program_id(1) < n_ctx_tiles, atc_ref[...], at_ref[...])
    else:
        (x_ref, at_ref, y_ref, rg_ref, gr_ref, ga_ref, gt1_ref, sh2_ref, sc2_ref, g2_ref,
         wr_ref, wa_ref, wo_ref, wrh_ref, wrl_ref, brt_ref, x1_ref, h2_ref, route_ref, cnt_ref, run_scr) = refs
        at = at_ref[...]
    yg = (y_ref[...].astype(F32) * _gelu_tanh(rg_ref[...])).astype(BF16)
    o_r = jnp.dot(yg, wr_ref[...], preferred_element_type=F32)
    o_a = jnp.dot(at, wa_ref[...], preferred_element_type=F32)
    mix = (_sigmoid(gr_ref[...]) * o_r + _sigmoid(ga_ref[...]) * o_a).astype(BF16)
    x1 = x_ref[...] + gt1_ref[...] * jnp.dot(mix, wo_ref[...], preferred_element_type=F32)
    x1_ref[...] = x1
    h2 = _rms(x1, g2_ref[...]) * (1.0 + sc2_ref[...]) + sh2_ref[...]
    h2_ref[...] = h2

    h_hi = h2.astype(BF16)
    h_lo = (h2 - h_hi.astype(F32)).astype(BF16)
    logits = (jnp.dot(h_hi, wrh_ref[...], preferred_element_type=F32)
              + jnp.dot(h_lo, wrh_ref[...], preferred_element_type=F32)
              + jnp.dot(h_hi, wrl_ref[...], preferred_element_type=F32)) + brt_ref[...]
    lane = lax.broadcasted_iota(jnp.int32, logits.shape, 1)
    lane_f = lane.astype(F32)
    valid = lane < N_EXPERTS
    lg = jnp.where(valid, logits, NEG_BIG)
    e = jnp.exp(lg - jnp.max(lg, axis=-1, keepdims=True))
    e = jnp.where(valid, e, 0.0)
    p = e / jnp.sum(e, axis=-1, keepdims=True)
    p = jnp.where(valid, p, -1.0)
    grp = lane // EXPERTS_PER_GROUP
    best = jnp.max(jnp.where(grp == 0, p, -1.0), axis=-1, keepdims=True)
    best_g = jnp.zeros_like(best, dtype=jnp.int32)
    for gi in range(1, N_GROUPS):
        gm = jnp.max(jnp.where(grp == gi, p, -1.0), axis=-1, keepdims=True)
        better = gm > best
        best_g = jnp.where(better, gi, best_g)
        best = jnp.where(better, gm, best)
    pg = jnp.where(grp == best_g, p, -1.0)
    big = float(LANES)
    v1 = jnp.max(pg, axis=-1, keepdims=True)
    i1 = jnp.min(jnp.where(pg == v1, lane_f, big), axis=-1, keepdims=True)
    pg2 = jnp.where(lane_f == i1, -1.0, pg)
    v2 = jnp.max(pg2, axis=-1, keepdims=True)
    i2 = jnp.min(jnp.where(pg2 == v2, lane_f, big), axis=-1, keepdims=True)
    den = v1 + v2

    @pl.when(jnp.logical_and(pl.program_id(0) == 0, pl.program_id(1) == 0))
    def _():
        run_scr[...] = jnp.zeros_like(run_scr)

    rows = logits.shape[0]
    oh1 = (lane_f == i1).astype(F32)
    oh2 = (lane_f == i2).astype(F32)
    both = oh1 + oh2
    tri = (lax.broadcasted_iota(jnp.int32, (rows, rows), 1)
           < lax.broadcasted_iota(jnp.int32, (rows, rows), 0)).astype(BF16)
    before = jnp.dot(tri, both.astype(BF16), preferred_element_type=F32) + run_scr[...]
    r1 = jnp.sum(oh1 * before, axis=-1, keepdims=True)
    r2 = jnp.sum(oh2 * before, axis=-1, keepdims=True)
    run = run_scr[...] + jnp.sum(both, axis=0, keepdims=True)
    run_scr[...] = run
    cnt_ref[...] = jnp.broadcast_to(run, cnt_ref.shape)

    route = jnp.where(lane == 0, i1, jnp.where(lane == 1, i2, jnp.where(lane == 2, v1 / den, jnp.where(
        lane == 3, v2 / den, jnp.where(lane == 4, r1, r2)))))
    route_ref[...] = route


def _postmix(xs, attn, attn_ctx, y, rg, gr, ga, gt1, sh2, sc2, g2, wr, wa, wo, wrh, wrl, brt, n_ctx_tiles):
    b, lt, d = xs.shape
    has_ctx = attn_ctx is not None
    off = 0 if has_ctx else n_ctx_tiles
    lq = lt - off * ROW_TILE
    full = pl.BlockSpec((None, ROW_TILE, d), lambda bi, i: (bi, i + off, 0))
    part = pl.BlockSpec((None, ROW_TILE, d), lambda bi, i: (bi, i, 0))
    mod = pl.BlockSpec((None, 1, d), lambda bi, i: (jnp.where(i + off < n_ctx_tiles, b, bi), 0, 0))
    const = lambda bi, i: (0, 0)
    if has_ctx:
        at_specs = [pl.BlockSpec((None, ROW_TILE, d), lambda bi, i: (bi, jnp.maximum(i - n_ctx_tiles, 0), 0)),
                    pl.BlockSpec((None, ROW_TILE, d), lambda bi, i: (bi, jnp.minimum(i, n_ctx_tiles - 1), 0))]
        at_args = [attn, attn_ctx]
    else:
        at_specs, at_args = [part], [attn]
    return pl.pallas_call(
        functools.partial(_postmix_kernel, has_ctx=has_ctx, n_ctx_tiles=n_ctx_tiles),
        grid=(b, lq // ROW_TILE),
        in_specs=[full] + at_specs + [full, full, full, full, mod, mod, mod,
                                      pl.BlockSpec((1, d), const),
                                      pl.BlockSpec((d, d), const), pl.BlockSpec((d, d), const),
                                      pl.BlockSpec((d, d), const),
                                      pl.BlockSpec((d, LANES), const), pl.BlockSpec((d, LANES), const),
                                      pl.BlockSpec((1, LANES), const)],
        out_specs=[part, part, pl.BlockSpec((None, ROW_TILE, LANES), lambda bi, i: (bi, i, 0)),
                   pl.BlockSpec((SUBLANES, LANES), const)],
        out_shape=[jax.ShapeDtypeStruct((b, lq, d), F32),
                   jax.ShapeDtypeStruct((b, lq, d), F32),
                   jax.ShapeDtypeStruct((b, lq, LANES), F32),
                   jax.ShapeDtypeStruct((SUBLANES, LANES), F32)],
        scratch_shapes=[pltpu.VMEM((1, LANES), F32)],
        compiler_params=_cparams(("arbitrary", "arbitrary"), 48),
        name="postmix",
    )(xs, *at_args, y, rg, gr, ga, gt1, sh2, sc2, g2, wr, wa, wo, wrh, wrl, brt)


def _dispatch_kernel(pad_ref, dest_ref, h_ref, xs_hbm, zrows, sem, *, n_pad):
    i = pl.program_id(0)
    rows = h_ref.shape[0]

    @pl.when(i == 0)
    def _():
        zrows[...] = jnp.zeros_like(zrows)
        for c in range(n_pad // MOE_ROWS):
            def fill(r, carry):
                pltpu.make_async_copy(zrows.at[pl.ds(r, 1)],
                                      xs_hbm.at[pl.ds(pad_ref[c * MOE_ROWS + r], 1)], sem).start()
                return carry

            lax.fori_loop(0, MOE_ROWS, fill, 0, unroll=8)
            pltpu.make_async_copy(zrows, xs_hbm.at[pl.ds(0, MOE_ROWS)], sem).wait()

    def send(r, carry):
        for k in range(TOP_K):
            pltpu.make_async_copy(h_ref.at[pl.ds(r, 1)],
                                  xs_hbm.at[pl.ds(dest_ref[0, TOP_K * r + k], 1)], sem).start()
        return carry

    lax.fori_loop(0, rows, send, 0, unroll=8)
    for k in range(TOP_K):
        pltpu.make_async_copy(h_ref, xs_hbm.at[pl.ds(0, rows)], sem).wait()


def _dispatch(pad_slots, dest, h2, n_slots):
    t, d = h2.shape
    rows = DISPATCH_ROWS
    return pl.pallas_call(
        functools.partial(_dispatch_kernel, n_pad=pad_slots.shape[0]),
        grid_spec=pltpu.PrefetchScalarGridSpec(
            num_scalar_prefetch=1,
            grid=(t // rows,),
            in_specs=[pl.BlockSpec((None, 1, TOP_K * rows), lambda i, pad: (i, 0, 0), memory_space=pltpu.SMEM),
                      pl.BlockSpec((rows, d), lambda i, pad: (i, 0))],
            out_specs=pl.BlockSpec(memory_space=pl.ANY),
            scratch_shapes=[pltpu.VMEM((MOE_ROWS, d), F32), pltpu.SemaphoreType.DMA(())]),
        out_shape=jax.ShapeDtypeStruct((n_slots, d), F32),
        compiler_params=_cparams(("arbitrary",), 32),
        name="dispatch",
    )(pad_slots, dest.reshape(t // rows, 1, TOP_K * rows), h2)


def _expert_kernel(be_ref, nv_ref, x_ref, w1_ref, w3_ref, w2_ref, y_ref):
    i = pl.program_id(0)

    @pl.when(i < nv_ref[0])
    def _():
        xb = x_ref[...].astype(BF16)
        a1 = jnp.dot(xb, w1_ref[...], preferred_element_type=F32)
        a3 = jnp.dot(xb, w3_ref[...], preferred_element_type=F32)
        mid = (a1 * _sigmoid(a1) * a3).astype(BF16)
        y_ref[...] = jnp.dot(mid, w2_ref[...], preferred_element_type=F32)

    @pl.when(i >= nv_ref[0])
    def _():
        y_ref[...] = jnp.zeros_like(y_ref)


def _experts(block_e, n_valid, xs, w1, w3, w2):
    n_slots, d = xs.shape
    de = w1.shape[-1]
    wspec = lambda shp: pl.BlockSpec((None,) + shp, lambda i, be, nv: (be[i], 0, 0))
    return pl.pallas_call(
        _expert_kernel,
        grid_spec=pltpu.PrefetchScalarGridSpec(
            num_scalar_prefetch=2,
            grid=(n_slots // MOE_ROWS,),
            in_specs=[pl.BlockSpec((MOE_ROWS, d), lambda i, be, nv: (jnp.minimum(i, nv[0] - 1), 0)),
                      wspec((d, de)), wspec((d, de)), wspec((de, d))],
            out_specs=pl.BlockSpec((MOE_ROWS, d), lambda i, be, nv: (i, 0))),
        out_shape=jax.ShapeDtypeStruct((n_slots, d), F32),
        compiler_params=_cparams(("arbitrary",), 48),
        name="experts",
    )(block_e, n_valid, xs, w1, w3, w2)


def _route_tables(route, counts):
    expert = route[:, 0:TOP_K].astype(jnp.int32).reshape(-1)
    rank = route[:, 4:4 + TOP_K].astype(jnp.int32).reshape(-1)
    a = expert.shape[0]
    sizes = counts[0, :N_EXPERTS].astype(jnp.int32)
    padded = (sizes + MOE_ROWS - 1) // MOE_ROWS * MOE_ROWS
    pend = jnp.cumsum(padded)
    pstarts = pend - padded
    dest = (pstarts[expert] + rank).astype(jnp.int32)
    n_blocks = -(-a // MOE_ROWS) + N_EXPERTS
    n_slots = n_blocks * MOE_ROWS
    n_valid = (pend[-1] // MOE_ROWS).astype(jnp.int32)
    blk = jnp.minimum(jnp.arange(n_blocks, dtype=jnp.int32), n_valid - 1)
    block_e = jnp.minimum(jnp.searchsorted(pend, blk * MOE_ROWS, side='right'), N_EXPERTS - 1).astype(jnp.int32)
    gap_start = jnp.concatenate([pstarts + sizes, pend[-1:]]).astype(jnp.int32)
    gap_len = jnp.concatenate([padded - sizes, n_slots - pend[-1:]]).astype(jnp.int32)
    gap_end = jnp.cumsum(gap_len)
    j = jnp.arange(n_slots - a, dtype=jnp.int32)
    seg = jnp.searchsorted(gap_end, j, side='right')
    pad_slots = (gap_start[seg] + j - (gap_end - gap_len)[seg]).astype(jnp.int32)
    return dest, pad_slots, block_e, n_valid.reshape(1), n_slots


def _combine_kernel(dcur_ref, dnxt_ref, x_ref, route_ref, gt2_ref, g_ref, y_hbm, o_ref, ybuf, sems, *, final):
    g = pl.program_id(0)
    n = pl.num_programs(0)
    rows = x_ref.shape[0]
    slot = g % 2

    def fetch(tbl_ref, s):
        def body(r, carry):
            for k in range(TOP_K):
                pltpu.make_async_copy(y_hbm.at[pl.ds(tbl_ref[0, TOP_K * r + k], 1)],
                                      ybuf.at[s, k, pl.ds(r, 1)], sems.at[s]).start()
            return carry

        lax.fori_loop(0, rows, body, 0, unroll=8)

    @pl.when(g == 0)
    def _():
        fetch(dcur_ref, 0)

    @pl.when(g + 1 < n)
    def _():
        fetch(dnxt_ref, 1 - slot)

    for k in range(TOP_K):
        pltpu.make_async_copy(y_hbm.at[pl.ds(0, rows)], ybuf.at[slot, k], sems.at[slot]).wait()

    rt = route_ref[...]
    y = rt[:, 2:3] * ybuf[slot, 0] + rt[:, 3:4] * ybuf[slot, 1]
    x2 = x_ref[...] + gt2_ref[...] * y
    o_ref[...] = _rms(x2, g_ref[...]) if final else x2


def _combine(x1, y_slots, dest, route, gt2, g_final, n_ctx_tiles, has_ctx, final):
    b, lq, d = x1.shape
    nt = lq // ROW_TILE
    n_tiles = b * nt
    ctx_tiles = n_ctx_tiles if has_ctx else 0
    tok = lambda w: pl.BlockSpec((None, ROW_TILE, w), lambda g: (g // nt, g % nt, 0))
    tbl = lambda f: pl.BlockSpec((None, 1, TOP_K * ROW_TILE), lambda g: (f(g), 0, 0), memory_space=pltpu.SMEM)
    dest3 = dest.reshape(n_tiles, 1, TOP_K * ROW_TILE)
    return pl.pallas_call(
        functools.partial(_combine_kernel, final=final),
        grid=(n_tiles,),
        in_specs=[tbl(lambda g: g), tbl(lambda g: jnp.minimum(g + 1, n_tiles - 1)),
                  tok(d), tok(LANES),
                  pl.BlockSpec((None, 1, d), lambda g: (jnp.where(g % nt < ctx_tiles, b, g // nt), 0, 0)),
                  pl.BlockSpec((1, d), lambda g: (0, 0)),
                  pl.BlockSpec(memory_space=pl.ANY)],
        out_specs=tok(d),
        out_shape=jax.ShapeDtypeStruct((b, lq, d), F32),
        scratch_shapes=[pltpu.VMEM((2, TOP_K, ROW_TILE, d), F32), pltpu.SemaphoreType.DMA((2,))],
        compiler_params=_cparams(("arbitrary",), 32),
        name="combine",
    )(dest3, dest3, x1, route, gt2, g_final, y_slots)


def _rope_tables(n_ctx, seq):
    n_pairs = HEAD_DIM // 4
    inv = ROPE_BASE ** (-jnp.arange(n_pairs, dtype=F32) / n_pairs)
    rows = seq // GRID_W
    r = jnp.repeat(jnp.arange(rows, dtype=F32), GRID_W)
    col = jnp.tile(jnp.arange(GRID_W, dtype=F32), rows)
    ang = jnp.concatenate([r[:, None] * inv, col[:, None] * inv], axis=-1)
    cos = jnp.concatenate([jnp.ones((n_ctx, HEAD_DIM // 2), F32), jnp.cos(ang)], axis=0)
    sin = jnp.concatenate([jnp.zeros((n_ctx, HEAD_DIM // 2), F32), jnp.sin(ang)], axis=0)
    return jnp.tile(cos, (1, 4)), jnp.concatenate([-sin, -sin, sin, sin], axis=-1)


def _qk_perm():
    lane = jnp.arange(LANES)
    g, i = lane // 32, lane % 32
    within = (g % 2) * HEAD_DIM + 2 * i + g // 2
    return (jnp.arange(N_HEADS)[:, None] * LANES + within[None, :]).reshape(-1)


def kernel(x, c, ctx, c_ctx, w_mod, b_mod, g_norm1, g_norm2, w_in, conv_w, conv_b, lru_wa, lru_ba, lru_wx,
           lru_bx, lru_lambda, diff_lambda, g_subln, w_rnn_proj, w_attn_proj, w_o, w_router, b_router,
           w_e1, w_e3, w_e2, g_final):
    b, seq, d = x.shape
    n_ctx = ctx.shape[1]
    depth = w_mod.shape[0]
    n_ctx_tiles = n_ctx // ROW_TILE
    assert b == SUBLANES and d == N_HEADS * V_DIM and n_ctx % ROW_TILE == 0 and seq % ROW_TILE == 0

    cond = jnp.zeros((MOD_ROWS, d), F32).at[:b].set(c).at[b].set(c_ctx)
    mods = _adaln(cond, w_mod, b_mod).reshape(depth, MOD_ROWS, 6, 1, d)
    cos_t, sin_t = _rope_tables(n_ctx, seq)
    perm = _qk_perm()
    wrt = jnp.zeros((d, LANES), F32).at[:, :N_EXPERTS].set(w_router)
    wrt_hi = wrt.astype(BF16)
    wrt_lo = (wrt - wrt_hi.astype(F32)).astype(BF16)
    brt = jnp.zeros((1, LANES), F32).at[0, :N_EXPERTS].set(b_router)

    xs = jnp.concatenate([ctx, x], axis=1)
    out = None
    for l in range(depth):
        last = l == depth - 1
        lam_init = 0.8 - 0.6 * math.exp(-0.3 * l)
        sh1, sc1, gt1, sh2, sc2, gt2 = [mods[l, :, j] for j in range(6)]
        wl = w_in[l]
        w_perm = jnp.concatenate(
            [wl[:, d:2 * d][:, perm], wl[:, 2 * d:3 * d], wl[:, 3 * d:4 * d][:, perm],
             wl[:, 0:d], wl[:, 4 * d:]], axis=1).astype(BF16)
        k, v, q, rx, rg, gr, ga = _inproj(xs, sh1, sc1, g_norm1[l][None], w_perm, cos_t, sin_t, n_ctx_tiles)

        lq1, lk1, lq2, lk2 = diff_lambda[l].astype(F32)
        lam = (jnp.exp(jnp.sum(lq1 * lk1)) - jnp.exp(jnp.sum(lq2 * lk2)) + lam_init).reshape(1)
        attn, attn_ctx = _attention(lam, q, k, v, g_subln[l][None], n_ctx, not last, 1.0 - lam_init)

        rx_t = jnp.transpose(rx, (1, 0, 2))
        cdec = LRU_C * jax.nn.softplus(-lru_lambda[l].astype(F32))
        n_ctx_chunks = n_ctx // SCAN_CHUNK
        hf = None
        for direction in range(2):
            wg = jnp.concatenate([lru_wa[l, direction], lru_wx[l, direction]], axis=-1).astype(BF16)
            bg = jnp.concatenate([lru_ba[l, direction].reshape(RNN_BLOCKS, 1, RNN_BW),
                                  lru_bx[l, direction].reshape(RNN_BLOCKS, 1, RNN_BW)], axis=-1)
            hf = _scan(rx_t, conv_w[l], conv_b[l][None], wg, bg, cdec[direction][None], hf,
                       n_ctx_chunks, reverse=direction == 1)
        y = jnp.transpose(hf, (1, 0, 2)).astype(BF16)

        x1, h2, route, counts = _postmix(xs, attn, attn_ctx, y, rg, gr, ga, gt1, sh2, sc2, g_norm2[l][None],
                                         w_rnn_proj[l].astype(BF16), w_attn_proj[l].astype(BF16),
                                         w_o[l].astype(BF16), wrt_hi, wrt_lo, brt, n_ctx_tiles)
        lq = x1.shape[1]
        n_tok = b * lq
        dest, pad_slots, block_e, n_valid, n_slots = _route_tables(route.reshape(n_tok, LANES), counts)
        xs_sorted = _dispatch(pad_slots, dest, h2.reshape(n_tok, d), n_slots)
        y_slots = _experts(block_e, n_valid, xs_sorted, w_e1[l].astype(BF16), w_e3[l].astype(BF16),
                           w_e2[l].astype(BF16))
        res = _combine(x1, y_slots, dest, route, gt2, g_final[None], n_ctx_tiles, not last, last)
        if last:
            out = res
        else:
            xs = res
    return out
```

```python
import functools
import math

import jax
import jax.numpy as jnp
from jax import lax
from jax.experimental import pallas as pl
from jax.experimental.pallas import tpu as pltpu

F32 = jnp.float32
BF16 = jnp.bfloat16

N_HEADS = 8
HEAD_DIM = 64
V_DIM = 2 * HEAD_DIM
GRID_W = 64
EPS = 1e-6
RNN_BLOCKS = 8
RNN_BW = 128
CONV_W = 4
CONV_LEFT = 2
LRU_C = 8.0
ROPE_BASE = 10000.0
N_EXPERTS = 16
N_GROUPS = 4
EXPERTS_PER_GROUP = N_EXPERTS // N_GROUPS
TOP_K = 2

LANES = 128
SUBLANES = 8
ROW_TILE = 256
SCAN_CHUNK = 128
ATTN_TQ = 512
ATTN_CTX_TQ = 256
ATTN_TK = 512
MOE_ROWS = 512
DISPATCH_ROWS = 512
MOD_ROWS = 16
NEG_BIG = -1e30
LOG2_E = math.log2(math.e)


def _cparams(sem, vmem_mib):
    return pltpu.CompilerParams(dimension_semantics=sem, vmem_limit_bytes=vmem_mib * 1024 * 1024)


def _sigmoid(x):
    return 1.0 / (1.0 + jnp.exp(-x))


def _gelu_tanh(x):
    return 0.5 * x * (1.0 + jnp.tanh(math.sqrt(2.0 / math.pi) * (x + 0.044715 * (x * x * x))))


def _rms(x, g):
    return x * lax.rsqrt(jnp.mean(x * x, axis=-1, keepdims=True) + EPS) * g


def _adaln_kernel(c_ref, w_ref, b_ref, o_ref):
    c = c_ref[...]
    a = c * _sigmoid(c)
    o_ref[...] = jnp.dot(a, w_ref[...], preferred_element_type=F32,
                         precision=lax.Precision.HIGHEST) + b_ref[...]


def _adaln(cond, w_mod, b_mod):
    depth, d, d6 = w_mod.shape
    n = d6 // d
    return pl.pallas_call(
        _adaln_kernel,
        grid=(depth, n),
        in_specs=[pl.BlockSpec((MOD_ROWS, d), lambda l, j: (0, 0)),
                  pl.BlockSpec((None, d, d), lambda l, j: (l, 0, j)),
                  pl.BlockSpec((None, 1, d), lambda l, j: (l, 0, j))],
        out_specs=pl.BlockSpec((None, MOD_ROWS, d), lambda l, j: (l, 0, j)),
        out_shape=jax.ShapeDtypeStruct((depth, MOD_ROWS, d6), F32),
        compiler_params=_cparams(("arbitrary", "arbitrary"), 32),
        name="adaln",
    )(cond, w_mod, b_mod.reshape(depth, 1, d6))


def _stream_tile(x_ref, xc_ref, n_ctx_tiles):
    if xc_ref is None:
        return x_ref[...]
    return jnp.where(pl.program_id(1) < n_ctx_tiles, xc_ref[...], x_ref[...])


def _stream_specs(x, x_ctx, n_ctx_tiles):
    d = x.shape[-1]
    if x_ctx is None:
        return [pl.BlockSpec((None, ROW_TILE, d), lambda bi, i: (bi, i, 0))], [x]
    return ([pl.BlockSpec((None, ROW_TILE, d), lambda bi, i: (bi, jnp.maximum(i - n_ctx_tiles, 0), 0)),
             pl.BlockSpec((None, ROW_TILE, d), lambda bi, i: (bi, jnp.minimum(i, n_ctx_tiles - 1), 0))],
            [x, x_ctx])


def _inproj_kernel(*refs, d, split, n_ctx_tiles):
    if split:
        (x_ref, xc_ref, sh_ref, sc_ref, g_ref, w_ref, cos_ref, sin_ref,
         k_ref, v_ref, q_ref, rx_ref, rg_ref, gr_ref, ga_ref) = refs
    else:
        (x_ref, sh_ref, sc_ref, g_ref, w_ref, cos_ref, sin_ref,
         k_ref, v_ref, q_ref, rx_ref, rg_ref, gr_ref, ga_ref) = refs
        xc_ref = None
    x = _stream_tile(x_ref, xc_ref, n_ctx_tiles)
    h = (_rms(x, g_ref[...]) * (1.0 + sc_ref[...]) + sh_ref[...]).astype(BF16)
    cosv = cos_ref[...]
    sinv = sin_ref[...]

    def proj(p):
        return jnp.dot(h, w_ref[:, p * d:(p + 1) * d], preferred_element_type=F32)

    def rope_store(res, ref, scale):
        for hh in range(N_HEADS):
            xs = res[:, hh * LANES:(hh + 1) * LANES]
            out = xs * cosv + pltpu.roll(xs, LANES // 2, 1) * sinv
            ref[:, hh * LANES:(hh + 1) * LANES] = (out * scale).astype(BF16)

    rope_store(proj(0), k_ref, 1.0)
    v_ref[...] = proj(1).T.astype(BF16)
    rope_store(proj(2), q_ref, HEAD_DIM ** -0.5 * LOG2_E)
    rx_ref[...] = proj(3)
    rg_ref[...] = proj(4)
    gr_ref[...] = proj(5)
    ga_ref[...] = proj(6)


def _inproj(x, x_ctx, sh, sc, g, w, cos_t, sin_t, n_ctx_tiles):
    b, _, d = x.shape
    lt = x.shape[1] + (0 if x_ctx is None else x_ctx.shape[1])
    nt = lt // ROW_TILE
    mod_map = lambda bi, i: (jnp.where(i < n_ctx_tiles, b, bi), 0, 0)
    tok = pl.BlockSpec((None, ROW_TILE, d), lambda bi, i: (bi, i, 0))
    x_specs, x_args = _stream_specs(x, x_ctx, n_ctx_tiles)
    out_bf = jax.ShapeDtypeStruct((b, lt, d), BF16)
    out_f = jax.ShapeDtypeStruct((b, lt, d), F32)
    return pl.pallas_call(
        functools.partial(_inproj_kernel, d=d, split=x_ctx is not None, n_ctx_tiles=n_ctx_tiles),
        grid=(b, nt),
        in_specs=x_specs + [
                  pl.BlockSpec((None, 1, d), mod_map),
                  pl.BlockSpec((None, 1, d), mod_map),
                  pl.BlockSpec((1, d), lambda bi, i: (0, 0)),
                  pl.BlockSpec(w.shape, lambda bi, i: (0, 0)),
                  pl.BlockSpec((ROW_TILE, LANES), lambda bi, i: (i, 0)),
                  pl.BlockSpec((ROW_TILE, LANES), lambda bi, i: (i, 0))],
        out_specs=[tok, pl.BlockSpec((None, d, ROW_TILE), lambda bi, i: (bi, 0, i))] + [tok] * 5,
        out_shape=[out_bf, jax.ShapeDtypeStruct((b, d, lt), BF16), out_bf, out_f, out_f, out_f, out_f],
        compiler_params=_cparams(("parallel", "arbitrary"), 56),
        name="inproj",
    )(*x_args, sh, sc, g, w, cos_t, sin_t)


def _scan_chunk(i, n_ctx, n_chunks, reverse):
    if not reverse:
        return i
    return jnp.where(i < n_ctx, n_ctx - 1 - i, n_chunks - 1 - (i - n_ctx))


def _scan_kernel(*refs, reverse, tt, n_ctx, n_chunks):
    if reverse:
        (rx_ref, prev_ref, next_ref, cw_ref, cb_ref, wg_ref, bg_ref, c_ref, hf_ref,
         out_ref, ext_scr, a_scr, b_scr, h_scr) = refs
    else:
        (rx_ref, prev_ref, next_ref, cw_ref, cb_ref, wg_ref, bg_ref, c_ref,
         out_ref, ext_scr, a_scr, b_scr, h_scr) = refs
        hf_ref = None
    i = pl.program_id(0)
    c = _scan_chunk(i, n_ctx, n_chunks, reverse)
    seq_start = jnp.logical_or(c == 0, c == n_ctx)
    seq_end = jnp.logical_or(c == n_ctx - 1, c == n_chunks - 1)

    ext_scr[0:CONV_LEFT] = jnp.where(seq_start, 0.0, prev_ref[...])
    ext_scr[CONV_LEFT:CONV_LEFT + tt] = rx_ref[...]
    ext_scr[CONV_LEFT + tt:CONV_LEFT + tt + 1] = jnp.where(seq_end, 0.0, next_ref[...])

    rows = tt * SUBLANES
    for n in range(RNN_BLOCKS):
        sl = slice(n * RNN_BW, (n + 1) * RNN_BW)
        u = cb_ref[:, sl] + cw_ref[0:1, sl] * ext_scr[0:tt, :, sl]
        for j in range(1, CONV_W):
            u = u + cw_ref[j:j + 1, sl] * ext_scr[j:j + tt, :, sl]
        u2 = u.reshape(rows, RNN_BW)
        z = jnp.dot(u2.astype(BF16), wg_ref[n], preferred_element_type=F32) + bg_ref[n]
        r = _sigmoid(z[:, :RNN_BW])
        ig = _sigmoid(z[:, RNN_BW:])
        a = jnp.exp(-c_ref[:, sl] * r)
        bb = jnp.sqrt(1.0 - a * a) * ig * u2
        a_scr[:, :, sl] = a.reshape(tt, SUBLANES, RNN_BW)
        b_scr[:, :, sl] = bb.reshape(tt, SUBLANES, RNN_BW)

    @pl.when(i == 0)
    def _():
        h_scr[...] = jnp.zeros_like(h_scr)

    def step(s, h):
        t = tt - 1 - s if reverse else s
        h = a_scr[t] * h + b_scr[t]
        if reverse:
            out_ref[t] = hf_ref[t] + h
        else:
            out_ref[t] = h
        return h

    h_scr[...] = lax.fori_loop(0, tt, step, h_scr[...], unroll=8)


def _scan(rx_t, cw, cb, wg, bg, cdec, hf, n_ctx, reverse):
    lt, b, d = rx_t.shape
    tt = SCAN_CHUNK
    n_chunks = lt // tt
    chunk = lambda i: _scan_chunk(i, n_ctx, n_chunks, reverse)
    blk = pl.BlockSpec((tt, b, d), lambda i: (chunk(i), 0, 0))
    const2 = lambda i: (0, 0)
    const3 = lambda i: (0, 0, 0)
    in_specs = [blk,
                pl.BlockSpec((CONV_LEFT, b, d), lambda i: (jnp.maximum(chunk(i) * (tt // CONV_LEFT) - 1, 0), 0, 0)),
                pl.BlockSpec((1, b, d), lambda i: (jnp.minimum((chunk(i) + 1) * tt, lt - 1), 0, 0)),
                pl.BlockSpec(cw.shape, const2),
                pl.BlockSpec(cb.shape, const2),
                pl.BlockSpec(wg.shape, const3),
                pl.BlockSpec(bg.shape, const3),
                pl.BlockSpec(cdec.shape, const2)]
    args = [rx_t, rx_t, rx_t, cw, cb, wg, bg, cdec]
    if reverse:
        in_specs.append(blk)
        args.append(hf)
    return pl.pallas_call(
        functools.partial(_scan_kernel, reverse=reverse, tt=tt, n_ctx=n_ctx, n_chunks=n_chunks),
        grid=(n_chunks,),
        in_specs=in_specs,
        out_specs=blk,
        out_shape=jax.ShapeDtypeStruct((lt, b, d), F32),
        scratch_shapes=[pltpu.VMEM((tt + CONV_W - 1, b, d), F32),
                        pltpu.VMEM((tt, b, d), F32),
                        pltpu.VMEM((tt, b, d), F32),
                        pltpu.VMEM((b, d), F32)],
        compiler_params=_cparams(("arbitrary",), 56),
        name="scan_bwd" if reverse else "scan_fwd",
    )(*args)


def _attn_passes(lam, q_refs, k_ref, vt_ref, g_ref, o_ref, *, write, read, tq, w_chunks, r_chunks, out_scale):
    n2 = 2 * tq
    if write is not None:
        s_w, m_w = write
        q = jnp.concatenate([r[...] for r in q_refs], axis=0) if len(q_refs) > 1 else q_refs[0][...]
        lane = lax.broadcasted_iota(jnp.int32, q.shape, 1)
        first = (lane // (HEAD_DIM // 2)) % 2 == 0
        zero = jnp.zeros_like(q)
        qs = jnp.concatenate([jnp.where(first, q, zero), jnp.where(first, zero, q)], axis=0)
        m8 = jnp.full((SUBLANES, n2), NEG_BIG, F32)
    if read is not None:
        s_r, m_r = read
        m = m_r[...]
        l8 = jnp.zeros((SUBLANES, n2), F32)
        acc = jnp.zeros((V_DIM, n2), F32)
    for j in range(max(len(w_chunks) if write is not None else 0, len(r_chunks) if read is not None else 0)):
        if write is not None and j < len(w_chunks):
            start, size = w_chunks[j]
            st = lax.dot_general(k_ref[start:start + size, :], qs, (((1,), (1,)), ((), ())),
                                 preferred_element_type=F32)
            s_w[start:start + size, :] = st
            m8 = jnp.maximum(m8, jnp.max(st.reshape(size // SUBLANES, SUBLANES, n2), axis=0))
        if read is not None and j < len(r_chunks):
            start, size = r_chunks[j]
            p = jnp.exp2(s_r[start:start + size, :] - m)
            l8 = l8 + jnp.sum(p.reshape(size // SUBLANES, SUBLANES, n2), axis=0)
            acc = acc + jnp.dot(vt_ref[:, start:start + size], p.astype(BF16), preferred_element_type=F32)
    if write is not None:
        m_w[...] = jnp.max(m8, axis=0, keepdims=True)
    if read is not None:
        on = acc * (1.0 / jnp.sum(l8, axis=0, keepdims=True))
        dt = on[:, :tq] - lam * on[:, tq:]
        yt = dt * lax.rsqrt(jnp.mean(dt * dt, axis=0, keepdims=True) + EPS)
        o_ref[...] = (yt.T * g_ref[...] * out_scale).astype(BF16)


def _attn_pipe_kernel(lam_ref, *refs, tq, nq, n_q_refs, chunks, out_scale):
    q_refs = refs[:n_q_refs]
    k_ref, vt_ref, g_ref, o_ref, s_a, s_b, m_a, m_b = refs[n_q_refs:]
    s = pl.program_id(2)
    run = functools.partial(_attn_passes, lam_ref[0], q_refs, k_ref, vt_ref, g_ref, o_ref, tq=tq,
                            w_chunks=chunks, r_chunks=chunks, out_scale=out_scale)
    buf_a, buf_b = (s_a, m_a), (s_b, m_b)
    inner = jnp.logical_and(s > 0, s < nq)
    even = s % 2 == 0

    @pl.when(s == 0)
    def _():
        run(write=buf_a, read=None)

    @pl.when(jnp.logical_and(inner, even))
    def _():
        run(write=buf_a, read=buf_b)

    @pl.when(jnp.logical_and(inner, jnp.logical_not(even)))
    def _():
        run(write=buf_b, read=buf_a)

    @pl.when(s == nq)
    def _():
        run(write=None, read=buf_a if (nq - 1) % 2 == 0 else buf_b)


def _attn_single_kernel(lam_ref, q_ref, k_ref, vt_ref, g_ref, o_ref, s_a, m_a, *, tq, chunks, out_scale):
    run = functools.partial(_attn_passes, lam_ref[0], (q_ref,), k_ref, vt_ref, g_ref, o_ref, tq=tq,
                            w_chunks=chunks, r_chunks=chunks, out_scale=out_scale)
    run(write=(s_a, m_a), read=None)
    run(write=None, read=(s_a, m_a))


def _key_chunks(n_ctx_rows, lt):
    ctx = [(s, min(ATTN_TK, n_ctx_rows - s)) for s in range(0, n_ctx_rows, ATTN_TK)]
    lat = [(s, min(ATTN_TK, lt - s)) for s in range(n_ctx_rows, lt, ATTN_TK)]
    return tuple(ctx), tuple(ctx + lat)


def _attention(lam, q, k, vt, g_sub, n_ctx_rows, with_ctx_queries, out_scale):
    b, lt, d = q.shape
    tq, tqc = ATTN_TQ, ATTN_CTX_TQ
    ctx_q_tiles = n_ctx_rows // tqc
    n_sub = tq // tqc
    nq = (lt - n_ctx_rows) // tq
    ctx_chunks, all_chunks = _key_chunks(n_ctx_rows, lt)
    n2 = 2 * tq
    lam_in = pl.BlockSpec(memory_space=pltpu.SMEM)
    g_in = pl.BlockSpec((1, V_DIM), lambda bi, h, s: (0, 0))
    lat = pl.pallas_call(
        functools.partial(_attn_pipe_kernel, tq=tq, nq=nq, n_q_refs=n_sub, chunks=all_chunks,
                          out_scale=out_scale),
        grid=(b, N_HEADS, nq + 1),
        in_specs=[lam_in] + [
                  pl.BlockSpec((None, tqc, V_DIM),
                               lambda bi, h, s, j=j: (bi, jnp.minimum(s, nq - 1) * n_sub + ctx_q_tiles + j, h))
                  for j in range(n_sub)] + [
                  pl.BlockSpec((None, lt, V_DIM), lambda bi, h, s: (bi, 0, h)),
                  pl.BlockSpec((None, V_DIM, lt), lambda bi, h, s: (bi, h, 0)),
                  g_in],
        out_specs=pl.BlockSpec((None, tq, V_DIM), lambda bi, h, s: (bi, jnp.maximum(s - 1, 0), h)),
        out_shape=jax.ShapeDtypeStruct((b, nq * tq, d), BF16),
        scratch_shapes=[pltpu.VMEM((lt, n2), F32), pltpu.VMEM((lt, n2), F32),
                        pltpu.VMEM((1, n2), F32), pltpu.VMEM((1, n2), F32)],
        compiler_params=_cparams(("parallel", "parallel", "arbitrary"), 56),
        name="diff_attn",
    )(lam, *([q] * n_sub), k, vt, g_sub)
    if not with_ctx_queries:
        return lat, None
    ctx = pl.pallas_call(
        functools.partial(_attn_single_kernel, tq=tqc, chunks=ctx_chunks, out_scale=out_scale),
        grid=(b, N_HEADS, ctx_q_tiles),
        in_specs=[lam_in,
                  pl.BlockSpec((None, tqc, V_DIM), lambda bi, h, s: (bi, s, h)),
                  pl.BlockSpec((None, n_ctx_rows, V_DIM), lambda bi, h, s: (bi, 0, h)),
                  pl.BlockSpec((None, V_DIM, n_ctx_rows), lambda bi, h, s: (bi, h, 0)),
                  g_in],
        out_specs=pl.BlockSpec((None, tqc, V_DIM), lambda bi, h, s: (bi, s, h)),
        out_shape=jax.ShapeDtypeStruct((b, n_ctx_rows, d), BF16),
        scratch_shapes=[pltpu.VMEM((n_ctx_rows, 2 * tqc), F32), pltpu.VMEM((1, 2 * tqc), F32)],
        compiler_params=_cparams(("parallel", "parallel", "arbitrary"), 32),
        name="diff_attn_ctx",
    )(lam, q, k, vt, g_sub)
    return lat, ctx


def _postmix_kernel(*refs, has_ctx, split, n_ctx_tiles):
    refs = list(refs)
    x_ref = refs.pop(0)
    xc_ref = refs.pop(0) if split else None
    at_ref = refs.pop(0)
    atc_ref = refs.pop(0) if has_ctx else None
    (y_ref, rg_ref, gr_ref, ga_ref, gt1_ref, sh2_ref, sc2_ref, g2_ref,
     wr_ref, wa_ref, wo_ref, wrh_ref, wrl_ref, brt_ref, x1_ref, h2_ref, route_ref, cnt_ref, run_scr) = refs
    x = _stream_tile(x_ref, xc_ref, n_ctx_tiles)
    at = _stream_tile(at_ref, atc_ref, n_ctx_tiles)
    yg = (y_ref[...].astype(F32) * _gelu_tanh(rg_ref[...])).astype(BF16)
    o_r = jnp.dot(yg, wr_ref[...], preferred_element_type=F32)
    o_a = jnp.dot(at, wa_ref[...], preferred_element_type=F32)
    mix = (_sigmoid(gr_ref[...]) * o_r + _sigmoid(ga_ref[...]) * o_a).astype(BF16)
    x1 = x + gt1_ref[...] * jnp.dot(mix, wo_ref[...], preferred_element_type=F32)
    x1_ref[...] = x1
    h2 = _rms(x1, g2_ref[...]) * (1.0 + sc2_ref[...]) + sh2_ref[...]
    h2_ref[...] = h2

    h_hi = h2.astype(BF16)
    h_lo = (h2 - h_hi.astype(F32)).astype(BF16)
    logits = (jnp.dot(h_hi, wrh_ref[...], preferred_element_type=F32)
              + jnp.dot(h_lo, wrh_ref[...], preferred_element_type=F32)
              + jnp.dot(h_hi, wrl_ref[...], preferred_element_type=F32)) + brt_ref[...]
    lane = lax.broadcasted_iota(jnp.int32, logits.shape, 1)
    lane_f = lane.astype(F32)
    valid = lane < N_EXPERTS
    lg = jnp.where(valid, logits, NEG_BIG)
    e = jnp.exp(lg - jnp.max(lg, axis=-1, keepdims=True))
    e = jnp.where(valid, e, 0.0)
    p = e / jnp.sum(e, axis=-1, keepdims=True)
    p = jnp.where(valid, p, -1.0)
    grp = lane // EXPERTS_PER_GROUP
    best = jnp.max(jnp.where(grp == 0, p, -1.0), axis=-1, keepdims=True)
    best_g = jnp.zeros_like(best, dtype=jnp.int32)
    for gi in range(1, N_GROUPS):
        gm = jnp.max(jnp.where(grp == gi, p, -1.0), axis=-1, keepdims=True)
        better = gm > best
        best_g = jnp.where(better, gi, best_g)
        best = jnp.where(better, gm, best)
    pg = jnp.where(grp == best_g, p, -1.0)
    big = float(LANES)
    v1 = jnp.max(pg, axis=-1, keepdims=True)
    i1 = jnp.min(jnp.where(pg == v1, lane_f, big), axis=-1, keepdims=True)
    pg2 = jnp.where(lane_f == i1, -1.0, pg)
    v2 = jnp.max(pg2, axis=-1, keepdims=True)
    i2 = jnp.min(jnp.where(pg2 == v2, lane_f, big), axis=-1, keepdims=True)
    den = v1 + v2

    @pl.when(jnp.logical_and(pl.program_id(0) == 0, pl.program_id(1) == 0))
    def _():
        run_scr[...] = jnp.zeros_like(run_scr)

    rows = logits.shape[0]
    oh1 = (lane_f == i1).astype(F32)
    oh2 = (lane_f == i2).astype(F32)
    both = oh1 + oh2
    tri = (lax.broadcasted_iota(jnp.int32, (rows, rows), 1)
           < lax.broadcasted_iota(jnp.int32, (rows, rows), 0)).astype(BF16)
    before = jnp.dot(tri, both.astype(BF16), preferred_element_type=F32) + run_scr[...]
    r1 = jnp.sum(oh1 * before, axis=-1, keepdims=True)
    r2 = jnp.sum(oh2 * before, axis=-1, keepdims=True)
    run = run_scr[...] + jnp.sum(both, axis=0, keepdims=True)
    run_scr[...] = run
    cnt_ref[...] = jnp.broadcast_to(run, cnt_ref.shape)

    route = jnp.where(lane == 0, i1, jnp.where(lane == 1, i2, jnp.where(lane == 2, v1 / den, jnp.where(
        lane == 3, v2 / den, jnp.where(lane == 4, r1, r2)))))
    route_ref[...] = route


def _postmix(x, x_ctx, attn, attn_ctx, y, rg, gr, ga, gt1, sh2, sc2, g2, wr, wa, wo, wrh, wrl, brt, n_ctx_tiles):
    b, _, d = y.shape
    lt = y.shape[1]
    has_ctx = attn_ctx is not None
    off = 0 if has_ctx else n_ctx_tiles
    lq = lt - off * ROW_TILE
    full = pl.BlockSpec((None, ROW_TILE, d), lambda bi, i: (bi, i + off, 0))
    part = pl.BlockSpec((None, ROW_TILE, d), lambda bi, i: (bi, i, 0))
    mod = pl.BlockSpec((None, 1, d), lambda bi, i: (jnp.where(i + off < n_ctx_tiles, b, bi), 0, 0))
    const = lambda bi, i: (0, 0)
    if x_ctx is None:
        x_specs, x_args = [full], [x]
    else:
        x_specs, x_args = _stream_specs(x, x_ctx, n_ctx_tiles)
    at_specs, at_args = _stream_specs(attn, attn_ctx, n_ctx_tiles)
    return pl.pallas_call(
        functools.partial(_postmix_kernel, has_ctx=has_ctx, split=x_ctx is not None, n_ctx_tiles=n_ctx_tiles),
        grid=(b, lq // ROW_TILE),
        in_specs=x_specs + at_specs + [full, full, full, full, mod, mod, mod,
                                      pl.BlockSpec((1, d), const),
                                      pl.BlockSpec((d, d), const), pl.BlockSpec((d, d), const),
                                      pl.BlockSpec((d, d), const),
                                      pl.BlockSpec((d, LANES), const), pl.BlockSpec((d, LANES), const),
                                      pl.BlockSpec((1, LANES), const)],
        out_specs=[part, part, pl.BlockSpec((None, ROW_TILE, LANES), lambda bi, i: (bi, i, 0)),
                   pl.BlockSpec((SUBLANES, LANES), const)],
        out_shape=[jax.ShapeDtypeStruct((b, lq, d), F32),
                   jax.ShapeDtypeStruct((b, lq, d), F32),
                   jax.ShapeDtypeStruct((b, lq, LANES), F32),
                   jax.ShapeDtypeStruct((SUBLANES, LANES), F32)],
        scratch_shapes=[pltpu.VMEM((1, LANES), F32)],
        compiler_params=_cparams(("arbitrary", "arbitrary"), 48),
        name="postmix",
    )(*x_args, *at_args, y, rg, gr, ga, gt1, sh2, sc2, g2, wr, wa, wo, wrh, wrl, brt)


def _dispatch_kernel(pad_ref, dest_ref, h_ref, xs_hbm, zrows, sem, *, n_pad):
    i = pl.program_id(0)
    rows = h_ref.shape[0]

    @pl.when(i == 0)
    def _():
        zrows[...] = jnp.zeros_like(zrows)
        for c in range(n_pad // MOE_ROWS):
            def fill(r, carry):
                pltpu.make_async_copy(zrows.at[pl.ds(r, 1)],
                                      xs_hbm.at[pl.ds(pad_ref[c * MOE_ROWS + r], 1)], sem).start()
                return carry

            lax.fori_loop(0, MOE_ROWS, fill, 0, unroll=8)
            pltpu.make_async_copy(zrows, xs_hbm.at[pl.ds(0, MOE_ROWS)], sem).wait()

    def send(r, carry):
        for k in range(TOP_K):
            pltpu.make_async_copy(h_ref.at[pl.ds(r, 1)],
                                  xs_hbm.at[pl.ds(dest_ref[0, TOP_K * r + k], 1)], sem).start()
        return carry

    lax.fori_loop(0, rows, send, 0, unroll=8)
    for k in range(TOP_K):
        pltpu.make_async_copy(h_ref, xs_hbm.at[pl.ds(0, rows)], sem).wait()


def _dispatch(pad_slots, dest, h2, n_slots):
    t, d = h2.shape
    rows = DISPATCH_ROWS
    return pl.pallas_call(
        functools.partial(_dispatch_kernel, n_pad=pad_slots.shape[0]),
        grid_spec=pltpu.PrefetchScalarGridSpec(
            num_scalar_prefetch=1,
            grid=(t // rows,),
            in_specs=[pl.BlockSpec((None, 1, TOP_K * rows), lambda i, pad: (i, 0, 0), memory_space=pltpu.SMEM),
                      pl.BlockSpec((rows, d), lambda i, pad: (i, 0))],
            out_specs=pl.BlockSpec(memory_space=pl.ANY),
            scratch_shapes=[pltpu.VMEM((MOE_ROWS, d), F32), pltpu.SemaphoreType.DMA(())]),
        out_shape=jax.ShapeDtypeStruct((n_slots, d), F32),
        compiler_params=_cparams(("arbitrary",), 32),
        name="dispatch",
    )(pad_slots, dest.reshape(t // rows, 1, TOP_K * rows), h2)


def _expert_kernel(be_ref, nv_ref, x_ref, w1_ref, w3_ref, w2_ref, y_ref, w1_b, w3_b, w2_b):
    i = pl.program_id(0)

    @pl.when(i < nv_ref[0])
    def _():
        @pl.when(jnp.logical_or(i == 0, be_ref[i] != be_ref[jnp.maximum(i - 1, 0)]))
        def _():
            w1_b[...] = w1_ref[...].astype(BF16)
            w3_b[...] = w3_ref[...].astype(BF16)
            w2_b[...] = w2_ref[...].astype(BF16)

        xb = x_ref[...].astype(BF16)
        a1 = jnp.dot(xb, w1_b[...], preferred_element_type=F32)
        a3 = jnp.dot(xb, w3_b[...], preferred_element_type=F32)
        mid = (a1 * _sigmoid(a1) * a3).astype(BF16)
        y_ref[...] = jnp.dot(mid, w2_b[...], preferred_element_type=F32)

    @pl.when(i >= nv_ref[0])
    def _():
        y_ref[...] = jnp.zeros_like(y_ref)


def _experts(block_e, n_valid, xs, w1, w3, w2):
    n_slots, d = xs.shape
    de = w1.shape[-1]
    wspec = lambda shp: pl.BlockSpec((None,) + shp, lambda i, be, nv: (be[i], 0, 0))
    return pl.pallas_call(
        _expert_kernel,
        grid_spec=pltpu.PrefetchScalarGridSpec(
            num_scalar_prefetch=2,
            grid=(n_slots // MOE_ROWS,),
            in_specs=[pl.BlockSpec((MOE_ROWS, d), lambda i, be, nv: (jnp.minimum(i, nv[0] - 1), 0)),
                      wspec((d, de)), wspec((d, de)), wspec((de, d))],
            out_specs=pl.BlockSpec((MOE_ROWS, d), lambda i, be, nv: (i, 0)),
            scratch_shapes=[pltpu.VMEM((d, de), BF16), pltpu.VMEM((d, de), BF16), pltpu.VMEM((de, d), BF16)]),
        out_shape=jax.ShapeDtypeStruct((n_slots, d), F32),
        compiler_params=_cparams(("arbitrary",), 56),
        name="experts",
    )(block_e, n_valid, xs, w1, w3, w2)


def _route_tables(route, counts):
    expert = route[:, 0:TOP_K].astype(jnp.int32).reshape(-1)
    rank = route[:, 4:4 + TOP_K].astype(jnp.int32).reshape(-1)
    a = expert.shape[0]
    sizes = counts[0, :N_EXPERTS].astype(jnp.int32)
    padded = (sizes + MOE_ROWS - 1) // MOE_ROWS * MOE_ROWS
    pend = jnp.cumsum(padded)
    pstarts = pend - padded
    dest = (pstarts[expert] + rank).astype(jnp.int32)
    n_blocks = -(-a // MOE_ROWS) + N_EXPERTS
    n_slots = n_blocks * MOE_ROWS
    n_valid = (pend[-1] // MOE_ROWS).astype(jnp.int32)
    blk = jnp.minimum(jnp.arange(n_blocks, dtype=jnp.int32), n_valid - 1)
    block_e = jnp.minimum(jnp.sum(blk[:, None] * MOE_ROWS >= pend[None, :], axis=1),
                          N_EXPERTS - 1).astype(jnp.int32)
    gap_start = jnp.concatenate([pstarts + sizes, pend[-1:]]).astype(jnp.int32)
    gap_len = jnp.concatenate([padded - sizes, n_slots - pend[-1:]]).astype(jnp.int32)
    gap_end = jnp.cumsum(gap_len)
    j = jnp.arange(n_slots - a, dtype=jnp.int32)
    seg = jnp.sum(j[:, None] >= gap_end[None, :], axis=1)
    pad_slots = (gap_start[seg] + j - (gap_end - gap_len)[seg]).astype(jnp.int32)
    return dest, pad_slots, block_e, n_valid.reshape(1), n_slots


def _combine_kernel(dcur_ref, dnxt_ref, x_ref, route_ref, gt2_ref, g_ref, y_hbm, o_ref, ybuf, sems, *, final):
    g = pl.program_id(0)
    n = pl.num_programs(0)
    rows = x_ref.shape[0]
    slot = g % 2

    def fetch(tbl_ref, s):
        def body(r, carry):
            for k in range(TOP_K):
                pltpu.make_async_copy(y_hbm.at[pl.ds(tbl_ref[0, TOP_K * r + k], 1)],
                                      ybuf.at[s, k, pl.ds(r, 1)], sems.at[s]).start()
            return carry

        lax.fori_loop(0, rows, body, 0, unroll=8)

    @pl.when(g == 0)
    def _():
        fetch(dcur_ref, 0)

    @pl.when(g + 1 < n)
    def _():
        fetch(dnxt_ref, 1 - slot)

    for k in range(TOP_K):
        pltpu.make_async_copy(y_hbm.at[pl.ds(0, rows)], ybuf.at[slot, k], sems.at[slot]).wait()

    rt = route_ref[...]
    y = rt[:, 2:3] * ybuf[slot, 0] + rt[:, 3:4] * ybuf[slot, 1]
    x2 = x_ref[...] + gt2_ref[...] * y
    o_ref[...] = _rms(x2, g_ref[...]) if final else x2


def _combine(x1, y_slots, dest, route, gt2, g_final, n_ctx_tiles, has_ctx, final):
    b, lq, d = x1.shape
    nt = lq // ROW_TILE
    n_tiles = b * nt
    ctx_tiles = n_ctx_tiles if has_ctx else 0
    tok = lambda w: pl.BlockSpec((None, ROW_TILE, w), lambda g: (g // nt, g % nt, 0))
    tbl = lambda f: pl.BlockSpec((None, 1, TOP_K * ROW_TILE), lambda g: (f(g), 0, 0), memory_space=pltpu.SMEM)
    dest3 = dest.reshape(n_tiles, 1, TOP_K * ROW_TILE)
    return pl.pallas_call(
        functools.partial(_combine_kernel, final=final),
        grid=(n_tiles,),
        in_specs=[tbl(lambda g: g), tbl(lambda g: jnp.minimum(g + 1, n_tiles - 1)),
                  tok(d), tok(LANES),
                  pl.BlockSpec((None, 1, d), lambda g: (jnp.where(g % nt < ctx_tiles, b, g // nt), 0, 0)),
                  pl.BlockSpec((1, d), lambda g: (0, 0)),
                  pl.BlockSpec(memory_space=pl.ANY)],
        out_specs=tok(d),
        out_shape=jax.ShapeDtypeStruct((b, lq, d), F32),
        scratch_shapes=[pltpu.VMEM((2, TOP_K, ROW_TILE, d), F32), pltpu.SemaphoreType.DMA((2,))],
        compiler_params=_cparams(("arbitrary",), 32),
        name="combine",
    )(dest3, dest3, x1, route, gt2, g_final, y_slots)


def _rope_tables(n_ctx, seq):
    n_pairs = HEAD_DIM // 4
    inv = ROPE_BASE ** (-jnp.arange(n_pairs, dtype=F32) / n_pairs)
    rows = seq // GRID_W
    r = jnp.repeat(jnp.arange(rows, dtype=F32), GRID_W)
    col = jnp.tile(jnp.arange(GRID_W, dtype=F32), rows)
    ang = jnp.concatenate([r[:, None] * inv, col[:, None] * inv], axis=-1)
    cos = jnp.concatenate([jnp.ones((n_ctx, HEAD_DIM // 2), F32), jnp.cos(ang)], axis=0)
    sin = jnp.concatenate([jnp.zeros((n_ctx, HEAD_DIM // 2), F32), jnp.sin(ang)], axis=0)
    return jnp.tile(cos, (1, 4)), jnp.concatenate([-sin, -sin, sin, sin], axis=-1)


def _qk_perm():
    lane = jnp.arange(LANES)
    g, i = lane // 32, lane % 32
    within = (g % 2) * HEAD_DIM + 2 * i + g // 2
    return (jnp.arange(N_HEADS)[:, None] * LANES + within[None, :]).reshape(-1)


def kernel(x, c, ctx, c_ctx, w_mod, b_mod, g_norm1, g_norm2, w_in, conv_w, conv_b, lru_wa, lru_ba, lru_wx,
           lru_bx, lru_lambda, diff_lambda, g_subln, w_rnn_proj, w_attn_proj, w_o, w_router, b_router,
           w_e1, w_e3, w_e2, g_final):
    b, seq, d = x.shape
    n_ctx = ctx.shape[1]
    depth = w_mod.shape[0]
    n_ctx_tiles = n_ctx // ROW_TILE
    assert b == SUBLANES and d == N_HEADS * V_DIM and n_ctx % ROW_TILE == 0 and seq % ROW_TILE == 0

    cond = jnp.zeros((MOD_ROWS, d), F32).at[:b].set(c).at[b].set(c_ctx)
    mods = _adaln(cond, w_mod, b_mod).reshape(depth, MOD_ROWS, 6, 1, d)
    cos_t, sin_t = _rope_tables(n_ctx, seq)
    perm = _qk_perm()
    wrt = jnp.zeros((d, LANES), F32).at[:, :N_EXPERTS].set(w_router)
    wrt_hi = wrt.astype(BF16)
    wrt_lo = (wrt - wrt_hi.astype(F32)).astype(BF16)
    brt = jnp.zeros((1, LANES), F32).at[0, :N_EXPERTS].set(b_router)

    xs, xs_ctx = x, ctx
    out = None
    for l in range(depth):
        last = l == depth - 1
        lam_init = 0.8 - 0.6 * math.exp(-0.3 * l)
        sh1, sc1, gt1, sh2, sc2, gt2 = [mods[l, :, j] for j in range(6)]
        wl = w_in[l]
        w_perm = jnp.concatenate(
            [wl[:, d:2 * d][:, perm], wl[:, 2 * d:3 * d], wl[:, 3 * d:4 * d][:, perm],
             wl[:, 0:d], wl[:, 4 * d:]], axis=1).astype(BF16)
        k, v, q, rx, rg, gr, ga = _inproj(xs, xs_ctx, sh1, sc1, g_norm1[l][None], w_perm, cos_t, sin_t,
                                          n_ctx_tiles)

        lq1, lk1, lq2, lk2 = diff_lambda[l].astype(F32)
        lam = (jnp.exp(jnp.sum(lq1 * lk1)) - jnp.exp(jnp.sum(lq2 * lk2)) + lam_init).reshape(1)
        attn, attn_ctx = _attention(lam, q, k, v, g_subln[l][None], n_ctx, not last, 1.0 - lam_init)

        rx_t = jnp.transpose(rx, (1, 0, 2))
        cdec = LRU_C * jax.nn.softplus(-lru_lambda[l].astype(F32))
        n_ctx_chunks = n_ctx // SCAN_CHUNK
        hf = None
        for direction in range(2):
            wg = jnp.concatenate([lru_wa[l, direction], lru_wx[l, direction]], axis=-1).astype(BF16)
            bg = jnp.concatenate([lru_ba[l, direction].reshape(RNN_BLOCKS, 1, RNN_BW),
                                  lru_bx[l, direction].reshape(RNN_BLOCKS, 1, RNN_BW)], axis=-1)
            hf = _scan(rx_t, conv_w[l], conv_b[l][None], wg, bg, cdec[direction][None], hf,
                       n_ctx_chunks, reverse=direction == 1)
        y = jnp.transpose(hf, (1, 0, 2)).astype(BF16)

        x1, h2, route, counts = _postmix(xs, xs_ctx, attn, attn_ctx, y, rg, gr, ga, gt1, sh2, sc2,
                                         g_norm2[l][None],
                                         w_rnn_proj[l].astype(BF16), w_attn_proj[l].astype(BF16),
                                         w_o[l].astype(BF16), wrt_hi, wrt_lo, brt, n_ctx_tiles)
        lq = x1.shape[1]
        n_tok = b * lq
        dest, pad_slots, block_e, n_valid, n_slots = _route_tables(route.reshape(n_tok, LANES), counts)
        xs_sorted = _dispatch(pad_slots, dest, h2.reshape(n_tok, d), n_slots)
        y_slots = _experts(block_e, n_valid, xs_sorted, w_e1[l], w_e3[l], w_e2[l])
        res = _combine(x1, y_slots, dest, route, gt2, g_final[None], n_ctx_tiles, not last, last)
        if last:
            out = res
        else:
            xs, xs_ctx = res, None
    return out
```

```python
import functools
import math

import jax
import jax.numpy as jnp
from jax import lax
from jax.experimental import pallas as pl
from jax.experimental.pallas import tpu as pltpu

F32 = jnp.float32
BF16 = jnp.bfloat16

N_HEADS = 8
HEAD_DIM = 64
V_DIM = 2 * HEAD_DIM
V_ROWS = V_DIM + 16
GRID_W = 64
EPS = 1e-6
RNN_BLOCKS = 8
RNN_BW = 128
CONV_W = 4
CONV_LEFT = 2
LRU_C = 8.0
ROPE_BASE = 10000.0
N_EXPERTS = 16
N_GROUPS = 4
EXPERTS_PER_GROUP = N_EXPERTS // N_GROUPS
TOP_K = 2

LANES = 128
SUBLANES = 8
ROW_TILE = 256
SCAN_CHUNK = 128
ATTN_TQ = 256
ATTN_CTX_TQ = 256
ATTN_TK = 512
MOE_ROWS = 512
DISPATCH_ROWS = 512
MOD_ROWS = 16
NEG_BIG = -1e30
LOG2_E = math.log2(math.e)


def _cparams(sem, vmem_mib):
    return pltpu.CompilerParams(dimension_semantics=sem, vmem_limit_bytes=vmem_mib * 1024 * 1024)


def _sigmoid(x):
    return 1.0 / (1.0 + jnp.exp(-x))


def _gelu_tanh(x):
    return 0.5 * x * (1.0 + jnp.tanh(math.sqrt(2.0 / math.pi) * (x + 0.044715 * (x * x * x))))


def _rms(x, g):
    return x * lax.rsqrt(jnp.mean(x * x, axis=-1, keepdims=True) + EPS) * g


def _adaln_kernel(c_ref, w_ref, b_ref, o_ref):
    c = c_ref[...]
    a = c * _sigmoid(c)
    o_ref[...] = jnp.dot(a, w_ref[...], preferred_element_type=F32,
                         precision=lax.Precision.HIGHEST) + b_ref[...]


def _adaln(cond, w_mod, b_mod):
    depth, d, d6 = w_mod.shape
    n = d6 // d
    return pl.pallas_call(
        _adaln_kernel,
        grid=(depth, n),
        in_specs=[pl.BlockSpec((MOD_ROWS, d), lambda l, j: (0, 0)),
                  pl.BlockSpec((None, d, d), lambda l, j: (l, 0, j)),
                  pl.BlockSpec((None, 1, d), lambda l, j: (l, 0, j))],
        out_specs=pl.BlockSpec((None, MOD_ROWS, d), lambda l, j: (l, 0, j)),
        out_shape=jax.ShapeDtypeStruct((depth, MOD_ROWS, d6), F32),
        compiler_params=_cparams(("arbitrary", "arbitrary"), 32),
        name="adaln",
    )(cond, w_mod, b_mod.reshape(depth, 1, d6))


def _stream_tile(x_ref, xc_ref, n_ctx_tiles):
    if xc_ref is None:
        return x_ref[...]
    return jnp.where(pl.program_id(1) < n_ctx_tiles, xc_ref[...], x_ref[...])


def _stream_specs(x, x_ctx, n_ctx_tiles):
    d = x.shape[-1]
    if x_ctx is None:
        return [pl.BlockSpec((None, ROW_TILE, d), lambda bi, i: (bi, i, 0))], [x]
    return ([pl.BlockSpec((None, ROW_TILE, d), lambda bi, i: (bi, jnp.maximum(i - n_ctx_tiles, 0), 0)),
             pl.BlockSpec((None, ROW_TILE, d), lambda bi, i: (bi, jnp.minimum(i, n_ctx_tiles - 1), 0))],
            [x, x_ctx])


def _inproj_kernel(*refs, d, split, n_ctx_tiles):
    if split:
        (x_ref, xc_ref, sh_ref, sc_ref, g_ref, w_ref, cos_ref, sin_ref,
         k_ref, v_ref, q_ref, rx_ref, rg_ref, gr_ref, ga_ref) = refs
    else:
        (x_ref, sh_ref, sc_ref, g_ref, w_ref, cos_ref, sin_ref,
         k_ref, v_ref, q_ref, rx_ref, rg_ref, gr_ref, ga_ref) = refs
        xc_ref = None
    x = _stream_tile(x_ref, xc_ref, n_ctx_tiles)
    h = (_rms(x, g_ref[...]) * (1.0 + sc_ref[...]) + sh_ref[...]).astype(BF16)
    cosv = cos_ref[...]
    sinv = sin_ref[...]

    def proj(p):
        return jnp.dot(h, w_ref[:, p * d:(p + 1) * d], preferred_element_type=F32)

    def rope_store(res, ref, scale):
        for hh in range(N_HEADS):
            xs = res[:, hh * LANES:(hh + 1) * LANES]
            out = xs * cosv + pltpu.roll(xs, LANES // 2, 1) * sinv
            ref[:, hh * LANES:(hh + 1) * LANES] = (out * scale).astype(BF16)

    rope_store(proj(0), k_ref, 1.0)
    vt = proj(1).T.astype(BF16)
    ones = jnp.ones((V_ROWS - V_DIM, vt.shape[1]), BF16)
    for hh in range(N_HEADS):
        v_ref[hh * V_ROWS:hh * V_ROWS + V_DIM, :] = vt[hh * V_DIM:(hh + 1) * V_DIM, :]
        v_ref[hh * V_ROWS + V_DIM:(hh + 1) * V_ROWS, :] = ones
    rope_store(proj(2), q_ref, HEAD_DIM ** -0.5 * LOG2_E)
    rx_ref[...] = proj(3)
    rg_ref[...] = proj(4)
    gr_ref[...] = proj(5)
    ga_ref[...] = proj(6)


def _inproj(x, x_ctx, sh, sc, g, w, cos_t, sin_t, n_ctx_tiles):
    b, _, d = x.shape
    lt = x.shape[1] + (0 if x_ctx is None else x_ctx.shape[1])
    nt = lt // ROW_TILE
    mod_map = lambda bi, i: (jnp.where(i < n_ctx_tiles, b, bi), 0, 0)
    tok = pl.BlockSpec((None, ROW_TILE, d), lambda bi, i: (bi, i, 0))
    x_specs, x_args = _stream_specs(x, x_ctx, n_ctx_tiles)
    out_bf = jax.ShapeDtypeStruct((b, lt, d), BF16)
    out_f = jax.ShapeDtypeStruct((b, lt, d), F32)
    return pl.pallas_call(
        functools.partial(_inproj_kernel, d=d, split=x_ctx is not None, n_ctx_tiles=n_ctx_tiles),
        grid=(b, nt),
        in_specs=x_specs + [
                  pl.BlockSpec((None, 1, d), mod_map),
                  pl.BlockSpec((None, 1, d), mod_map),
                  pl.BlockSpec((1, d), lambda bi, i: (0, 0)),
                  pl.BlockSpec(w.shape, lambda bi, i: (0, 0)),
                  pl.BlockSpec((ROW_TILE, LANES), lambda bi, i: (i, 0)),
                  pl.BlockSpec((ROW_TILE, LANES), lambda bi, i: (i, 0))],
        out_specs=[tok, pl.BlockSpec((None, N_HEADS * V_ROWS, ROW_TILE), lambda bi, i: (bi, 0, i))] + [tok] * 5,
        out_shape=[out_bf, jax.ShapeDtypeStruct((b, N_HEADS * V_ROWS, lt), BF16), out_bf,
                   out_f, out_f, out_f, out_f],
        compiler_params=_cparams(("parallel", "arbitrary"), 56),
        name="inproj",
    )(*x_args, sh, sc, g, w, cos_t, sin_t)


def _scan_chunk(i, n_ctx, n_chunks, reverse):
    if not reverse:
        return i
    return jnp.where(i < n_ctx, n_ctx - 1 - i, n_chunks - 1 - (i - n_ctx))


def _scan_kernel(*refs, reverse, tt, n_ctx, n_chunks):
    if reverse:
        (rx_ref, prev_ref, next_ref, cw_ref, cb_ref, wg_ref, bg_ref, c_ref, hf_ref,
         out_ref, ext_scr, a_scr, b_scr, h_scr) = refs
    else:
        (rx_ref, prev_ref, next_ref, cw_ref, cb_ref, wg_ref, bg_ref, c_ref,
         out_ref, ext_scr, a_scr, b_scr, h_scr) = refs
        hf_ref = None
    i = pl.program_id(0)
    c = _scan_chunk(i, n_ctx, n_chunks, reverse)
    seq_start = jnp.logical_or(c == 0, c == n_ctx)
    seq_end = jnp.logical_or(c == n_ctx - 1, c == n_chunks - 1)

    ext_scr[0:CONV_LEFT] = jnp.where(seq_start, 0.0, prev_ref[...])
    ext_scr[CONV_LEFT:CONV_LEFT + tt] = rx_ref[...]
    ext_scr[CONV_LEFT + tt:CONV_LEFT + tt + 1] = jnp.where(seq_end, 0.0, next_ref[...])

    rows = tt * SUBLANES
    for n in range(RNN_BLOCKS):
        sl = slice(n * RNN_BW, (n + 1) * RNN_BW)
        u = cb_ref[:, sl] + cw_ref[0:1, sl] * ext_scr[0:tt, :, sl]
        for j in range(1, CONV_W):
            u = u + cw_ref[j:j + 1, sl] * ext_scr[j:j + tt, :, sl]
        u2 = u.reshape(rows, RNN_BW)
        z = jnp.dot(u2.astype(BF16), wg_ref[n], preferred_element_type=F32) + bg_ref[n]
        r = _sigmoid(z[:, :RNN_BW])
        ig = _sigmoid(z[:, RNN_BW:])
        a = jnp.exp(-c_ref[:, sl] * r)
        bb = jnp.sqrt(1.0 - a * a) * ig * u2
        a_scr[:, :, sl] = a.reshape(tt, SUBLANES, RNN_BW)
        b_scr[:, :, sl] = bb.reshape(tt, SUBLANES, RNN_BW)

    @pl.when(i == 0)
    def _():
        h_scr[...] = jnp.zeros_like(h_scr)

    def step(s, h):
        t = tt - 1 - s if reverse else s
        h = a_scr[t] * h + b_scr[t]
        if reverse:
            out_ref[t] = hf_ref[t] + h
        else:
            out_ref[t] = h
        return h

    h_scr[...] = lax.fori_loop(0, tt, step, h_scr[...], unroll=8)


def _scan(rx_t, cw, cb, wg, bg, cdec, hf, n_ctx, reverse):
    lt, b, d = rx_t.shape
    tt = SCAN_CHUNK
    n_chunks = lt // tt
    chunk = lambda i: _scan_chunk(i, n_ctx, n_chunks, reverse)
    blk = pl.BlockSpec((tt, b, d), lambda i: (chunk(i), 0, 0))
    const2 = lambda i: (0, 0)
    const3 = lambda i: (0, 0, 0)
    in_specs = [blk,
                pl.BlockSpec((CONV_LEFT, b, d), lambda i: (jnp.maximum(chunk(i) * (tt // CONV_LEFT) - 1, 0), 0, 0)),
                pl.BlockSpec((1, b, d), lambda i: (jnp.minimum((chunk(i) + 1) * tt, lt - 1), 0, 0)),
                pl.BlockSpec(cw.shape, const2),
                pl.BlockSpec(cb.shape, const2),
                pl.BlockSpec(wg.shape, const3),
                pl.BlockSpec(bg.shape, const3),
                pl.BlockSpec(cdec.shape, const2)]
    args = [rx_t, rx_t, rx_t, cw, cb, wg, bg, cdec]
    if reverse:
        in_specs.append(blk)
        args.append(hf)
    return pl.pallas_call(
        functools.partial(_scan_kernel, reverse=reverse, tt=tt, n_ctx=n_ctx, n_chunks=n_chunks),
        grid=(n_chunks,),
        in_specs=in_specs,
        out_specs=blk,
        out_shape=jax.ShapeDtypeStruct((lt, b, d), F32),
        scratch_shapes=[pltpu.VMEM((tt + CONV_W - 1, b, d), F32),
                        pltpu.VMEM((tt, b, d), F32),
                        pltpu.VMEM((tt, b, d), F32),
                        pltpu.VMEM((b, d), F32)],
        compiler_params=_cparams(("arbitrary",), 56),
        name="scan_bwd" if reverse else "scan_fwd",
    )(*args)


def _attn_passes(lam, q_refs, k_ref, vt_ref, g_ref, o_ref, *, write, read, tq, w_chunks, r_chunks, out_scale):
    n2 = 2 * tq
    if write is not None:
        s_w, m_w = write
        q = jnp.concatenate([r[...] for r in q_refs], axis=0) if len(q_refs) > 1 else q_refs[0][...]
        lane = lax.broadcasted_iota(jnp.int32, q.shape, 1)
        first = (lane // (HEAD_DIM // 2)) % 2 == 0
        zero = jnp.zeros_like(q)
        qs = jnp.concatenate([jnp.where(first, q, zero), jnp.where(first, zero, q)], axis=0)
        m8 = jnp.full((SUBLANES, n2), NEG_BIG, F32)
    if read is not None:
        s_r, m_r = read
        m = m_r[...]
        acc = jnp.zeros((V_ROWS, n2), F32)
    for j in range(max(len(w_chunks) if write is not None else 0, len(r_chunks) if read is not None else 0)):
        if write is not None and j < len(w_chunks):
            start, size = w_chunks[j]
            st = lax.dot_general(k_ref[start:start + size, :], qs, (((1,), (1,)), ((), ())),
                                 preferred_element_type=F32)
            s_w[start:start + size, :] = st
            m8 = jnp.maximum(m8, jnp.max(st.reshape(size // SUBLANES, SUBLANES, n2), axis=0))
        if read is not None and j < len(r_chunks):
            start, size = r_chunks[j]
            p = jnp.exp2((s_r[start:start + size, :] - m).astype(BF16))
            acc = acc + jnp.dot(vt_ref[:, start:start + size], p, preferred_element_type=F32)
    if write is not None:
        m_w[...] = jnp.max(m8, axis=0, keepdims=True)
    if read is not None:
        on = acc[:V_DIM] * (1.0 / acc[V_DIM:V_DIM + 1])
        dt = on[:, :tq] - lam * on[:, tq:]
        yt = dt * lax.rsqrt(jnp.mean(dt * dt, axis=0, keepdims=True) + EPS)
        o_ref[...] = (yt.T * g_ref[...] * out_scale).astype(BF16)


def _attn_pipe_kernel(lam_ref, *refs, tq, nq, n_q_refs, chunks, out_scale):
    q_refs = refs[:n_q_refs]
    k_ref, vt_ref, g_ref, o_ref, s_a, s_b, m_a, m_b = refs[n_q_refs:]
    s = pl.program_id(2)
    run = functools.partial(_attn_passes, lam_ref[0], q_refs, k_ref, vt_ref, g_ref, o_ref, tq=tq,
                            w_chunks=chunks, r_chunks=chunks, out_scale=out_scale)
    buf_a, buf_b = (s_a, m_a), (s_b, m_b)
    inner = jnp.logical_and(s > 0, s < nq)
    even = s % 2 == 0

    @pl.when(s == 0)
    def _():
        run(write=buf_a, read=None)

    @pl.when(jnp.logical_and(inner, even))
    def _():
        run(write=buf_a, read=buf_b)

    @pl.when(jnp.logical_and(inner, jnp.logical_not(even)))
    def _():
        run(write=buf_b, read=buf_a)

    @pl.when(s == nq)
    def _():
        run(write=None, read=buf_a if (nq - 1) % 2 == 0 else buf_b)


def _attn_single_kernel(lam_ref, q_ref, k_ref, vt_ref, g_ref, o_ref, s_a, m_a, *, tq, chunks, out_scale):
    run = functools.partial(_attn_passes, lam_ref[0], (q_ref,), k_ref, vt_ref, g_ref, o_ref, tq=tq,
                            w_chunks=chunks, r_chunks=chunks, out_scale=out_scale)
    run(write=(s_a, m_a), read=None)
    run(write=None, read=(s_a, m_a))


def _key_chunks(n_ctx_rows, lt):
    ctx = [(s, min(ATTN_TK, n_ctx_rows - s)) for s in range(0, n_ctx_rows, ATTN_TK)]
    lat = [(s, min(ATTN_TK, lt - s)) for s in range(n_ctx_rows, lt, ATTN_TK)]
    return tuple(ctx), tuple(ctx + lat)


def _attention(lam, q, k, vt, g_sub, n_ctx_rows, with_ctx_queries, out_scale):
    b, lt, d = q.shape
    tq, tqc = ATTN_TQ, ATTN_CTX_TQ
    ctx_q_tiles = n_ctx_rows // tqc
    n_sub = tq // tqc
    nq = (lt - n_ctx_rows) // tq
    ctx_chunks, all_chunks = _key_chunks(n_ctx_rows, lt)
    n2 = 2 * tq
    lam_in = pl.BlockSpec(memory_space=pltpu.SMEM)
    g_in = pl.BlockSpec((1, V_DIM), lambda bi, h, s: (0, 0))
    lat = pl.pallas_call(
        functools.partial(_attn_pipe_kernel, tq=tq, nq=nq, n_q_refs=n_sub, chunks=all_chunks,
                          out_scale=out_scale),
        grid=(b, N_HEADS, nq + 1),
        in_specs=[lam_in] + [
                  pl.BlockSpec((None, tqc, V_DIM),
                               lambda bi, h, s, j=j: (bi, jnp.minimum(s, nq - 1) * n_sub + ctx_q_tiles + j, h))
                  for j in range(n_sub)] + [
                  pl.BlockSpec((None, lt, V_DIM), lambda bi, h, s: (bi, 0, h)),
                  pl.BlockSpec((None, V_ROWS, lt), lambda bi, h, s: (bi, h, 0)),
                  g_in],
        out_specs=pl.BlockSpec((None, tq, V_DIM), lambda bi, h, s: (bi, jnp.maximum(s - 1, 0), h)),
        out_shape=jax.ShapeDtypeStruct((b, nq * tq, d), BF16),
        scratch_shapes=[pltpu.VMEM((lt, n2), F32), pltpu.VMEM((lt, n2), F32),
                        pltpu.VMEM((1, n2), F32), pltpu.VMEM((1, n2), F32)],
        compiler_params=_cparams(("parallel", "parallel", "arbitrary"), 56),
        name="diff_attn",
    )(lam, *([q] * n_sub), k, vt, g_sub)
    if not with_ctx_queries:
        return lat, None
    ctx = pl.pallas_call(
        functools.partial(_attn_single_kernel, tq=tqc, chunks=ctx_chunks, out_scale=out_scale),
        grid=(b, N_HEADS, ctx_q_tiles),
        in_specs=[lam_in,
                  pl.BlockSpec((None, tqc, V_DIM), lambda bi, h, s: (bi, s, h)),
                  pl.BlockSpec((None, n_ctx_rows, V_DIM), lambda bi, h, s: (bi, 0, h)),
                  pl.BlockSpec((None, V_ROWS, n_ctx_rows), lambda bi, h, s: (bi, h, 0)),
                  g_in],
        out_specs=pl.BlockSpec((None, tqc, V_DIM), lambda bi, h, s: (bi, s, h)),
        out_shape=jax.ShapeDtypeStruct((b, n_ctx_rows, d), BF16),
        scratch_shapes=[pltpu.VMEM((n_ctx_rows, 2 * tqc), F32), pltpu.VMEM((1, 2 * tqc), F32)],
        compiler_params=_cparams(("parallel", "parallel", "arbitrary"), 32),
        name="diff_attn_ctx",
    )(lam, q, k, vt, g_sub)
    return lat, ctx


def _postmix_kernel(*refs, has_ctx, split, n_ctx_tiles):
    refs = list(refs)
    x_ref = refs.pop(0)
    xc_ref = refs.pop(0) if split else None
    at_ref = refs.pop(0)
    atc_ref = refs.pop(0) if has_ctx else None
    (y_ref, rg_ref, gr_ref, ga_ref, gt1_ref, sh2_ref, sc2_ref, g2_ref,
     wr_ref, wa_ref, wo_ref, wrh_ref, wrl_ref, brt_ref, x1_ref, h2_ref, route_ref, cnt_ref, run_scr) = refs
    x = _stream_tile(x_ref, xc_ref, n_ctx_tiles)
    at = _stream_tile(at_ref, atc_ref, n_ctx_tiles)
    yg = (y_ref[...].astype(F32) * _gelu_tanh(rg_ref[...])).astype(BF16)
    o_r = jnp.dot(yg, wr_ref[...], preferred_element_type=F32)
    o_a = jnp.dot(at, wa_ref[...], preferred_element_type=F32)
    mix = (_sigmoid(gr_ref[...]) * o_r + _sigmoid(ga_ref[...]) * o_a).astype(BF16)
    x1 = x + gt1_ref[...] * jnp.dot(mix, wo_ref[...], preferred_element_type=F32)
    x1_ref[...] = x1
    h2 = _rms(x1, g2_ref[...]) * (1.0 + sc2_ref[...]) + sh2_ref[...]
    h2_ref[...] = h2

    h_hi = h2.astype(BF16)
    h_lo = (h2 - h_hi.astype(F32)).astype(BF16)
    logits = (jnp.dot(h_hi, wrh_ref[...], preferred_element_type=F32)
              + jnp.dot(h_lo, wrh_ref[...], preferred_element_type=F32)
              + jnp.dot(h_hi, wrl_ref[...], preferred_element_type=F32)) + brt_ref[...]
    lane = lax.broadcasted_iota(jnp.int32, logits.shape, 1)
    lane_f = lane.astype(F32)
    valid = lane < N_EXPERTS
    lg = jnp.where(valid, logits, NEG_BIG)
    e = jnp.exp(lg - jnp.max(lg, axis=-1, keepdims=True))
    e = jnp.where(valid, e, 0.0)
    p = e / jnp.sum(e, axis=-1, keepdims=True)
    p = jnp.where(valid, p, -1.0)
    grp = lane // EXPERTS_PER_GROUP
    best = jnp.max(jnp.where(grp == 0, p, -1.0), axis=-1, keepdims=True)
    best_g = jnp.zeros_like(best, dtype=jnp.int32)
    for gi in range(1, N_GROUPS):
        gm = jnp.max(jnp.where(grp == gi, p, -1.0), axis=-1, keepdims=True)
        better = gm > best
        best_g = jnp.where(better, gi, best_g)
        best = jnp.where(better, gm, best)
    pg = jnp.where(grp == best_g, p, -1.0)
    big = float(LANES)
    v1 = jnp.max(pg, axis=-1, keepdims=True)
    i1 = jnp.min(jnp.where(pg == v1, lane_f, big), axis=-1, keepdims=True)
    pg2 = jnp.where(lane_f == i1, -1.0, pg)
    v2 = jnp.max(pg2, axis=-1, keepdims=True)
    i2 = jnp.min(jnp.where(pg2 == v2, lane_f, big), axis=-1, keepdims=True)
    den = v1 + v2

    @pl.when(jnp.logical_and(pl.program_id(0) == 0, pl.program_id(1) == 0))
    def _():
        run_scr[...] = jnp.zeros_like(run_scr)

    rows = logits.shape[0]
    oh1 = (lane_f == i1).astype(F32)
    oh2 = (lane_f == i2).astype(F32)
    both = oh1 + oh2
    tri = (lax.broadcasted_iota(jnp.int32, (rows, rows), 1)
           < lax.broadcasted_iota(jnp.int32, (rows, rows), 0)).astype(BF16)
    before = jnp.dot(tri, both.astype(BF16), preferred_element_type=F32) + run_scr[...]
    r1 = jnp.sum(oh1 * before, axis=-1, keepdims=True)
    r2 = jnp.sum(oh2 * before, axis=-1, keepdims=True)
    run = run_scr[...] + jnp.sum(both, axis=0, keepdims=True)
    run_scr[...] = run
    cnt_ref[...] = jnp.broadcast_to(run, cnt_ref.shape)

    route = jnp.where(lane == 0, i1, jnp.where(lane == 1, i2, jnp.where(lane == 2, v1 / den, jnp.where(
        lane == 3, v2 / den, jnp.where(lane == 4, r1, r2)))))
    route_ref[...] = route


def _postmix(x, x_ctx, attn, attn_ctx, y, rg, gr, ga, gt1, sh2, sc2, g2, wr, wa, wo, wrh, wrl, brt, n_ctx_tiles):
    b, _, d = y.shape
    lt = y.shape[1]
    has_ctx = attn_ctx is not None
    off = 0 if has_ctx else n_ctx_tiles
    lq = lt - off * ROW_TILE
    full = pl.BlockSpec((None, ROW_TILE, d), lambda bi, i: (bi, i + off, 0))
    part = pl.BlockSpec((None, ROW_TILE, d), lambda bi, i: (bi, i, 0))
    mod = pl.BlockSpec((None, 1, d), lambda bi, i: (jnp.where(i + off < n_ctx_tiles, b, bi), 0, 0))
    const = lambda bi, i: (0, 0)
    if x_ctx is None:
        x_specs, x_args = [full], [x]
    else:
        x_specs, x_args = _stream_specs(x, x_ctx, n_ctx_tiles)
    at_specs, at_args = _stream_specs(attn, attn_ctx, n_ctx_tiles)
    return pl.pallas_call(
        functools.partial(_postmix_kernel, has_ctx=has_ctx, split=x_ctx is not None, n_ctx_tiles=n_ctx_tiles),
        grid=(b, lq // ROW_TILE),
        in_specs=x_specs + at_specs + [full, full, full, full, mod, mod, mod,
                                      pl.BlockSpec((1, d), const),
                                      pl.BlockSpec((d, d), const), pl.BlockSpec((d, d), const),
                                      pl.BlockSpec((d, d), const),
                                      pl.BlockSpec((d, LANES), const), pl.BlockSpec((d, LANES), const),
                                      pl.BlockSpec((1, LANES), const)],
        out_specs=[part, part, pl.BlockSpec((None, ROW_TILE, LANES), lambda bi, i: (bi, i, 0)),
                   pl.BlockSpec((SUBLANES, LANES), const)],
        out_shape=[jax.ShapeDtypeStruct((b, lq, d), F32),
                   jax.ShapeDtypeStruct((b, lq, d), F32),
                   jax.ShapeDtypeStruct((b, lq, LANES), F32),
                   jax.ShapeDtypeStruct((SUBLANES, LANES), F32)],
        scratch_shapes=[pltpu.VMEM((1, LANES), F32)],
        compiler_params=_cparams(("arbitrary", "arbitrary"), 48),
        name="postmix",
    )(*x_args, *at_args, y, rg, gr, ga, gt1, sh2, sc2, g2, wr, wa, wo, wrh, wrl, brt)


def _dispatch_kernel(pad_ref, dest_ref, h_ref, xs_hbm, zrows, sem, *, n_pad):
    i = pl.program_id(0)
    rows = h_ref.shape[0]

    @pl.when(i == 0)
    def _():
        zrows[...] = jnp.zeros_like(zrows)
        for c in range(n_pad // MOE_ROWS):
            def fill(r, carry):
                pltpu.make_async_copy(zrows.at[pl.ds(r, 1)],
                                      xs_hbm.at[pl.ds(pad_ref[c * MOE_ROWS + r], 1)], sem).start()
                return carry

            lax.fori_loop(0, MOE_ROWS, fill, 0, unroll=8)
            pltpu.make_async_copy(zrows, xs_hbm.at[pl.ds(0, MOE_ROWS)], sem).wait()

    def send(r, carry):
        for k in range(TOP_K):
            pltpu.make_async_copy(h_ref.at[pl.ds(r, 1)],
                                  xs_hbm.at[pl.ds(dest_ref[0, TOP_K * r + k], 1)], sem).start()
        return carry

    lax.fori_loop(0, rows, send, 0, unroll=8)
    for k in range(TOP_K):
        pltpu.make_async_copy(h_ref, xs_hbm.at[pl.ds(0, rows)], sem).wait()


def _dispatch(pad_slots, dest, h2, n_slots):
    t, d = h2.shape
    rows = DISPATCH_ROWS
    return pl.pallas_call(
        functools.partial(_dispatch_kernel, n_pad=pad_slots.shape[0]),
        grid_spec=pltpu.PrefetchScalarGridSpec(
            num_scalar_prefetch=1,
            grid=(t // rows,),
            in_specs=[pl.BlockSpec((None, 1, TOP_K * rows), lambda i, pad: (i, 0, 0), memory_space=pltpu.SMEM),
                      pl.BlockSpec((rows, d), lambda i, pad: (i, 0))],
            out_specs=pl.BlockSpec(memory_space=pl.ANY),
            scratch_shapes=[pltpu.VMEM((MOE_ROWS, d), F32), pltpu.SemaphoreType.DMA(())]),
        out_shape=jax.ShapeDtypeStruct((n_slots, d), F32),
        compiler_params=_cparams(("arbitrary",), 32),
        name="dispatch",
    )(pad_slots, dest.reshape(t // rows, 1, TOP_K * rows), h2)


def _expert_kernel(be_ref, nv_ref, x_ref, w1_ref, w3_ref, w2_ref, y_ref, w1_b, w3_b, w2_b):
    i = pl.program_id(0)

    @pl.when(i < nv_ref[0])
    def _():
        @pl.when(jnp.logical_or(i == 0, be_ref[i] != be_ref[jnp.maximum(i - 1, 0)]))
        def _():
            w1_b[...] = w1_ref[...].astype(BF16)
            w3_b[...] = w3_ref[...].astype(BF16)
            w2_b[...] = w2_ref[...].astype(BF16)

        xb = x_ref[...].astype(BF16)
        a1 = jnp.dot(xb, w1_b[...], preferred_element_type=F32)
        a3 = jnp.dot(xb, w3_b[...], preferred_element_type=F32)
        mid = (a1 * _sigmoid(a1) * a3).astype(BF16)
        y_ref[...] = jnp.dot(mid, w2_b[...], preferred_element_type=F32)

    @pl.when(i >= nv_ref[0])
    def _():
        y_ref[...] = jnp.zeros_like(y_ref)


def _experts(block_e, n_valid, xs, w1, w3, w2):
    n_slots, d = xs.shape
    de = w1.shape[-1]
    wspec = lambda shp: pl.BlockSpec((None,) + shp, lambda i, be, nv: (be[i], 0, 0))
    return pl.pallas_call(
        _expert_kernel,
        grid_spec=pltpu.PrefetchScalarGridSpec(
            num_scalar_prefetch=2,
            grid=(n_slots // MOE_ROWS,),
            in_specs=[pl.BlockSpec((MOE_ROWS, d), lambda i, be, nv: (jnp.minimum(i, nv[0] - 1), 0)),
                      wspec((d, de)), wspec((d, de)), wspec((de, d))],
            out_specs=pl.BlockSpec((MOE_ROWS, d), lambda i, be, nv: (i, 0)),
            scratch_shapes=[pltpu.VMEM((d, de), BF16), pltpu.VMEM((d, de), BF16), pltpu.VMEM((de, d), BF16)]),
        out_shape=jax.ShapeDtypeStruct((n_slots, d), F32),
        compiler_params=_cparams(("arbitrary",), 56),
        name="experts",
    )(block_e, n_valid, xs, w1, w3, w2)


def _route_tables(route, counts):
    expert = route[:, 0:TOP_K].astype(jnp.int32).reshape(-1)
    rank = route[:, 4:4 + TOP_K].astype(jnp.int32).reshape(-1)
    a = expert.shape[0]
    sizes = counts[0, :N_EXPERTS].astype(jnp.int32)
    padded = (sizes + MOE_ROWS - 1) // MOE_ROWS * MOE_ROWS
    pend = jnp.cumsum(padded)
    pstarts = pend - padded
    dest = (pstarts[expert] + rank).astype(jnp.int32)
    n_blocks = -(-a // MOE_ROWS) + N_EXPERTS
    n_slots = n_blocks * MOE_ROWS
    n_valid = (pend[-1] // MOE_ROWS).astype(jnp.int32)
    blk = jnp.minimum(jnp.arange(n_blocks, dtype=jnp.int32), n_valid - 1)
    block_e = jnp.minimum(jnp.sum(blk[:, None] * MOE_ROWS >= pend[None, :], axis=1),
                          N_EXPERTS - 1).astype(jnp.int32)
    gap_start = jnp.concatenate([pstarts + sizes, pend[-1:]]).astype(jnp.int32)
    gap_len = jnp.concatenate([padded - sizes, n_slots - pend[-1:]]).astype(jnp.int32)
    gap_end = jnp.cumsum(gap_len)
    j = jnp.arange(n_slots - a, dtype=jnp.int32)
    seg = jnp.sum(j[:, None] >= gap_end[None, :], axis=1)
    pad_slots = (gap_start[seg] + j - (gap_end - gap_len)[seg]).astype(jnp.int32)
    return dest, pad_slots, block_e, n_valid.reshape(1), n_slots


def _combine_kernel(dcur_ref, dnxt_ref, x_ref, route_ref, gt2_ref, g_ref, y_hbm, o_ref, ybuf, sems, *, final):
    g = pl.program_id(0)
    n = pl.num_programs(0)
    rows = x_ref.shape[0]
    slot = g % 2

    def fetch(tbl_ref, s):
        def body(r, carry):
            for k in range(TOP_K):
                pltpu.make_async_copy(y_hbm.at[pl.ds(tbl_ref[0, TOP_K * r + k], 1)],
                                      ybuf.at[s, k, pl.ds(r, 1)], sems.at[s]).start()
            return carry

        lax.fori_loop(0, rows, body, 0, unroll=8)

    @pl.when(g == 0)
    def _():
        fetch(dcur_ref, 0)

    @pl.when(g + 1 < n)
    def _():
        fetch(dnxt_ref, 1 - slot)

    for k in range(TOP_K):
        pltpu.make_async_copy(y_hbm.at[pl.ds(0, rows)], ybuf.at[slot, k], sems.at[slot]).wait()

    rt = route_ref[...]
    y = rt[:, 2:3] * ybuf[slot, 0] + rt[:, 3:4] * ybuf[slot, 1]
    x2 = x_ref[...] + gt2_ref[...] * y
    o_ref[...] = _rms(x2, g_ref[...]) if final else x2


def _combine(x1, y_slots, dest, route, gt2, g_final, n_ctx_tiles, has_ctx, final):
    b, lq, d = x1.shape
    nt = lq // ROW_TILE
    n_tiles = b * nt
    ctx_tiles = n_ctx_tiles if has_ctx else 0
    tok = lambda w: pl.BlockSpec((None, ROW_TILE, w), lambda g: (g // nt, g % nt, 0))
    tbl = lambda f: pl.BlockSpec((None, 1, TOP_K * ROW_TILE), lambda g: (f(g), 0, 0), memory_space=pltpu.SMEM)
    dest3 = dest.reshape(n_tiles, 1, TOP_K * ROW_TILE)
    return pl.pallas_call(
        functools.partial(_combine_kernel, final=final),
        grid=(n_tiles,),
        in_specs=[tbl(lambda g: g), tbl(lambda g: jnp.minimum(g + 1, n_tiles - 1)),
                  tok(d), tok(LANES),
                  pl.BlockSpec((None, 1, d), lambda g: (jnp.where(g % nt < ctx_tiles, b, g // nt), 0, 0)),
                  pl.BlockSpec((1, d), lambda g: (0, 0)),
                  pl.BlockSpec(memory_space=pl.ANY)],
        out_specs=tok(d),
        out_shape=jax.ShapeDtypeStruct((b, lq, d), F32),
        scratch_shapes=[pltpu.VMEM((2, TOP_K, ROW_TILE, d), F32), pltpu.SemaphoreType.DMA((2,))],
        compiler_params=_cparams(("arbitrary",), 32),
        name="combine",
    )(dest3, dest3, x1, route, gt2, g_final, y_slots)


def _rope_tables(n_ctx, seq):
    n_pairs = HEAD_DIM // 4
    inv = ROPE_BASE ** (-jnp.arange(n_pairs, dtype=F32) / n_pairs)
    rows = seq // GRID_W
    r = jnp.repeat(jnp.arange(rows, dtype=F32), GRID_W)
    col = jnp.tile(jnp.arange(GRID_W, dtype=F32), rows)
    ang = jnp.concatenate([r[:, None] * inv, col[:, None] * inv], axis=-1)
    cos = jnp.concatenate([jnp.ones((n_ctx, HEAD_DIM // 2), F32), jnp.cos(ang)], axis=0)
    sin = jnp.concatenate([jnp.zeros((n_ctx, HEAD_DIM // 2), F32), jnp.sin(ang)], axis=0)
    return jnp.tile(cos, (1, 4)), jnp.concatenate([-sin, -sin, sin, sin], axis=-1)


def _qk_perm():
    lane = jnp.arange(LANES)
    g, i = lane // 32, lane % 32
    within = (g % 2) * HEAD_DIM + 2 * i + g // 2
    return (jnp.arange(N_HEADS)[:, None] * LANES + within[None, :]).reshape(-1)


def kernel(x, c, ctx, c_ctx, w_mod, b_mod, g_norm1, g_norm2, w_in, conv_w, conv_b, lru_wa, lru_ba, lru_wx,
           lru_bx, lru_lambda, diff_lambda, g_subln, w_rnn_proj, w_attn_proj, w_o, w_router, b_router,
           w_e1, w_e3, w_e2, g_final):
    b, seq, d = x.shape
    n_ctx = ctx.shape[1]
    depth = w_mod.shape[0]
    n_ctx_tiles = n_ctx // ROW_TILE
    assert b == SUBLANES and d == N_HEADS * V_DIM and n_ctx % ROW_TILE == 0 and seq % ROW_TILE == 0

    cond = jnp.zeros((MOD_ROWS, d), F32).at[:b].set(c).at[b].set(c_ctx)
    mods = _adaln(cond, w_mod, b_mod).reshape(depth, MOD_ROWS, 6, 1, d)
    cos_t, sin_t = _rope_tables(n_ctx, seq)
    perm = _qk_perm()
    wrt = jnp.zeros((d, LANES), F32).at[:, :N_EXPERTS].set(w_router)
    wrt_hi = wrt.astype(BF16)
    wrt_lo = (wrt - wrt_hi.astype(F32)).astype(BF16)
    brt = jnp.zeros((1, LANES), F32).at[0, :N_EXPERTS].set(b_router)

    xs, xs_ctx = x, ctx
    out = None
    for l in range(depth):
        last = l == depth - 1
        lam_init = 0.8 - 0.6 * math.exp(-0.3 * l)
        sh1, sc1, gt1, sh2, sc2, gt2 = [mods[l, :, j] for j in range(6)]
        wl = w_in[l]
        w_perm = jnp.concatenate(
            [wl[:, d:2 * d][:, perm], wl[:, 2 * d:3 * d], wl[:, 3 * d:4 * d][:, perm],
             wl[:, 0:d], wl[:, 4 * d:]], axis=1).astype(BF16)
        k, v, q, rx, rg, gr, ga = _inproj(xs, xs_ctx, sh1, sc1, g_norm1[l][None], w_perm, cos_t, sin_t,
                                          n_ctx_tiles)

        lq1, lk1, lq2, lk2 = diff_lambda[l].astype(F32)
        lam = (jnp.exp(jnp.sum(lq1 * lk1)) - jnp.exp(jnp.sum(lq2 * lk2)) + lam_init).reshape(1)
        attn, attn_ctx = _attention(lam, q, k, v, g_subln[l][None], n_ctx, not last, 1.0 - lam_init)

        rx_t = jnp.transpose(rx, (1, 0, 2))
        cdec = LRU_C * jax.nn.softplus(-lru_lambda[l].astype(F32))
        n_ctx_chunks = n_ctx // SCAN_CHUNK
        hf = None
        for direction in range(2):
            wg = jnp.concatenate([lru_wa[l, direction], lru_wx[l, direction]], axis=-1).astype(BF16)
            bg = jnp.concatenate([lru_ba[l, direction].reshape(RNN_BLOCKS, 1, RNN_BW),
                                  lru_bx[l, direction].reshape(RNN_BLOCKS, 1, RNN_BW)], axis=-1)
            hf = _scan(rx_t, conv_w[l], conv_b[l][None], wg, bg, cdec[direction][None], hf,
                       n_ctx_chunks, reverse=direction == 1)
        y = jnp.transpose(hf, (1, 0, 2)).astype(BF16)

        x1, h2, route, counts = _postmix(xs, xs_ctx, attn, attn_ctx, y, rg, gr, ga, gt1, sh2, sc2,
                                         g_norm2[l][None],
                                         w_rnn_proj[l].astype(BF16), w_attn_proj[l].astype(BF16),
                                         w_o[l].astype(BF16), wrt_hi, wrt_lo, brt, n_ctx_tiles)
        lq = x1.shape[1]
        n_tok = b * lq
        dest, pad_slots, block_e, n_valid, n_slots = _route_tables(route.reshape(n_tok, LANES), counts)
        xs_sorted = _dispatch(pad_slots, dest, h2.reshape(n_tok, d), n_slots)
        y_slots = _experts(block_e, n_valid, xs_sorted, w_e1[l], w_e3[l], w_e2[l])
        res = _combine(x1, y_slots, dest, route, gt2, g_final[None], n_ctx_tiles, not last, last)
        if last:
            out = res
        else:
            xs, xs_ctx = res, None
    return out
```

```python
import functools
import math

import jax
import jax.numpy as jnp
from jax import lax
from jax.experimental import pallas as pl
from jax.experimental.pallas import tpu as pltpu

F32 = jnp.float32
BF16 = jnp.bfloat16

N_HEADS = 8
HEAD_DIM = 64
V_DIM = 2 * HEAD_DIM
V_ROWS = V_DIM + 16
GRID_W = 64
EPS = 1e-6
RNN_BLOCKS = 8
RNN_BW = 128
CONV_W = 4
CONV_LEFT = 2
LRU_C = 8.0
ROPE_BASE = 10000.0
N_EXPERTS = 16
N_GROUPS = 4
EXPERTS_PER_GROUP = N_EXPERTS // N_GROUPS
TOP_K = 2

LANES = 128
SUBLANES = 8
ROW_TILE = 256
SCAN_CHUNK = 128
ATTN_TQ = 256
ATTN_CTX_TQ = 256
ATTN_TK = 512
MOE_ROWS = 512
DISPATCH_ROWS = 512
MOD_ROWS = 16
NEG_BIG = -1e30
LOG2_E = math.log2(math.e)


def _cparams(sem, vmem_mib):
    return pltpu.CompilerParams(dimension_semantics=sem, vmem_limit_bytes=vmem_mib * 1024 * 1024)


def _sigmoid(x):
    return 1.0 / (1.0 + jnp.exp(-x))


def _gelu_tanh(x):
    return 0.5 * x * (1.0 + jnp.tanh(math.sqrt(2.0 / math.pi) * (x + 0.044715 * (x * x * x))))


def _rms(x, g):
    return x * lax.rsqrt(jnp.mean(x * x, axis=-1, keepdims=True) + EPS) * g


def _adaln_kernel(c_ref, w_ref, b_ref, o_ref):
    c = c_ref[...]
    a = c * _sigmoid(c)
    o_ref[...] = jnp.dot(a, w_ref[...], preferred_element_type=F32,
                         precision=lax.Precision.HIGHEST) + b_ref[...]


def _adaln(cond, w_mod, b_mod):
    depth, d, d6 = w_mod.shape
    n = d6 // d
    return pl.pallas_call(
        _adaln_kernel,
        grid=(depth, n),
        in_specs=[pl.BlockSpec((MOD_ROWS, d), lambda l, j: (0, 0)),
                  pl.BlockSpec((None, d, d), lambda l, j: (l, 0, j)),
                  pl.BlockSpec((None, 1, d), lambda l, j: (l, 0, j))],
        out_specs=pl.BlockSpec((None, MOD_ROWS, d), lambda l, j: (l, 0, j)),
        out_shape=jax.ShapeDtypeStruct((depth, MOD_ROWS, d6), F32),
        compiler_params=_cparams(("arbitrary", "arbitrary"), 32),
        name="adaln",
    )(cond, w_mod, b_mod.reshape(depth, 1, d6))


def _stream_tile(x_ref, xc_ref, is_ctx):
    if xc_ref is None:
        return x_ref[...]
    return jnp.where(is_ctx, xc_ref[...], x_ref[...])


def _stream_specs(x, x_ctx, n_ctx_tiles, wrap=lambda f: f):
    d = x.shape[-1]
    if x_ctx is None:
        return [pl.BlockSpec((None, ROW_TILE, d), wrap(lambda bi, i: (bi, i, 0)))], [x]
    return ([pl.BlockSpec((None, ROW_TILE, d), wrap(lambda bi, i: (bi, jnp.maximum(i - n_ctx_tiles, 0), 0))),
             pl.BlockSpec((None, ROW_TILE, d), wrap(lambda bi, i: (bi, jnp.minimum(i, n_ctx_tiles - 1), 0)))],
            [x, x_ctx])


def _inproj_kernel(*refs, d, split, n_ctx_tiles):
    if split:
        (x_ref, xc_ref, sh_ref, sc_ref, g_ref, w_ref, cos_ref, sin_ref,
         k_ref, v_ref, q_ref, rx_ref, rg_ref, gr_ref, ga_ref) = refs
    else:
        (x_ref, sh_ref, sc_ref, g_ref, w_ref, cos_ref, sin_ref,
         k_ref, v_ref, q_ref, rx_ref, rg_ref, gr_ref, ga_ref) = refs
        xc_ref = None
    x = _stream_tile(x_ref, xc_ref, pl.program_id(1) < n_ctx_tiles)
    h = (_rms(x, g_ref[...]) * (1.0 + sc_ref[...]) + sh_ref[...]).astype(BF16)
    cosv = cos_ref[...]
    sinv = sin_ref[...]

    def proj(p):
        return jnp.dot(h, w_ref[:, p * d:(p + 1) * d], preferred_element_type=F32)

    def rope_store(res, ref, scale):
        for hh in range(N_HEADS):
            xs = res[:, hh * LANES:(hh + 1) * LANES]
            out = xs * cosv + pltpu.roll(xs, LANES // 2, 1) * sinv
            ref[:, hh * LANES:(hh + 1) * LANES] = (out * scale).astype(BF16)

    rope_store(proj(0), k_ref, 1.0)
    vt = proj(1).T.astype(BF16)
    ones = jnp.ones((V_ROWS - V_DIM, vt.shape[1]), BF16)
    for hh in range(N_HEADS):
        v_ref[hh * V_ROWS:hh * V_ROWS + V_DIM, :] = vt[hh * V_DIM:(hh + 1) * V_DIM, :]
        v_ref[hh * V_ROWS + V_DIM:(hh + 1) * V_ROWS, :] = ones
    rope_store(proj(2), q_ref, HEAD_DIM ** -0.5 * LOG2_E)
    rx_ref[...] = proj(3)
    rg_ref[...] = proj(4)
    gr_ref[...] = proj(5)
    ga_ref[...] = proj(6)


def _inproj(x, x_ctx, sh, sc, g, w, cos_t, sin_t, n_ctx_tiles):
    b, _, d = x.shape
    lt = x.shape[1] + (0 if x_ctx is None else x_ctx.shape[1])
    nt = lt // ROW_TILE
    mod_map = lambda bi, i: (jnp.where(i < n_ctx_tiles, b, bi), 0, 0)
    tok = pl.BlockSpec((None, ROW_TILE, d), lambda bi, i: (bi, i, 0))
    x_specs, x_args = _stream_specs(x, x_ctx, n_ctx_tiles)
    out_bf = jax.ShapeDtypeStruct((b, lt, d), BF16)
    out_f = jax.ShapeDtypeStruct((b, lt, d), F32)
    return pl.pallas_call(
        functools.partial(_inproj_kernel, d=d, split=x_ctx is not None, n_ctx_tiles=n_ctx_tiles),
        grid=(b, nt),
        in_specs=x_specs + [
                  pl.BlockSpec((None, 1, d), mod_map),
                  pl.BlockSpec((None, 1, d), mod_map),
                  pl.BlockSpec((1, d), lambda bi, i: (0, 0)),
                  pl.BlockSpec(w.shape, lambda bi, i: (0, 0)),
                  pl.BlockSpec((ROW_TILE, LANES), lambda bi, i: (i, 0)),
                  pl.BlockSpec((ROW_TILE, LANES), lambda bi, i: (i, 0))],
        out_specs=[tok, pl.BlockSpec((None, N_HEADS * V_ROWS, ROW_TILE), lambda bi, i: (bi, 0, i))] + [tok] * 5,
        out_shape=[out_bf, jax.ShapeDtypeStruct((b, N_HEADS * V_ROWS, lt), BF16), out_bf,
                   out_f, out_f, out_f, out_f],
        compiler_params=_cparams(("parallel", "arbitrary"), 56),
        name="inproj",
    )(*x_args, sh, sc, g, w, cos_t, sin_t)


def _scan_chunk(i, n_ctx, n_chunks, reverse):
    if not reverse:
        return i
    return jnp.where(i < n_ctx, n_ctx - 1 - i, n_chunks - 1 - (i - n_ctx))


def _scan_kernel(*refs, reverse, tt, n_ctx, n_chunks):
    if reverse:
        (rx_ref, prev_ref, next_ref, cw_ref, cb_ref, wg_ref, bg_ref, c_ref, hf_ref,
         out_ref, ext_scr, a_scr, b_scr, h_scr) = refs
    else:
        (rx_ref, prev_ref, next_ref, cw_ref, cb_ref, wg_ref, bg_ref, c_ref,
         out_ref, ext_scr, a_scr, b_scr, h_scr) = refs
        hf_ref = None
    i = pl.program_id(0)
    c = _scan_chunk(i, n_ctx, n_chunks, reverse)
    seq_start = jnp.logical_or(c == 0, c == n_ctx)
    seq_end = jnp.logical_or(c == n_ctx - 1, c == n_chunks - 1)

    ext_scr[0:CONV_LEFT] = jnp.where(seq_start, 0.0, prev_ref[...])
    ext_scr[CONV_LEFT:CONV_LEFT + tt] = rx_ref[...]
    ext_scr[CONV_LEFT + tt:CONV_LEFT + tt + 1] = jnp.where(seq_end, 0.0, next_ref[...])

    rows = tt * SUBLANES
    for n in range(RNN_BLOCKS):
        sl = slice(n * RNN_BW, (n + 1) * RNN_BW)
        u = cb_ref[:, sl] + cw_ref[0:1, sl] * ext_scr[0:tt, :, sl]
        for j in range(1, CONV_W):
            u = u + cw_ref[j:j + 1, sl] * ext_scr[j:j + tt, :, sl]
        u2 = u.reshape(rows, RNN_BW)
        z = jnp.dot(u2.astype(BF16), wg_ref[n], preferred_element_type=F32) + bg_ref[n]
        r = _sigmoid(z[:, :RNN_BW])
        ig = _sigmoid(z[:, RNN_BW:])
        a = jnp.exp(-c_ref[:, sl] * r)
        bb = jnp.sqrt(1.0 - a * a) * ig * u2
        a_scr[:, :, sl] = a.reshape(tt, SUBLANES, RNN_BW)
        b_scr[:, :, sl] = bb.reshape(tt, SUBLANES, RNN_BW)

    @pl.when(i == 0)
    def _():
        h_scr[...] = jnp.zeros_like(h_scr)

    def step(s, h):
        t = tt - 1 - s if reverse else s
        h = a_scr[t] * h + b_scr[t]
        if reverse:
            out_ref[t] = hf_ref[t] + h
        else:
            out_ref[t] = h
        return h

    h_scr[...] = lax.fori_loop(0, tt, step, h_scr[...], unroll=8)


def _scan(rx_t, cw, cb, wg, bg, cdec, hf, n_ctx, reverse):
    lt, b, d = rx_t.shape
    tt = SCAN_CHUNK
    n_chunks = lt // tt
    chunk = lambda i: _scan_chunk(i, n_ctx, n_chunks, reverse)
    blk = pl.BlockSpec((tt, b, d), lambda i: (chunk(i), 0, 0))
    const2 = lambda i: (0, 0)
    const3 = lambda i: (0, 0, 0)
    in_specs = [blk,
                pl.BlockSpec((CONV_LEFT, b, d), lambda i: (jnp.maximum(chunk(i) * (tt // CONV_LEFT) - 1, 0), 0, 0)),
                pl.BlockSpec((1, b, d), lambda i: (jnp.minimum((chunk(i) + 1) * tt, lt - 1), 0, 0)),
                pl.BlockSpec(cw.shape, const2),
                pl.BlockSpec(cb.shape, const2),
                pl.BlockSpec(wg.shape, const3),
                pl.BlockSpec(bg.shape, const3),
                pl.BlockSpec(cdec.shape, const2)]
    args = [rx_t, rx_t, rx_t, cw, cb, wg, bg, cdec]
    if reverse:
        in_specs.append(blk)
        args.append(hf)
    return pl.pallas_call(
        functools.partial(_scan_kernel, reverse=reverse, tt=tt, n_ctx=n_ctx, n_chunks=n_chunks),
        grid=(n_chunks,),
        in_specs=in_specs,
        out_specs=blk,
        out_shape=jax.ShapeDtypeStruct((lt, b, d), F32),
        scratch_shapes=[pltpu.VMEM((tt + CONV_W - 1, b, d), F32),
                        pltpu.VMEM((tt, b, d), F32),
                        pltpu.VMEM((tt, b, d), F32),
                        pltpu.VMEM((b, d), F32)],
        compiler_params=_cparams(("arbitrary",), 56),
        name="scan_bwd" if reverse else "scan_fwd",
    )(*args)


def _attn_passes(lam, q_refs, k_ref, vt_ref, g_ref, o_ref, *, write, read, tq, w_chunks, r_chunks, out_scale):
    n2 = 2 * tq
    if write is not None:
        s_w, m_w = write
        q = jnp.concatenate([r[...] for r in q_refs], axis=0) if len(q_refs) > 1 else q_refs[0][...]
        lane = lax.broadcasted_iota(jnp.int32, q.shape, 1)
        first = (lane // (HEAD_DIM // 2)) % 2 == 0
        zero = jnp.zeros_like(q)
        qs = jnp.concatenate([jnp.where(first, q, zero), jnp.where(first, zero, q)], axis=0)
        m8 = jnp.full((SUBLANES, n2), NEG_BIG, F32)
    if read is not None:
        s_r, m_r = read
        m = m_r[...]
        acc = jnp.zeros((V_ROWS, n2), F32)
    for j in range(max(len(w_chunks) if write is not None else 0, len(r_chunks) if read is not None else 0)):
        if write is not None and j < len(w_chunks):
            start, size = w_chunks[j]
            st = lax.dot_general(k_ref[start:start + size, :], qs, (((1,), (1,)), ((), ())),
                                 preferred_element_type=F32)
            s_w[start:start + size, :] = st
            m8 = jnp.maximum(m8, jnp.max(st.reshape(size // SUBLANES, SUBLANES, n2), axis=0))
        if read is not None and j < len(r_chunks):
            start, size = r_chunks[j]
            p = jnp.exp2(s_r[start:start + size, :] - m).astype(BF16)
            acc = acc + jnp.dot(vt_ref[:, start:start + size], p, preferred_element_type=F32)
    if write is not None:
        m_w[...] = jnp.max(m8, axis=0, keepdims=True)
    if read is not None:
        on = acc[:V_DIM] * (1.0 / acc[V_DIM:V_DIM + 1])
        dt = on[:, :tq] - lam * on[:, tq:]
        yt = dt * lax.rsqrt(jnp.mean(dt * dt, axis=0, keepdims=True) + EPS)
        o_ref[...] = (yt.T * g_ref[...] * out_scale).astype(BF16)


def _attn_pipe_kernel(lam_ref, *refs, tq, nq, n_q_refs, chunks, out_scale):
    q_refs = refs[:n_q_refs]
    k_ref, vt_ref, g_ref, o_ref, s_a, s_b, m_a, m_b = refs[n_q_refs:]
    s = pl.program_id(2)
    run = functools.partial(_attn_passes, lam_ref[0], q_refs, k_ref, vt_ref, g_ref, o_ref, tq=tq,
                            w_chunks=chunks, r_chunks=chunks, out_scale=out_scale)
    buf_a, buf_b = (s_a, m_a), (s_b, m_b)
    inner = jnp.logical_and(s > 0, s < nq)
    even = s % 2 == 0

    @pl.when(s == 0)
    def _():
        run(write=buf_a, read=None)

    @pl.when(jnp.logical_and(inner, even))
    def _():
        run(write=buf_a, read=buf_b)

    @pl.when(jnp.logical_and(inner, jnp.logical_not(even)))
    def _():
        run(write=buf_b, read=buf_a)

    @pl.when(s == nq)
    def _():
        run(write=None, read=buf_a if (nq - 1) % 2 == 0 else buf_b)


def _attn_single_kernel(lam_ref, q_ref, k_ref, vt_ref, g_ref, o_ref, s_a, m_a, *, tq, chunks, out_scale):
    run = functools.partial(_attn_passes, lam_ref[0], (q_ref,), k_ref, vt_ref, g_ref, o_ref, tq=tq,
                            w_chunks=chunks, r_chunks=chunks, out_scale=out_scale)
    run(write=(s_a, m_a), read=None)
    run(write=None, read=(s_a, m_a))


def _key_chunks(n_ctx_rows, lt):
    ctx = [(s, min(ATTN_TK, n_ctx_rows - s)) for s in range(0, n_ctx_rows, ATTN_TK)]
    lat = [(s, min(ATTN_TK, lt - s)) for s in range(n_ctx_rows, lt, ATTN_TK)]
    return tuple(ctx), tuple(ctx + lat)


def _attention(lam, q, k, vt, g_sub, n_ctx_rows, with_ctx_queries, out_scale):
    b, lt, d = q.shape
    tq, tqc = ATTN_TQ, ATTN_CTX_TQ
    ctx_q_tiles = n_ctx_rows // tqc
    n_sub = tq // tqc
    nq = (lt - n_ctx_rows) // tq
    ctx_chunks, all_chunks = _key_chunks(n_ctx_rows, lt)
    n2 = 2 * tq
    lam_in = pl.BlockSpec(memory_space=pltpu.SMEM)
    g_in = pl.BlockSpec((1, V_DIM), lambda bi, h, s: (0, 0))
    lat = pl.pallas_call(
        functools.partial(_attn_pipe_kernel, tq=tq, nq=nq, n_q_refs=n_sub, chunks=all_chunks,
                          out_scale=out_scale),
        grid=(b, N_HEADS, nq + 1),
        in_specs=[lam_in] + [
                  pl.BlockSpec((None, tqc, V_DIM),
                               lambda bi, h, s, j=j: (bi, jnp.minimum(s, nq - 1) * n_sub + ctx_q_tiles + j, h))
                  for j in range(n_sub)] + [
                  pl.BlockSpec((None, lt, V_DIM), lambda bi, h, s: (bi, 0, h)),
                  pl.BlockSpec((None, V_ROWS, lt), lambda bi, h, s: (bi, h, 0)),
                  g_in],
        out_specs=pl.BlockSpec((None, tq, V_DIM), lambda bi, h, s: (bi, jnp.maximum(s - 1, 0), h)),
        out_shape=jax.ShapeDtypeStruct((b, nq * tq, d), BF16),
        scratch_shapes=[pltpu.VMEM((lt, n2), F32), pltpu.VMEM((lt, n2), F32),
                        pltpu.VMEM((1, n2), F32), pltpu.VMEM((1, n2), F32)],
        compiler_params=_cparams(("parallel", "parallel", "arbitrary"), 56),
        name="diff_attn",
    )(lam, *([q] * n_sub), k, vt, g_sub)
    if not with_ctx_queries:
        return lat, None
    ctx = pl.pallas_call(
        functools.partial(_attn_single_kernel, tq=tqc, chunks=ctx_chunks, out_scale=out_scale),
        grid=(b, N_HEADS, ctx_q_tiles),
        in_specs=[lam_in,
                  pl.BlockSpec((None, tqc, V_DIM), lambda bi, h, s: (bi, s, h)),
                  pl.BlockSpec((None, n_ctx_rows, V_DIM), lambda bi, h, s: (bi, 0, h)),
                  pl.BlockSpec((None, V_ROWS, n_ctx_rows), lambda bi, h, s: (bi, h, 0)),
                  g_in],
        out_specs=pl.BlockSpec((None, tqc, V_DIM), lambda bi, h, s: (bi, s, h)),
        out_shape=jax.ShapeDtypeStruct((b, n_ctx_rows, d), BF16),
        scratch_shapes=[pltpu.VMEM((n_ctx_rows, 2 * tqc), F32), pltpu.VMEM((1, 2 * tqc), F32)],
        compiler_params=_cparams(("parallel", "parallel", "arbitrary"), 32),
        name="diff_attn_ctx",
    )(lam, q, k, vt, g_sub)
    return lat, ctx


def _postmix_kernel(*refs, has_ctx, split, n_ctx_tiles, nt, n_tiles):
    refs = list(refs)
    x_ref = refs.pop(0)
    xc_ref = refs.pop(0) if split else None
    at_ref = refs.pop(0)
    atc_ref = refs.pop(0) if has_ctx else None
    (y_ref, rg_ref, gr_ref, ga_ref, gt1_ref, sh2_ref, sc2_ref, g2_ref,
     wr_ref, wa_ref, wo_ref, wrh_ref, wrl_ref, brt_ref, x1_ref, h2_ref, route_ref, cnt_ref,
     run_scr, hs_scr) = refs
    g = pl.program_id(0)

    @pl.when(g == 0)
    def _():
        run_scr[...] = jnp.zeros_like(run_scr)
        hs_scr[...] = jnp.zeros_like(hs_scr)

    h_hi = hs_scr[0]
    h_lo = hs_scr[1]
    nt_dims = (((1,), (1,)), ((), ()))
    lg = (lax.dot_general(wrh_ref[...], h_hi, nt_dims, preferred_element_type=F32)
          + lax.dot_general(wrh_ref[...], h_lo, nt_dims, preferred_element_type=F32)
          + lax.dot_general(wrl_ref[...], h_hi, nt_dims, preferred_element_type=F32)) + brt_ref[:, 0:1]

    is_ctx = jnp.minimum(g, n_tiles - 1) % nt < n_ctx_tiles
    x = _stream_tile(x_ref, xc_ref, is_ctx)
    at = _stream_tile(at_ref, atc_ref, is_ctx)
    yg = (y_ref[...].astype(F32) * _gelu_tanh(rg_ref[...])).astype(BF16)
    o_r = jnp.dot(yg, wr_ref[...], preferred_element_type=F32)
    o_a = jnp.dot(at, wa_ref[...], preferred_element_type=F32)
    mix = (_sigmoid(gr_ref[...]) * o_r + _sigmoid(ga_ref[...]) * o_a).astype(BF16)
    x1 = x + gt1_ref[...] * jnp.dot(mix, wo_ref[...], preferred_element_type=F32)
    x1_ref[...] = x1
    h2 = _rms(x1, g2_ref[...]) * (1.0 + sc2_ref[...]) + sh2_ref[...]
    h2_ref[...] = h2
    h2_hi = h2.astype(BF16)
    hs_scr[0] = h2_hi
    hs_scr[1] = (h2 - h2_hi.astype(F32)).astype(BF16)

    rows = h_hi.shape[0]
    ex = lax.broadcasted_iota(jnp.int32, lg.shape, 0)
    ex_f = ex.astype(F32)
    e = jnp.exp(lg - jnp.max(lg, axis=0, keepdims=True))
    p = e / jnp.sum(e, axis=0, keepdims=True)
    grp = ex // EXPERTS_PER_GROUP
    best = jnp.max(jnp.where(grp == 0, p, -1.0), axis=0, keepdims=True)
    best_g = jnp.zeros_like(best, dtype=jnp.int32)
    for gi in range(1, N_GROUPS):
        gm = jnp.max(jnp.where(grp == gi, p, -1.0), axis=0, keepdims=True)
        better = gm > best
        best_g = jnp.where(better, gi, best_g)
        best = jnp.where(better, gm, best)
    pg = jnp.where(grp == best_g, p, -1.0)
    big = float(N_EXPERTS)
    v1 = jnp.max(pg, axis=0, keepdims=True)
    i1 = jnp.min(jnp.where(pg == v1, ex_f, big), axis=0, keepdims=True)
    pg2 = jnp.where(ex_f == i1, -1.0, pg)
    v2 = jnp.max(pg2, axis=0, keepdims=True)
    i2 = jnp.min(jnp.where(pg2 == v2, ex_f, big), axis=0, keepdims=True)
    den = v1 + v2

    live = (g > 0).astype(F32)
    oh1 = (ex_f == i1).astype(F32)
    oh2 = (ex_f == i2).astype(F32)
    both = (oh1 + oh2) * live
    tri = (lax.broadcasted_iota(jnp.int32, (rows, rows), 0)
           < lax.broadcasted_iota(jnp.int32, (rows, rows), 1)).astype(BF16)
    before = jnp.dot(both.astype(BF16), tri, preferred_element_type=F32) + run_scr[:, 0:1]
    r1 = jnp.sum(oh1 * before, axis=0, keepdims=True)
    r2 = jnp.sum(oh2 * before, axis=0, keepdims=True)
    run = run_scr[...] + jnp.sum(both, axis=1, keepdims=True)
    run_scr[...] = run
    cnt_ref[...] = run
    row = lax.broadcasted_iota(jnp.int32, route_ref.shape, 0)
    route_ref[...] = jnp.where(row == 0, i1, jnp.where(row == 1, i2, jnp.where(row == 2, v1 / den, jnp.where(
        row == 3, v2 / den, jnp.where(row == 4, r1, r2)))))


def _postmix(x, x_ctx, attn, attn_ctx, y, rg, gr, ga, gt1, sh2, sc2, g2, wr, wa, wo, wrh, wrl, brt, n_ctx_tiles):
    b, _, d = y.shape
    lt = y.shape[1]
    has_ctx = attn_ctx is not None
    off = 0 if has_ctx else n_ctx_tiles
    lq = lt - off * ROW_TILE
    nt = lq // ROW_TILE
    n_tiles = b * nt

    def wrap(f):
        def index_map(g):
            t = jnp.minimum(g, n_tiles - 1)
            return f(t // nt, t % nt)
        return index_map

    full = pl.BlockSpec((None, ROW_TILE, d), wrap(lambda bi, i: (bi, i + off, 0)))
    part = pl.BlockSpec((None, ROW_TILE, d), wrap(lambda bi, i: (bi, i, 0)))
    mod = pl.BlockSpec((None, 1, d), wrap(lambda bi, i: (jnp.where(i + off < n_ctx_tiles, b, bi), 0, 0)))
    const = lambda g: (0, 0)
    if x_ctx is None:
        x_specs, x_args = [full], [x]
    else:
        x_specs, x_args = _stream_specs(x, x_ctx, n_ctx_tiles, wrap)
    at_specs, at_args = _stream_specs(attn, attn_ctx, n_ctx_tiles, wrap)
    return pl.pallas_call(
        functools.partial(_postmix_kernel, has_ctx=has_ctx, split=x_ctx is not None, n_ctx_tiles=n_ctx_tiles,
                          nt=nt, n_tiles=n_tiles),
        grid=(n_tiles + 1,),
        in_specs=x_specs + at_specs + [full, full, full, full, mod, mod, mod,
                                      pl.BlockSpec((1, d), const),
                                      pl.BlockSpec((d, d), const), pl.BlockSpec((d, d), const),
                                      pl.BlockSpec((d, d), const),
                                      pl.BlockSpec((N_EXPERTS, d), const), pl.BlockSpec((N_EXPERTS, d), const),
                                      pl.BlockSpec((N_EXPERTS, LANES), const)],
        out_specs=[part, part, pl.BlockSpec((SUBLANES, ROW_TILE), lambda g: (0, jnp.maximum(g - 1, 0))),
                   pl.BlockSpec((N_EXPERTS, LANES), const)],
        out_shape=[jax.ShapeDtypeStruct((b, lq, d), F32),
                   jax.ShapeDtypeStruct((b, lq, d), F32),
                   jax.ShapeDtypeStruct((SUBLANES, b * lq), F32),
                   jax.ShapeDtypeStruct((N_EXPERTS, LANES), F32)],
        scratch_shapes=[pltpu.VMEM((N_EXPERTS, LANES), F32), pltpu.VMEM((2, ROW_TILE, d), BF16)],
        compiler_params=_cparams(("arbitrary",), 48),
        name="postmix",
    )(*x_args, *at_args, y, rg, gr, ga, gt1, sh2, sc2, g2, wr, wa, wo, wrh, wrl, brt)


def _dispatch_kernel(pad_ref, dest_ref, h_ref, xs_hbm, zrows, sem, *, n_pad):
    i = pl.program_id(0)
    rows = h_ref.shape[0]

    @pl.when(i == 0)
    def _():
        zrows[...] = jnp.zeros_like(zrows)
        for c in range(n_pad // MOE_ROWS):
            def fill(r, carry):
                pltpu.make_async_copy(zrows.at[pl.ds(r, 1)],
                                      xs_hbm.at[pl.ds(pad_ref[c * MOE_ROWS + r], 1)], sem).start()
                return carry

            lax.fori_loop(0, MOE_ROWS, fill, 0, unroll=8)
            pltpu.make_async_copy(zrows, xs_hbm.at[pl.ds(0, MOE_ROWS)], sem).wait()

    def send(r, carry):
        for k in range(TOP_K):
            pltpu.make_async_copy(h_ref.at[pl.ds(r, 1)],
                                  xs_hbm.at[pl.ds(dest_ref[0, TOP_K * r + k], 1)], sem).start()
        return carry

    lax.fori_loop(0, rows, send, 0, unroll=8)
    for k in range(TOP_K):
        pltpu.make_async_copy(h_ref, xs_hbm.at[pl.ds(0, rows)], sem).wait()


def _dispatch(pad_slots, dest, h2, n_slots):
    t, d = h2.shape
    rows = DISPATCH_ROWS
    return pl.pallas_call(
        functools.partial(_dispatch_kernel, n_pad=pad_slots.shape[0]),
        grid_spec=pltpu.PrefetchScalarGridSpec(
            num_scalar_prefetch=1,
            grid=(t // rows,),
            in_specs=[pl.BlockSpec((None, 1, TOP_K * rows), lambda i, pad: (i, 0, 0), memory_space=pltpu.SMEM),
                      pl.BlockSpec((rows, d), lambda i, pad: (i, 0))],
            out_specs=pl.BlockSpec(memory_space=pl.ANY),
            scratch_shapes=[pltpu.VMEM((MOE_ROWS, d), F32), pltpu.SemaphoreType.DMA(())]),
        out_shape=jax.ShapeDtypeStruct((n_slots, d), F32),
        compiler_params=_cparams(("arbitrary",), 32),
        name="dispatch",
    )(pad_slots, dest.reshape(t // rows, 1, TOP_K * rows), h2)


def _expert_kernel(be_ref, nv_ref, x_ref, w1_ref, w3_ref, w2_ref, y_ref, w1_b, w3_b, w2_b):
    i = pl.program_id(0)

    @pl.when(i < nv_ref[0])
    def _():
        @pl.when(jnp.logical_or(i == 0, be_ref[i] != be_ref[jnp.maximum(i - 1, 0)]))
        def _():
            w1_b[...] = w1_ref[...].astype(BF16)
            w3_b[...] = w3_ref[...].astype(BF16)
            w2_b[...] = w2_ref[...].astype(BF16)

        xb = x_ref[...].astype(BF16)
        a1 = jnp.dot(xb, w1_b[...], preferred_element_type=F32)
        a3 = jnp.dot(xb, w3_b[...], preferred_element_type=F32)
        mid = (a1 * _sigmoid(a1) * a3).astype(BF16)
        y_ref[...] = jnp.dot(mid, w2_b[...], preferred_element_type=F32)

    @pl.when(i >= nv_ref[0])
    def _():
        y_ref[...] = jnp.zeros_like(y_ref)


def _experts(block_e, n_valid, xs, w1, w3, w2, layer):
    n_slots, d = xs.shape
    de = w1.shape[-1]
    wspec = lambda shp: pl.BlockSpec((None, None) + shp, lambda i, be, nv: (layer, be[i], 0, 0))
    return pl.pallas_call(
        _expert_kernel,
        grid_spec=pltpu.PrefetchScalarGridSpec(
            num_scalar_prefetch=2,
            grid=(n_slots // MOE_ROWS,),
            in_specs=[pl.BlockSpec((MOE_ROWS, d), lambda i, be, nv: (jnp.minimum(i, nv[0] - 1), 0)),
                      wspec((d, de)), wspec((d, de)), wspec((de, d))],
            out_specs=pl.BlockSpec((MOE_ROWS, d), lambda i, be, nv: (i, 0)),
            scratch_shapes=[pltpu.VMEM((d, de), BF16), pltpu.VMEM((d, de), BF16), pltpu.VMEM((de, d), BF16)]),
        out_shape=jax.ShapeDtypeStruct((n_slots, d), F32),
        compiler_params=_cparams(("arbitrary",), 56),
        name="experts",
    )(block_e, n_valid, xs, w1, w3, w2)


def _route_tables(route, counts):
    expert = route[0:TOP_K].astype(jnp.int32).T.reshape(-1)
    rank = route[4:4 + TOP_K].astype(jnp.int32).T.reshape(-1)
    a = expert.shape[0]
    sizes = counts[:, 0].astype(jnp.int32)
    padded = (sizes + MOE_ROWS - 1) // MOE_ROWS * MOE_ROWS
    pend = jnp.cumsum(padded)
    pstarts = pend - padded
    dest = (pstarts[expert] + rank).astype(jnp.int32)
    n_blocks = -(-a // MOE_ROWS) + N_EXPERTS
    n_slots = n_blocks * MOE_ROWS
    n_valid = (pend[-1] // MOE_ROWS).astype(jnp.int32)
    blk = jnp.minimum(jnp.arange(n_blocks, dtype=jnp.int32), n_valid - 1)
    block_e = jnp.minimum(jnp.sum(blk[:, None] * MOE_ROWS >= pend[None, :], axis=1),
                          N_EXPERTS - 1).astype(jnp.int32)
    gap_start = jnp.concatenate([pstarts + sizes, pend[-1:]]).astype(jnp.int32)
    gap_len = jnp.concatenate([padded - sizes, n_slots - pend[-1:]]).astype(jnp.int32)
    gap_end = jnp.cumsum(gap_len)
    j = jnp.arange(n_slots - a, dtype=jnp.int32)
    seg = jnp.sum(j[:, None] >= gap_end[None, :], axis=1)
    pad_slots = (gap_start[seg] + j - (gap_end - gap_len)[seg]).astype(jnp.int32)
    return dest, pad_slots, block_e, n_valid.reshape(1), n_slots


def _combine_kernel(dcur_ref, dnxt_ref, x_ref, gate_ref, gt2_ref, g_ref, y_hbm, o_ref, ybuf, sems, *, final):
    g = pl.program_id(0)
    n = pl.num_programs(0)
    rows = x_ref.shape[0]
    slot = g % 2

    def fetch(tbl_ref, s):
        def body(r, carry):
            for k in range(TOP_K):
                pltpu.make_async_copy(y_hbm.at[pl.ds(tbl_ref[0, TOP_K * r + k], 1)],
                                      ybuf.at[s, k, pl.ds(r, 1)], sems.at[s]).start()
            return carry

        lax.fori_loop(0, rows, body, 0, unroll=8)

    @pl.when(g == 0)
    def _():
        fetch(dcur_ref, 0)

    @pl.when(g + 1 < n)
    def _():
        fetch(dnxt_ref, 1 - slot)

    for k in range(TOP_K):
        pltpu.make_async_copy(y_hbm.at[pl.ds(0, rows)], ybuf.at[slot, k], sems.at[slot]).wait()

    gate = gate_ref[...]
    y = gate[:, 0:1] * ybuf[slot, 0] + gate[:, 1:2] * ybuf[slot, 1]
    x2 = x_ref[...] + gt2_ref[...] * y
    o_ref[...] = _rms(x2, g_ref[...]) if final else x2


def _combine(x1, y_slots, dest, gates, gt2, g_final, n_ctx_tiles, has_ctx, final):
    b, lq, d = x1.shape
    nt = lq // ROW_TILE
    n_tiles = b * nt
    ctx_tiles = n_ctx_tiles if has_ctx else 0
    tok = lambda w: pl.BlockSpec((None, ROW_TILE, w), lambda g: (g // nt, g % nt, 0))
    tbl = lambda f: pl.BlockSpec((None, 1, TOP_K * ROW_TILE), lambda g: (f(g), 0, 0), memory_space=pltpu.SMEM)
    dest3 = dest.reshape(n_tiles, 1, TOP_K * ROW_TILE)
    return pl.pallas_call(
        functools.partial(_combine_kernel, final=final),
        grid=(n_tiles,),
        in_specs=[tbl(lambda g: g), tbl(lambda g: jnp.minimum(g + 1, n_tiles - 1)),
                  tok(d), pl.BlockSpec((ROW_TILE, TOP_K), lambda g: (g, 0)),
                  pl.BlockSpec((None, 1, d), lambda g: (jnp.where(g % nt < ctx_tiles, b, g // nt), 0, 0)),
                  pl.BlockSpec((1, d), lambda g: (0, 0)),
                  pl.BlockSpec(memory_space=pl.ANY)],
        out_specs=tok(d),
        out_shape=jax.ShapeDtypeStruct((b, lq, d), F32),
        scratch_shapes=[pltpu.VMEM((2, TOP_K, ROW_TILE, d), F32), pltpu.SemaphoreType.DMA((2,))],
        compiler_params=_cparams(("arbitrary",), 32),
        name="combine",
    )(dest3, dest3, x1, gates, gt2, g_final, y_slots)


def _rope_tables(n_ctx, seq):
    n_pairs = HEAD_DIM // 4
    inv = ROPE_BASE ** (-jnp.arange(n_pairs, dtype=F32) / n_pairs)
    rows = seq // GRID_W
    r = jnp.repeat(jnp.arange(rows, dtype=F32), GRID_W)
    col = jnp.tile(jnp.arange(GRID_W, dtype=F32), rows)
    ang = jnp.concatenate([r[:, None] * inv, col[:, None] * inv], axis=-1)
    cos = jnp.concatenate([jnp.ones((n_ctx, HEAD_DIM // 2), F32), jnp.cos(ang)], axis=0)
    sin = jnp.concatenate([jnp.zeros((n_ctx, HEAD_DIM // 2), F32), jnp.sin(ang)], axis=0)
    return jnp.tile(cos, (1, 4)), jnp.concatenate([-sin, -sin, sin, sin], axis=-1)


def _qk_perm():
    lane = jnp.arange(LANES)
    g, i = lane // 32, lane % 32
    within = (g % 2) * HEAD_DIM + 2 * i + g // 2
    return (jnp.arange(N_HEADS)[:, None] * LANES + within[None, :]).reshape(-1)


def kernel(x, c, ctx, c_ctx, w_mod, b_mod, g_norm1, g_norm2, w_in, conv_w, conv_b, lru_wa, lru_ba, lru_wx,
           lru_bx, lru_lambda, diff_lambda, g_subln, w_rnn_proj, w_attn_proj, w_o, w_router, b_router,
           w_e1, w_e3, w_e2, g_final):
    b, seq, d = x.shape
    n_ctx = ctx.shape[1]
    depth = w_mod.shape[0]
    n_ctx_tiles = n_ctx // ROW_TILE
    assert b == SUBLANES and d == N_HEADS * V_DIM and n_ctx % ROW_TILE == 0 and seq % ROW_TILE == 0

    cond = jnp.zeros((MOD_ROWS, d), F32).at[:b].set(c).at[b].set(c_ctx)
    mods = _adaln(cond, w_mod, b_mod).reshape(depth, MOD_ROWS, 6, 1, d)
    cos_t, sin_t = _rope_tables(n_ctx, seq)
    perm = _qk_perm()
    wrt = w_router.T
    wrt_hi = wrt.astype(BF16)
    wrt_lo = (wrt - wrt_hi.astype(F32)).astype(BF16)
    brt = jnp.broadcast_to(b_router.astype(F32)[:, None], (N_EXPERTS, LANES))

    xs, xs_ctx = x, ctx
    out = None
    for l in range(depth):
        last = l == depth - 1
        lam_init = 0.8 - 0.6 * math.exp(-0.3 * l)
        sh1, sc1, gt1, sh2, sc2, gt2 = [mods[l, :, j] for j in range(6)]
        wl = w_in[l]
        w_perm = jnp.concatenate(
            [wl[:, d:2 * d][:, perm], wl[:, 2 * d:3 * d], wl[:, 3 * d:4 * d][:, perm],
             wl[:, 0:d], wl[:, 4 * d:]], axis=1).astype(BF16)
        k, v, q, rx, rg, gr, ga = _inproj(xs, xs_ctx, sh1, sc1, g_norm1[l][None], w_perm, cos_t, sin_t,
                                          n_ctx_tiles)

        lq1, lk1, lq2, lk2 = diff_lambda[l].astype(F32)
        lam = (jnp.exp(jnp.sum(lq1 * lk1)) - jnp.exp(jnp.sum(lq2 * lk2)) + lam_init).reshape(1)
        attn, attn_ctx = _attention(lam, q, k, v, g_subln[l][None], n_ctx, not last, 1.0 - lam_init)

        rx_t = jnp.transpose(rx, (1, 0, 2))
        cdec = LRU_C * jax.nn.softplus(-lru_lambda[l].astype(F32))
        n_ctx_chunks = n_ctx // SCAN_CHUNK
        hf = None
        for direction in range(2):
            wg = jnp.concatenate([lru_wa[l, direction], lru_wx[l, direction]], axis=-1).astype(BF16)
            bg = jnp.concatenate([lru_ba[l, direction].reshape(RNN_BLOCKS, 1, RNN_BW),
                                  lru_bx[l, direction].reshape(RNN_BLOCKS, 1, RNN_BW)], axis=-1)
            hf = _scan(rx_t, conv_w[l], conv_b[l][None], wg, bg, cdec[direction][None], hf,
                       n_ctx_chunks, reverse=direction == 1)
        y = jnp.transpose(hf, (1, 0, 2)).astype(BF16)

        x1, h2, route, counts = _postmix(xs, xs_ctx, attn, attn_ctx, y, rg, gr, ga, gt1, sh2, sc2,
                                         g_norm2[l][None],
                                         w_rnn_proj[l].astype(BF16), w_attn_proj[l].astype(BF16),
                                         w_o[l].astype(BF16), wrt_hi, wrt_lo, brt, n_ctx_tiles)
        lq = x1.shape[1]
        n_tok = b * lq
        dest, pad_slots, block_e, n_valid, n_slots = _route_tables(route, counts)
        xs_sorted = _dispatch(pad_slots, dest, h2.reshape(n_tok, d), n_slots)
        y_slots = _experts(block_e, n_valid, xs_sorted, w_e1, w_e3, w_e2, l)
        res = _combine(x1, y_slots, dest, route[2:2 + TOP_K].T, gt2, g_final[None], n_ctx_tiles, not last, last)
        if last:
            out = res
        else:
            xs, xs_ctx = res, None
    return out
```

```python
import functools
import math

import jax
import jax.numpy as jnp
from jax import lax
from jax.experimental import pallas as pl
from jax.experimental.pallas import tpu as pltpu

F32 = jnp.float32
BF16 = jnp.bfloat16

N_HEADS = 8
HEAD_DIM = 64
V_DIM = 2 * HEAD_DIM
GRID_W = 64
EPS = 1e-6
RNN_BLOCKS = 8
RNN_BW = 128
CONV_W = 4
CONV_LEFT = 2
LRU_C = 8.0
ROPE_BASE = 10000.0
N_EXPERTS = 16
N_GROUPS = 4
EXPERTS_PER_GROUP = N_EXPERTS // N_GROUPS
TOP_K = 2

LANES = 128
SUBLANES = 8
ROW_TILE = 256
SCAN_CHUNK = 128
ATTN_TQ = 256
ATTN_CTX_TQ = 256
ATTN_TK = 512
MOE_ROWS = 512
DISPATCH_ROWS = 1024
MOD_ROWS = 16
NEG_BIG = -1e30
LOG2_E = math.log2(math.e)


def _cparams(sem, vmem_mib):
    return pltpu.CompilerParams(dimension_semantics=sem, vmem_limit_bytes=vmem_mib * 1024 * 1024)


def _sigmoid(x):
    return 0.5 * jnp.tanh(0.5 * x) + 0.5


def _gelu_tanh(x):
    return 0.5 * x * (1.0 + jnp.tanh(math.sqrt(2.0 / math.pi) * (x + 0.044715 * (x * x * x))))


def _rms(x, g):
    return x * lax.rsqrt(jnp.mean(x * x, axis=-1, keepdims=True) + EPS) * g


def _adaln_kernel(c_ref, w_ref, b_ref, o_ref):
    c = c_ref[...]
    a = c * _sigmoid(c)
    o_ref[...] = jnp.dot(a, w_ref[...], preferred_element_type=F32,
                         precision=lax.Precision.HIGHEST) + b_ref[...]


def _adaln(cond, w_mod, b_mod):
    depth, d, d6 = w_mod.shape
    n = d6 // d
    return pl.pallas_call(
        _adaln_kernel,
        grid=(depth, n),
        in_specs=[pl.BlockSpec((MOD_ROWS, d), lambda l, j: (0, 0)),
                  pl.BlockSpec((None, d, d), lambda l, j: (l, 0, j)),
                  pl.BlockSpec((None, 1, d), lambda l, j: (l, 0, j))],
        out_specs=pl.BlockSpec((None, MOD_ROWS, d), lambda l, j: (l, 0, j)),
        out_shape=jax.ShapeDtypeStruct((depth, MOD_ROWS, d6), F32),
        compiler_params=_cparams(("arbitrary", "arbitrary"), 32),
        name="adaln",
    )(cond, w_mod, b_mod.reshape(depth, 1, d6))


def _stream_tile(x_ref, xc_ref, is_ctx):
    if xc_ref is None:
        return x_ref[...]
    return jnp.where(is_ctx, xc_ref[...], x_ref[...])


def _stream_specs(x, x_ctx, n_ctx_tiles, wrap=lambda f: f):
    d = x.shape[-1]
    if x_ctx is None:
        return [pl.BlockSpec((None, ROW_TILE, d), wrap(lambda bi, i: (bi, i, 0)))], [x]
    return ([pl.BlockSpec((None, ROW_TILE, d), wrap(lambda bi, i: (bi, jnp.maximum(i - n_ctx_tiles, 0), 0))),
             pl.BlockSpec((None, ROW_TILE, d), wrap(lambda bi, i: (bi, jnp.minimum(i, n_ctx_tiles - 1), 0)))],
            [x, x_ctx])


def _inproj_kernel(*refs, d, split, n_ctx_tiles):
    if split:
        (x_ref, xc_ref, sh_ref, sc_ref, g_ref, w_ref, cos_ref, sin_ref,
         k_ref, v_ref, q_ref, rx_ref, rg_ref, gr_ref, ga_ref) = refs
    else:
        (x_ref, sh_ref, sc_ref, g_ref, w_ref, cos_ref, sin_ref,
         k_ref, v_ref, q_ref, rx_ref, rg_ref, gr_ref, ga_ref) = refs
        xc_ref = None
    x = _stream_tile(x_ref, xc_ref, pl.program_id(1) < n_ctx_tiles)
    h = (_rms(x, g_ref[...]) * (1.0 + sc_ref[...]) + sh_ref[...]).astype(BF16)
    cosv = cos_ref[...]
    sinv = sin_ref[...]

    def proj(p):
        return jnp.dot(h, w_ref[:, p * d:(p + 1) * d], preferred_element_type=F32)

    def rope_store(res, ref, scale):
        for hh in range(N_HEADS):
            xs = res[:, hh * LANES:(hh + 1) * LANES]
            out = xs * cosv + pltpu.roll(xs, LANES // 2, 1) * sinv
            ref[:, hh * LANES:(hh + 1) * LANES] = (out * scale).astype(BF16)

    rope_store(proj(0), k_ref, 1.0)
    v_ref[...] = proj(1).T.astype(BF16)
    rope_store(proj(2), q_ref, HEAD_DIM ** -0.5 * LOG2_E)
    rx_ref[...] = proj(3)
    rg_ref[...] = proj(4)
    gr_ref[...] = proj(5)
    ga_ref[...] = proj(6)


def _inproj(x, x_ctx, sh, sc, g, w, cos_t, sin_t, n_ctx_tiles):
    b, _, d = x.shape
    lt = x.shape[1] + (0 if x_ctx is None else x_ctx.shape[1])
    nt = lt // ROW_TILE
    mod_map = lambda bi, i: (jnp.where(i < n_ctx_tiles, b, bi), 0, 0)
    tok = pl.BlockSpec((None, ROW_TILE, d), lambda bi, i: (bi, i, 0))
    x_specs, x_args = _stream_specs(x, x_ctx, n_ctx_tiles)
    out_bf = jax.ShapeDtypeStruct((b, lt, d), BF16)
    out_f = jax.ShapeDtypeStruct((b, lt, d), F32)
    return pl.pallas_call(
        functools.partial(_inproj_kernel, d=d, split=x_ctx is not None, n_ctx_tiles=n_ctx_tiles),
        grid=(b, nt),
        in_specs=x_specs + [
                  pl.BlockSpec((None, 1, d), mod_map),
                  pl.BlockSpec((None, 1, d), mod_map),
                  pl.BlockSpec((1, d), lambda bi, i: (0, 0)),
                  pl.BlockSpec(w.shape, lambda bi, i: (0, 0)),
                  pl.BlockSpec((ROW_TILE, LANES), lambda bi, i: (i, 0)),
                  pl.BlockSpec((ROW_TILE, LANES), lambda bi, i: (i, 0))],
        out_specs=[tok, pl.BlockSpec((None, d, ROW_TILE), lambda bi, i: (bi, 0, i))] + [tok] * 5,
        out_shape=[out_bf, jax.ShapeDtypeStruct((b, d, lt), BF16), out_bf, out_f, out_f, out_f, out_f],
        compiler_params=_cparams(("parallel", "arbitrary"), 56),
        name="inproj",
    )(*x_args, sh, sc, g, w, cos_t, sin_t)


def _scan_chunk(i, n_ctx, n_chunks, reverse):
    if not reverse:
        return i
    return jnp.where(i < n_ctx, n_ctx - 1 - i, n_chunks - 1 - (i - n_ctx))


def _scan_kernel(*refs, reverse, tt, n_ctx, n_chunks):
    if reverse:
        (rx_ref, prev_ref, next_ref, cw_ref, cb_ref, wg_ref, bg_ref, c_ref, hf_ref,
         out_ref, ext_scr, a_scr, b_scr, h_scr) = refs
    else:
        (rx_ref, prev_ref, next_ref, cw_ref, cb_ref, wg_ref, bg_ref, c_ref,
         out_ref, ext_scr, a_scr, b_scr, h_scr) = refs
        hf_ref = None
    i = pl.program_id(0)
    c = _scan_chunk(i, n_ctx, n_chunks, reverse)
    seq_start = jnp.logical_or(c == 0, c == n_ctx)
    seq_end = jnp.logical_or(c == n_ctx - 1, c == n_chunks - 1)

    ext_scr[0:CONV_LEFT] = jnp.where(seq_start, 0.0, prev_ref[...])
    ext_scr[CONV_LEFT:CONV_LEFT + tt] = rx_ref[...]
    ext_scr[CONV_LEFT + tt:CONV_LEFT + tt + 1] = jnp.where(seq_end, 0.0, next_ref[...])

    rows = tt * SUBLANES
    for n in range(RNN_BLOCKS):
        sl = slice(n * RNN_BW, (n + 1) * RNN_BW)
        u = cb_ref[:, sl] + cw_ref[0:1, sl] * ext_scr[0:tt, :, sl]
        for j in range(1, CONV_W):
            u = u + cw_ref[j:j + 1, sl] * ext_scr[j:j + tt, :, sl]
        u2 = u.reshape(rows, RNN_BW)
        z = jnp.dot(u2.astype(BF16), wg_ref[n], preferred_element_type=F32) + bg_ref[n]
        r = _sigmoid(z[:, :RNN_BW])
        ig = _sigmoid(z[:, RNN_BW:])
        a = jnp.exp(-c_ref[:, sl] * r)
        bb = jnp.sqrt(1.0 - a * a) * ig * u2
        a_scr[:, :, sl] = a.reshape(tt, SUBLANES, RNN_BW)
        b_scr[:, :, sl] = bb.reshape(tt, SUBLANES, RNN_BW)

    @pl.when(i == 0)
    def _():
        h_scr[...] = jnp.zeros_like(h_scr)

    def step(s, h):
        t = tt - 1 - s if reverse else s
        h = a_scr[t] * h + b_scr[t]
        if reverse:
            out_ref[t] = hf_ref[t] + h
        else:
            out_ref[t] = h
        return h

    h_scr[...] = lax.fori_loop(0, tt, step, h_scr[...], unroll=8)


def _scan(rx_t, cw, cb, wg, bg, cdec, hf, n_ctx, reverse):
    lt, b, d = rx_t.shape
    tt = SCAN_CHUNK
    n_chunks = lt // tt
    chunk = lambda i: _scan_chunk(i, n_ctx, n_chunks, reverse)
    blk = pl.BlockSpec((tt, b, d), lambda i: (chunk(i), 0, 0))
    const2 = lambda i: (0, 0)
    const3 = lambda i: (0, 0, 0)
    in_specs = [blk,
                pl.BlockSpec((CONV_LEFT, b, d), lambda i: (jnp.maximum(chunk(i) * (tt // CONV_LEFT) - 1, 0), 0, 0)),
                pl.BlockSpec((1, b, d), lambda i: (jnp.minimum((chunk(i) + 1) * tt, lt - 1), 0, 0)),
                pl.BlockSpec(cw.shape, const2),
                pl.BlockSpec(cb.shape, const2),
                pl.BlockSpec(wg.shape, const3),
                pl.BlockSpec(bg.shape, const3),
                pl.BlockSpec(cdec.shape, const2)]
    args = [rx_t, rx_t, rx_t, cw, cb, wg, bg, cdec]
    if reverse:
        in_specs.append(blk)
        args.append(hf)
    return pl.pallas_call(
        functools.partial(_scan_kernel, reverse=reverse, tt=tt, n_ctx=n_ctx, n_chunks=n_chunks),
        grid=(n_chunks,),
        in_specs=in_specs,
        out_specs=blk,
        out_shape=jax.ShapeDtypeStruct((lt, b, d), F32),
        scratch_shapes=[pltpu.VMEM((tt + CONV_W - 1, b, d), F32),
                        pltpu.VMEM((tt, b, d), F32),
                        pltpu.VMEM((tt, b, d), F32),
                        pltpu.VMEM((b, d), F32)],
        compiler_params=_cparams(("arbitrary",), 56),
        name="scan_bwd" if reverse else "scan_fwd",
    )(*args)


def _attn_passes(lam, q_refs, k_ref, vt_ref, g_ref, o_ref, *, write, read, tq, w_chunks, r_chunks, out_scale):
    n2 = 2 * tq
    if write is not None:
        s_w, m_w = write
        q = jnp.concatenate([r[...] for r in q_refs], axis=0) if len(q_refs) > 1 else q_refs[0][...]
        lane = lax.broadcasted_iota(jnp.int32, q.shape, 1)
        first = (lane // (HEAD_DIM // 2)) % 2 == 0
        zero = jnp.zeros_like(q)
        qs = jnp.concatenate([jnp.where(first, q, zero), jnp.where(first, zero, q)], axis=0)
        m8 = jnp.full((SUBLANES, n2), NEG_BIG, F32)
    if read is not None:
        s_r, m_r = read
        m = m_r[...]
        l8 = jnp.zeros((SUBLANES, n2), F32)
        acc = jnp.zeros((V_DIM, n2), F32)
    for j in range(max(len(w_chunks) if write is not None else 0, len(r_chunks) if read is not None else 0)):
        if write is not None and j < len(w_chunks):
            start, size = w_chunks[j]
            st = lax.dot_general(k_ref[start:start + size, :], qs, (((1,), (1,)), ((), ())),
                                 preferred_element_type=F32)
            s_w[start:start + size, :] = st
            m8 = jnp.maximum(m8, jnp.max(st.reshape(size // SUBLANES, SUBLANES, n2), axis=0))
        if read is not None and j < len(r_chunks):
            start, size = r_chunks[j]
            p = jnp.exp2(s_r[start:start + size, :] - m)
            l8 = l8 + jnp.sum(p.reshape(size // SUBLANES, SUBLANES, n2), axis=0)
            acc = acc + jnp.dot(vt_ref[:, start:start + size], p.astype(BF16), preferred_element_type=F32)
    if write is not None:
        m_w[...] = jnp.max(m8, axis=0, keepdims=True)
    if read is not None:
        on = acc * (1.0 / jnp.sum(l8, axis=0, keepdims=True))
        dt = on[:, :tq] - lam * on[:, tq:]
        yt = dt * lax.rsqrt(jnp.mean(dt * dt, axis=0, keepdims=True) + EPS)
        o_ref[...] = (yt.T * g_ref[...] * out_scale).astype(BF16)


def _attn_pipe_kernel(lam_ref, *refs, tq, nq, n_q_refs, chunks, out_scale):
    q_refs = refs[:n_q_refs]
    k_ref, vt_ref, g_ref, o_ref, s_a, s_b, m_a, m_b = refs[n_q_refs:]
    s = pl.program_id(2)
    run = functools.partial(_attn_passes, lam_ref[0], q_refs, k_ref, vt_ref, g_ref, o_ref, tq=tq,
                            w_chunks=chunks, r_chunks=chunks, out_scale=out_scale)
    buf_a, buf_b = (s_a, m_a), (s_b, m_b)
    inner = jnp.logical_and(s > 0, s < nq)
    even = s % 2 == 0

    @pl.when(s == 0)
    def _():
        run(write=buf_a, read=None)

    @pl.when(jnp.logical_and(inner, even))
    def _():
        run(write=buf_a, read=buf_b)

    @pl.when(jnp.logical_and(inner, jnp.logical_not(even)))
    def _():
        run(write=buf_b, read=buf_a)

    @pl.when(s == nq)
    def _():
        run(write=None, read=buf_a if (nq - 1) % 2 == 0 else buf_b)


def _attn_single_kernel(lam_ref, q_ref, k_ref, vt_ref, g_ref, o_ref, s_a, m_a, *, tq, chunks, out_scale):
    run = functools.partial(_attn_passes, lam_ref[0], (q_ref,), k_ref, vt_ref, g_ref, o_ref, tq=tq,
                            w_chunks=chunks, r_chunks=chunks, out_scale=out_scale)
    run(write=(s_a, m_a), read=None)
    run(write=None, read=(s_a, m_a))


def _key_chunks(n_ctx_rows, lt):
    ctx = [(s, min(ATTN_TK, n_ctx_rows - s)) for s in range(0, n_ctx_rows, ATTN_TK)]
    lat = [(s, min(ATTN_TK, lt - s)) for s in range(n_ctx_rows, lt, ATTN_TK)]
    return tuple(ctx), tuple(ctx + lat)


def _attention(lam, q, k, vt, g_sub, n_ctx_rows, with_ctx_queries, out_scale):
    b, lt, d = q.shape
    tq, tqc = ATTN_TQ, ATTN_CTX_TQ
    ctx_q_tiles = n_ctx_rows // tqc
    n_sub = tq // tqc
    nq = (lt - n_ctx_rows) // tq
    ctx_chunks, all_chunks = _key_chunks(n_ctx_rows, lt)
    n2 = 2 * tq
    lam_in = pl.BlockSpec(memory_space=pltpu.SMEM)
    g_in = pl.BlockSpec((1, V_DIM), lambda bi, h, s: (0, 0))
    lat = pl.pallas_call(
        functools.partial(_attn_pipe_kernel, tq=tq, nq=nq, n_q_refs=n_sub, chunks=all_chunks,
                          out_scale=out_scale),
        grid=(b, N_HEADS, nq + 1),
        in_specs=[lam_in] + [
                  pl.BlockSpec((None, tqc, V_DIM),
                               lambda bi, h, s, j=j: (bi, jnp.minimum(s, nq - 1) * n_sub + ctx_q_tiles + j, h))
                  for j in range(n_sub)] + [
                  pl.BlockSpec((None, lt, V_DIM), lambda bi, h, s: (bi, 0, h)),
                  pl.BlockSpec((None, V_DIM, lt), lambda bi, h, s: (bi, h, 0)),
                  g_in],
        out_specs=pl.BlockSpec((None, tq, V_DIM), lambda bi, h, s: (bi, jnp.maximum(s - 1, 0), h)),
        out_shape=jax.ShapeDtypeStruct((b, nq * tq, d), BF16),
        scratch_shapes=[pltpu.VMEM((lt, n2), F32), pltpu.VMEM((lt, n2), F32),
                        pltpu.VMEM((1, n2), F32), pltpu.VMEM((1, n2), F32)],
        compiler_params=_cparams(("parallel", "parallel", "arbitrary"), 56),
        name="diff_attn",
    )(lam, *([q] * n_sub), k, vt, g_sub)
    if not with_ctx_queries:
        return lat, None
    ctx = pl.pallas_call(
        functools.partial(_attn_single_kernel, tq=tqc, chunks=ctx_chunks, out_scale=out_scale),
        grid=(b, N_HEADS, ctx_q_tiles),
        in_specs=[lam_in,
                  pl.BlockSpec((None, tqc, V_DIM), lambda bi, h, s: (bi, s, h)),
                  pl.BlockSpec((None, n_ctx_rows, V_DIM), lambda bi, h, s: (bi, 0, h)),
                  pl.BlockSpec((None, V_DIM, n_ctx_rows), lambda bi, h, s: (bi, h, 0)),
                  g_in],
        out_specs=pl.BlockSpec((None, tqc, V_DIM), lambda bi, h, s: (bi, s, h)),
        out_shape=jax.ShapeDtypeStruct((b, n_ctx_rows, d), BF16),
        scratch_shapes=[pltpu.VMEM((n_ctx_rows, 2 * tqc), F32), pltpu.VMEM((1, 2 * tqc), F32)],
        compiler_params=_cparams(("parallel", "parallel", "arbitrary"), 32),
        name="diff_attn_ctx",
    )(lam, q, k, vt, g_sub)
    return lat, ctx


def _postmix_kernel(*refs, has_ctx, split, n_ctx_tiles, nt, n_tiles):
    refs = list(refs)
    x_ref = refs.pop(0)
    xc_ref = refs.pop(0) if split else None
    at_ref = refs.pop(0)
    atc_ref = refs.pop(0) if has_ctx else None
    (y_ref, rg_ref, gr_ref, ga_ref, gt1_ref, sh2_ref, sc2_ref, g2_ref,
     wr_ref, wa_ref, wo_ref, wrh_ref, wrl_ref, brt_ref, x1_ref, h2_ref, route_ref, cnt_ref,
     run_scr, hs_scr) = refs
    g = pl.program_id(0)

    @pl.when(g == 0)
    def _():
        run_scr[...] = jnp.zeros_like(run_scr)
        hs_scr[...] = jnp.zeros_like(hs_scr)

    h_hi = hs_scr[0]
    h_lo = hs_scr[1]
    nt_dims = (((1,), (1,)), ((), ()))
    lg = (lax.dot_general(wrh_ref[...], h_hi, nt_dims, preferred_element_type=F32)
          + lax.dot_general(wrh_ref[...], h_lo, nt_dims, preferred_element_type=F32)
          + lax.dot_general(wrl_ref[...], h_hi, nt_dims, preferred_element_type=F32)) + brt_ref[:, 0:1]

    is_ctx = jnp.minimum(g, n_tiles - 1) % nt < n_ctx_tiles
    x = _stream_tile(x_ref, xc_ref, is_ctx)
    at = _stream_tile(at_ref, atc_ref, is_ctx)
    yg = (y_ref[...].astype(F32) * _gelu_tanh(rg_ref[...])).astype(BF16)
    o_r = jnp.dot(yg, wr_ref[...], preferred_element_type=F32)
    o_a = jnp.dot(at, wa_ref[...], preferred_element_type=F32)
    mix = (_sigmoid(gr_ref[...]) * o_r + _sigmoid(ga_ref[...]) * o_a).astype(BF16)
    x1 = x + gt1_ref[...] * jnp.dot(mix, wo_ref[...], preferred_element_type=F32)
    x1_ref[...] = x1
    h2 = _rms(x1, g2_ref[...]) * (1.0 + sc2_ref[...]) + sh2_ref[...]
    h2_ref[...] = h2
    h2_hi = h2.astype(BF16)
    hs_scr[0] = h2_hi
    hs_scr[1] = (h2 - h2_hi.astype(F32)).astype(BF16)

    rows = h_hi.shape[0]
    ex = lax.broadcasted_iota(jnp.int32, lg.shape, 0)
    ex_f = ex.astype(F32)
    e = jnp.exp(lg - jnp.max(lg, axis=0, keepdims=True))
    p = e / jnp.sum(e, axis=0, keepdims=True)
    grp = ex // EXPERTS_PER_GROUP
    best = jnp.max(jnp.where(grp == 0, p, -1.0), axis=0, keepdims=True)
    best_g = jnp.zeros_like(best, dtype=jnp.int32)
    for gi in range(1, N_GROUPS):
        gm = jnp.max(jnp.where(grp == gi, p, -1.0), axis=0, keepdims=True)
        better = gm > best
        best_g = jnp.where(better, gi, best_g)
        best = jnp.where(better, gm, best)
    pg = jnp.where(grp == best_g, p, -1.0)
    big = float(N_EXPERTS)
    v1 = jnp.max(pg, axis=0, keepdims=True)
    i1 = jnp.min(jnp.where(pg == v1, ex_f, big), axis=0, keepdims=True)
    pg2 = jnp.where(ex_f == i1, -1.0, pg)
    v2 = jnp.max(pg2, axis=0, keepdims=True)
    i2 = jnp.min(jnp.where(pg2 == v2, ex_f, big), axis=0, keepdims=True)
    den = v1 + v2

    live = (g > 0).astype(F32)
    oh1 = (ex_f == i1).astype(F32)
    oh2 = (ex_f == i2).astype(F32)
    both = (oh1 + oh2) * live
    tri = (lax.broadcasted_iota(jnp.int32, (rows, rows), 0)
           < lax.broadcasted_iota(jnp.int32, (rows, rows), 1)).astype(BF16)
    before = jnp.dot(both.astype(BF16), tri, preferred_element_type=F32) + run_scr[:, 0:1]
    r1 = jnp.sum(oh1 * before, axis=0, keepdims=True)
    r2 = jnp.sum(oh2 * before, axis=0, keepdims=True)
    run = run_scr[...] + jnp.sum(both, axis=1, keepdims=True)
    run_scr[...] = run
    cnt_ref[...] = run
    row = lax.broadcasted_iota(jnp.int32, route_ref.shape, 0)
    route_ref[...] = jnp.where(row == 0, i1, jnp.where(row == 1, i2, jnp.where(row == 2, v1 / den, jnp.where(
        row == 3, v2 / den, jnp.where(row == 4, r1, r2)))))


def _postmix(x, x_ctx, attn, attn_ctx, y, rg, gr, ga, gt1, sh2, sc2, g2, wr, wa, wo, wrh, wrl, brt, n_ctx_tiles):
    b, _, d = y.shape
    lt = y.shape[1]
    has_ctx = attn_ctx is not None
    off = 0 if has_ctx else n_ctx_tiles
    lq = lt - off * ROW_TILE
    nt = lq // ROW_TILE
    n_tiles = b * nt

    def wrap(f):
        def index_map(g):
            t = jnp.minimum(g, n_tiles - 1)
            return f(t // nt, t % nt)
        return index_map

    full = pl.BlockSpec((None, ROW_TILE, d), wrap(lambda bi, i: (bi, i + off, 0)))
    part = pl.BlockSpec((None, ROW_TILE, d), wrap(lambda bi, i: (bi, i, 0)))
    mod = pl.BlockSpec((None, 1, d), wrap(lambda bi, i: (jnp.where(i + off < n_ctx_tiles, b, bi), 0, 0)))
    const = lambda g: (0, 0)
    if x_ctx is None:
        x_specs, x_args = [full], [x]
    else:
        x_specs, x_args = _stream_specs(x, x_ctx, n_ctx_tiles, wrap)
    at_specs, at_args = _stream_specs(attn, attn_ctx, n_ctx_tiles, wrap)
    return pl.pallas_call(
        functools.partial(_postmix_kernel, has_ctx=has_ctx, split=x_ctx is not None, n_ctx_tiles=n_ctx_tiles,
                          nt=nt, n_tiles=n_tiles),
        grid=(n_tiles + 1,),
        in_specs=x_specs + at_specs + [full, full, full, full, mod, mod, mod,
                                      pl.BlockSpec((1, d), const),
                                      pl.BlockSpec((d, d), const), pl.BlockSpec((d, d), const),
                                      pl.BlockSpec((d, d), const),
                                      pl.BlockSpec((N_EXPERTS, d), const), pl.BlockSpec((N_EXPERTS, d), const),
                                      pl.BlockSpec((N_EXPERTS, LANES), const)],
        out_specs=[part, part, pl.BlockSpec((SUBLANES, ROW_TILE), lambda g: (0, jnp.maximum(g - 1, 0))),
                   pl.BlockSpec((N_EXPERTS, LANES), const)],
        out_shape=[jax.ShapeDtypeStruct((b, lq, d), F32),
                   jax.ShapeDtypeStruct((b, lq, d), F32),
                   jax.ShapeDtypeStruct((SUBLANES, b * lq), F32),
                   jax.ShapeDtypeStruct((N_EXPERTS, LANES), F32)],
        scratch_shapes=[pltpu.VMEM((N_EXPERTS, LANES), F32), pltpu.VMEM((2, ROW_TILE, d), BF16)],
        compiler_params=_cparams(("arbitrary",), 48),
        name="postmix",
    )(*x_args, *at_args, y, rg, gr, ga, gt1, sh2, sc2, g2, wr, wa, wo, wrh, wrl, brt)


def _dispatch_kernel(pad_ref, dest_ref, h_ref, xs_hbm, zrows, sem, *, n_pad):
    i = pl.program_id(0)
    rows = h_ref.shape[0]

    @pl.when(i == 0)
    def _():
        zrows[...] = jnp.zeros_like(zrows)
        for c in range(n_pad // MOE_ROWS):
            def fill(r, carry):
                pltpu.make_async_copy(zrows.at[pl.ds(r, 1)],
                                      xs_hbm.at[pl.ds(pad_ref[c * MOE_ROWS + r], 1)], sem).start()
                return carry

            lax.fori_loop(0, MOE_ROWS, fill, 0, unroll=8)
            pltpu.make_async_copy(zrows, xs_hbm.at[pl.ds(0, MOE_ROWS)], sem).wait()

    def send(r, carry):
        for k in range(TOP_K):
            pltpu.make_async_copy(h_ref.at[pl.ds(r, 1)],
                                  xs_hbm.at[pl.ds(dest_ref[0, TOP_K * r + k], 1)], sem).start()
        return carry

    lax.fori_loop(0, rows, send, 0, unroll=8)
    for k in range(TOP_K):
        pltpu.make_async_copy(h_ref, xs_hbm.at[pl.ds(0, rows)], sem).wait()


def _dispatch(pad_slots, dest, h2, n_slots):
    t, d = h2.shape
    rows = DISPATCH_ROWS
    return pl.pallas_call(
        functools.partial(_dispatch_kernel, n_pad=pad_slots.shape[0]),
        grid_spec=pltpu.PrefetchScalarGridSpec(
            num_scalar_prefetch=1,
            grid=(t // rows,),
            in_specs=[pl.BlockSpec((None, 1, TOP_K * rows), lambda i, pad: (i, 0, 0), memory_space=pltpu.SMEM),
                      pl.BlockSpec((rows, d), lambda i, pad: (i, 0))],
            out_specs=pl.BlockSpec(memory_space=pl.ANY),
            scratch_shapes=[pltpu.VMEM((MOE_ROWS, d), F32), pltpu.SemaphoreType.DMA(())]),
        out_shape=jax.ShapeDtypeStruct((n_slots, d), F32),
        compiler_params=_cparams(("arbitrary",), 32),
        name="dispatch",
    )(pad_slots, dest.reshape(t // rows, 1, TOP_K * rows), h2)


def _expert_kernel(be_ref, nv_ref, x_ref, w1_ref, w3_ref, w2_ref, y_ref, w1_b, w3_b, w2_b):
    i = pl.program_id(0)

    @pl.when(i < nv_ref[0])
    def _():
        @pl.when(jnp.logical_or(i == 0, be_ref[i] != be_ref[jnp.maximum(i - 1, 0)]))
        def _():
            w1_b[...] = w1_ref[...].astype(BF16)
            w3_b[...] = w3_ref[...].astype(BF16)
            w2_b[...] = w2_ref[...].astype(BF16)

        xb = x_ref[...].astype(BF16)
        a1 = jnp.dot(xb, w1_b[...], preferred_element_type=F32)
        a3 = jnp.dot(xb, w3_b[...], preferred_element_type=F32)
        mid = (a1 * _sigmoid(a1) * a3).astype(BF16)
        y_ref[...] = jnp.dot(mid, w2_b[...], preferred_element_type=F32)

    @pl.when(i >= nv_ref[0])
    def _():
        y_ref[...] = jnp.zeros_like(y_ref)


def _experts(block_e, n_valid, xs, w1, w3, w2, layer):
    n_slots, d = xs.shape
    de = w1.shape[-1]
    wspec = lambda shp: pl.BlockSpec((None, None) + shp, lambda i, be, nv: (layer, be[i], 0, 0))
    return pl.pallas_call(
        _expert_kernel,
        grid_spec=pltpu.PrefetchScalarGridSpec(
            num_scalar_prefetch=2,
            grid=(n_slots // MOE_ROWS,),
            in_specs=[pl.BlockSpec((MOE_ROWS, d), lambda i, be, nv: (jnp.minimum(i, nv[0] - 1), 0)),
                      wspec((d, de)), wspec((d, de)), wspec((de, d))],
            out_specs=pl.BlockSpec((MOE_ROWS, d), lambda i, be, nv: (i, 0)),
            scratch_shapes=[pltpu.VMEM((d, de), BF16), pltpu.VMEM((d, de), BF16), pltpu.VMEM((de, d), BF16)]),
        out_shape=jax.ShapeDtypeStruct((n_slots, d), F32),
        compiler_params=_cparams(("arbitrary",), 56),
        name="experts",
    )(block_e, n_valid, xs, w1, w3, w2)


def _route_tables(route, counts):
    expert = route[0:TOP_K].astype(jnp.int32).T.reshape(-1)
    rank = route[4:4 + TOP_K].astype(jnp.int32).T.reshape(-1)
    a = expert.shape[0]
    sizes = counts[:, 0].astype(jnp.int32)
    padded = (sizes + MOE_ROWS - 1) // MOE_ROWS * MOE_ROWS
    pend = jnp.cumsum(padded)
    pstarts = pend - padded
    dest = (pstarts[expert] + rank).astype(jnp.int32)
    n_blocks = -(-a // MOE_ROWS) + N_EXPERTS
    n_slots = n_blocks * MOE_ROWS
    n_valid = (pend[-1] // MOE_ROWS).astype(jnp.int32)
    blk = jnp.minimum(jnp.arange(n_blocks, dtype=jnp.int32), n_valid - 1)
    block_e = jnp.minimum(jnp.sum(blk[:, None] * MOE_ROWS >= pend[None, :], axis=1),
                          N_EXPERTS - 1).astype(jnp.int32)
    gap_start = jnp.concatenate([pstarts + sizes, pend[-1:]]).astype(jnp.int32)
    gap_len = jnp.concatenate([padded - sizes, n_slots - pend[-1:]]).astype(jnp.int32)
    gap_end = jnp.cumsum(gap_len)
    j = jnp.arange(n_slots - a, dtype=jnp.int32)
    seg = jnp.sum(j[:, None] >= gap_end[None, :], axis=1)
    pad_slots = (gap_start[seg] + j - (gap_end - gap_len)[seg]).astype(jnp.int32)
    return dest, pad_slots, block_e, n_valid.reshape(1), n_slots


def _combine_kernel(dcur_ref, dnxt_ref, x_ref, gate_ref, gt2_ref, g_ref, y_hbm, o_ref, ybuf, sems, *, final):
    g = pl.program_id(0)
    n = pl.num_programs(0)
    rows = x_ref.shape[0]
    slot = g % 2

    def fetch(tbl_ref, s):
        def body(r, carry):
            for k in range(TOP_K):
                pltpu.make_async_copy(y_hbm.at[pl.ds(tbl_ref[0, TOP_K * r + k], 1)],
                                      ybuf.at[s, k, pl.ds(r, 1)], sems.at[s]).start()
            return carry

        lax.fori_loop(0, rows, body, 0, unroll=8)

    @pl.when(g == 0)
    def _():
        fetch(dcur_ref, 0)

    @pl.when(g + 1 < n)
    def _():
        fetch(dnxt_ref, 1 - slot)

    for k in range(TOP_K):
        pltpu.make_async_copy(y_hbm.at[pl.ds(0, rows)], ybuf.at[slot, k], sems.at[slot]).wait()

    gate = gate_ref[...]
    y = gate[:, 0:1] * ybuf[slot, 0] + gate[:, 1:2] * ybuf[slot, 1]
    x2 = x_ref[...] + gt2_ref[...] * y
    o_ref[...] = _rms(x2, g_ref[...]) if final else x2


def _combine(x1, y_slots, dest, gates, gt2, g_final, n_ctx_tiles, has_ctx, final):
    b, lq, d = x1.shape
    nt = lq // ROW_TILE
    n_tiles = b * nt
    ctx_tiles = n_ctx_tiles if has_ctx else 0
    tok = lambda w: pl.BlockSpec((None, ROW_TILE, w), lambda g: (g // nt, g % nt, 0))
    tbl = lambda f: pl.BlockSpec((None, 1, TOP_K * ROW_TILE), lambda g: (f(g), 0, 0), memory_space=pltpu.SMEM)
    dest3 = dest.reshape(n_tiles, 1, TOP_K * ROW_TILE)
    return pl.pallas_call(
        functools.partial(_combine_kernel, final=final),
        grid=(n_tiles,),
        in_specs=[tbl(lambda g: g), tbl(lambda g: jnp.minimum(g + 1, n_tiles - 1)),
                  tok(d), pl.BlockSpec((ROW_TILE, TOP_K), lambda g: (g, 0)),
                  pl.BlockSpec((None, 1, d), lambda g: (jnp.where(g % nt < ctx_tiles, b, g // nt), 0, 0)),
                  pl.BlockSpec((1, d), lambda g: (0, 0)),
                  pl.BlockSpec(memory_space=pl.ANY)],
        out_specs=tok(d),
        out_shape=jax.ShapeDtypeStruct((b, lq, d), F32),
        scratch_shapes=[pltpu.VMEM((2, TOP_K, ROW_TILE, d), F32), pltpu.SemaphoreType.DMA((2,))],
        compiler_params=_cparams(("arbitrary",), 32),
        name="combine",
    )(dest3, dest3, x1, gates, gt2, g_final, y_slots)


def _rope_tables(n_ctx, seq):
    n_pairs = HEAD_DIM // 4
    inv = ROPE_BASE ** (-jnp.arange(n_pairs, dtype=F32) / n_pairs)
    rows = seq // GRID_W
    r = jnp.repeat(jnp.arange(rows, dtype=F32), GRID_W)
    col = jnp.tile(jnp.arange(GRID_W, dtype=F32), rows)
    ang = jnp.concatenate([r[:, None] * inv, col[:, None] * inv], axis=-1)
    cos = jnp.concatenate([jnp.ones((n_ctx, HEAD_DIM // 2), F32), jnp.cos(ang)], axis=0)
    sin = jnp.concatenate([jnp.zeros((n_ctx, HEAD_DIM // 2), F32), jnp.sin(ang)], axis=0)
    return jnp.tile(cos, (1, 4)), jnp.concatenate([-sin, -sin, sin, sin], axis=-1)


def _qk_perm():
    lane = jnp.arange(LANES)
    g, i = lane // 32, lane % 32
    within = (g % 2) * HEAD_DIM + 2 * i + g // 2
    return (jnp.arange(N_HEADS)[:, None] * LANES + within[None, :]).reshape(-1)


def kernel(x, c, ctx, c_ctx, w_mod, b_mod, g_norm1, g_norm2, w_in, conv_w, conv_b, lru_wa, lru_ba, lru_wx,
           lru_bx, lru_lambda, diff_lambda, g_subln, w_rnn_proj, w_attn_proj, w_o, w_router, b_router,
           w_e1, w_e3, w_e2, g_final):
    b, seq, d = x.shape
    n_ctx = ctx.shape[1]
    depth = w_mod.shape[0]
    n_ctx_tiles = n_ctx // ROW_TILE
    assert b == SUBLANES and d == N_HEADS * V_DIM and n_ctx % ROW_TILE == 0 and seq % ROW_TILE == 0

    cond = jnp.zeros((MOD_ROWS, d), F32).at[:b].set(c).at[b].set(c_ctx)
    mods = _adaln(cond, w_mod, b_mod).reshape(depth, MOD_ROWS, 6, 1, d)
    cos_t, sin_t = _rope_tables(n_ctx, seq)
    perm = _qk_perm()
    wrt = w_router.T
    wrt_hi = wrt.astype(BF16)
    wrt_lo = (wrt - wrt_hi.astype(F32)).astype(BF16)
    brt = jnp.broadcast_to(b_router.astype(F32)[:, None], (N_EXPERTS, LANES))

    xs, xs_ctx = x, ctx
    out = None
    for l in range(depth):
        last = l == depth - 1
        lam_init = 0.8 - 0.6 * math.exp(-0.3 * l)
        sh1, sc1, gt1, sh2, sc2, gt2 = [mods[l, :, j] for j in range(6)]
        wl = w_in[l]
        w_perm = jnp.concatenate(
            [wl[:, d:2 * d][:, perm], wl[:, 2 * d:3 * d], wl[:, 3 * d:4 * d][:, perm],
             wl[:, 0:d], wl[:, 4 * d:]], axis=1).astype(BF16)
        k, v, q, rx, rg, gr, ga = _inproj(xs, xs_ctx, sh1, sc1, g_norm1[l][None], w_perm, cos_t, sin_t,
                                          n_ctx_tiles)

        lq1, lk1, lq2, lk2 = diff_lambda[l].astype(F32)
        lam = (jnp.exp(jnp.sum(lq1 * lk1)) - jnp.exp(jnp.sum(lq2 * lk2)) + lam_init).reshape(1)
        attn, attn_ctx = _attention(lam, q, k, v, g_subln[l][None], n_ctx, not last, 1.0 - lam_init)

        rx_t = jnp.transpose(rx, (1, 0, 2))
        cdec = LRU_C * jax.nn.softplus(-lru_lambda[l].astype(F32))
        n_ctx_chunks = n_ctx // SCAN_CHUNK
        hf = None
        for direction in range(2):
            wg = jnp.concatenate([lru_wa[l, direction], lru_wx[l, direction]], axis=-1).astype(BF16)
            bg = jnp.concatenate([lru_ba[l, direction].reshape(RNN_BLOCKS, 1, RNN_BW),
                                  lru_bx[l, direction].reshape(RNN_BLOCKS, 1, RNN_BW)], axis=-1)
            hf = _scan(rx_t, conv_w[l], conv_b[l][None], wg, bg, cdec[direction][None], hf,
                       n_ctx_chunks, reverse=direction == 1)
        y = jnp.transpose(hf, (1, 0, 2)).astype(BF16)

        x1, h2, route, counts = _postmix(xs, xs_ctx, attn, attn_ctx, y, rg, gr, ga, gt1, sh2, sc2,
                                         g_norm2[l][None],
                                         w_rnn_proj[l].astype(BF16), w_attn_proj[l].astype(BF16),
                                         w_o[l].astype(BF16), wrt_hi, wrt_lo, brt, n_ctx_tiles)
        lq = x1.shape[1]
        n_tok = b * lq
        dest, pad_slots, block_e, n_valid, n_slots = _route_tables(route, counts)
        xs_sorted = _dispatch(pad_slots, dest, h2.reshape(n_tok, d), n_slots)
        y_slots = _experts(block_e, n_valid, xs_sorted, w_e1, w_e3, w_e2, l)
        res = _combine(x1, y_slots, dest, route[2:2 + TOP_K].T, gt2, g_final[None], n_ctx_tiles, not last, last)
        if last:
            out = res
        else:
            xs, xs_ctx = res, None
    return out
```

```python
import functools
import math

import jax
import jax.numpy as jnp
from jax import lax
from jax.experimental import pallas as pl
from jax.experimental.pallas import tpu as pltpu

F32 = jnp.float32
BF16 = jnp.bfloat16

N_HEADS = 8
HEAD_DIM = 64
V_DIM = 2 * HEAD_DIM
GRID_W = 64
EPS = 1e-6
RNN_BLOCKS = 8
RNN_BW = 128
CONV_W = 4
CONV_LEFT = 2
LRU_C = 8.0
ROPE_BASE = 10000.0
N_EXPERTS = 16
N_GROUPS = 4
EXPERTS_PER_GROUP = N_EXPERTS // N_GROUPS
TOP_K = 2

LANES = 128
SUBLANES = 8
ROW_TILE = 256
SCAN_CHUNK = 128
ATTN_TQ = 256
ATTN_CTX_TQ = 256
ATTN_TK = 512
MOE_ROWS = 512
DISPATCH_ROWS = 1024
MOD_ROWS = 16
NEG_BIG = -1e30
LOG2_E = math.log2(math.e)


def _cparams(sem, vmem_mib):
    return pltpu.CompilerParams(dimension_semantics=sem, vmem_limit_bytes=vmem_mib * 1024 * 1024)


def _sigmoid(x):
    return 0.5 * jnp.tanh(0.5 * x) + 0.5


def _gelu_tanh(x):
    return 0.5 * x * (1.0 + jnp.tanh(math.sqrt(2.0 / math.pi) * (x + 0.044715 * (x * x * x))))


def _rms(x, g):
    return x * lax.rsqrt(jnp.mean(x * x, axis=-1, keepdims=True) + EPS) * g


def _store_tiled(ref, x):
    rows = x.shape[0]
    for j in range(SUBLANES):
        ref[pl.ds(j, rows, stride=SUBLANES), :] = x[:, j * LANES:(j + 1) * LANES]


def _load_tiled(ref):
    rows = ref.shape[0] // SUBLANES
    return jnp.concatenate([ref[pl.ds(j, rows, stride=SUBLANES), :] for j in range(SUBLANES)], axis=1)


def _adaln_kernel(c_ref, w_ref, b_ref, o_ref):
    c = c_ref[...]
    a = c * _sigmoid(c)
    o_ref[...] = jnp.dot(a, w_ref[...], preferred_element_type=F32,
                         precision=lax.Precision.HIGHEST) + b_ref[...]


def _adaln(cond, w_mod, b_mod):
    depth, d, d6 = w_mod.shape
    n = d6 // d
    return pl.pallas_call(
        _adaln_kernel,
        grid=(depth, n),
        in_specs=[pl.BlockSpec((MOD_ROWS, d), lambda l, j: (0, 0)),
                  pl.BlockSpec((None, d, d), lambda l, j: (l, 0, j)),
                  pl.BlockSpec((None, 1, d), lambda l, j: (l, 0, j))],
        out_specs=pl.BlockSpec((None, MOD_ROWS, d), lambda l, j: (l, 0, j)),
        out_shape=jax.ShapeDtypeStruct((depth, MOD_ROWS, d6), F32),
        compiler_params=_cparams(("arbitrary", "arbitrary"), 32),
        name="adaln",
    )(cond, w_mod, b_mod.reshape(depth, 1, d6))


def _stream_tile(x_ref, xc_ref, is_ctx):
    if xc_ref is None:
        return x_ref[...]
    return jnp.where(is_ctx, xc_ref[...], x_ref[...])


def _stream_specs(x, x_ctx, n_ctx_tiles, wrap=lambda f: f):
    d = x.shape[-1]
    if x_ctx is None:
        return [pl.BlockSpec((None, ROW_TILE, d), wrap(lambda bi, i: (bi, i, 0)))], [x]
    return ([pl.BlockSpec((None, ROW_TILE, d), wrap(lambda bi, i: (bi, jnp.maximum(i - n_ctx_tiles, 0), 0))),
             pl.BlockSpec((None, ROW_TILE, d), wrap(lambda bi, i: (bi, jnp.minimum(i, n_ctx_tiles - 1), 0)))],
            [x, x_ctx])


def _inproj_kernel(*refs, d, split, n_ctx_tiles):
    if split:
        (x_ref, xc_ref, sh_ref, sc_ref, g_ref, w_ref, cos_ref, sin_ref,
         k_ref, v_ref, q_ref, rx_ref, rg_ref, gr_ref, ga_ref) = refs
    else:
        (x_ref, sh_ref, sc_ref, g_ref, w_ref, cos_ref, sin_ref,
         k_ref, v_ref, q_ref, rx_ref, rg_ref, gr_ref, ga_ref) = refs
        xc_ref = None
    x = _stream_tile(x_ref, xc_ref, pl.program_id(1) < n_ctx_tiles)
    h = (_rms(x, g_ref[...]) * (1.0 + sc_ref[...]) + sh_ref[...]).astype(BF16)
    cosv = cos_ref[...]
    sinv = sin_ref[...]

    def proj(p):
        return jnp.dot(h, w_ref[:, p * d:(p + 1) * d], preferred_element_type=F32)

    def rope_store(res, ref, scale):
        for hh in range(N_HEADS):
            xs = res[:, hh * LANES:(hh + 1) * LANES]
            out = xs * cosv + pltpu.roll(xs, LANES // 2, 1) * sinv
            ref[:, hh * LANES:(hh + 1) * LANES] = (out * scale).astype(BF16)

    rope_store(proj(0), k_ref, 1.0)
    v_ref[...] = proj(1).T.astype(BF16)
    rope_store(proj(2), q_ref, HEAD_DIM ** -0.5 * LOG2_E)
    rx_ref[...] = proj(3)
    rg_ref[...] = proj(4)
    gr_ref[...] = proj(5)
    ga_ref[...] = proj(6)


def _inproj(x, x_ctx, sh, sc, g, w, cos_t, sin_t, n_ctx_tiles):
    b, _, d = x.shape
    lt = x.shape[1] + (0 if x_ctx is None else x_ctx.shape[1])
    nt = lt // ROW_TILE
    mod_map = lambda bi, i: (jnp.where(i < n_ctx_tiles, b, bi), 0, 0)
    tok = pl.BlockSpec((None, ROW_TILE, d), lambda bi, i: (bi, i, 0))
    x_specs, x_args = _stream_specs(x, x_ctx, n_ctx_tiles)
    out_bf = jax.ShapeDtypeStruct((b, lt, d), BF16)
    out_f = jax.ShapeDtypeStruct((b, lt, d), F32)
    return pl.pallas_call(
        functools.partial(_inproj_kernel, d=d, split=x_ctx is not None, n_ctx_tiles=n_ctx_tiles),
        grid=(b, nt),
        in_specs=x_specs + [
                  pl.BlockSpec((None, 1, d), mod_map),
                  pl.BlockSpec((None, 1, d), mod_map),
                  pl.BlockSpec((1, d), lambda bi, i: (0, 0)),
                  pl.BlockSpec(w.shape, lambda bi, i: (0, 0)),
                  pl.BlockSpec((ROW_TILE, LANES), lambda bi, i: (i, 0)),
                  pl.BlockSpec((ROW_TILE, LANES), lambda bi, i: (i, 0))],
        out_specs=[tok, pl.BlockSpec((None, d, ROW_TILE), lambda bi, i: (bi, 0, i))] + [tok] * 5,
        out_shape=[out_bf, jax.ShapeDtypeStruct((b, d, lt), BF16), out_bf, out_f, out_f, out_f, out_f],
        compiler_params=_cparams(("parallel", "arbitrary"), 56),
        name="inproj",
    )(*x_args, sh, sc, g, w, cos_t, sin_t)


def _scan_chunk(i, n_ctx, n_chunks, reverse):
    if not reverse:
        return i
    return jnp.where(i < n_ctx, n_ctx - 1 - i, n_chunks - 1 - (i - n_ctx))


def _scan_kernel(*refs, reverse, tt, n_ctx, n_chunks):
    if reverse:
        (rx_ref, prev_ref, next_ref, cw_ref, cb_ref, wg_ref, bg_ref, c_ref, hf_ref,
         out_ref, ext_scr, a_scr, b_scr, h_scr) = refs
    else:
        (rx_ref, prev_ref, next_ref, cw_ref, cb_ref, wg_ref, bg_ref, c_ref,
         out_ref, ext_scr, a_scr, b_scr, h_scr) = refs
        hf_ref = None
    i = pl.program_id(0)
    c = _scan_chunk(i, n_ctx, n_chunks, reverse)
    seq_start = jnp.logical_or(c == 0, c == n_ctx)
    seq_end = jnp.logical_or(c == n_ctx - 1, c == n_chunks - 1)

    ext_scr[0:CONV_LEFT] = jnp.where(seq_start, 0.0, prev_ref[...])
    ext_scr[CONV_LEFT:CONV_LEFT + tt] = rx_ref[...]
    ext_scr[CONV_LEFT + tt:CONV_LEFT + tt + 1] = jnp.where(seq_end, 0.0, next_ref[...])

    rows = tt * SUBLANES
    for n in range(RNN_BLOCKS):
        sl = slice(n * RNN_BW, (n + 1) * RNN_BW)
        u = cb_ref[:, sl] + cw_ref[0:1, sl] * ext_scr[0:tt, :, sl]
        for j in range(1, CONV_W):
            u = u + cw_ref[j:j + 1, sl] * ext_scr[j:j + tt, :, sl]
        u2 = u.reshape(rows, RNN_BW)
        z = jnp.dot(u2.astype(BF16), wg_ref[n], preferred_element_type=F32) + bg_ref[n]
        r = _sigmoid(z[:, :RNN_BW])
        ig = _sigmoid(z[:, RNN_BW:])
        a = jnp.exp(-c_ref[:, sl] * r)
        bb = jnp.sqrt(1.0 - a * a) * ig * u2
        a_scr[:, :, sl] = a.reshape(tt, SUBLANES, RNN_BW)
        b_scr[:, :, sl] = bb.reshape(tt, SUBLANES, RNN_BW)

    @pl.when(i == 0)
    def _():
        h_scr[...] = jnp.zeros_like(h_scr)

    def step(s, h):
        t = tt - 1 - s if reverse else s
        h = a_scr[t] * h + b_scr[t]
        if reverse:
            out_ref[t] = hf_ref[t] + h
        else:
            out_ref[t] = h
        return h

    h_scr[...] = lax.fori_loop(0, tt, step, h_scr[...], unroll=8)


def _scan(rx_t, cw, cb, wg, bg, cdec, hf, n_ctx, reverse):
    lt, b, d = rx_t.shape
    tt = SCAN_CHUNK
    n_chunks = lt // tt
    chunk = lambda i: _scan_chunk(i, n_ctx, n_chunks, reverse)
    blk = pl.BlockSpec((tt, b, d), lambda i: (chunk(i), 0, 0))
    const2 = lambda i: (0, 0)
    const3 = lambda i: (0, 0, 0)
    in_specs = [blk,
                pl.BlockSpec((CONV_LEFT, b, d), lambda i: (jnp.maximum(chunk(i) * (tt // CONV_LEFT) - 1, 0), 0, 0)),
                pl.BlockSpec((1, b, d), lambda i: (jnp.minimum((chunk(i) + 1) * tt, lt - 1), 0, 0)),
                pl.BlockSpec(cw.shape, const2),
                pl.BlockSpec(cb.shape, const2),
                pl.BlockSpec(wg.shape, const3),
                pl.BlockSpec(bg.shape, const3),
                pl.BlockSpec(cdec.shape, const2)]
    args = [rx_t, rx_t, rx_t, cw, cb, wg, bg, cdec]
    if reverse:
        in_specs.append(blk)
        args.append(hf)
    return pl.pallas_call(
        functools.partial(_scan_kernel, reverse=reverse, tt=tt, n_ctx=n_ctx, n_chunks=n_chunks),
        grid=(n_chunks,),
        in_specs=in_specs,
        out_specs=blk,
        out_shape=jax.ShapeDtypeStruct((lt, b, d), F32),
        scratch_shapes=[pltpu.VMEM((tt + CONV_W - 1, b, d), F32),
                        pltpu.VMEM((tt, b, d), F32),
                        pltpu.VMEM((tt, b, d), F32),
                        pltpu.VMEM((b, d), F32)],
        compiler_params=_cparams(("arbitrary",), 56),
        name="scan_bwd" if reverse else "scan_fwd",
    )(*args)


def _attn_passes(lam, q_refs, k_ref, vt_ref, g_ref, o_ref, *, write, read, tq, w_chunks, r_chunks, out_scale):
    n2 = 2 * tq
    if write is not None:
        s_w, m_w = write
        q = jnp.concatenate([r[...] for r in q_refs], axis=0) if len(q_refs) > 1 else q_refs[0][...]
        lane = lax.broadcasted_iota(jnp.int32, q.shape, 1)
        first = (lane // (HEAD_DIM // 2)) % 2 == 0
        zero = jnp.zeros_like(q)
        qs = jnp.concatenate([jnp.where(first, q, zero), jnp.where(first, zero, q)], axis=0)
        m8 = jnp.full((SUBLANES, n2), NEG_BIG, F32)
    if read is not None:
        s_r, m_r = read
        m = m_r[...]
        l8 = jnp.zeros((SUBLANES, n2), F32)
        acc = jnp.zeros((V_DIM, n2), F32)
    for j in range(max(len(w_chunks) if write is not None else 0, len(r_chunks) if read is not None else 0)):
        if write is not None and j < len(w_chunks):
            start, size = w_chunks[j]
            st = lax.dot_general(k_ref[start:start + size, :], qs, (((1,), (1,)), ((), ())),
                                 preferred_element_type=F32)
            s_w[start:start + size, :] = st
            m8 = jnp.maximum(m8, jnp.max(st.reshape(size // SUBLANES, SUBLANES, n2), axis=0))
        if read is not None and j < len(r_chunks):
            start, size = r_chunks[j]
            p = jnp.exp2(s_r[start:start + size, :] - m)
            l8 = l8 + jnp.sum(p.reshape(size // SUBLANES, SUBLANES, n2), axis=0)
            acc = acc + jnp.dot(vt_ref[:, start:start + size], p.astype(BF16), preferred_element_type=F32)
    if write is not None:
        m_w[...] = jnp.max(m8, axis=0, keepdims=True)
    if read is not None:
        on = acc * (1.0 / jnp.sum(l8, axis=0, keepdims=True))
        dt = on[:, :tq] - lam * on[:, tq:]
        yt = dt * lax.rsqrt(jnp.mean(dt * dt, axis=0, keepdims=True) + EPS)
        o_ref[...] = (yt.T * g_ref[...] * out_scale).astype(BF16)


def _attn_pipe_kernel(lam_ref, *refs, tq, nq, n_q_refs, chunks, out_scale):
    q_refs = refs[:n_q_refs]
    k_ref, vt_ref, g_ref, o_ref, s_a, s_b, m_a, m_b = refs[n_q_refs:]
    s = pl.program_id(2)
    run = functools.partial(_attn_passes, lam_ref[0], q_refs, k_ref, vt_ref, g_ref, o_ref, tq=tq,
                            w_chunks=chunks, r_chunks=chunks, out_scale=out_scale)
    buf_a, buf_b = (s_a, m_a), (s_b, m_b)
    inner = jnp.logical_and(s > 0, s < nq)
    even = s % 2 == 0

    @pl.when(s == 0)
    def _():
        run(write=buf_a, read=None)

    @pl.when(jnp.logical_and(inner, even))
    def _():
        run(write=buf_a, read=buf_b)

    @pl.when(jnp.logical_and(inner, jnp.logical_not(even)))
    def _():
        run(write=buf_b, read=buf_a)

    @pl.when(s == nq)
    def _():
        run(write=None, read=buf_a if (nq - 1) % 2 == 0 else buf_b)


def _attn_single_kernel(lam_ref, q_ref, k_ref, vt_ref, g_ref, o_ref, s_a, m_a, *, tq, chunks, out_scale):
    run = functools.partial(_attn_passes, lam_ref[0], (q_ref,), k_ref, vt_ref, g_ref, o_ref, tq=tq,
                            w_chunks=chunks, r_chunks=chunks, out_scale=out_scale)
    run(write=(s_a, m_a), read=None)
    run(write=None, read=(s_a, m_a))


def _key_chunks(n_ctx_rows, lt):
    ctx = [(s, min(ATTN_TK, n_ctx_rows - s)) for s in range(0, n_ctx_rows, ATTN_TK)]
    lat = [(s, min(ATTN_TK, lt - s)) for s in range(n_ctx_rows, lt, ATTN_TK)]
    return tuple(ctx), tuple(ctx + lat)


def _attention(lam, q, k, vt, g_sub, n_ctx_rows, with_ctx_queries, out_scale):
    b, lt, d = q.shape
    tq, tqc = ATTN_TQ, ATTN_CTX_TQ
    ctx_q_tiles = n_ctx_rows // tqc
    n_sub = tq // tqc
    nq = (lt - n_ctx_rows) // tq
    ctx_chunks, all_chunks = _key_chunks(n_ctx_rows, lt)
    n2 = 2 * tq
    lam_in = pl.BlockSpec(memory_space=pltpu.SMEM)
    g_in = pl.BlockSpec((1, V_DIM), lambda bi, h, s: (0, 0))
    lat = pl.pallas_call(
        functools.partial(_attn_pipe_kernel, tq=tq, nq=nq, n_q_refs=n_sub, chunks=all_chunks,
                          out_scale=out_scale),
        grid=(b, N_HEADS, nq + 1),
        in_specs=[lam_in] + [
                  pl.BlockSpec((None, tqc, V_DIM),
                               lambda bi, h, s, j=j: (bi, jnp.minimum(s, nq - 1) * n_sub + ctx_q_tiles + j, h))
                  for j in range(n_sub)] + [
                  pl.BlockSpec((None, lt, V_DIM), lambda bi, h, s: (bi, 0, h)),
                  pl.BlockSpec((None, V_DIM, lt), lambda bi, h, s: (bi, h, 0)),
                  g_in],
        out_specs=pl.BlockSpec((None, tq, V_DIM), lambda bi, h, s: (bi, jnp.maximum(s - 1, 0), h)),
        out_shape=jax.ShapeDtypeStruct((b, nq * tq, d), BF16),
        scratch_shapes=[pltpu.VMEM((lt, n2), F32), pltpu.VMEM((lt, n2), F32),
                        pltpu.VMEM((1, n2), F32), pltpu.VMEM((1, n2), F32)],
        compiler_params=_cparams(("parallel", "parallel", "arbitrary"), 56),
        name="diff_attn",
    )(lam, *([q] * n_sub), k, vt, g_sub)
    if not with_ctx_queries:
        return lat, None
    ctx = pl.pallas_call(
        functools.partial(_attn_single_kernel, tq=tqc, chunks=ctx_chunks, out_scale=out_scale),
        grid=(b, N_HEADS, ctx_q_tiles),
        in_specs=[lam_in,
                  pl.BlockSpec((None, tqc, V_DIM), lambda bi, h, s: (bi, s, h)),
                  pl.BlockSpec((None, n_ctx_rows, V_DIM), lambda bi, h, s: (bi, 0, h)),
                  pl.BlockSpec((None, V_DIM, n_ctx_rows), lambda bi, h, s: (bi, h, 0)),
                  g_in],
        out_specs=pl.BlockSpec((None, tqc, V_DIM), lambda bi, h, s: (bi, s, h)),
        out_shape=jax.ShapeDtypeStruct((b, n_ctx_rows, d), BF16),
        scratch_shapes=[pltpu.VMEM((n_ctx_rows, 2 * tqc), F32), pltpu.VMEM((1, 2 * tqc), F32)],
        compiler_params=_cparams(("parallel", "parallel", "arbitrary"), 32),
        name="diff_attn_ctx",
    )(lam, q, k, vt, g_sub)
    return lat, ctx


def _postmix_kernel(*refs, has_ctx, split, n_ctx_tiles, nt, n_tiles):
    refs = list(refs)
    x_ref = refs.pop(0)
    xc_ref = refs.pop(0) if split else None
    at_ref = refs.pop(0)
    atc_ref = refs.pop(0) if has_ctx else None
    (y_ref, rg_ref, gr_ref, ga_ref, gt1_ref, sh2_ref, sc2_ref, g2_ref,
     wr_ref, wa_ref, wo_ref, wrh_ref, wrl_ref, brt_ref, x1_ref, h2_ref, route_ref, cnt_ref,
     run_scr, hs_scr) = refs
    g = pl.program_id(0)

    @pl.when(g == 0)
    def _():
        run_scr[...] = jnp.zeros_like(run_scr)
        hs_scr[...] = jnp.zeros_like(hs_scr)

    h_hi = hs_scr[0]
    h_lo = hs_scr[1]
    nt_dims = (((1,), (1,)), ((), ()))
    lg = (lax.dot_general(wrh_ref[...], h_hi, nt_dims, preferred_element_type=F32)
          + lax.dot_general(wrh_ref[...], h_lo, nt_dims, preferred_element_type=F32)
          + lax.dot_general(wrl_ref[...], h_hi, nt_dims, preferred_element_type=F32)) + brt_ref[:, 0:1]

    is_ctx = jnp.minimum(g, n_tiles - 1) % nt < n_ctx_tiles
    x = _stream_tile(x_ref, xc_ref, is_ctx)
    at = _stream_tile(at_ref, atc_ref, is_ctx)
    yg = (y_ref[...].astype(F32) * _gelu_tanh(rg_ref[...])).astype(BF16)
    o_r = jnp.dot(yg, wr_ref[...], preferred_element_type=F32)
    o_a = jnp.dot(at, wa_ref[...], preferred_element_type=F32)
    mix = (_sigmoid(gr_ref[...]) * o_r + _sigmoid(ga_ref[...]) * o_a).astype(BF16)
    x1 = x + gt1_ref[...] * jnp.dot(mix, wo_ref[...], preferred_element_type=F32)
    x1_ref[...] = x1
    h2 = _rms(x1, g2_ref[...]) * (1.0 + sc2_ref[...]) + sh2_ref[...]
    _store_tiled(h2_ref, h2)
    h2_hi = h2.astype(BF16)
    hs_scr[0] = h2_hi
    hs_scr[1] = (h2 - h2_hi.astype(F32)).astype(BF16)

    rows = h_hi.shape[0]
    ex = lax.broadcasted_iota(jnp.int32, lg.shape, 0)
    ex_f = ex.astype(F32)
    e = jnp.exp(lg - jnp.max(lg, axis=0, keepdims=True))
    p = e / jnp.sum(e, axis=0, keepdims=True)
    grp = ex // EXPERTS_PER_GROUP
    best = jnp.max(jnp.where(grp == 0, p, -1.0), axis=0, keepdims=True)
    best_g = jnp.zeros_like(best, dtype=jnp.int32)
    for gi in range(1, N_GROUPS):
        gm = jnp.max(jnp.where(grp == gi, p, -1.0), axis=0, keepdims=True)
        better = gm > best
        best_g = jnp.where(better, gi, best_g)
        best = jnp.where(better, gm, best)
    pg = jnp.where(grp == best_g, p, -1.0)
    big = float(N_EXPERTS)
    v1 = jnp.max(pg, axis=0, keepdims=True)
    i1 = jnp.min(jnp.where(pg == v1, ex_f, big), axis=0, keepdims=True)
    pg2 = jnp.where(ex_f == i1, -1.0, pg)
    v2 = jnp.max(pg2, axis=0, keepdims=True)
    i2 = jnp.min(jnp.where(pg2 == v2, ex_f, big), axis=0, keepdims=True)
    den = v1 + v2

    live = (g > 0).astype(F32)
    oh1 = (ex_f == i1).astype(F32)
    oh2 = (ex_f == i2).astype(F32)
    both = (oh1 + oh2) * live
    tri = (lax.broadcasted_iota(jnp.int32, (rows, rows), 0)
           < lax.broadcasted_iota(jnp.int32, (rows, rows), 1)).astype(BF16)
    before = jnp.dot(both.astype(BF16), tri, preferred_element_type=F32) + run_scr[:, 0:1]
    r1 = jnp.sum(oh1 * before, axis=0, keepdims=True)
    r2 = jnp.sum(oh2 * before, axis=0, keepdims=True)
    run = run_scr[...] + jnp.sum(both, axis=1, keepdims=True)
    run_scr[...] = run
    cnt_ref[...] = run
    row = lax.broadcasted_iota(jnp.int32, route_ref.shape, 0)
    route_ref[...] = jnp.where(row == 0, i1, jnp.where(row == 1, i2, jnp.where(row == 2, v1 / den, jnp.where(
        row == 3, v2 / den, jnp.where(row == 4, r1, r2)))))


def _postmix(x, x_ctx, attn, attn_ctx, y, rg, gr, ga, gt1, sh2, sc2, g2, wr, wa, wo, wrh, wrl, brt, n_ctx_tiles):
    b, _, d = y.shape
    lt = y.shape[1]
    has_ctx = attn_ctx is not None
    off = 0 if has_ctx else n_ctx_tiles
    lq = lt - off * ROW_TILE
    nt = lq // ROW_TILE
    n_tiles = b * nt

    def wrap(f):
        def index_map(g):
            t = jnp.minimum(g, n_tiles - 1)
            return f(t // nt, t % nt)
        return index_map

    full = pl.BlockSpec((None, ROW_TILE, d), wrap(lambda bi, i: (bi, i + off, 0)))
    part = pl.BlockSpec((None, ROW_TILE, d), wrap(lambda bi, i: (bi, i, 0)))
    mod = pl.BlockSpec((None, 1, d), wrap(lambda bi, i: (jnp.where(i + off < n_ctx_tiles, b, bi), 0, 0)))
    const = lambda g: (0, 0)
    if x_ctx is None:
        x_specs, x_args = [full], [x]
    else:
        x_specs, x_args = _stream_specs(x, x_ctx, n_ctx_tiles, wrap)
    at_specs, at_args = _stream_specs(attn, attn_ctx, n_ctx_tiles, wrap)
    return pl.pallas_call(
        functools.partial(_postmix_kernel, has_ctx=has_ctx, split=x_ctx is not None, n_ctx_tiles=n_ctx_tiles,
                          nt=nt, n_tiles=n_tiles),
        grid=(n_tiles + 1,),
        in_specs=x_specs + at_specs + [full, full, full, full, mod, mod, mod,
                                      pl.BlockSpec((1, d), const),
                                      pl.BlockSpec((d, d), const), pl.BlockSpec((d, d), const),
                                      pl.BlockSpec((d, d), const),
                                      pl.BlockSpec((N_EXPERTS, d), const), pl.BlockSpec((N_EXPERTS, d), const),
                                      pl.BlockSpec((N_EXPERTS, LANES), const)],
        out_specs=[part, pl.BlockSpec((ROW_TILE * SUBLANES, LANES), wrap(lambda bi, i: (bi * nt + i, 0))),
                   pl.BlockSpec((SUBLANES, ROW_TILE), lambda g: (0, jnp.maximum(g - 1, 0))),
                   pl.BlockSpec((N_EXPERTS, LANES), const)],
        out_shape=[jax.ShapeDtypeStruct((b, lq, d), F32),
                   jax.ShapeDtypeStruct((b * lq * SUBLANES, LANES), F32),
                   jax.ShapeDtypeStruct((SUBLANES, b * lq), F32),
                   jax.ShapeDtypeStruct((N_EXPERTS, LANES), F32)],
        scratch_shapes=[pltpu.VMEM((N_EXPERTS, LANES), F32), pltpu.VMEM((2, ROW_TILE, d), BF16)],
        compiler_params=_cparams(("arbitrary",), 48),
        name="postmix",
    )(*x_args, *at_args, y, rg, gr, ga, gt1, sh2, sc2, g2, wr, wa, wo, wrh, wrl, brt)


def _dispatch_kernel(pad_ref, dest_ref, h_ref, xs_hbm, zrows, sem, *, n_pad):
    i = pl.program_id(0)
    rows = h_ref.shape[0] // SUBLANES

    def tile(ref, r):
        return ref.at[pl.ds(pl.multiple_of(r * SUBLANES, SUBLANES), SUBLANES)]

    @pl.when(i == 0)
    def _():
        zrows[...] = jnp.zeros_like(zrows)
        for c in range(n_pad // MOE_ROWS):
            def fill(r, carry):
                pltpu.make_async_copy(tile(zrows, r), tile(xs_hbm, pad_ref[c * MOE_ROWS + r]), sem).start()
                return carry

            lax.fori_loop(0, MOE_ROWS, fill, 0, unroll=8)
            pltpu.make_async_copy(zrows, xs_hbm.at[pl.ds(0, MOE_ROWS * SUBLANES)], sem).wait()

    def send(r, carry):
        for k in range(TOP_K):
            pltpu.make_async_copy(tile(h_ref, r), tile(xs_hbm, dest_ref[0, TOP_K * r + k]), sem).start()
        return carry

    lax.fori_loop(0, rows, send, 0, unroll=8)
    for k in range(TOP_K):
        pltpu.make_async_copy(h_ref, xs_hbm.at[pl.ds(0, rows * SUBLANES)], sem).wait()


def _dispatch(pad_slots, dest, h2, n_slots):
    t = h2.shape[0] // SUBLANES
    rows = DISPATCH_ROWS
    return pl.pallas_call(
        functools.partial(_dispatch_kernel, n_pad=pad_slots.shape[0]),
        grid_spec=pltpu.PrefetchScalarGridSpec(
            num_scalar_prefetch=1,
            grid=(t // rows,),
            in_specs=[pl.BlockSpec((None, 1, TOP_K * rows), lambda i, pad: (i, 0, 0), memory_space=pltpu.SMEM),
                      pl.BlockSpec((rows * SUBLANES, LANES), lambda i, pad: (i, 0))],
            out_specs=pl.BlockSpec(memory_space=pl.ANY),
            scratch_shapes=[pltpu.VMEM((MOE_ROWS * SUBLANES, LANES), F32), pltpu.SemaphoreType.DMA(())]),
        out_shape=jax.ShapeDtypeStruct((n_slots * SUBLANES, LANES), F32),
        compiler_params=_cparams(("arbitrary",), 32),
        name="dispatch",
    )(pad_slots, dest.reshape(t // rows, 1, TOP_K * rows), h2)


def _expert_kernel(be_ref, nv_ref, x_ref, w1_ref, w3_ref, w2_ref, y_ref, w1_b, w3_b, w2_b):
    i = pl.program_id(0)

    @pl.when(i < nv_ref[0])
    def _():
        @pl.when(jnp.logical_or(i == 0, be_ref[i] != be_ref[jnp.maximum(i - 1, 0)]))
        def _():
            w1_b[...] = w1_ref[...].astype(BF16)
            w3_b[...] = w3_ref[...].astype(BF16)
            w2_b[...] = w2_ref[...].astype(BF16)

        xb = _load_tiled(x_ref).astype(BF16)
        a1 = jnp.dot(xb, w1_b[...], preferred_element_type=F32)
        a3 = jnp.dot(xb, w3_b[...], preferred_element_type=F32)
        mid = (a1 * _sigmoid(a1) * a3).astype(BF16)
        _store_tiled(y_ref, jnp.dot(mid, w2_b[...], preferred_element_type=F32))

    @pl.when(i >= nv_ref[0])
    def _():
        y_ref[...] = jnp.zeros_like(y_ref)


def _experts(block_e, n_valid, xs, w1, w3, w2, layer):
    n_slots = xs.shape[0] // SUBLANES
    d, de = w1.shape[-2:]
    wspec = lambda shp: pl.BlockSpec((None, None) + shp, lambda i, be, nv: (layer, be[i], 0, 0))
    return pl.pallas_call(
        _expert_kernel,
        grid_spec=pltpu.PrefetchScalarGridSpec(
            num_scalar_prefetch=2,
            grid=(n_slots // MOE_ROWS,),
            in_specs=[pl.BlockSpec((MOE_ROWS * SUBLANES, LANES), lambda i, be, nv: (jnp.minimum(i, nv[0] - 1), 0)),
                      wspec((d, de)), wspec((d, de)), wspec((de, d))],
            out_specs=pl.BlockSpec((MOE_ROWS * SUBLANES, LANES), lambda i, be, nv: (i, 0)),
            scratch_shapes=[pltpu.VMEM((d, de), BF16), pltpu.VMEM((d, de), BF16), pltpu.VMEM((de, d), BF16)]),
        out_shape=jax.ShapeDtypeStruct((n_slots * SUBLANES, LANES), F32),
        compiler_params=_cparams(("arbitrary",), 56),
        name="experts",
    )(block_e, n_valid, xs, w1, w3, w2)


def _route_tables(route, counts):
    expert = route[0:TOP_K].astype(jnp.int32).T.reshape(-1)
    rank = route[4:4 + TOP_K].astype(jnp.int32).T.reshape(-1)
    a = expert.shape[0]
    sizes = counts[:, 0].astype(jnp.int32)
    padded = (sizes + MOE_ROWS - 1) // MOE_ROWS * MOE_ROWS
    pend = jnp.cumsum(padded)
    pstarts = pend - padded
    dest = (pstarts[expert] + rank).astype(jnp.int32)
    n_blocks = -(-a // MOE_ROWS) + N_EXPERTS
    n_slots = n_blocks * MOE_ROWS
    n_valid = (pend[-1] // MOE_ROWS).astype(jnp.int32)
    blk = jnp.minimum(jnp.arange(n_blocks, dtype=jnp.int32), n_valid - 1)
    block_e = jnp.minimum(jnp.sum(blk[:, None] * MOE_ROWS >= pend[None, :], axis=1),
                          N_EXPERTS - 1).astype(jnp.int32)
    gap_start = jnp.concatenate([pstarts + sizes, pend[-1:]]).astype(jnp.int32)
    gap_len = jnp.concatenate([padded - sizes, n_slots - pend[-1:]]).astype(jnp.int32)
    gap_end = jnp.cumsum(gap_len)
    j = jnp.arange(n_slots - a, dtype=jnp.int32)
    seg = jnp.sum(j[:, None] >= gap_end[None, :], axis=1)
    pad_slots = (gap_start[seg] + j - (gap_end - gap_len)[seg]).astype(jnp.int32)
    return dest, pad_slots, block_e, n_valid.reshape(1), n_slots


def _combine_kernel(dcur_ref, dnxt_ref, x_ref, gate_ref, gt2_ref, g_ref, y_hbm, o_ref, ybuf, sems, *, final):
    g = pl.program_id(0)
    n = pl.num_programs(0)
    rows = x_ref.shape[0]
    slot = g % 2

    def fetch(tbl_ref, s):
        def body(r, carry):
            for k in range(TOP_K):
                src = pl.multiple_of(tbl_ref[0, TOP_K * r + k] * SUBLANES, SUBLANES)
                pltpu.make_async_copy(y_hbm.at[pl.ds(src, SUBLANES)],
                                      ybuf.at[s, k, pl.ds(pl.multiple_of(r * SUBLANES, SUBLANES), SUBLANES)],
                                      sems.at[s]).start()
            return carry

        lax.fori_loop(0, rows, body, 0, unroll=8)

    @pl.when(g == 0)
    def _():
        fetch(dcur_ref, 0)

    @pl.when(g + 1 < n)
    def _():
        fetch(dnxt_ref, 1 - slot)

    for k in range(TOP_K):
        pltpu.make_async_copy(y_hbm.at[pl.ds(0, rows * SUBLANES)], ybuf.at[slot, k], sems.at[slot]).wait()

    gate = gate_ref[...]
    y = gate[:, 0:1] * _load_tiled(ybuf.at[slot, 0]) + gate[:, 1:2] * _load_tiled(ybuf.at[slot, 1])
    x2 = x_ref[...] + gt2_ref[...] * y
    o_ref[...] = _rms(x2, g_ref[...]) if final else x2


def _combine(x1, y_slots, dest, gates, gt2, g_final, n_ctx_tiles, has_ctx, final):
    b, lq, d = x1.shape
    nt = lq // ROW_TILE
    n_tiles = b * nt
    ctx_tiles = n_ctx_tiles if has_ctx else 0
    tok = lambda w: pl.BlockSpec((None, ROW_TILE, w), lambda g: (g // nt, g % nt, 0))
    tbl = lambda f: pl.BlockSpec((None, 1, TOP_K * ROW_TILE), lambda g: (f(g), 0, 0), memory_space=pltpu.SMEM)
    dest3 = dest.reshape(n_tiles, 1, TOP_K * ROW_TILE)
    return pl.pallas_call(
        functools.partial(_combine_kernel, final=final),
        grid=(n_tiles,),
        in_specs=[tbl(lambda g: g), tbl(lambda g: jnp.minimum(g + 1, n_tiles - 1)),
                  tok(d), pl.BlockSpec((ROW_TILE, TOP_K), lambda g: (g, 0)),
                  pl.BlockSpec((None, 1, d), lambda g: (jnp.where(g % nt < ctx_tiles, b, g // nt), 0, 0)),
                  pl.BlockSpec((1, d), lambda g: (0, 0)),
                  pl.BlockSpec(memory_space=pl.ANY)],
        out_specs=tok(d),
        out_shape=jax.ShapeDtypeStruct((b, lq, d), F32),
        scratch_shapes=[pltpu.VMEM((2, TOP_K, ROW_TILE * SUBLANES, LANES), F32), pltpu.SemaphoreType.DMA((2,))],
        compiler_params=_cparams(("arbitrary",), 32),
        name="combine",
    )(dest3, dest3, x1, gates, gt2, g_final, y_slots)


def _rope_tables(n_ctx, seq):
    n_pairs = HEAD_DIM // 4
    inv = ROPE_BASE ** (-jnp.arange(n_pairs, dtype=F32) / n_pairs)
    rows = seq // GRID_W
    r = jnp.repeat(jnp.arange(rows, dtype=F32), GRID_W)
    col = jnp.tile(jnp.arange(GRID_W, dtype=F32), rows)
    ang = jnp.concatenate([r[:, None] * inv, col[:, None] * inv], axis=-1)
    cos = jnp.concatenate([jnp.ones((n_ctx, HEAD_DIM // 2), F32), jnp.cos(ang)], axis=0)
    sin = jnp.concatenate([jnp.zeros((n_ctx, HEAD_DIM // 2), F32), jnp.sin(ang)], axis=0)
    return jnp.tile(cos, (1, 4)), jnp.concatenate([-sin, -sin, sin, sin], axis=-1)


def _qk_perm():
    lane = jnp.arange(LANES)
    g, i = lane // 32, lane % 32
    within = (g % 2) * HEAD_DIM + 2 * i + g // 2
    return (jnp.arange(N_HEADS)[:, None] * LANES + within[None, :]).reshape(-1)


def kernel(x, c, ctx, c_ctx, w_mod, b_mod, g_norm1, g_norm2, w_in, conv_w, conv_b, lru_wa, lru_ba, lru_wx,
           lru_bx, lru_lambda, diff_lambda, g_subln, w_rnn_proj, w_attn_proj, w_o, w_router, b_router,
           w_e1, w_e3, w_e2, g_final):
    b, seq, d = x.shape
    n_ctx = ctx.shape[1]
    depth = w_mod.shape[0]
    n_ctx_tiles = n_ctx // ROW_TILE
    assert b == SUBLANES and d == N_HEADS * V_DIM and n_ctx % ROW_TILE == 0 and seq % ROW_TILE == 0

    cond = jnp.zeros((MOD_ROWS, d), F32).at[:b].set(c).at[b].set(c_ctx)
    mods = _adaln(cond, w_mod, b_mod).reshape(depth, MOD_ROWS, 6, 1, d)
    cos_t, sin_t = _rope_tables(n_ctx, seq)
    perm = _qk_perm()
    wrt = w_router.T
    wrt_hi = wrt.astype(BF16)
    wrt_lo = (wrt - wrt_hi.astype(F32)).astype(BF16)
    brt = jnp.broadcast_to(b_router.astype(F32)[:, None], (N_EXPERTS, LANES))

    xs, xs_ctx = x, ctx
    out = None
    for l in range(depth):
        last = l == depth - 1
        lam_init = 0.8 - 0.6 * math.exp(-0.3 * l)
        sh1, sc1, gt1, sh2, sc2, gt2 = [mods[l, :, j] for j in range(6)]
        wl = w_in[l]
        w_perm = jnp.concatenate(
            [wl[:, d:2 * d][:, perm], wl[:, 2 * d:3 * d], wl[:, 3 * d:4 * d][:, perm],
             wl[:, 0:d], wl[:, 4 * d:]], axis=1).astype(BF16)
        k, v, q, rx, rg, gr, ga = _inproj(xs, xs_ctx, sh1, sc1, g_norm1[l][None], w_perm, cos_t, sin_t,
                                          n_ctx_tiles)

        lq1, lk1, lq2, lk2 = diff_lambda[l].astype(F32)
        lam = (jnp.exp(jnp.sum(lq1 * lk1)) - jnp.exp(jnp.sum(lq2 * lk2)) + lam_init).reshape(1)
        attn, attn_ctx = _attention(lam, q, k, v, g_subln[l][None], n_ctx, not last, 1.0 - lam_init)

        rx_t = jnp.transpose(rx, (1, 0, 2))
        cdec = LRU_C * jax.nn.softplus(-lru_lambda[l].astype(F32))
        n_ctx_chunks = n_ctx // SCAN_CHUNK
        hf = None
        for direction in range(2):
            wg = jnp.concatenate([lru_wa[l, direction], lru_wx[l, direction]], axis=-1).astype(BF16)
            bg = jnp.concatenate([lru_ba[l, direction].reshape(RNN_BLOCKS, 1, RNN_BW),
                                  lru_bx[l, direction].reshape(RNN_BLOCKS, 1, RNN_BW)], axis=-1)
            hf = _scan(rx_t, conv_w[l], conv_b[l][None], wg, bg, cdec[direction][None], hf,
                       n_ctx_chunks, reverse=direction == 1)
        y = jnp.transpose(hf, (1, 0, 2)).astype(BF16)

        x1, h2, route, counts = _postmix(xs, xs_ctx, attn, attn_ctx, y, rg, gr, ga, gt1, sh2, sc2,
                                         g_norm2[l][None],
                                         w_rnn_proj[l].astype(BF16), w_attn_proj[l].astype(BF16),
                                         w_o[l].astype(BF16), wrt_hi, wrt_lo, brt, n_ctx_tiles)
        lq = x1.shape[1]
        n_tok = b * lq
        dest, pad_slots, block_e, n_valid, n_slots = _route_tables(route, counts)
        xs_sorted = _dispatch(pad_slots, dest, h2, n_slots)
        y_slots = _experts(block_e, n_valid, xs_sorted, w_e1, w_e3, w_e2, l)
        res = _combine(x1, y_slots, dest, route[2:2 + TOP_K].T, gt2, g_final[None], n_ctx_tiles, not last, last)
        if last:
            out = res
        else:
            xs, xs_ctx = res, None
    return out
```

```python
import functools
import math

import jax
import jax.numpy as jnp
from jax import lax
from jax.experimental import pallas as pl
from jax.experimental.pallas import tpu as pltpu

F32 = jnp.float32
BF16 = jnp.bfloat16

N_HEADS = 8
HEAD_DIM = 64
V_DIM = 2 * HEAD_DIM
GRID_W = 64
EPS = 1e-6
RNN_BLOCKS = 8
RNN_BW = 128
CONV_W = 4
CONV_LEFT = 2
LRU_C = 8.0
ROPE_BASE = 10000.0
N_EXPERTS = 16
N_GROUPS = 4
EXPERTS_PER_GROUP = N_EXPERTS // N_GROUPS
TOP_K = 2

LANES = 128
SUBLANES = 8
ROW_TILE = 256
SCAN_CHUNK = 128
ATTN_TQ = 256
ATTN_CTX_TQ = 256
ATTN_TK = 512
MOE_ROWS = 512
DISPATCH_ROWS = 1024
MOD_ROWS = 16
NEG_BIG = -1e30
LOG2_E = math.log2(math.e)


def _cparams(sem, vmem_mib):
    return pltpu.CompilerParams(dimension_semantics=sem, vmem_limit_bytes=vmem_mib * 1024 * 1024)


def _sigmoid(x):
    return 0.5 * jnp.tanh(0.5 * x) + 0.5


def _gelu_tanh(x):
    return 0.5 * x * (1.0 + jnp.tanh(math.sqrt(2.0 / math.pi) * (x + 0.044715 * (x * x * x))))


def _rms(x, g):
    return x * lax.rsqrt(jnp.mean(x * x, axis=-1, keepdims=True) + EPS) * g


def _store_tiled(ref, x):
    rows = x.shape[0]
    for j in range(SUBLANES):
        ref[pl.ds(j, rows, stride=SUBLANES), :] = x[:, j * LANES:(j + 1) * LANES]


def _load_tiled(ref):
    rows = ref.shape[0] // SUBLANES
    return jnp.concatenate([ref[pl.ds(j, rows, stride=SUBLANES), :] for j in range(SUBLANES)], axis=1)


def _adaln_kernel(c_ref, w_ref, b_ref, o_ref):
    c = c_ref[...]
    a = c * _sigmoid(c)
    o_ref[...] = jnp.dot(a, w_ref[...], preferred_element_type=F32,
                         precision=lax.Precision.HIGHEST) + b_ref[...]


def _adaln(cond, w_mod, b_mod):
    depth, d, d6 = w_mod.shape
    n = d6 // d
    return pl.pallas_call(
        _adaln_kernel,
        grid=(depth, n),
        in_specs=[pl.BlockSpec((MOD_ROWS, d), lambda l, j: (0, 0)),
                  pl.BlockSpec((None, d, d), lambda l, j: (l, 0, j)),
                  pl.BlockSpec((None, 1, d), lambda l, j: (l, 0, j))],
        out_specs=pl.BlockSpec((None, MOD_ROWS, d), lambda l, j: (l, 0, j)),
        out_shape=jax.ShapeDtypeStruct((depth, MOD_ROWS, d6), F32),
        compiler_params=_cparams(("arbitrary", "arbitrary"), 32),
        name="adaln",
    )(cond, w_mod, b_mod.reshape(depth, 1, d6))


def _stream_tile(x_ref, xc_ref, is_ctx):
    if xc_ref is None:
        return x_ref[...]
    return jnp.where(is_ctx, xc_ref[...], x_ref[...])


def _stream_specs(x, x_ctx, n_ctx_tiles, wrap=lambda f: f):
    d = x.shape[-1]
    if x_ctx is None:
        return [pl.BlockSpec((None, ROW_TILE, d), wrap(lambda bi, i: (bi, i, 0)))], [x]
    return ([pl.BlockSpec((None, ROW_TILE, d), wrap(lambda bi, i: (bi, jnp.maximum(i - n_ctx_tiles, 0), 0))),
             pl.BlockSpec((None, ROW_TILE, d), wrap(lambda bi, i: (bi, jnp.minimum(i, n_ctx_tiles - 1), 0)))],
            [x, x_ctx])


def _inproj_kernel(*refs, d, split, n_ctx_tiles):
    if split:
        (x_ref, xc_ref, sh_ref, sc_ref, g_ref, w_ref, cos_ref, sin_ref,
         k_ref, v_ref, q_ref, rx_ref, rg_ref, gr_ref, ga_ref) = refs
    else:
        (x_ref, sh_ref, sc_ref, g_ref, w_ref, cos_ref, sin_ref,
         k_ref, v_ref, q_ref, rx_ref, rg_ref, gr_ref, ga_ref) = refs
        xc_ref = None
    x = _stream_tile(x_ref, xc_ref, pl.program_id(1) < n_ctx_tiles)
    h = (_rms(x, g_ref[...]) * (1.0 + sc_ref[...]) + sh_ref[...]).astype(BF16)
    cosv = cos_ref[...]
    sinv = sin_ref[...]

    def proj(p):
        return jnp.dot(h, w_ref[:, p * d:(p + 1) * d], preferred_element_type=F32)

    def rope_store(res, ref, scale):
        for hh in range(N_HEADS):
            xs = res[:, hh * LANES:(hh + 1) * LANES]
            out = xs * cosv + pltpu.roll(xs, LANES // 2, 1) * sinv
            ref[:, hh * LANES:(hh + 1) * LANES] = (out * scale).astype(BF16)

    rope_store(proj(0), k_ref, 1.0)
    v_ref[...] = proj(1).T.astype(BF16)
    rope_store(proj(2), q_ref, HEAD_DIM ** -0.5 * LOG2_E)
    _store_tiled(rx_ref, proj(3))
    rg_ref[...] = proj(4)
    gr_ref[...] = proj(5)
    ga_ref[...] = proj(6)


def _inproj(x, x_ctx, sh, sc, g, w, cos_t, sin_t, n_ctx_tiles):
    b, _, d = x.shape
    lt = x.shape[1] + (0 if x_ctx is None else x_ctx.shape[1])
    nt = lt // ROW_TILE
    mod_map = lambda bi, i: (jnp.where(i < n_ctx_tiles, b, bi), 0, 0)
    tok = pl.BlockSpec((None, ROW_TILE, d), lambda bi, i: (bi, i, 0))
    x_specs, x_args = _stream_specs(x, x_ctx, n_ctx_tiles)
    out_bf = jax.ShapeDtypeStruct((b, lt, d), BF16)
    out_f = jax.ShapeDtypeStruct((b, lt, d), F32)
    return pl.pallas_call(
        functools.partial(_inproj_kernel, d=d, split=x_ctx is not None, n_ctx_tiles=n_ctx_tiles),
        grid=(b, nt),
        in_specs=x_specs + [
                  pl.BlockSpec((None, 1, d), mod_map),
                  pl.BlockSpec((None, 1, d), mod_map),
                  pl.BlockSpec((1, d), lambda bi, i: (0, 0)),
                  pl.BlockSpec(w.shape, lambda bi, i: (0, 0)),
                  pl.BlockSpec((ROW_TILE, LANES), lambda bi, i: (i, 0)),
                  pl.BlockSpec((ROW_TILE, LANES), lambda bi, i: (i, 0))],
        out_specs=[tok, pl.BlockSpec((None, d, ROW_TILE), lambda bi, i: (bi, 0, i)), tok,
                   pl.BlockSpec((None, ROW_TILE * SUBLANES, LANES), lambda bi, i: (bi, i, 0)), tok, tok, tok],
        out_shape=[out_bf, jax.ShapeDtypeStruct((b, d, lt), BF16), out_bf,
                   jax.ShapeDtypeStruct((b, lt * SUBLANES, LANES), F32), out_f, out_f, out_f],
        compiler_params=_cparams(("parallel", "arbitrary"), 56),
        name="inproj",
    )(*x_args, sh, sc, g, w, cos_t, sin_t)


def _scan_chunk(i, n_ctx, n_chunks, reverse):
    if not reverse:
        return i
    return jnp.where(i < n_ctx, n_ctx - 1 - i, n_chunks - 1 - (i - n_ctx))


def _scan_kernel(*refs, reverse, tt, n_ctx, n_chunks):
    if reverse:
        (rx_ref, prev_ref, next_ref, cw_ref, cb_ref, wg_ref, bg_ref, c_ref, hf_ref,
         out_ref, ext_scr, u_scr, a_scr, b_scr, h_scr) = refs
    else:
        (rx_ref, prev_ref, next_ref, cw_ref, cb_ref, wg_ref, bg_ref, c_ref,
         out_ref, ext_scr, u_scr, a_scr, b_scr, h_scr) = refs
        hf_ref = None
    i = pl.program_id(0)
    c = _scan_chunk(i, n_ctx, n_chunks, reverse)
    seq_start = jnp.logical_or(c == 0, c == n_ctx)
    seq_end = jnp.logical_or(c == n_ctx - 1, c == n_chunks - 1)
    nb = rx_ref.shape[0]
    tr = tt * SUBLANES
    left = CONV_LEFT * SUBLANES

    ext_scr[:, 0:left, :] = jnp.where(seq_start, 0.0, prev_ref[...])
    ext_scr[:, left:left + tr, :] = rx_ref[...]
    ext_scr[:, left + tr:left + tr + SUBLANES, :] = jnp.where(seq_end, 0.0, next_ref[...])
    u = cb_ref[...] + cw_ref[0] * ext_scr[:, 0:tr, :].reshape(nb, tt, SUBLANES, LANES)
    for j in range(1, CONV_W):
        u = u + cw_ref[j] * ext_scr[:, j * SUBLANES:j * SUBLANES + tr, :].reshape(nb, tt, SUBLANES, LANES)
    u_scr[...] = u.reshape(nb * tr, LANES)

    rows = nb * tt
    for n in range(RNN_BLOCKS):
        un = u_scr[pl.ds(n, rows, stride=SUBLANES), :]
        z = jnp.dot(un.astype(BF16), wg_ref[n], preferred_element_type=F32) + bg_ref[n]
        r = _sigmoid(z[:, :RNN_BW])
        ig = _sigmoid(z[:, RNN_BW:])
        a = jnp.exp(-c_ref[n:n + 1, :] * r)
        a_scr[pl.ds(n, rows, stride=SUBLANES), :] = a
        b_scr[pl.ds(n, rows, stride=SUBLANES), :] = jnp.sqrt(1.0 - a * a) * ig * un

    @pl.when(i == 0)
    def _():
        h_scr[...] = jnp.zeros_like(h_scr)

    def step(s, hs):
        t = tt - 1 - s if reverse else s
        new = []
        for bi in range(nb):
            at = pl.ds(pl.multiple_of((bi * tt + t) * SUBLANES, SUBLANES), SUBLANES)
            h = a_scr[at, :] * hs[bi] + b_scr[at, :]
            ot = pl.ds(pl.multiple_of(t * SUBLANES, SUBLANES), SUBLANES)
            out_ref[bi, ot, :] = hf_ref[bi, ot, :] + h if reverse else h
            new.append(h)
        return tuple(new)

    hs = lax.fori_loop(0, tt, step, tuple(h_scr[bi] for bi in range(nb)), unroll=4)
    for bi in range(nb):
        h_scr[bi] = hs[bi]


def _scan(rx, cw, cb, wg, bg, cdec, hf, n_ctx, reverse):
    b, lt8, _ = rx.shape
    tt = SCAN_CHUNK
    n_chunks = lt8 // (tt * SUBLANES)
    chunk = lambda i: _scan_chunk(i, n_ctx, n_chunks, reverse)
    blk = pl.BlockSpec((b, tt * SUBLANES, LANES), lambda i: (0, chunk(i), 0))
    const2 = lambda i: (0, 0)
    const3 = lambda i: (0, 0, 0)
    in_specs = [blk,
                pl.BlockSpec((b, CONV_LEFT * SUBLANES, LANES),
                             lambda i: (0, jnp.maximum(chunk(i) * (tt // CONV_LEFT) - 1, 0), 0)),
                pl.BlockSpec((b, SUBLANES, LANES), lambda i: (0, jnp.minimum((chunk(i) + 1) * tt, n_chunks * tt - 1), 0)),
                pl.BlockSpec(cw.shape, const3),
                pl.BlockSpec(cb.shape, const2),
                pl.BlockSpec(wg.shape, const3),
                pl.BlockSpec(bg.shape, const3),
                pl.BlockSpec(cdec.shape, const2)]
    args = [rx, rx, rx, cw, cb, wg, bg, cdec]
    if reverse:
        in_specs.append(blk)
        args.append(hf)
    tr = tt * SUBLANES
    return pl.pallas_call(
        functools.partial(_scan_kernel, reverse=reverse, tt=tt, n_ctx=n_ctx, n_chunks=n_chunks),
        grid=(n_chunks,),
        in_specs=in_specs,
        out_specs=blk,
        out_shape=jax.ShapeDtypeStruct(rx.shape, F32),
        scratch_shapes=[pltpu.VMEM((b, tr + (CONV_W - 1) * SUBLANES, LANES), F32),
                        pltpu.VMEM((b * tr, LANES), F32),
                        pltpu.VMEM((b * tr, LANES), F32),
                        pltpu.VMEM((b * tr, LANES), F32),
                        pltpu.VMEM((b, SUBLANES, LANES), F32)],
        compiler_params=_cparams(("arbitrary",), 56),
        name="scan_bwd" if reverse else "scan_fwd",
    )(*args)


def _attn_passes(lam, q_refs, k_ref, vt_ref, g_ref, o_ref, *, write, read, tq, w_chunks, r_chunks, out_scale):
    n2 = 2 * tq
    if write is not None:
        s_w, m_w = write
        q = jnp.concatenate([r[...] for r in q_refs], axis=0) if len(q_refs) > 1 else q_refs[0][...]
        lane = lax.broadcasted_iota(jnp.int32, q.shape, 1)
        first = (lane // (HEAD_DIM // 2)) % 2 == 0
        zero = jnp.zeros_like(q)
        qs = jnp.concatenate([jnp.where(first, q, zero), jnp.where(first, zero, q)], axis=0)
        m8 = jnp.full((SUBLANES, n2), NEG_BIG, F32)
    if read is not None:
        s_r, m_r = read
        m = m_r[...]
        l8 = jnp.zeros((SUBLANES, n2), F32)
        acc = jnp.zeros((V_DIM, n2), F32)
    for j in range(max(len(w_chunks) if write is not None else 0, len(r_chunks) if read is not None else 0)):
        if write is not None and j < len(w_chunks):
            start, size = w_chunks[j]
            st = lax.dot_general(k_ref[start:start + size, :], qs, (((1,), (1,)), ((), ())),
                                 preferred_element_type=F32)
            s_w[start:start + size, :] = st
            m8 = jnp.maximum(m8, jnp.max(st.reshape(size // SUBLANES, SUBLANES, n2), axis=0))
        if read is not None and j < len(r_chunks):
            start, size = r_chunks[j]
            p = jnp.exp2(s_r[start:start + size, :] - m)
            l8 = l8 + jnp.sum(p.reshape(size // SUBLANES, SUBLANES, n2), axis=0)
            acc = acc + jnp.dot(vt_ref[:, start:start + size], p.astype(BF16), preferred_element_type=F32)
    if write is not None:
        m_w[...] = jnp.max(m8, axis=0, keepdims=True)
    if read is not None:
        on = acc * (1.0 / jnp.sum(l8, axis=0, keepdims=True))
        dt = on[:, :tq] - lam * on[:, tq:]
        yt = dt * lax.rsqrt(jnp.mean(dt * dt, axis=0, keepdims=True) + EPS)
        o_ref[...] = (yt.T * g_ref[...] * out_scale).astype(BF16)


def _attn_pipe_kernel(lam_ref, *refs, tq, nq, n_q_refs, chunks, out_scale):
    q_refs = refs[:n_q_refs]
    k_ref, vt_ref, g_ref, o_ref, s_a, s_b, m_a, m_b = refs[n_q_refs:]
    s = pl.program_id(2)
    run = functools.partial(_attn_passes, lam_ref[0], q_refs, k_ref, vt_ref, g_ref, o_ref, tq=tq,
                            w_chunks=chunks, r_chunks=chunks, out_scale=out_scale)
    buf_a, buf_b = (s_a, m_a), (s_b, m_b)
    inner = jnp.logical_and(s > 0, s < nq)
    even = s % 2 == 0

    @pl.when(s == 0)
    def _():
        run(write=buf_a, read=None)

    @pl.when(jnp.logical_and(inner, even))
    def _():
        run(write=buf_a, read=buf_b)

    @pl.when(jnp.logical_and(inner, jnp.logical_not(even)))
    def _():
        run(write=buf_b, read=buf_a)

    @pl.when(s == nq)
    def _():
        run(write=None, read=buf_a if (nq - 1) % 2 == 0 else buf_b)


def _attn_single_kernel(lam_ref, q_ref, k_ref, vt_ref, g_ref, o_ref, s_a, m_a, *, tq, chunks, out_scale):
    run = functools.partial(_attn_passes, lam_ref[0], (q_ref,), k_ref, vt_ref, g_ref, o_ref, tq=tq,
                            w_chunks=chunks, r_chunks=chunks, out_scale=out_scale)
    run(write=(s_a, m_a), read=None)
    run(write=None, read=(s_a, m_a))


def _key_chunks(n_ctx_rows, lt):
    ctx = [(s, min(ATTN_TK, n_ctx_rows - s)) for s in range(0, n_ctx_rows, ATTN_TK)]
    lat = [(s, min(ATTN_TK, lt - s)) for s in range(n_ctx_rows, lt, ATTN_TK)]
    return tuple(ctx), tuple(ctx + lat)


def _attention(lam, q, k, vt, g_sub, n_ctx_rows, with_ctx_queries, out_scale):
    b, lt, d = q.shape
    tq, tqc = ATTN_TQ, ATTN_CTX_TQ
    ctx_q_tiles = n_ctx_rows // tqc
    n_sub = tq // tqc
    nq = (lt - n_ctx_rows) // tq
    ctx_chunks, all_chunks = _key_chunks(n_ctx_rows, lt)
    n2 = 2 * tq
    lam_in = pl.BlockSpec(memory_space=pltpu.SMEM)
    g_in = pl.BlockSpec((1, V_DIM), lambda bi, h, s: (0, 0))
    lat = pl.pallas_call(
        functools.partial(_attn_pipe_kernel, tq=tq, nq=nq, n_q_refs=n_sub, chunks=all_chunks,
                          out_scale=out_scale),
        grid=(b, N_HEADS, nq + 1),
        in_specs=[lam_in] + [
                  pl.BlockSpec((None, tqc, V_DIM),
                               lambda bi, h, s, j=j: (bi, jnp.minimum(s, nq - 1) * n_sub + ctx_q_tiles + j, h))
                  for j in range(n_sub)] + [
                  pl.BlockSpec((None, lt, V_DIM), lambda bi, h, s: (bi, 0, h)),
                  pl.BlockSpec((None, V_DIM, lt), lambda bi, h, s: (bi, h, 0)),
                  g_in],
        out_specs=pl.BlockSpec((None, tq, V_DIM), lambda bi, h, s: (bi, jnp.maximum(s - 1, 0), h)),
        out_shape=jax.ShapeDtypeStruct((b, nq * tq, d), BF16),
        scratch_shapes=[pltpu.VMEM((lt, n2), F32), pltpu.VMEM((lt, n2), F32),
                        pltpu.VMEM((1, n2), F32), pltpu.VMEM((1, n2), F32)],
        compiler_params=_cparams(("parallel", "parallel", "arbitrary"), 56),
        name="diff_attn",
    )(lam, *([q] * n_sub), k, vt, g_sub)
    if not with_ctx_queries:
        return lat, None
    ctx = pl.pallas_call(
        functools.partial(_attn_single_kernel, tq=tqc, chunks=ctx_chunks, out_scale=out_scale),
        grid=(b, N_HEADS, ctx_q_tiles),
        in_specs=[lam_in,
                  pl.BlockSpec((None, tqc, V_DIM), lambda bi, h, s: (bi, s, h)),
                  pl.BlockSpec((None, n_ctx_rows, V_DIM), lambda bi, h, s: (bi, 0, h)),
                  pl.BlockSpec((None, V_DIM, n_ctx_rows), lambda bi, h, s: (bi, h, 0)),
                  g_in],
        out_specs=pl.BlockSpec((None, tqc, V_DIM), lambda bi, h, s: (bi, s, h)),
        out_shape=jax.ShapeDtypeStruct((b, n_ctx_rows, d), BF16),
        scratch_shapes=[pltpu.VMEM((n_ctx_rows, 2 * tqc), F32), pltpu.VMEM((1, 2 * tqc), F32)],
        compiler_params=_cparams(("parallel", "parallel", "arbitrary"), 32),
        name="diff_attn_ctx",
    )(lam, q, k, vt, g_sub)
    return lat, ctx


def _postmix_kernel(*refs, has_ctx, split, n_ctx_tiles, nt, n_tiles):
    refs = list(refs)
    x_ref = refs.pop(0)
    xc_ref = refs.pop(0) if split else None
    at_ref = refs.pop(0)
    atc_ref = refs.pop(0) if has_ctx else None
    (y_ref, rg_ref, gr_ref, ga_ref, gt1_ref, sh2_ref, sc2_ref, g2_ref,
     wr_ref, wa_ref, wo_ref, wrh_ref, wrl_ref, brt_ref, x1_ref, h2_ref, route_ref, cnt_ref,
     run_scr, hs_scr) = refs
    g = pl.program_id(0)

    @pl.when(g == 0)
    def _():
        run_scr[...] = jnp.zeros_like(run_scr)
        hs_scr[...] = jnp.zeros_like(hs_scr)

    h_hi = hs_scr[0]
    h_lo = hs_scr[1]
    nt_dims = (((1,), (1,)), ((), ()))
    lg = (lax.dot_general(wrh_ref[...], h_hi, nt_dims, preferred_element_type=F32)
          + lax.dot_general(wrh_ref[...], h_lo, nt_dims, preferred_element_type=F32)
          + lax.dot_general(wrl_ref[...], h_hi, nt_dims, preferred_element_type=F32)) + brt_ref[:, 0:1]

    is_ctx = jnp.minimum(g, n_tiles - 1) % nt < n_ctx_tiles
    x = _stream_tile(x_ref, xc_ref, is_ctx)
    at = _stream_tile(at_ref, atc_ref, is_ctx)
    yg = (_load_tiled(y_ref) * _gelu_tanh(rg_ref[...])).astype(BF16)
    o_r = jnp.dot(yg, wr_ref[...], preferred_element_type=F32)
    o_a = jnp.dot(at, wa_ref[...], preferred_element_type=F32)
    mix = (_sigmoid(gr_ref[...]) * o_r + _sigmoid(ga_ref[...]) * o_a).astype(BF16)
    x1 = x + gt1_ref[...] * jnp.dot(mix, wo_ref[...], preferred_element_type=F32)
    x1_ref[...] = x1
    h2 = _rms(x1, g2_ref[...]) * (1.0 + sc2_ref[...]) + sh2_ref[...]
    _store_tiled(h2_ref, h2)
    h2_hi = h2.astype(BF16)
    hs_scr[0] = h2_hi
    hs_scr[1] = (h2 - h2_hi.astype(F32)).astype(BF16)

    rows = h_hi.shape[0]
    ex = lax.broadcasted_iota(jnp.int32, lg.shape, 0)
    ex_f = ex.astype(F32)
    e = jnp.exp(lg - jnp.max(lg, axis=0, keepdims=True))
    p = e / jnp.sum(e, axis=0, keepdims=True)
    grp = ex // EXPERTS_PER_GROUP
    best = jnp.max(jnp.where(grp == 0, p, -1.0), axis=0, keepdims=True)
    best_g = jnp.zeros_like(best, dtype=jnp.int32)
    for gi in range(1, N_GROUPS):
        gm = jnp.max(jnp.where(grp == gi, p, -1.0), axis=0, keepdims=True)
        better = gm > best
        best_g = jnp.where(better, gi, best_g)
        best = jnp.where(better, gm, best)
    pg = jnp.where(grp == best_g, p, -1.0)
    big = float(N_EXPERTS)
    v1 = jnp.max(pg, axis=0, keepdims=True)
    i1 = jnp.min(jnp.where(pg == v1, ex_f, big), axis=0, keepdims=True)
    pg2 = jnp.where(ex_f == i1, -1.0, pg)
    v2 = jnp.max(pg2, axis=0, keepdims=True)
    i2 = jnp.min(jnp.where(pg2 == v2, ex_f, big), axis=0, keepdims=True)
    den = v1 + v2

    live = (g > 0).astype(F32)
    oh1 = (ex_f == i1).astype(F32)
    oh2 = (ex_f == i2).astype(F32)
    both = (oh1 + oh2) * live
    tri = (lax.broadcasted_iota(jnp.int32, (rows, rows), 0)
           < lax.broadcasted_iota(jnp.int32, (rows, rows), 1)).astype(BF16)
    before = jnp.dot(both.astype(BF16), tri, preferred_element_type=F32) + run_scr[:, 0:1]
    r1 = jnp.sum(oh1 * before, axis=0, keepdims=True)
    r2 = jnp.sum(oh2 * before, axis=0, keepdims=True)
    run = run_scr[...] + jnp.sum(both, axis=1, keepdims=True)
    run_scr[...] = run
    cnt_ref[...] = run
    row = lax.broadcasted_iota(jnp.int32, route_ref.shape, 0)
    route_ref[...] = jnp.where(row == 0, i1, jnp.where(row == 1, i2, jnp.where(row == 2, v1 / den, jnp.where(
        row == 3, v2 / den, jnp.where(row == 4, r1, r2)))))


def _postmix(x, x_ctx, attn, attn_ctx, y, rg, gr, ga, gt1, sh2, sc2, g2, wr, wa, wo, wrh, wrl, brt, n_ctx_tiles):
    b, lt, d = rg.shape
    has_ctx = attn_ctx is not None
    off = 0 if has_ctx else n_ctx_tiles
    lq = lt - off * ROW_TILE
    nt = lq // ROW_TILE
    n_tiles = b * nt

    def wrap(f):
        def index_map(g):
            t = jnp.minimum(g, n_tiles - 1)
            return f(t // nt, t % nt)
        return index_map

    full = pl.BlockSpec((None, ROW_TILE, d), wrap(lambda bi, i: (bi, i + off, 0)))
    part = pl.BlockSpec((None, ROW_TILE, d), wrap(lambda bi, i: (bi, i, 0)))
    tiled = pl.BlockSpec((None, ROW_TILE * SUBLANES, LANES), wrap(lambda bi, i: (bi, i + off, 0)))
    mod = pl.BlockSpec((None, 1, d), wrap(lambda bi, i: (jnp.where(i + off < n_ctx_tiles, b, bi), 0, 0)))
    const = lambda g: (0, 0)
    if x_ctx is None:
        x_specs, x_args = [full], [x]
    else:
        x_specs, x_args = _stream_specs(x, x_ctx, n_ctx_tiles, wrap)
    at_specs, at_args = _stream_specs(attn, attn_ctx, n_ctx_tiles, wrap)
    return pl.pallas_call(
        functools.partial(_postmix_kernel, has_ctx=has_ctx, split=x_ctx is not None, n_ctx_tiles=n_ctx_tiles,
                          nt=nt, n_tiles=n_tiles),
        grid=(n_tiles + 1,),
        in_specs=x_specs + at_specs + [tiled, full, full, full, mod, mod, mod,
                                      pl.BlockSpec((1, d), const),
                                      pl.BlockSpec((d, d), const), pl.BlockSpec((d, d), const),
                                      pl.BlockSpec((d, d), const),
                                      pl.BlockSpec((N_EXPERTS, d), const), pl.BlockSpec((N_EXPERTS, d), const),
                                      pl.BlockSpec((N_EXPERTS, LANES), const)],
        out_specs=[part, pl.BlockSpec((ROW_TILE * SUBLANES, LANES), wrap(lambda bi, i: (bi * nt + i, 0))),
                   pl.BlockSpec((SUBLANES, ROW_TILE), lambda g: (0, jnp.maximum(g - 1, 0))),
                   pl.BlockSpec((N_EXPERTS, LANES), const)],
        out_shape=[jax.ShapeDtypeStruct((b, lq, d), F32),
                   jax.ShapeDtypeStruct((b * lq * SUBLANES, LANES), F32),
                   jax.ShapeDtypeStruct((SUBLANES, b * lq), F32),
                   jax.ShapeDtypeStruct((N_EXPERTS, LANES), F32)],
        scratch_shapes=[pltpu.VMEM((N_EXPERTS, LANES), F32), pltpu.VMEM((2, ROW_TILE, d), BF16)],
        compiler_params=_cparams(("arbitrary",), 48),
        name="postmix",
    )(*x_args, *at_args, y, rg, gr, ga, gt1, sh2, sc2, g2, wr, wa, wo, wrh, wrl, brt)


def _dispatch_kernel(pad_ref, dest_ref, h_ref, xs_hbm, zrows, sem, *, n_pad):
    i = pl.program_id(0)
    rows = h_ref.shape[0] // SUBLANES

    def tile(ref, r):
        return ref.at[pl.ds(pl.multiple_of(r * SUBLANES, SUBLANES), SUBLANES)]

    @pl.when(i == 0)
    def _():
        zrows[...] = jnp.zeros_like(zrows)
        for c in range(n_pad // MOE_ROWS):
            def fill(r, carry):
                pltpu.make_async_copy(tile(zrows, r), tile(xs_hbm, pad_ref[c * MOE_ROWS + r]), sem).start()
                return carry

            lax.fori_loop(0, MOE_ROWS, fill, 0, unroll=8)
            pltpu.make_async_copy(zrows, xs_hbm.at[pl.ds(0, MOE_ROWS * SUBLANES)], sem).wait()

    def send(r, carry):
        for k in range(TOP_K):
            pltpu.make_async_copy(tile(h_ref, r), tile(xs_hbm, dest_ref[0, k * rows + r]), sem).start()
        return carry

    lax.fori_loop(0, rows, send, 0, unroll=8)
    for k in range(TOP_K):
        pltpu.make_async_copy(h_ref, xs_hbm.at[pl.ds(0, rows * SUBLANES)], sem).wait()


def _dispatch(pad_slots, dest, h2, n_slots):
    t = h2.shape[0] // SUBLANES
    rows = DISPATCH_ROWS
    return pl.pallas_call(
        functools.partial(_dispatch_kernel, n_pad=pad_slots.shape[0]),
        grid_spec=pltpu.PrefetchScalarGridSpec(
            num_scalar_prefetch=1,
            grid=(t // rows,),
            in_specs=[pl.BlockSpec((None, 1, TOP_K * rows), lambda i, pad: (i, 0, 0), memory_space=pltpu.SMEM),
                      pl.BlockSpec((rows * SUBLANES, LANES), lambda i, pad: (i, 0))],
            out_specs=pl.BlockSpec(memory_space=pl.ANY),
            scratch_shapes=[pltpu.VMEM((MOE_ROWS * SUBLANES, LANES), F32), pltpu.SemaphoreType.DMA(())]),
        out_shape=jax.ShapeDtypeStruct((n_slots * SUBLANES, LANES), F32),
        compiler_params=_cparams(("arbitrary",), 32),
        name="dispatch",
    )(pad_slots, _tile_table(dest, rows), h2)


def _expert_kernel(be_ref, nv_ref, x_ref, w1_ref, w3_ref, w2_ref, y_ref, w1_b, w3_b, w2_b):
    i = pl.program_id(0)

    @pl.when(i < nv_ref[0])
    def _():
        @pl.when(jnp.logical_or(i == 0, be_ref[i] != be_ref[jnp.maximum(i - 1, 0)]))
        def _():
            w1_b[...] = w1_ref[...].astype(BF16)
            w3_b[...] = w3_ref[...].astype(BF16)
            w2_b[...] = w2_ref[...].astype(BF16)

        xb = _load_tiled(x_ref).astype(BF16)
        a1 = jnp.dot(xb, w1_b[...], preferred_element_type=F32)
        a3 = jnp.dot(xb, w3_b[...], preferred_element_type=F32)
        mid = (a1 * _sigmoid(a1) * a3).astype(BF16)
        _store_tiled(y_ref, jnp.dot(mid, w2_b[...], preferred_element_type=F32))

    @pl.when(i >= nv_ref[0])
    def _():
        y_ref[...] = jnp.zeros_like(y_ref)


def _experts(block_e, n_valid, xs, w1, w3, w2, layer):
    n_slots = xs.shape[0] // SUBLANES
    d, de = w1.shape[-2:]
    wspec = lambda shp: pl.BlockSpec((None, None) + shp, lambda i, be, nv: (layer, be[i], 0, 0))
    return pl.pallas_call(
        _expert_kernel,
        grid_spec=pltpu.PrefetchScalarGridSpec(
            num_scalar_prefetch=2,
            grid=(n_slots // MOE_ROWS,),
            in_specs=[pl.BlockSpec((MOE_ROWS * SUBLANES, LANES), lambda i, be, nv: (jnp.minimum(i, nv[0] - 1), 0)),
                      wspec((d, de)), wspec((d, de)), wspec((de, d))],
            out_specs=pl.BlockSpec((MOE_ROWS * SUBLANES, LANES), lambda i, be, nv: (i, 0)),
            scratch_shapes=[pltpu.VMEM((d, de), BF16), pltpu.VMEM((d, de), BF16), pltpu.VMEM((de, d), BF16)]),
        out_shape=jax.ShapeDtypeStruct((n_slots * SUBLANES, LANES), F32),
        compiler_params=_cparams(("arbitrary",), 56),
        name="experts",
    )(block_e, n_valid, xs, w1, w3, w2)


def _tile_table(dest, rows):
    n_tiles = dest.shape[1] // rows
    return dest.reshape(TOP_K, n_tiles, rows).transpose(1, 0, 2).reshape(n_tiles, 1, TOP_K * rows)


def _route_tables(route, counts):
    expert = route[0:TOP_K].astype(jnp.int32)
    rank = route[4:4 + TOP_K].astype(jnp.int32)
    a = expert.size
    sizes = counts[:, 0].astype(jnp.int32)
    padded = (sizes + MOE_ROWS - 1) // MOE_ROWS * MOE_ROWS
    pend = jnp.cumsum(padded)
    pstarts = pend - padded
    dest = (pstarts[expert] + rank).astype(jnp.int32)
    n_blocks = -(-a // MOE_ROWS) + N_EXPERTS
    n_slots = n_blocks * MOE_ROWS
    n_valid = (pend[-1] // MOE_ROWS).astype(jnp.int32)
    blk = jnp.minimum(jnp.arange(n_blocks, dtype=jnp.int32), n_valid - 1)
    block_e = jnp.minimum(jnp.sum(blk[:, None] * MOE_ROWS >= pend[None, :], axis=1),
                          N_EXPERTS - 1).astype(jnp.int32)
    gap_start = jnp.concatenate([pstarts + sizes, pend[-1:]]).astype(jnp.int32)
    gap_len = jnp.concatenate([padded - sizes, n_slots - pend[-1:]]).astype(jnp.int32)
    gap_end = jnp.cumsum(gap_len)
    j = jnp.arange(n_slots - a, dtype=jnp.int32)
    seg = jnp.sum(j[:, None] >= gap_end[None, :], axis=1)
    pad_slots = (gap_start[seg] + j - (gap_end - gap_len)[seg]).astype(jnp.int32)
    return dest, pad_slots, block_e, n_valid.reshape(1), n_slots


def _combine_kernel(dcur_ref, dnxt_ref, x_ref, gate_ref, gt2_ref, g_ref, y_hbm, o_ref, ybuf, sems, *, final):
    g = pl.program_id(0)
    n = pl.num_programs(0)
    rows = x_ref.shape[0]
    slot = g % 2

    def fetch(tbl_ref, s):
        def body(r, carry):
            for k in range(TOP_K):
                src = pl.multiple_of(tbl_ref[0, k * rows + r] * SUBLANES, SUBLANES)
                pltpu.make_async_copy(y_hbm.at[pl.ds(src, SUBLANES)],
                                      ybuf.at[s, k, pl.ds(pl.multiple_of(r * SUBLANES, SUBLANES), SUBLANES)],
                                      sems.at[s]).start()
            return carry

        lax.fori_loop(0, rows, body, 0, unroll=8)

    @pl.when(g == 0)
    def _():
        fetch(dcur_ref, 0)

    @pl.when(g + 1 < n)
    def _():
        fetch(dnxt_ref, 1 - slot)

    for k in range(TOP_K):
        pltpu.make_async_copy(y_hbm.at[pl.ds(0, rows * SUBLANES)], ybuf.at[slot, k], sems.at[slot]).wait()

    gate = gate_ref[...]
    y = gate[:, 0:1] * _load_tiled(ybuf.at[slot, 0]) + gate[:, 1:2] * _load_tiled(ybuf.at[slot, 1])
    x2 = x_ref[...] + gt2_ref[...] * y
    o_ref[...] = _rms(x2, g_ref[...]) if final else x2


def _combine(x1, y_slots, dest, gates, gt2, g_final, n_ctx_tiles, has_ctx, final):
    b, lq, d = x1.shape
    nt = lq // ROW_TILE
    n_tiles = b * nt
    ctx_tiles = n_ctx_tiles if has_ctx else 0
    tok = lambda w: pl.BlockSpec((None, ROW_TILE, w), lambda g: (g // nt, g % nt, 0))
    tbl = lambda f: pl.BlockSpec((None, 1, TOP_K * ROW_TILE), lambda g: (f(g), 0, 0), memory_space=pltpu.SMEM)
    dest3 = _tile_table(dest, ROW_TILE)
    return pl.pallas_call(
        functools.partial(_combine_kernel, final=final),
        grid=(n_tiles,),
        in_specs=[tbl(lambda g: g), tbl(lambda g: jnp.minimum(g + 1, n_tiles - 1)),
                  tok(d), pl.BlockSpec((ROW_TILE, TOP_K), lambda g: (g, 0)),
                  pl.BlockSpec((None, 1, d), lambda g: (jnp.where(g % nt < ctx_tiles, b, g // nt), 0, 0)),
                  pl.BlockSpec((1, d), lambda g: (0, 0)),
                  pl.BlockSpec(memory_space=pl.ANY)],
        out_specs=tok(d),
        out_shape=jax.ShapeDtypeStruct((b, lq, d), F32),
        scratch_shapes=[pltpu.VMEM((2, TOP_K, ROW_TILE * SUBLANES, LANES), F32), pltpu.SemaphoreType.DMA((2,))],
        compiler_params=_cparams(("arbitrary",), 32),
        name="combine",
    )(dest3, dest3, x1, gates, gt2, g_final, y_slots)


def _rope_tables(n_ctx, seq):
    n_pairs = HEAD_DIM // 4
    inv = ROPE_BASE ** (-jnp.arange(n_pairs, dtype=F32) / n_pairs)
    rows = seq // GRID_W
    r = jnp.repeat(jnp.arange(rows, dtype=F32), GRID_W)
    col = jnp.tile(jnp.arange(GRID_W, dtype=F32), rows)
    ang = jnp.concatenate([r[:, None] * inv, col[:, None] * inv], axis=-1)
    cos = jnp.concatenate([jnp.ones((n_ctx, HEAD_DIM // 2), F32), jnp.cos(ang)], axis=0)
    sin = jnp.concatenate([jnp.zeros((n_ctx, HEAD_DIM // 2), F32), jnp.sin(ang)], axis=0)
    return jnp.tile(cos, (1, 4)), jnp.concatenate([-sin, -sin, sin, sin], axis=-1)


def _qk_perm():
    lane = jnp.arange(LANES)
    g, i = lane // 32, lane % 32
    within = (g % 2) * HEAD_DIM + 2 * i + g // 2
    return (jnp.arange(N_HEADS)[:, None] * LANES + within[None, :]).reshape(-1)


def kernel(x, c, ctx, c_ctx, w_mod, b_mod, g_norm1, g_norm2, w_in, conv_w, conv_b, lru_wa, lru_ba, lru_wx,
           lru_bx, lru_lambda, diff_lambda, g_subln, w_rnn_proj, w_attn_proj, w_o, w_router, b_router,
           w_e1, w_e3, w_e2, g_final):
    b, seq, d = x.shape
    n_ctx = ctx.shape[1]
    depth = w_mod.shape[0]
    n_ctx_tiles = n_ctx // ROW_TILE
    assert d == N_HEADS * V_DIM == SUBLANES * LANES and n_ctx % ROW_TILE == 0 and seq % ROW_TILE == 0

    cond = jnp.zeros((MOD_ROWS, d), F32).at[:b].set(c).at[b].set(c_ctx)
    mods = _adaln(cond, w_mod, b_mod).reshape(depth, MOD_ROWS, 6, 1, d)
    cos_t, sin_t = _rope_tables(n_ctx, seq)
    perm = _qk_perm()
    wrt = w_router.T
    wrt_hi = wrt.astype(BF16)
    wrt_lo = (wrt - wrt_hi.astype(F32)).astype(BF16)
    brt = jnp.broadcast_to(b_router.astype(F32)[:, None], (N_EXPERTS, LANES))

    xs, xs_ctx = x, ctx
    out = None
    for l in range(depth):
        last = l == depth - 1
        lam_init = 0.8 - 0.6 * math.exp(-0.3 * l)
        sh1, sc1, gt1, sh2, sc2, gt2 = [mods[l, :, j] for j in range(6)]
        wl = w_in[l]
        w_perm = jnp.concatenate(
            [wl[:, d:2 * d][:, perm], wl[:, 2 * d:3 * d], wl[:, 3 * d:4 * d][:, perm],
             wl[:, 0:d], wl[:, 4 * d:]], axis=1).astype(BF16)
        k, v, q, rx, rg, gr, ga = _inproj(xs, xs_ctx, sh1, sc1, g_norm1[l][None], w_perm, cos_t, sin_t,
                                          n_ctx_tiles)

        lq1, lk1, lq2, lk2 = diff_lambda[l].astype(F32)
        lam = (jnp.exp(jnp.sum(lq1 * lk1)) - jnp.exp(jnp.sum(lq2 * lk2)) + lam_init).reshape(1)
        attn, attn_ctx = _attention(lam, q, k, v, g_subln[l][None], n_ctx, not last, 1.0 - lam_init)

        cdec = LRU_C * jax.nn.softplus(-lru_lambda[l].astype(F32))
        tile_of = lambda vec: vec.reshape(vec.shape[:-1] + (SUBLANES, LANES))
        n_ctx_chunks = n_ctx // SCAN_CHUNK
        hf = None
        for direction in range(2):
            wg = jnp.concatenate([lru_wa[l, direction], lru_wx[l, direction]], axis=-1).astype(BF16)
            bg = jnp.concatenate([lru_ba[l, direction].reshape(RNN_BLOCKS, 1, RNN_BW),
                                  lru_bx[l, direction].reshape(RNN_BLOCKS, 1, RNN_BW)], axis=-1)
            hf = _scan(rx, tile_of(conv_w[l]), tile_of(conv_b[l]), wg, bg, tile_of(cdec[direction]), hf,
                       n_ctx_chunks, reverse=direction == 1)
        y = hf

        x1, h2, route, counts = _postmix(xs, xs_ctx, attn, attn_ctx, y, rg, gr, ga, gt1, sh2, sc2,
                                         g_norm2[l][None],
                                         w_rnn_proj[l].astype(BF16), w_attn_proj[l].astype(BF16),
                                         w_o[l].astype(BF16), wrt_hi, wrt_lo, brt, n_ctx_tiles)
        lq = x1.shape[1]
        n_tok = b * lq
        dest, pad_slots, block_e, n_valid, n_slots = _route_tables(route, counts)
        xs_sorted = _dispatch(pad_slots, dest, h2, n_slots)
        y_slots = _experts(block_e, n_valid, xs_sorted, w_e1, w_e3, w_e2, l)
        res = _combine(x1, y_slots, dest, route[2:2 + TOP_K].T, gt2, g_final[None], n_ctx_tiles, not last, last)
        if last:
            out = res
        else:
            xs, xs_ctx = res, None
    return out
```

```python
import functools
import math

import jax
import jax.numpy as jnp
from jax import lax
from jax.experimental import pallas as pl
from jax.experimental.pallas import tpu as pltpu

F32 = jnp.float32
BF16 = jnp.bfloat16

N_HEADS = 8
HEAD_DIM = 64
V_DIM = 2 * HEAD_DIM
GRID_W = 64
EPS = 1e-6
RNN_BLOCKS = 8
RNN_BW = 128
CONV_W = 4
CONV_LEFT = 2
LRU_C = 8.0
ROPE_BASE = 10000.0
N_EXPERTS = 16
N_GROUPS = 4
EXPERTS_PER_GROUP = N_EXPERTS // N_GROUPS
TOP_K = 2

LANES = 128
SUBLANES = 8
ROW_TILE = 256
SCAN_CHUNK = 128
ATTN_TQ = 256
ATTN_CTX_TQ = 256
ATTN_TK = 512
MOE_ROWS = 512
DISPATCH_ROWS = 1024
MOD_ROWS = 16
NEG_BIG = -1e30
LOG2_E = math.log2(math.e)


def _cparams(sem, vmem_mib):
    return pltpu.CompilerParams(dimension_semantics=sem, vmem_limit_bytes=vmem_mib * 1024 * 1024)


def _sigmoid(x):
    return 0.5 * jnp.tanh(0.5 * x) + 0.5


def _gelu_tanh(x):
    return 0.5 * x * (1.0 + jnp.tanh(math.sqrt(2.0 / math.pi) * (x + 0.044715 * (x * x * x))))


def _rms(x, g):
    return x * lax.rsqrt(jnp.mean(x * x, axis=-1, keepdims=True) + EPS) * g


def _store_tiled(ref, x):
    rows = x.shape[0]
    for j in range(SUBLANES):
        ref[pl.ds(j, rows, stride=SUBLANES), :] = x[:, j * LANES:(j + 1) * LANES]


def _load_tiled(ref):
    rows = ref.shape[0] // SUBLANES
    return jnp.concatenate([ref[pl.ds(j, rows, stride=SUBLANES), :] for j in range(SUBLANES)], axis=1)


def _adaln_kernel(c_ref, w_ref, b_ref, o_ref):
    c = c_ref[...]
    a = c * _sigmoid(c)
    o_ref[...] = jnp.dot(a, w_ref[...], preferred_element_type=F32,
                         precision=lax.Precision.HIGHEST) + b_ref[...]


def _adaln(cond, w_mod, b_mod):
    depth, d, d6 = w_mod.shape
    n = d6 // d
    return pl.pallas_call(
        _adaln_kernel,
        grid=(depth, n),
        in_specs=[pl.BlockSpec((MOD_ROWS, d), lambda l, j: (0, 0)),
                  pl.BlockSpec((None, d, d), lambda l, j: (l, 0, j)),
                  pl.BlockSpec((None, 1, d), lambda l, j: (l, 0, j))],
        out_specs=pl.BlockSpec((None, MOD_ROWS, d), lambda l, j: (l, 0, j)),
        out_shape=jax.ShapeDtypeStruct((depth, MOD_ROWS, d6), F32),
        compiler_params=_cparams(("arbitrary", "arbitrary"), 32),
        name="adaln",
    )(cond, w_mod, b_mod.reshape(depth, 1, d6))


def _stream_tile(x_ref, xc_ref, is_ctx):
    if xc_ref is None:
        return x_ref[...]
    return jnp.where(is_ctx, xc_ref[...], x_ref[...])


def _stream_specs(x, x_ctx, n_ctx_tiles, wrap=lambda f: f):
    d = x.shape[-1]
    if x_ctx is None:
        return [pl.BlockSpec((None, ROW_TILE, d), wrap(lambda bi, i: (bi, i, 0)))], [x]
    return ([pl.BlockSpec((None, ROW_TILE, d), wrap(lambda bi, i: (bi, jnp.maximum(i - n_ctx_tiles, 0), 0))),
             pl.BlockSpec((None, ROW_TILE, d), wrap(lambda bi, i: (bi, jnp.minimum(i, n_ctx_tiles - 1), 0)))],
            [x, x_ctx])


def _inproj_kernel(*refs, d, split, n_ctx_tiles):
    if split:
        (x_ref, xc_ref, sh_ref, sc_ref, g_ref, w_ref, cos_ref, sin_ref,
         k_ref, v_ref, q_ref, rx_ref, rg_ref, gr_ref, ga_ref) = refs
    else:
        (x_ref, sh_ref, sc_ref, g_ref, w_ref, cos_ref, sin_ref,
         k_ref, v_ref, q_ref, rx_ref, rg_ref, gr_ref, ga_ref) = refs
        xc_ref = None
    x = _stream_tile(x_ref, xc_ref, pl.program_id(1) < n_ctx_tiles)
    h = (_rms(x, g_ref[...]) * (1.0 + sc_ref[...]) + sh_ref[...]).astype(BF16)
    cosv = cos_ref[...]
    sinv = sin_ref[...]

    def proj(p):
        return jnp.dot(h, w_ref[:, p * d:(p + 1) * d], preferred_element_type=F32)

    def rope_store(res, ref, scale):
        for hh in range(N_HEADS):
            xs = res[:, hh * LANES:(hh + 1) * LANES]
            out = xs * cosv + pltpu.roll(xs, LANES // 2, 1) * sinv
            ref[:, hh * LANES:(hh + 1) * LANES] = (out * scale).astype(BF16)

    rope_store(proj(0), k_ref, 1.0)
    v_ref[...] = proj(1).T.astype(BF16)
    rope_store(proj(2), q_ref, HEAD_DIM ** -0.5 * LOG2_E)
    _store_tiled(rx_ref, proj(3))
    rg_ref[...] = proj(4)
    gr_ref[...] = proj(5)
    ga_ref[...] = proj(6)


def _inproj(x, x_ctx, sh, sc, g, w, cos_t, sin_t, n_ctx_tiles):
    b, _, d = x.shape
    lt = x.shape[1] + (0 if x_ctx is None else x_ctx.shape[1])
    nt = lt // ROW_TILE
    mod_map = lambda bi, i: (jnp.where(i < n_ctx_tiles, b, bi), 0, 0)
    tok = pl.BlockSpec((None, ROW_TILE, d), lambda bi, i: (bi, i, 0))
    x_specs, x_args = _stream_specs(x, x_ctx, n_ctx_tiles)
    out_bf = jax.ShapeDtypeStruct((b, lt, d), BF16)
    out_f = jax.ShapeDtypeStruct((b, lt, d), F32)
    return pl.pallas_call(
        functools.partial(_inproj_kernel, d=d, split=x_ctx is not None, n_ctx_tiles=n_ctx_tiles),
        grid=(b, nt),
        in_specs=x_specs + [
                  pl.BlockSpec((None, 1, d), mod_map),
                  pl.BlockSpec((None, 1, d), mod_map),
                  pl.BlockSpec((1, d), lambda bi, i: (0, 0)),
                  pl.BlockSpec(w.shape, lambda bi, i: (0, 0)),
                  pl.BlockSpec((ROW_TILE, LANES), lambda bi, i: (i, 0)),
                  pl.BlockSpec((ROW_TILE, LANES), lambda bi, i: (i, 0))],
        out_specs=[tok, pl.BlockSpec((None, d, ROW_TILE), lambda bi, i: (bi, 0, i)), tok,
                   pl.BlockSpec((None, ROW_TILE * SUBLANES, LANES), lambda bi, i: (bi, i, 0)), tok, tok, tok],
        out_shape=[out_bf, jax.ShapeDtypeStruct((b, d, lt), BF16), out_bf,
                   jax.ShapeDtypeStruct((b, lt * SUBLANES, LANES), F32), out_f, out_f, out_f],
        compiler_params=_cparams(("parallel", "arbitrary"), 56),
        name="inproj",
    )(*x_args, sh, sc, g, w, cos_t, sin_t)


def _scan_chunk(i, n_ctx, n_chunks, reverse):
    if not reverse:
        return i
    return jnp.where(i < n_ctx, n_ctx - 1 - i, n_chunks - 1 - (i - n_ctx))


def _scan_kernel(*refs, reverse, tt, n_ctx, n_chunks):
    if reverse:
        (rx_ref, prev_ref, next_ref, cw_ref, cb_ref, wg_ref, bg_ref, c_ref, hf_ref,
         out_ref, ext_scr, u_scr, a_scr, b_scr, h_scr) = refs
    else:
        (rx_ref, prev_ref, next_ref, cw_ref, cb_ref, wg_ref, bg_ref, c_ref,
         out_ref, ext_scr, u_scr, a_scr, b_scr, h_scr) = refs
        hf_ref = None
    i = pl.program_id(0)
    c = _scan_chunk(i, n_ctx, n_chunks, reverse)
    seq_start = jnp.logical_or(c == 0, c == n_ctx)
    seq_end = jnp.logical_or(c == n_ctx - 1, c == n_chunks - 1)
    nb = rx_ref.shape[0]
    tr = tt * SUBLANES
    left = CONV_LEFT * SUBLANES

    ext_scr[:, 0:left, :] = jnp.where(seq_start, 0.0, prev_ref[...])
    ext_scr[:, left:left + tr, :] = rx_ref[...]
    ext_scr[:, left + tr:left + tr + SUBLANES, :] = jnp.where(seq_end, 0.0, next_ref[...])
    u = cb_ref[...] + cw_ref[0] * ext_scr[:, 0:tr, :].reshape(nb, tt, SUBLANES, LANES)
    for j in range(1, CONV_W):
        u = u + cw_ref[j] * ext_scr[:, j * SUBLANES:j * SUBLANES + tr, :].reshape(nb, tt, SUBLANES, LANES)
    u_scr[...] = u.reshape(nb * tr, LANES)

    rows = nb * tt
    for n in range(RNN_BLOCKS):
        un = u_scr[pl.ds(n, rows, stride=SUBLANES), :]
        z = jnp.dot(un.astype(BF16), wg_ref[n], preferred_element_type=F32) + bg_ref[n]
        r = _sigmoid(z[:, :RNN_BW])
        ig = _sigmoid(z[:, RNN_BW:])
        a = jnp.exp(-c_ref[n:n + 1, :] * r)
        a_scr[pl.ds(n, rows, stride=SUBLANES), :] = a
        b_scr[pl.ds(n, rows, stride=SUBLANES), :] = jnp.sqrt(1.0 - a * a) * ig * un

    @pl.when(i == 0)
    def _():
        h_scr[...] = jnp.zeros_like(h_scr)

    def step(s, hs):
        t = tt - 1 - s if reverse else s
        new = []
        for bi in range(nb):
            at = pl.ds(pl.multiple_of((bi * tt + t) * SUBLANES, SUBLANES), SUBLANES)
            h = a_scr[at, :] * hs[bi] + b_scr[at, :]
            ot = pl.ds(pl.multiple_of(t * SUBLANES, SUBLANES), SUBLANES)
            out_ref[bi, ot, :] = hf_ref[bi, ot, :] + h if reverse else h
            new.append(h)
        return tuple(new)

    hs = lax.fori_loop(0, tt, step, tuple(h_scr[bi] for bi in range(nb)), unroll=4)
    for bi in range(nb):
        h_scr[bi] = hs[bi]


def _scan(rx, cw, cb, wg, bg, cdec, hf, n_ctx, reverse):
    b, lt8, _ = rx.shape
    tt = SCAN_CHUNK
    n_chunks = lt8 // (tt * SUBLANES)
    chunk = lambda i: _scan_chunk(i, n_ctx, n_chunks, reverse)
    blk = pl.BlockSpec((b, tt * SUBLANES, LANES), lambda i: (0, chunk(i), 0))
    const2 = lambda i: (0, 0)
    const3 = lambda i: (0, 0, 0)
    in_specs = [blk,
                pl.BlockSpec((b, CONV_LEFT * SUBLANES, LANES),
                             lambda i: (0, jnp.maximum(chunk(i) * (tt // CONV_LEFT) - 1, 0), 0)),
                pl.BlockSpec((b, SUBLANES, LANES), lambda i: (0, jnp.minimum((chunk(i) + 1) * tt, n_chunks * tt - 1), 0)),
                pl.BlockSpec(cw.shape, const3),
                pl.BlockSpec(cb.shape, const2),
                pl.BlockSpec(wg.shape, const3),
                pl.BlockSpec(bg.shape, const3),
                pl.BlockSpec(cdec.shape, const2)]
    args = [rx, rx, rx, cw, cb, wg, bg, cdec]
    if reverse:
        in_specs.append(blk)
        args.append(hf)
    tr = tt * SUBLANES
    return pl.pallas_call(
        functools.partial(_scan_kernel, reverse=reverse, tt=tt, n_ctx=n_ctx, n_chunks=n_chunks),
        grid=(n_chunks,),
        in_specs=in_specs,
        out_specs=blk,
        out_shape=jax.ShapeDtypeStruct(rx.shape, F32),
        scratch_shapes=[pltpu.VMEM((b, tr + (CONV_W - 1) * SUBLANES, LANES), F32),
                        pltpu.VMEM((b * tr, LANES), F32),
                        pltpu.VMEM((b * tr, LANES), F32),
                        pltpu.VMEM((b * tr, LANES), F32),
                        pltpu.VMEM((b, SUBLANES, LANES), F32)],
        compiler_params=_cparams(("arbitrary",), 56),
        name="scan_bwd" if reverse else "scan_fwd",
    )(*args)


def _attn_passes(lam, q_refs, k_ref, vt_ref, g_ref, o_ref, *, write, read, tq, w_chunks, r_chunks, out_scale):
    n2 = 2 * tq
    if write is not None:
        s_w, m_w = write
        q = jnp.concatenate([r[...] for r in q_refs], axis=0) if len(q_refs) > 1 else q_refs[0][...]
        lane = lax.broadcasted_iota(jnp.int32, q.shape, 1)
        first = (lane // (HEAD_DIM // 2)) % 2 == 0
        zero = jnp.zeros_like(q)
        qs = jnp.concatenate([jnp.where(first, q, zero), jnp.where(first, zero, q)], axis=0)
        m8 = jnp.full((SUBLANES, n2), NEG_BIG, F32)
    if read is not None:
        s_r, m_r = read
        m = m_r[...]
        l8 = jnp.zeros((SUBLANES, n2), F32)
        acc = jnp.zeros((V_DIM, n2), F32)
    for j in range(max(len(w_chunks) if write is not None else 0, len(r_chunks) if read is not None else 0)):
        if write is not None and j < len(w_chunks):
            start, size = w_chunks[j]
            st = lax.dot_general(k_ref[start:start + size, :], qs, (((1,), (1,)), ((), ())),
                                 preferred_element_type=F32)
            s_w[start:start + size, :] = st
            m8 = jnp.maximum(m8, jnp.max(st.reshape(size // SUBLANES, SUBLANES, n2), axis=0))
        if read is not None and j < len(r_chunks):
            start, size = r_chunks[j]
            p = jnp.exp2(s_r[start:start + size, :] - m)
            l8 = l8 + jnp.sum(p.reshape(size // SUBLANES, SUBLANES, n2), axis=0)
            acc = acc + jnp.dot(vt_ref[:, start:start + size], p.astype(BF16), preferred_element_type=F32)
    if write is not None:
        m_w[...] = jnp.max(m8, axis=0, keepdims=True)
    if read is not None:
        on = acc * (1.0 / jnp.sum(l8, axis=0, keepdims=True))
        dt = on[:, :tq] - lam * on[:, tq:]
        yt = dt * lax.rsqrt(jnp.mean(dt * dt, axis=0, keepdims=True) + EPS)
        o_ref[...] = (yt.T * g_ref[...] * out_scale).astype(BF16)


def _attn_pipe_kernel(lam_ref, *refs, tq, n_tiles, n_q_refs, chunks, out_scale):
    q_refs = refs[:n_q_refs]
    k_ref, vt_ref, g_ref, o_ref, s_a, s_b, m_a, m_b = refs[n_q_refs:]
    s = pl.program_id(0)
    run = functools.partial(_attn_passes, lam_ref[0], q_refs, k_ref, vt_ref, g_ref, o_ref, tq=tq,
                            w_chunks=chunks, r_chunks=chunks, out_scale=out_scale)
    buf_a, buf_b = (s_a, m_a), (s_b, m_b)
    inner = jnp.logical_and(s > 0, s < n_tiles)
    even = s % 2 == 0

    @pl.when(s == 0)
    def _():
        run(write=buf_a, read=None)

    @pl.when(jnp.logical_and(inner, even))
    def _():
        run(write=buf_a, read=buf_b)

    @pl.when(jnp.logical_and(inner, jnp.logical_not(even)))
    def _():
        run(write=buf_b, read=buf_a)

    @pl.when(s == n_tiles)
    def _():
        run(write=None, read=buf_a if (n_tiles - 1) % 2 == 0 else buf_b)


def _attn_single_kernel(lam_ref, q_ref, k_ref, vt_ref, g_ref, o_ref, s_a, m_a, *, tq, chunks, out_scale):
    run = functools.partial(_attn_passes, lam_ref[0], (q_ref,), k_ref, vt_ref, g_ref, o_ref, tq=tq,
                            w_chunks=chunks, r_chunks=chunks, out_scale=out_scale)
    run(write=(s_a, m_a), read=None)
    run(write=None, read=(s_a, m_a))


def _key_chunks(n_ctx_rows, lt):
    ctx = [(s, min(ATTN_TK, n_ctx_rows - s)) for s in range(0, n_ctx_rows, ATTN_TK)]
    lat = [(s, min(ATTN_TK, lt - s)) for s in range(n_ctx_rows, lt, ATTN_TK)]
    return tuple(ctx), tuple(ctx + lat)


def _attention(lam, q, k, vt, g_sub, n_ctx_rows, with_ctx_queries, out_scale):
    b, lt, d = q.shape
    tq, tqc = ATTN_TQ, ATTN_CTX_TQ
    ctx_q_tiles = n_ctx_rows // tqc
    n_sub = tq // tqc
    nq = (lt - n_ctx_rows) // tq
    ctx_chunks, all_chunks = _key_chunks(n_ctx_rows, lt)
    n2 = 2 * tq
    lam_in = pl.BlockSpec(memory_space=pltpu.SMEM)
    g_in = pl.BlockSpec((1, V_DIM), lambda bi, h, s: (0, 0))
    n_flat = b * N_HEADS * nq

    def tile_map(f, lag):
        def index_map(g):
            t = jnp.clip(g - lag, 0, n_flat - 1)
            bh = t // nq
            return f(bh // N_HEADS, bh % N_HEADS, t % nq)
        return index_map

    lat = pl.pallas_call(
        functools.partial(_attn_pipe_kernel, tq=tq, n_tiles=n_flat, n_q_refs=n_sub, chunks=all_chunks,
                          out_scale=out_scale),
        grid=(n_flat + 1,),
        in_specs=[lam_in] + [
                  pl.BlockSpec((None, tqc, V_DIM),
                               tile_map(lambda bi, h, qi, j=j: (bi, qi * n_sub + ctx_q_tiles + j, h), 0))
                  for j in range(n_sub)] + [
                  pl.BlockSpec((None, lt, V_DIM), tile_map(lambda bi, h, qi: (bi, 0, h), 0)),
                  pl.BlockSpec((None, V_DIM, lt), tile_map(lambda bi, h, qi: (bi, h, 0), 1)),
                  pl.BlockSpec((1, V_DIM), lambda g: (0, 0))],
        out_specs=pl.BlockSpec((None, tq, V_DIM), tile_map(lambda bi, h, qi: (bi, qi, h), 1)),
        out_shape=jax.ShapeDtypeStruct((b, nq * tq, d), BF16),
        scratch_shapes=[pltpu.VMEM((lt, n2), F32), pltpu.VMEM((lt, n2), F32),
                        pltpu.VMEM((1, n2), F32), pltpu.VMEM((1, n2), F32)],
        compiler_params=_cparams(("arbitrary",), 56),
        name="diff_attn",
    )(lam, *([q] * n_sub), k, vt, g_sub)
    if not with_ctx_queries:
        return lat, None
    ctx = pl.pallas_call(
        functools.partial(_attn_single_kernel, tq=tqc, chunks=ctx_chunks, out_scale=out_scale),
        grid=(b, N_HEADS, ctx_q_tiles),
        in_specs=[lam_in,
                  pl.BlockSpec((None, tqc, V_DIM), lambda bi, h, s: (bi, s, h)),
                  pl.BlockSpec((None, n_ctx_rows, V_DIM), lambda bi, h, s: (bi, 0, h)),
                  pl.BlockSpec((None, V_DIM, n_ctx_rows), lambda bi, h, s: (bi, h, 0)),
                  g_in],
        out_specs=pl.BlockSpec((None, tqc, V_DIM), lambda bi, h, s: (bi, s, h)),
        out_shape=jax.ShapeDtypeStruct((b, n_ctx_rows, d), BF16),
        scratch_shapes=[pltpu.VMEM((n_ctx_rows, 2 * tqc), F32), pltpu.VMEM((1, 2 * tqc), F32)],
        compiler_params=_cparams(("parallel", "parallel", "arbitrary"), 32),
        name="diff_attn_ctx",
    )(lam, q, k, vt, g_sub)
    return lat, ctx


def _postmix_kernel(*refs, has_ctx, split, n_ctx_tiles, nt, n_tiles):
    refs = list(refs)
    x_ref = refs.pop(0)
    xc_ref = refs.pop(0) if split else None
    at_ref = refs.pop(0)
    atc_ref = refs.pop(0) if has_ctx else None
    (y_ref, rg_ref, gr_ref, ga_ref, gt1_ref, sh2_ref, sc2_ref, g2_ref,
     wr_ref, wa_ref, wo_ref, wrh_ref, wrl_ref, brt_ref, x1_ref, h2_ref, route_ref, cnt_ref,
     run_scr, hs_scr) = refs
    g = pl.program_id(0)

    @pl.when(g == 0)
    def _():
        run_scr[...] = jnp.zeros_like(run_scr)
        hs_scr[...] = jnp.zeros_like(hs_scr)

    h_hi = hs_scr[0]
    h_lo = hs_scr[1]
    nt_dims = (((1,), (1,)), ((), ()))
    lg = (lax.dot_general(wrh_ref[...], h_hi, nt_dims, preferred_element_type=F32)
          + lax.dot_general(wrh_ref[...], h_lo, nt_dims, preferred_element_type=F32)
          + lax.dot_general(wrl_ref[...], h_hi, nt_dims, preferred_element_type=F32)) + brt_ref[:, 0:1]

    is_ctx = jnp.minimum(g, n_tiles - 1) % nt < n_ctx_tiles
    x = _stream_tile(x_ref, xc_ref, is_ctx)
    at = _stream_tile(at_ref, atc_ref, is_ctx)
    yg = (_load_tiled(y_ref) * _gelu_tanh(rg_ref[...])).astype(BF16)
    o_r = jnp.dot(yg, wr_ref[...], preferred_element_type=F32)
    o_a = jnp.dot(at, wa_ref[...], preferred_element_type=F32)
    mix = (_sigmoid(gr_ref[...]) * o_r + _sigmoid(ga_ref[...]) * o_a).astype(BF16)
    x1 = x + gt1_ref[...] * jnp.dot(mix, wo_ref[...], preferred_element_type=F32)
    x1_ref[...] = x1
    h2 = _rms(x1, g2_ref[...]) * (1.0 + sc2_ref[...]) + sh2_ref[...]
    _store_tiled(h2_ref, h2)
    h2_hi = h2.astype(BF16)
    hs_scr[0] = h2_hi
    hs_scr[1] = (h2 - h2_hi.astype(F32)).astype(BF16)

    rows = h_hi.shape[0]
    ex = lax.broadcasted_iota(jnp.int32, lg.shape, 0)
    ex_f = ex.astype(F32)
    e = jnp.exp(lg - jnp.max(lg, axis=0, keepdims=True))
    p = e / jnp.sum(e, axis=0, keepdims=True)
    grp = ex // EXPERTS_PER_GROUP
    best = jnp.max(jnp.where(grp == 0, p, -1.0), axis=0, keepdims=True)
    best_g = jnp.zeros_like(best, dtype=jnp.int32)
    for gi in range(1, N_GROUPS):
        gm = jnp.max(jnp.where(grp == gi, p, -1.0), axis=0, keepdims=True)
        better = gm > best
        best_g = jnp.where(better, gi, best_g)
        best = jnp.where(better, gm, best)
    pg = jnp.where(grp == best_g, p, -1.0)
    big = float(N_EXPERTS)
    v1 = jnp.max(pg, axis=0, keepdims=True)
    i1 = jnp.min(jnp.where(pg == v1, ex_f, big), axis=0, keepdims=True)
    pg2 = jnp.where(ex_f == i1, -1.0, pg)
    v2 = jnp.max(pg2, axis=0, keepdims=True)
    i2 = jnp.min(jnp.where(pg2 == v2, ex_f, big), axis=0, keepdims=True)
    den = v1 + v2

    live = (g > 0).astype(F32)
    oh1 = (ex_f == i1).astype(F32)
    oh2 = (ex_f == i2).astype(F32)
    both = (oh1 + oh2) * live
    tri = (lax.broadcasted_iota(jnp.int32, (rows, rows), 0)
           < lax.broadcasted_iota(jnp.int32, (rows, rows), 1)).astype(BF16)
    before = jnp.dot(both.astype(BF16), tri, preferred_element_type=F32) + run_scr[:, 0:1]
    r1 = jnp.sum(oh1 * before, axis=0, keepdims=True)
    r2 = jnp.sum(oh2 * before, axis=0, keepdims=True)
    run = run_scr[...] + jnp.sum(both, axis=1, keepdims=True)
    run_scr[...] = run
    cnt_ref[...] = run
    row = lax.broadcasted_iota(jnp.int32, route_ref.shape, 0)
    route_ref[...] = jnp.where(row == 0, i1, jnp.where(row == 1, i2, jnp.where(row == 2, v1 / den, jnp.where(
        row == 3, v2 / den, jnp.where(row == 4, r1, r2)))))


def _postmix(x, x_ctx, attn, attn_ctx, y, rg, gr, ga, gt1, sh2, sc2, g2, wr, wa, wo, wrh, wrl, brt, n_ctx_tiles):
    b, lt, d = rg.shape
    has_ctx = attn_ctx is not None
    off = 0 if has_ctx else n_ctx_tiles
    lq = lt - off * ROW_TILE
    nt = lq // ROW_TILE
    n_tiles = b * nt

    def wrap(f):
        def index_map(g):
            t = jnp.minimum(g, n_tiles - 1)
            return f(t // nt, t % nt)
        return index_map

    full = pl.BlockSpec((None, ROW_TILE, d), wrap(lambda bi, i: (bi, i + off, 0)))
    part = pl.BlockSpec((None, ROW_TILE, d), wrap(lambda bi, i: (bi, i, 0)))
    tiled = pl.BlockSpec((None, ROW_TILE * SUBLANES, LANES), wrap(lambda bi, i: (bi, i + off, 0)))
    mod = pl.BlockSpec((None, 1, d), wrap(lambda bi, i: (jnp.where(i + off < n_ctx_tiles, b, bi), 0, 0)))
    const = lambda g: (0, 0)
    if x_ctx is None:
        x_specs, x_args = [full], [x]
    else:
        x_specs, x_args = _stream_specs(x, x_ctx, n_ctx_tiles, wrap)
    at_specs, at_args = _stream_specs(attn, attn_ctx, n_ctx_tiles, wrap)
    return pl.pallas_call(
        functools.partial(_postmix_kernel, has_ctx=has_ctx, split=x_ctx is not None, n_ctx_tiles=n_ctx_tiles,
                          nt=nt, n_tiles=n_tiles),
        grid=(n_tiles + 1,),
        in_specs=x_specs + at_specs + [tiled, full, full, full, mod, mod, mod,
                                      pl.BlockSpec((1, d), const),
                                      pl.BlockSpec((d, d), const), pl.BlockSpec((d, d), const),
                                      pl.BlockSpec((d, d), const),
                                      pl.BlockSpec((N_EXPERTS, d), const), pl.BlockSpec((N_EXPERTS, d), const),
                                      pl.BlockSpec((N_EXPERTS, LANES), const)],
        out_specs=[part, pl.BlockSpec((ROW_TILE * SUBLANES, LANES), wrap(lambda bi, i: (bi * nt + i, 0))),
                   pl.BlockSpec((SUBLANES, ROW_TILE), lambda g: (0, jnp.maximum(g - 1, 0))),
                   pl.BlockSpec((N_EXPERTS, LANES), const)],
        out_shape=[jax.ShapeDtypeStruct((b, lq, d), F32),
                   jax.ShapeDtypeStruct((b * lq * SUBLANES, LANES), F32),
                   jax.ShapeDtypeStruct((SUBLANES, b * lq), F32),
                   jax.ShapeDtypeStruct((N_EXPERTS, LANES), F32)],
        scratch_shapes=[pltpu.VMEM((N_EXPERTS, LANES), F32), pltpu.VMEM((2, ROW_TILE, d), BF16)],
        compiler_params=_cparams(("arbitrary",), 48),
        name="postmix",
    )(*x_args, *at_args, y, rg, gr, ga, gt1, sh2, sc2, g2, wr, wa, wo, wrh, wrl, brt)


def _dispatch_kernel(pad_ref, dest_ref, h_ref, xs_hbm, zrows, sem, *, n_pad):
    i = pl.program_id(0)
    rows = h_ref.shape[0] // SUBLANES

    def tile(ref, r):
        return ref.at[pl.ds(pl.multiple_of(r * SUBLANES, SUBLANES), SUBLANES)]

    @pl.when(i == 0)
    def _():
        zrows[...] = jnp.zeros_like(zrows)
        for c in range(n_pad // MOE_ROWS):
            def fill(r, carry):
                pltpu.make_async_copy(tile(zrows, r), tile(xs_hbm, pad_ref[c * MOE_ROWS + r]), sem).start()
                return carry

            lax.fori_loop(0, MOE_ROWS, fill, 0, unroll=8)
            pltpu.make_async_copy(zrows, xs_hbm.at[pl.ds(0, MOE_ROWS * SUBLANES)], sem).wait()

    def send(r, carry):
        for k in range(TOP_K):
            pltpu.make_async_copy(tile(h_ref, r), tile(xs_hbm, dest_ref[0, k * rows + r]), sem).start()
        return carry

    lax.fori_loop(0, rows, send, 0, unroll=8)
    for k in range(TOP_K):
        pltpu.make_async_copy(h_ref, xs_hbm.at[pl.ds(0, rows * SUBLANES)], sem).wait()


def _dispatch(pad_slots, dest, h2, n_slots):
    t = h2.shape[0] // SUBLANES
    rows = DISPATCH_ROWS
    return pl.pallas_call(
        functools.partial(_dispatch_kernel, n_pad=pad_slots.shape[0]),
        grid_spec=pltpu.PrefetchScalarGridSpec(
            num_scalar_prefetch=1,
            grid=(t // rows,),
            in_specs=[pl.BlockSpec((None, 1, TOP_K * rows), lambda i, pad: (i, 0, 0), memory_space=pltpu.SMEM),
                      pl.BlockSpec((rows * SUBLANES, LANES), lambda i, pad: (i, 0))],
            out_specs=pl.BlockSpec(memory_space=pl.ANY),
            scratch_shapes=[pltpu.VMEM((MOE_ROWS * SUBLANES, LANES), F32), pltpu.SemaphoreType.DMA(())]),
        out_shape=jax.ShapeDtypeStruct((n_slots * SUBLANES, LANES), F32),
        compiler_params=_cparams(("arbitrary",), 32),
        name="dispatch",
    )(pad_slots, _tile_table(dest, rows), h2)


def _expert_kernel(be_ref, nv_ref, x_ref, w1_ref, w3_ref, w2_ref, y_ref, w1_b, w3_b, w2_b):
    i = pl.program_id(0)

    @pl.when(i < nv_ref[0])
    def _():
        @pl.when(jnp.logical_or(i == 0, be_ref[i] != be_ref[jnp.maximum(i - 1, 0)]))
        def _():
            w1_b[...] = w1_ref[...].astype(BF16)
            w3_b[...] = w3_ref[...].astype(BF16)
            w2_b[...] = w2_ref[...].astype(BF16)

        xb = _load_tiled(x_ref).astype(BF16)
        a1 = jnp.dot(xb, w1_b[...], preferred_element_type=F32)
        a3 = jnp.dot(xb, w3_b[...], preferred_element_type=F32)
        mid = (a1 * _sigmoid(a1) * a3).astype(BF16)
        _store_tiled(y_ref, jnp.dot(mid, w2_b[...], preferred_element_type=F32))

    @pl.when(i >= nv_ref[0])
    def _():
        y_ref[...] = jnp.zeros_like(y_ref)


def _experts(block_e, n_valid, xs, w1, w3, w2, layer):
    n_slots = xs.shape[0] // SUBLANES
    d, de = w1.shape[-2:]
    wspec = lambda shp: pl.BlockSpec((None, None) + shp, lambda i, be, nv: (layer, be[i], 0, 0))
    return pl.pallas_call(
        _expert_kernel,
        grid_spec=pltpu.PrefetchScalarGridSpec(
            num_scalar_prefetch=2,
            grid=(n_slots // MOE_ROWS,),
            in_specs=[pl.BlockSpec((MOE_ROWS * SUBLANES, LANES), lambda i, be, nv: (jnp.minimum(i, nv[0] - 1), 0)),
                      wspec((d, de)), wspec((d, de)), wspec((de, d))],
            out_specs=pl.BlockSpec((MOE_ROWS * SUBLANES, LANES), lambda i, be, nv: (i, 0)),
            scratch_shapes=[pltpu.VMEM((d, de), BF16), pltpu.VMEM((d, de), BF16), pltpu.VMEM((de, d), BF16)]),
        out_shape=jax.ShapeDtypeStruct((n_slots * SUBLANES, LANES), F32),
        compiler_params=_cparams(("arbitrary",), 56),
        name="experts",
    )(block_e, n_valid, xs, w1, w3, w2)


def _tile_table(dest, rows):
    n_tiles = dest.shape[1] // rows
    return dest.reshape(TOP_K, n_tiles, rows).transpose(1, 0, 2).reshape(n_tiles, 1, TOP_K * rows)


def _route_tables(route, counts):
    expert = route[0:TOP_K].astype(jnp.int32)
    rank = route[4:4 + TOP_K].astype(jnp.int32)
    a = expert.size
    sizes = counts[:, 0].astype(jnp.int32)
    padded = (sizes + MOE_ROWS - 1) // MOE_ROWS * MOE_ROWS
    pend = jnp.cumsum(padded)
    pstarts = pend - padded
    first = jnp.sum(jnp.where(expert[..., None] == jnp.arange(N_EXPERTS, dtype=jnp.int32), pstarts, 0), axis=-1)
    dest = (first + rank).astype(jnp.int32)
    n_blocks = -(-a // MOE_ROWS) + N_EXPERTS
    n_slots = n_blocks * MOE_ROWS
    n_valid = (pend[-1] // MOE_ROWS).astype(jnp.int32)
    blk = jnp.minimum(jnp.arange(n_blocks, dtype=jnp.int32), n_valid - 1)
    block_e = jnp.minimum(jnp.sum(blk[:, None] * MOE_ROWS >= pend[None, :], axis=1),
                          N_EXPERTS - 1).astype(jnp.int32)
    gap_start = jnp.concatenate([pstarts + sizes, pend[-1:]]).astype(jnp.int32)
    gap_len = jnp.concatenate([padded - sizes, n_slots - pend[-1:]]).astype(jnp.int32)
    gap_end = jnp.cumsum(gap_len)
    j = jnp.arange(n_slots - a, dtype=jnp.int32)
    seg = jnp.sum(j[:, None] >= gap_end[None, :], axis=1)
    pad_slots = (gap_start[seg] + j - (gap_end - gap_len)[seg]).astype(jnp.int32)
    return dest, pad_slots, block_e, n_valid.reshape(1), n_slots


def _combine_kernel(dcur_ref, dnxt_ref, x_ref, gate_ref, gt2_ref, g_ref, y_hbm, o_ref, ybuf, sems, *, final):
    g = pl.program_id(0)
    n = pl.num_programs(0)
    rows = x_ref.shape[0]
    slot = g % 2

    def fetch(tbl_ref, s):
        def body(r, carry):
            for k in range(TOP_K):
                src = pl.multiple_of(tbl_ref[0, k * rows + r] * SUBLANES, SUBLANES)
                pltpu.make_async_copy(y_hbm.at[pl.ds(src, SUBLANES)],
                                      ybuf.at[s, k, pl.ds(pl.multiple_of(r * SUBLANES, SUBLANES), SUBLANES)],
                                      sems.at[s]).start()
            return carry

        lax.fori_loop(0, rows, body, 0, unroll=8)

    @pl.when(g == 0)
    def _():
        fetch(dcur_ref, 0)

    @pl.when(g + 1 < n)
    def _():
        fetch(dnxt_ref, 1 - slot)

    for k in range(TOP_K):
        pltpu.make_async_copy(y_hbm.at[pl.ds(0, rows * SUBLANES)], ybuf.at[slot, k], sems.at[slot]).wait()

    gate = gate_ref[...]
    y = gate[:, 0:1] * _load_tiled(ybuf.at[slot, 0]) + gate[:, 1:2] * _load_tiled(ybuf.at[slot, 1])
    x2 = x_ref[...] + gt2_ref[...] * y
    o_ref[...] = _rms(x2, g_ref[...]) if final else x2


def _combine(x1, y_slots, dest, gates, gt2, g_final, n_ctx_tiles, has_ctx, final):
    b, lq, d = x1.shape
    nt = lq // ROW_TILE
    n_tiles = b * nt
    ctx_tiles = n_ctx_tiles if has_ctx else 0
    tok = lambda w: pl.BlockSpec((None, ROW_TILE, w), lambda g: (g // nt, g % nt, 0))
    tbl = lambda f: pl.BlockSpec((None, 1, TOP_K * ROW_TILE), lambda g: (f(g), 0, 0), memory_space=pltpu.SMEM)
    dest3 = _tile_table(dest, ROW_TILE)
    return pl.pallas_call(
        functools.partial(_combine_kernel, final=final),
        grid=(n_tiles,),
        in_specs=[tbl(lambda g: g), tbl(lambda g: jnp.minimum(g + 1, n_tiles - 1)),
                  tok(d), pl.BlockSpec((ROW_TILE, TOP_K), lambda g: (g, 0)),
                  pl.BlockSpec((None, 1, d), lambda g: (jnp.where(g % nt < ctx_tiles, b, g // nt), 0, 0)),
                  pl.BlockSpec((1, d), lambda g: (0, 0)),
                  pl.BlockSpec(memory_space=pl.ANY)],
        out_specs=tok(d),
        out_shape=jax.ShapeDtypeStruct((b, lq, d), F32),
        scratch_shapes=[pltpu.VMEM((2, TOP_K, ROW_TILE * SUBLANES, LANES), F32), pltpu.SemaphoreType.DMA((2,))],
        compiler_params=_cparams(("arbitrary",), 32),
        name="combine",
    )(dest3, dest3, x1, gates, gt2, g_final, y_slots)


def _rope_tables(n_ctx, seq):
    n_pairs = HEAD_DIM // 4
    inv = ROPE_BASE ** (-jnp.arange(n_pairs, dtype=F32) / n_pairs)
    rows = seq // GRID_W
    r = jnp.repeat(jnp.arange(rows, dtype=F32), GRID_W)
    col = jnp.tile(jnp.arange(GRID_W, dtype=F32), rows)
    ang = jnp.concatenate([r[:, None] * inv, col[:, None] * inv], axis=-1)
    cos = jnp.concatenate([jnp.ones((n_ctx, HEAD_DIM // 2), F32), jnp.cos(ang)], axis=0)
    sin = jnp.concatenate([jnp.zeros((n_ctx, HEAD_DIM // 2), F32), jnp.sin(ang)], axis=0)
    return jnp.tile(cos, (1, 4)), jnp.concatenate([-sin, -sin, sin, sin], axis=-1)


def _qk_perm():
    lane = jnp.arange(LANES)
    g, i = lane // 32, lane % 32
    within = (g % 2) * HEAD_DIM + 2 * i + g // 2
    return (jnp.arange(N_HEADS)[:, None] * LANES + within[None, :]).reshape(-1)


def kernel(x, c, ctx, c_ctx, w_mod, b_mod, g_norm1, g_norm2, w_in, conv_w, conv_b, lru_wa, lru_ba, lru_wx,
           lru_bx, lru_lambda, diff_lambda, g_subln, w_rnn_proj, w_attn_proj, w_o, w_router, b_router,
           w_e1, w_e3, w_e2, g_final):
    b, seq, d = x.shape
    n_ctx = ctx.shape[1]
    depth = w_mod.shape[0]
    n_ctx_tiles = n_ctx // ROW_TILE
    assert d == N_HEADS * V_DIM == SUBLANES * LANES and n_ctx % ROW_TILE == 0 and seq % ROW_TILE == 0

    cond = jnp.zeros((MOD_ROWS, d), F32).at[:b].set(c).at[b].set(c_ctx)
    mods = _adaln(cond, w_mod, b_mod).reshape(depth, MOD_ROWS, 6, 1, d)
    cos_t, sin_t = _rope_tables(n_ctx, seq)
    perm = _qk_perm()
    wrt = w_router.T
    wrt_hi = wrt.astype(BF16)
    wrt_lo = (wrt - wrt_hi.astype(F32)).astype(BF16)
    brt = jnp.broadcast_to(b_router.astype(F32)[:, None], (N_EXPERTS, LANES))

    xs, xs_ctx = x, ctx
    out = None
    for l in range(depth):
        last = l == depth - 1
        lam_init = 0.8 - 0.6 * math.exp(-0.3 * l)
        sh1, sc1, gt1, sh2, sc2, gt2 = [mods[l, :, j] for j in range(6)]
        wl = w_in[l]
        w_perm = jnp.concatenate(
            [wl[:, d:2 * d][:, perm], wl[:, 2 * d:3 * d], wl[:, 3 * d:4 * d][:, perm],
             wl[:, 0:d], wl[:, 4 * d:]], axis=1).astype(BF16)
        k, v, q, rx, rg, gr, ga = _inproj(xs, xs_ctx, sh1, sc1, g_norm1[l][None], w_perm, cos_t, sin_t,
                                          n_ctx_tiles)

        lq1, lk1, lq2, lk2 = diff_lambda[l].astype(F32)
        lam = (jnp.exp(jnp.sum(lq1 * lk1)) - jnp.exp(jnp.sum(lq2 * lk2)) + lam_init).reshape(1)
        attn, attn_ctx = _attention(lam, q, k, v, g_subln[l][None], n_ctx, not last, 1.0 - lam_init)

        cdec = LRU_C * jax.nn.softplus(-lru_lambda[l].astype(F32))
        tile_of = lambda vec: vec.reshape(vec.shape[:-1] + (SUBLANES, LANES))
        n_ctx_chunks = n_ctx // SCAN_CHUNK
        hf = None
        for direction in range(2):
            wg = jnp.concatenate([lru_wa[l, direction], lru_wx[l, direction]], axis=-1).astype(BF16)
            bg = jnp.concatenate([lru_ba[l, direction].reshape(RNN_BLOCKS, 1, RNN_BW),
                                  lru_bx[l, direction].reshape(RNN_BLOCKS, 1, RNN_BW)], axis=-1)
            hf = _scan(rx, tile_of(conv_w[l]), tile_of(conv_b[l]), wg, bg, tile_of(cdec[direction]), hf,
                       n_ctx_chunks, reverse=direction == 1)
        y = hf

        x1, h2, route, counts = _postmix(xs, xs_ctx, attn, attn_ctx, y, rg, gr, ga, gt1, sh2, sc2,
                                         g_norm2[l][None],
                                         w_rnn_proj[l].astype(BF16), w_attn_proj[l].astype(BF16),
                                         w_o[l].astype(BF16), wrt_hi, wrt_lo, brt, n_ctx_tiles)
        lq = x1.shape[1]
        n_tok = b * lq
        dest, pad_slots, block_e, n_valid, n_slots = _route_tables(route, counts)
        xs_sorted = _dispatch(pad_slots, dest, h2, n_slots)
        y_slots = _experts(block_e, n_valid, xs_sorted, w_e1, w_e3, w_e2, l)
        res = _combine(x1, y_slots, dest, route[2:2 + TOP_K].T, gt2, g_final[None], n_ctx_tiles, not last, last)
        if last:
            out = res
        else:
            xs, xs_ctx = res, None
    return out
```

```python
import functools
import math

import jax
import jax.numpy as jnp
from jax import lax
from jax.experimental import pallas as pl
from jax.experimental.pallas import tpu as pltpu

F32 = jnp.float32
BF16 = jnp.bfloat16

N_HEADS = 8
HEAD_DIM = 64
V_DIM = 2 * HEAD_DIM
GRID_W = 64
EPS = 1e-6
RNN_BLOCKS = 8
RNN_BW = 128
CONV_W = 4
CONV_LEFT = 2
LRU_C = 8.0
ROPE_BASE = 10000.0
N_EXPERTS = 16
N_GROUPS = 4
EXPERTS_PER_GROUP = N_EXPERTS // N_GROUPS
TOP_K = 2

LANES = 128
SUBLANES = 8
ROW_TILE = 256
SCAN_CHUNK = 128
ATTN_TQ = 256
ATTN_CTX_TQ = 256
ATTN_TK = 512
MOE_ROWS = 512
DISPATCH_ROWS = 1024
MOD_ROWS = 16
NEG_BIG = -1e30
LOG2_E = math.log2(math.e)


def _cparams(sem, vmem_mib):
    return pltpu.CompilerParams(dimension_semantics=sem, vmem_limit_bytes=vmem_mib * 1024 * 1024)


def _sigmoid(x):
    return 0.5 * jnp.tanh(0.5 * x) + 0.5


def _gelu_tanh(x):
    return 0.5 * x * (1.0 + jnp.tanh(math.sqrt(2.0 / math.pi) * (x + 0.044715 * (x * x * x))))


def _rms(x, g):
    return x * lax.rsqrt(jnp.mean(x * x, axis=-1, keepdims=True) + EPS) * g


def _store_tiled(ref, x):
    rows = x.shape[0]
    for j in range(SUBLANES):
        ref[pl.ds(j, rows, stride=SUBLANES), :] = x[:, j * LANES:(j + 1) * LANES]


def _load_tiled(ref):
    rows = ref.shape[0] // SUBLANES
    return jnp.concatenate([ref[pl.ds(j, rows, stride=SUBLANES), :] for j in range(SUBLANES)], axis=1)


def _adaln_kernel(c_ref, w_ref, b_ref, o_ref):
    c = c_ref[...]
    a = c * _sigmoid(c)
    o_ref[...] = jnp.dot(a, w_ref[...], preferred_element_type=F32,
                         precision=lax.Precision.HIGHEST) + b_ref[...]


def _adaln(cond, w_mod, b_mod):
    depth, d, d6 = w_mod.shape
    n = d6 // d
    return pl.pallas_call(
        _adaln_kernel,
        grid=(depth, n),
        in_specs=[pl.BlockSpec((MOD_ROWS, d), lambda l, j: (0, 0)),
                  pl.BlockSpec((None, d, d), lambda l, j: (l, 0, j)),
                  pl.BlockSpec((None, 1, d), lambda l, j: (l, 0, j))],
        out_specs=pl.BlockSpec((None, MOD_ROWS, d), lambda l, j: (l, 0, j)),
        out_shape=jax.ShapeDtypeStruct((depth, MOD_ROWS, d6), F32),
        compiler_params=_cparams(("arbitrary", "arbitrary"), 32),
        name="adaln",
    )(cond, w_mod, b_mod.reshape(depth, 1, d6))


def _stream_tile(x_ref, xc_ref, is_ctx):
    if xc_ref is None:
        return x_ref[...]
    return jnp.where(is_ctx, xc_ref[...], x_ref[...])


def _stream_specs(x, x_ctx, n_ctx_tiles, wrap=lambda f: f):
    d = x.shape[-1]
    if x_ctx is None:
        return [pl.BlockSpec((None, ROW_TILE, d), wrap(lambda bi, i: (bi, i, 0)))], [x]
    return ([pl.BlockSpec((None, ROW_TILE, d), wrap(lambda bi, i: (bi, jnp.maximum(i - n_ctx_tiles, 0), 0))),
             pl.BlockSpec((None, ROW_TILE, d), wrap(lambda bi, i: (bi, jnp.minimum(i, n_ctx_tiles - 1), 0)))],
            [x, x_ctx])


def _inproj_kernel(*refs, d, split, n_ctx_tiles):
    if split:
        (x_ref, xc_ref, sh_ref, sc_ref, g_ref, w_ref, cos_ref, sin_ref,
         k_ref, v_ref, q_ref, rx_ref, rg_ref, gr_ref, ga_ref) = refs
    else:
        (x_ref, sh_ref, sc_ref, g_ref, w_ref, cos_ref, sin_ref,
         k_ref, v_ref, q_ref, rx_ref, rg_ref, gr_ref, ga_ref) = refs
        xc_ref = None
    x = _stream_tile(x_ref, xc_ref, pl.program_id(1) < n_ctx_tiles)
    h = (_rms(x, g_ref[...]) * (1.0 + sc_ref[...]) + sh_ref[...]).astype(BF16)
    cosv = cos_ref[...]
    sinv = sin_ref[...]

    def proj(p):
        return jnp.dot(h, w_ref[:, p * d:(p + 1) * d], preferred_element_type=F32)

    def rope_store(res, ref, scale):
        for hh in range(N_HEADS):
            xs = res[:, hh * LANES:(hh + 1) * LANES]
            out = xs * cosv + pltpu.roll(xs, LANES // 2, 1) * sinv
            ref[:, hh * LANES:(hh + 1) * LANES] = (out * scale).astype(BF16)

    rope_store(proj(0), k_ref, 1.0)
    v_ref[...] = proj(1).T.astype(BF16)
    rope_store(proj(2), q_ref, HEAD_DIM ** -0.5 * LOG2_E)
    _store_tiled(rx_ref, proj(3))
    rg_ref[...] = proj(4)
    gr_ref[...] = proj(5)
    ga_ref[...] = proj(6)


def _inproj(x, x_ctx, sh, sc, g, w, cos_t, sin_t, n_ctx_tiles):
    b, _, d = x.shape
    lt = x.shape[1] + (0 if x_ctx is None else x_ctx.shape[1])
    nt = lt // ROW_TILE
    mod_map = lambda bi, i: (jnp.where(i < n_ctx_tiles, b, bi), 0, 0)
    tok = pl.BlockSpec((None, ROW_TILE, d), lambda bi, i: (bi, i, 0))
    x_specs, x_args = _stream_specs(x, x_ctx, n_ctx_tiles)
    out_bf = jax.ShapeDtypeStruct((b, lt, d), BF16)
    out_f = jax.ShapeDtypeStruct((b, lt, d), F32)
    return pl.pallas_call(
        functools.partial(_inproj_kernel, d=d, split=x_ctx is not None, n_ctx_tiles=n_ctx_tiles),
        grid=(b, nt),
        in_specs=x_specs + [
                  pl.BlockSpec((None, 1, d), mod_map),
                  pl.BlockSpec((None, 1, d), mod_map),
                  pl.BlockSpec((1, d), lambda bi, i: (0, 0)),
                  pl.BlockSpec(w.shape, lambda bi, i: (0, 0)),
                  pl.BlockSpec((ROW_TILE, LANES), lambda bi, i: (i, 0)),
                  pl.BlockSpec((ROW_TILE, LANES), lambda bi, i: (i, 0))],
        out_specs=[tok, pl.BlockSpec((None, d, ROW_TILE), lambda bi, i: (bi, 0, i)), tok,
                   pl.BlockSpec((None, ROW_TILE * SUBLANES, LANES), lambda bi, i: (bi, i, 0)), tok, tok, tok],
        out_shape=[out_bf, jax.ShapeDtypeStruct((b, d, lt), BF16), out_bf,
                   jax.ShapeDtypeStruct((b, lt * SUBLANES, LANES), F32), out_f, out_f, out_f],
        compiler_params=_cparams(("parallel", "arbitrary"), 56),
        name="inproj",
    )(*x_args, sh, sc, g, w, cos_t, sin_t)


def _scan_chunk(i, n_ctx, n_chunks, reverse):
    if not reverse:
        return i
    return jnp.where(i < n_ctx, n_ctx - 1 - i, n_chunks - 1 - (i - n_ctx))


def _scan_kernel(*refs, reverse, tt, n_ctx, n_chunks):
    if reverse:
        (rx_ref, prev_ref, next_ref, cw_ref, cb_ref, wg_ref, bg_ref, c_ref, hf_ref,
         out_ref, ext_scr, u_scr, a_scr, b_scr, h_scr) = refs
    else:
        (rx_ref, prev_ref, next_ref, cw_ref, cb_ref, wg_ref, bg_ref, c_ref,
         out_ref, ext_scr, u_scr, a_scr, b_scr, h_scr) = refs
        hf_ref = None
    i = pl.program_id(0)
    c = _scan_chunk(i, n_ctx, n_chunks, reverse)
    seq_start = jnp.logical_or(c == 0, c == n_ctx)
    seq_end = jnp.logical_or(c == n_ctx - 1, c == n_chunks - 1)
    nb = rx_ref.shape[0]
    tr = tt * SUBLANES
    left = CONV_LEFT * SUBLANES

    ext_scr[:, 0:left, :] = jnp.where(seq_start, 0.0, prev_ref[...])
    ext_scr[:, left:left + tr, :] = rx_ref[...]
    ext_scr[:, left + tr:left + tr + SUBLANES, :] = jnp.where(seq_end, 0.0, next_ref[...])
    u = cb_ref[...] + cw_ref[0] * ext_scr[:, 0:tr, :].reshape(nb, tt, SUBLANES, LANES)
    for j in range(1, CONV_W):
        u = u + cw_ref[j] * ext_scr[:, j * SUBLANES:j * SUBLANES + tr, :].reshape(nb, tt, SUBLANES, LANES)
    u_scr[...] = u.reshape(nb * tr, LANES)

    rows = nb * tt
    for n in range(RNN_BLOCKS):
        un = u_scr[pl.ds(n, rows, stride=SUBLANES), :]
        z = jnp.dot(un.astype(BF16), wg_ref[n], preferred_element_type=F32) + bg_ref[n]
        t = jnp.tanh(z)
        ig = 0.5 * t[:, RNN_BW:] + 0.5
        a = jnp.exp(c_ref[n:n + 1, :] * t[:, :RNN_BW] + c_ref[n:n + 1, :])
        a_scr[pl.ds(n, rows, stride=SUBLANES), :] = a
        b_scr[pl.ds(n, rows, stride=SUBLANES), :] = jnp.sqrt(1.0 - a * a) * ig * un

    @pl.when(i == 0)
    def _():
        h_scr[...] = jnp.zeros_like(h_scr)

    def step(s, hs):
        t = tt - 1 - s if reverse else s
        new = []
        for bi in range(nb):
            at = pl.ds(pl.multiple_of((bi * tt + t) * SUBLANES, SUBLANES), SUBLANES)
            h = a_scr[at, :] * hs[bi] + b_scr[at, :]
            ot = pl.ds(pl.multiple_of(t * SUBLANES, SUBLANES), SUBLANES)
            out_ref[bi, ot, :] = hf_ref[bi, ot, :] + h if reverse else h
            new.append(h)
        return tuple(new)

    hs = lax.fori_loop(0, tt, step, tuple(h_scr[bi] for bi in range(nb)), unroll=4)
    for bi in range(nb):
        h_scr[bi] = hs[bi]


def _scan(rx, cw, cb, wg, bg, cdec, hf, n_ctx, reverse):
    b, lt8, _ = rx.shape
    tt = SCAN_CHUNK
    n_chunks = lt8 // (tt * SUBLANES)
    chunk = lambda i: _scan_chunk(i, n_ctx, n_chunks, reverse)
    blk = pl.BlockSpec((b, tt * SUBLANES, LANES), lambda i: (0, chunk(i), 0))
    const2 = lambda i: (0, 0)
    const3 = lambda i: (0, 0, 0)
    in_specs = [blk,
                pl.BlockSpec((b, CONV_LEFT * SUBLANES, LANES),
                             lambda i: (0, jnp.maximum(chunk(i) * (tt // CONV_LEFT) - 1, 0), 0)),
                pl.BlockSpec((b, SUBLANES, LANES), lambda i: (0, jnp.minimum((chunk(i) + 1) * tt, n_chunks * tt - 1), 0)),
                pl.BlockSpec(cw.shape, const3),
                pl.BlockSpec(cb.shape, const2),
                pl.BlockSpec(wg.shape, const3),
                pl.BlockSpec(bg.shape, const3),
                pl.BlockSpec(cdec.shape, const2)]
    args = [rx, rx, rx, cw, cb, wg, bg, cdec]
    if reverse:
        in_specs.append(blk)
        args.append(hf)
    tr = tt * SUBLANES
    return pl.pallas_call(
        functools.partial(_scan_kernel, reverse=reverse, tt=tt, n_ctx=n_ctx, n_chunks=n_chunks),
        grid=(n_chunks,),
        in_specs=in_specs,
        out_specs=blk,
        out_shape=jax.ShapeDtypeStruct(rx.shape, F32),
        scratch_shapes=[pltpu.VMEM((b, tr + (CONV_W - 1) * SUBLANES, LANES), F32),
                        pltpu.VMEM((b * tr, LANES), F32),
                        pltpu.VMEM((b * tr, LANES), F32),
                        pltpu.VMEM((b * tr, LANES), F32),
                        pltpu.VMEM((b, SUBLANES, LANES), F32)],
        compiler_params=_cparams(("arbitrary",), 56),
        name="scan_bwd" if reverse else "scan_fwd",
    )(*args)


def _attn_passes(lam, q_refs, k_ref, vt_ref, g_ref, o_ref, *, write, read, tq, w_chunks, r_chunks, out_scale):
    n2 = 2 * tq
    if write is not None:
        s_w, m_w = write
        q = jnp.concatenate([r[...] for r in q_refs], axis=0) if len(q_refs) > 1 else q_refs[0][...]
        lane = lax.broadcasted_iota(jnp.int32, q.shape, 1)
        first = (lane // (HEAD_DIM // 2)) % 2 == 0
        zero = jnp.zeros_like(q)
        qs = jnp.concatenate([jnp.where(first, q, zero), jnp.where(first, zero, q)], axis=0)
        m8 = jnp.full((SUBLANES, n2), NEG_BIG, F32)
    if read is not None:
        s_r, m_r = read
        m = m_r[...]
        l8 = jnp.zeros((SUBLANES, n2), F32)
        acc = jnp.zeros((V_DIM, n2), F32)
    for j in range(max(len(w_chunks) if write is not None else 0, len(r_chunks) if read is not None else 0)):
        if write is not None and j < len(w_chunks):
            start, size = w_chunks[j]
            st = lax.dot_general(k_ref[start:start + size, :], qs, (((1,), (1,)), ((), ())),
                                 preferred_element_type=F32)
            s_w[start:start + size, :] = st
            m8 = jnp.maximum(m8, jnp.max(st.reshape(size // SUBLANES, SUBLANES, n2), axis=0))
        if read is not None and j < len(r_chunks):
            start, size = r_chunks[j]
            p = jnp.exp2(s_r[start:start + size, :] - m)
            l8 = l8 + jnp.sum(p.reshape(size // SUBLANES, SUBLANES, n2), axis=0)
            acc = acc + jnp.dot(vt_ref[:, start:start + size], p.astype(BF16), preferred_element_type=F32)
    if write is not None:
        m_w[...] = jnp.max(m8, axis=0, keepdims=True)
    if read is not None:
        on = acc * (1.0 / jnp.sum(l8, axis=0, keepdims=True))
        dt = on[:, :tq] - lam * on[:, tq:]
        yt = dt * lax.rsqrt(jnp.mean(dt * dt, axis=0, keepdims=True) + EPS)
        o_ref[...] = (yt.T * g_ref[...] * out_scale).astype(BF16)


def _attn_pipe_kernel(lam_ref, *refs, tq, n_tiles, n_q_refs, chunks, out_scale):
    q_refs = refs[:n_q_refs]
    k_ref, vt_ref, g_ref, o_ref, s_a, s_b, m_a, m_b = refs[n_q_refs:]
    s = pl.program_id(0)
    run = functools.partial(_attn_passes, lam_ref[0], q_refs, k_ref, vt_ref, g_ref, o_ref, tq=tq,
                            w_chunks=chunks, r_chunks=chunks, out_scale=out_scale)
    buf_a, buf_b = (s_a, m_a), (s_b, m_b)
    inner = jnp.logical_and(s > 0, s < n_tiles)
    even = s % 2 == 0

    @pl.when(s == 0)
    def _():
        run(write=buf_a, read=None)

    @pl.when(jnp.logical_and(inner, even))
    def _():
        run(write=buf_a, read=buf_b)

    @pl.when(jnp.logical_and(inner, jnp.logical_not(even)))
    def _():
        run(write=buf_b, read=buf_a)

    @pl.when(s == n_tiles)
    def _():
        run(write=None, read=buf_a if (n_tiles - 1) % 2 == 0 else buf_b)


def _attn_single_kernel(lam_ref, q_ref, k_ref, vt_ref, g_ref, o_ref, s_a, m_a, *, tq, chunks, out_scale):
    run = functools.partial(_attn_passes, lam_ref[0], (q_ref,), k_ref, vt_ref, g_ref, o_ref, tq=tq,
                            w_chunks=chunks, r_chunks=chunks, out_scale=out_scale)
    run(write=(s_a, m_a), read=None)
    run(write=None, read=(s_a, m_a))


def _key_chunks(n_ctx_rows, lt):
    ctx = [(s, min(ATTN_TK, n_ctx_rows - s)) for s in range(0, n_ctx_rows, ATTN_TK)]
    lat = [(s, min(ATTN_TK, lt - s)) for s in range(n_ctx_rows, lt, ATTN_TK)]
    return tuple(ctx), tuple(ctx + lat)


def _attention(lam, q, k, vt, g_sub, n_ctx_rows, with_ctx_queries, out_scale):
    b, lt, d = q.shape
    tq, tqc = ATTN_TQ, ATTN_CTX_TQ
    ctx_q_tiles = n_ctx_rows // tqc
    n_sub = tq // tqc
    nq = (lt - n_ctx_rows) // tq
    ctx_chunks, all_chunks = _key_chunks(n_ctx_rows, lt)
    n2 = 2 * tq
    lam_in = pl.BlockSpec(memory_space=pltpu.SMEM)
    g_in = pl.BlockSpec((1, V_DIM), lambda bi, h, s: (0, 0))
    n_flat = b * N_HEADS * nq

    def tile_map(f, lag):
        def index_map(g):
            t = jnp.clip(g - lag, 0, n_flat - 1)
            bh = t // nq
            return f(bh // N_HEADS, bh % N_HEADS, t % nq)
        return index_map

    lat = pl.pallas_call(
        functools.partial(_attn_pipe_kernel, tq=tq, n_tiles=n_flat, n_q_refs=n_sub, chunks=all_chunks,
                          out_scale=out_scale),
        grid=(n_flat + 1,),
        in_specs=[lam_in] + [
                  pl.BlockSpec((None, tqc, V_DIM),
                               tile_map(lambda bi, h, qi, j=j: (bi, qi * n_sub + ctx_q_tiles + j, h), 0))
                  for j in range(n_sub)] + [
                  pl.BlockSpec((None, lt, V_DIM), tile_map(lambda bi, h, qi: (bi, 0, h), 0)),
                  pl.BlockSpec((None, V_DIM, lt), tile_map(lambda bi, h, qi: (bi, h, 0), 1)),
                  pl.BlockSpec((1, V_DIM), lambda g: (0, 0))],
        out_specs=pl.BlockSpec((None, tq, V_DIM), tile_map(lambda bi, h, qi: (bi, qi, h), 1)),
        out_shape=jax.ShapeDtypeStruct((b, nq * tq, d), BF16),
        scratch_shapes=[pltpu.VMEM((lt, n2), F32), pltpu.VMEM((lt, n2), F32),
                        pltpu.VMEM((1, n2), F32), pltpu.VMEM((1, n2), F32)],
        compiler_params=_cparams(("arbitrary",), 56),
        name="diff_attn",
    )(lam, *([q] * n_sub), k, vt, g_sub)
    if not with_ctx_queries:
        return lat, None
    ctx = pl.pallas_call(
        functools.partial(_attn_single_kernel, tq=tqc, chunks=ctx_chunks, out_scale=out_scale),
        grid=(b, N_HEADS, ctx_q_tiles),
        in_specs=[lam_in,
                  pl.BlockSpec((None, tqc, V_DIM), lambda bi, h, s: (bi, s, h)),
                  pl.BlockSpec((None, n_ctx_rows, V_DIM), lambda bi, h, s: (bi, 0, h)),
                  pl.BlockSpec((None, V_DIM, n_ctx_rows), lambda bi, h, s: (bi, h, 0)),
                  g_in],
        out_specs=pl.BlockSpec((None, tqc, V_DIM), lambda bi, h, s: (bi, s, h)),
        out_shape=jax.ShapeDtypeStruct((b, n_ctx_rows, d), BF16),
        scratch_shapes=[pltpu.VMEM((n_ctx_rows, 2 * tqc), F32), pltpu.VMEM((1, 2 * tqc), F32)],
        compiler_params=_cparams(("parallel", "parallel", "arbitrary"), 32),
        name="diff_attn_ctx",
    )(lam, q, k, vt, g_sub)
    return lat, ctx


def _postmix_kernel(*refs, has_ctx, split, n_ctx_tiles, nt, n_tiles):
    refs = list(refs)
    x_ref = refs.pop(0)
    xc_ref = refs.pop(0) if split else None
    at_ref = refs.pop(0)
    atc_ref = refs.pop(0) if has_ctx else None
    (y_ref, rg_ref, gr_ref, ga_ref, gt1_ref, sh2_ref, sc2_ref, g2_ref,
     wr_ref, wa_ref, wo_ref, wrh_ref, wrl_ref, brt_ref, x1_ref, h2_ref, route_ref, cnt_ref,
     run_scr, hs_scr) = refs
    g = pl.program_id(0)

    @pl.when(g == 0)
    def _():
        run_scr[...] = jnp.zeros_like(run_scr)
        hs_scr[...] = jnp.zeros_like(hs_scr)

    h_hi = hs_scr[0]
    h_lo = hs_scr[1]
    nt_dims = (((1,), (1,)), ((), ()))
    lg = (lax.dot_general(wrh_ref[...], h_hi, nt_dims, preferred_element_type=F32)
          + lax.dot_general(wrh_ref[...], h_lo, nt_dims, preferred_element_type=F32)
          + lax.dot_general(wrl_ref[...], h_hi, nt_dims, preferred_element_type=F32)) + brt_ref[:, 0:1]

    is_ctx = jnp.minimum(g, n_tiles - 1) % nt < n_ctx_tiles
    x = _stream_tile(x_ref, xc_ref, is_ctx)
    at = _stream_tile(at_ref, atc_ref, is_ctx)
    yg = (_load_tiled(y_ref) * _gelu_tanh(rg_ref[...])).astype(BF16)
    o_r = jnp.dot(yg, wr_ref[...], preferred_element_type=F32)
    o_a = jnp.dot(at, wa_ref[...], preferred_element_type=F32)
    mix = (_sigmoid(gr_ref[...]) * o_r + _sigmoid(ga_ref[...]) * o_a).astype(BF16)
    x1 = x + gt1_ref[...] * jnp.dot(mix, wo_ref[...], preferred_element_type=F32)
    x1_ref[...] = x1
    h2 = _rms(x1, g2_ref[...]) * (1.0 + sc2_ref[...]) + sh2_ref[...]
    _store_tiled(h2_ref, h2)
    h2_hi = h2.astype(BF16)
    hs_scr[0] = h2_hi
    hs_scr[1] = (h2 - h2_hi.astype(F32)).astype(BF16)

    rows = h_hi.shape[0]
    ex = lax.broadcasted_iota(jnp.int32, lg.shape, 0)
    ex_f = ex.astype(F32)
    e = jnp.exp(lg - jnp.max(lg, axis=0, keepdims=True))
    p = e / jnp.sum(e, axis=0, keepdims=True)
    grp = ex // EXPERTS_PER_GROUP
    best = jnp.max(jnp.where(grp == 0, p, -1.0), axis=0, keepdims=True)
    best_g = jnp.zeros_like(best, dtype=jnp.int32)
    for gi in range(1, N_GROUPS):
        gm = jnp.max(jnp.where(grp == gi, p, -1.0), axis=0, keepdims=True)
        better = gm > best
        best_g = jnp.where(better, gi, best_g)
        best = jnp.where(better, gm, best)
    pg = jnp.where(grp == best_g, p, -1.0)
    big = float(N_EXPERTS)
    v1 = jnp.max(pg, axis=0, keepdims=True)
    i1 = jnp.min(jnp.where(pg == v1, ex_f, big), axis=0, keepdims=True)
    pg2 = jnp.where(ex_f == i1, -1.0, pg)
    v2 = jnp.max(pg2, axis=0, keepdims=True)
    i2 = jnp.min(jnp.where(pg2 == v2, ex_f, big), axis=0, keepdims=True)
    den = v1 + v2

    live = (g > 0).astype(F32)
    oh1 = (ex_f == i1).astype(F32)
    oh2 = (ex_f == i2).astype(F32)
    both = (oh1 + oh2) * live
    tri = (lax.broadcasted_iota(jnp.int32, (rows, rows), 0)
           < lax.broadcasted_iota(jnp.int32, (rows, rows), 1)).astype(BF16)
    before = jnp.dot(both.astype(BF16), tri, preferred_element_type=F32) + run_scr[:, 0:1]
    r1 = jnp.sum(oh1 * before, axis=0, keepdims=True)
    r2 = jnp.sum(oh2 * before, axis=0, keepdims=True)
    run = run_scr[...] + jnp.sum(both, axis=1, keepdims=True)
    run_scr[...] = run
    cnt_ref[...] = run
    row = lax.broadcasted_iota(jnp.int32, route_ref.shape, 0)
    route_ref[...] = jnp.where(row == 0, i1, jnp.where(row == 1, i2, jnp.where(row == 2, v1 / den, jnp.where(
        row == 3, v2 / den, jnp.where(row == 4, r1, r2)))))


def _postmix(x, x_ctx, attn, attn_ctx, y, rg, gr, ga, gt1, sh2, sc2, g2, wr, wa, wo, wrh, wrl, brt, n_ctx_tiles):
    b, lt, d = rg.shape
    has_ctx = attn_ctx is not None
    off = 0 if has_ctx else n_ctx_tiles
    lq = lt - off * ROW_TILE
    nt = lq // ROW_TILE
    n_tiles = b * nt

    def wrap(f):
        def index_map(g):
            t = jnp.minimum(g, n_tiles - 1)
            return f(t // nt, t % nt)
        return index_map

    full = pl.BlockSpec((None, ROW_TILE, d), wrap(lambda bi, i: (bi, i + off, 0)))
    part = pl.BlockSpec((None, ROW_TILE, d), wrap(lambda bi, i: (bi, i, 0)))
    tiled = pl.BlockSpec((None, ROW_TILE * SUBLANES, LANES), wrap(lambda bi, i: (bi, i + off, 0)))
    mod = pl.BlockSpec((None, 1, d), wrap(lambda bi, i: (jnp.where(i + off < n_ctx_tiles, b, bi), 0, 0)))
    const = lambda g: (0, 0)
    if x_ctx is None:
        x_specs, x_args = [full], [x]
    else:
        x_specs, x_args = _stream_specs(x, x_ctx, n_ctx_tiles, wrap)
    at_specs, at_args = _stream_specs(attn, attn_ctx, n_ctx_tiles, wrap)
    return pl.pallas_call(
        functools.partial(_postmix_kernel, has_ctx=has_ctx, split=x_ctx is not None, n_ctx_tiles=n_ctx_tiles,
                          nt=nt, n_tiles=n_tiles),
        grid=(n_tiles + 1,),
        in_specs=x_specs + at_specs + [tiled, full, full, full, mod, mod, mod,
                                      pl.BlockSpec((1, d), const),
                                      pl.BlockSpec((d, d), const), pl.BlockSpec((d, d), const),
                                      pl.BlockSpec((d, d), const),
                                      pl.BlockSpec((N_EXPERTS, d), const), pl.BlockSpec((N_EXPERTS, d), const),
                                      pl.BlockSpec((N_EXPERTS, LANES), const)],
        out_specs=[part, pl.BlockSpec((ROW_TILE * SUBLANES, LANES), wrap(lambda bi, i: (bi * nt + i, 0))),
                   pl.BlockSpec((SUBLANES, ROW_TILE), lambda g: (0, jnp.maximum(g - 1, 0))),
                   pl.BlockSpec((N_EXPERTS, LANES), const)],
        out_shape=[jax.ShapeDtypeStruct((b, lq, d), F32),
                   jax.ShapeDtypeStruct((b * lq * SUBLANES, LANES), F32),
                   jax.ShapeDtypeStruct((SUBLANES, b * lq), F32),
                   jax.ShapeDtypeStruct((N_EXPERTS, LANES), F32)],
        scratch_shapes=[pltpu.VMEM((N_EXPERTS, LANES), F32), pltpu.VMEM((2, ROW_TILE, d), BF16)],
        compiler_params=_cparams(("arbitrary",), 48),
        name="postmix",
    )(*x_args, *at_args, y, rg, gr, ga, gt1, sh2, sc2, g2, wr, wa, wo, wrh, wrl, brt)


def _dispatch_kernel(pad_ref, dest_ref, h_ref, xs_hbm, zrows, sem, *, n_pad):
    i = pl.program_id(0)
    rows = h_ref.shape[0] // SUBLANES

    def tile(ref, r):
        return ref.at[pl.ds(pl.multiple_of(r * SUBLANES, SUBLANES), SUBLANES)]

    @pl.when(i == 0)
    def _():
        zrows[...] = jnp.zeros_like(zrows)
        for c in range(n_pad // MOE_ROWS):
            def fill(r, carry):
                pltpu.make_async_copy(tile(zrows, r), tile(xs_hbm, pad_ref[c * MOE_ROWS + r]), sem).start()
                return carry

            lax.fori_loop(0, MOE_ROWS, fill, 0, unroll=8)
            pltpu.make_async_copy(zrows, xs_hbm.at[pl.ds(0, MOE_ROWS * SUBLANES)], sem).wait()

    def send(r, carry):
        for k in range(TOP_K):
            pltpu.make_async_copy(tile(h_ref, r), tile(xs_hbm, dest_ref[0, k * rows + r]), sem).start(priority=k)
        return carry

    lax.fori_loop(0, rows, send, 0, unroll=8)
    for k in range(TOP_K):
        pltpu.make_async_copy(h_ref, xs_hbm.at[pl.ds(0, rows * SUBLANES)], sem).wait()


def _dispatch(pad_slots, dest, h2, n_slots):
    t = h2.shape[0] // SUBLANES
    rows = DISPATCH_ROWS
    return pl.pallas_call(
        functools.partial(_dispatch_kernel, n_pad=pad_slots.shape[0]),
        grid_spec=pltpu.PrefetchScalarGridSpec(
            num_scalar_prefetch=1,
            grid=(t // rows,),
            in_specs=[pl.BlockSpec((None, 1, TOP_K * rows), lambda i, pad: (i, 0, 0), memory_space=pltpu.SMEM),
                      pl.BlockSpec((rows * SUBLANES, LANES), lambda i, pad: (i, 0))],
            out_specs=pl.BlockSpec(memory_space=pl.ANY),
            scratch_shapes=[pltpu.VMEM((MOE_ROWS * SUBLANES, LANES), F32), pltpu.SemaphoreType.DMA(())]),
        out_shape=jax.ShapeDtypeStruct((n_slots * SUBLANES, LANES), F32),
        compiler_params=_cparams(("arbitrary",), 32),
        name="dispatch",
    )(pad_slots, _tile_table(dest, rows), h2)


def _expert_kernel(be_ref, nv_ref, x_ref, w1_ref, w3_ref, w2_ref, y_ref, w1_b, w3_b, w2_b):
    i = pl.program_id(0)

    @pl.when(i < nv_ref[0])
    def _():
        @pl.when(jnp.logical_or(i == 0, be_ref[i] != be_ref[jnp.maximum(i - 1, 0)]))
        def _():
            w1_b[...] = w1_ref[...].astype(BF16)
            w3_b[...] = w3_ref[...].astype(BF16)
            w2_b[...] = w2_ref[...].astype(BF16)

        xb = _load_tiled(x_ref).astype(BF16)
        a1 = jnp.dot(xb, w1_b[...], preferred_element_type=F32)
        a3 = jnp.dot(xb, w3_b[...], preferred_element_type=F32)
        mid = (a1 * _sigmoid(a1) * a3).astype(BF16)
        _store_tiled(y_ref, jnp.dot(mid, w2_b[...], preferred_element_type=F32))

    @pl.when(i >= nv_ref[0])
    def _():
        y_ref[...] = jnp.zeros_like(y_ref)


def _experts(block_e, n_valid, xs, w1, w3, w2, layer):
    n_slots = xs.shape[0] // SUBLANES
    d, de = w1.shape[-2:]
    wspec = lambda shp: pl.BlockSpec((None, None) + shp, lambda i, be, nv: (layer, be[i], 0, 0))
    return pl.pallas_call(
        _expert_kernel,
        grid_spec=pltpu.PrefetchScalarGridSpec(
            num_scalar_prefetch=2,
            grid=(n_slots // MOE_ROWS,),
            in_specs=[pl.BlockSpec((MOE_ROWS * SUBLANES, LANES), lambda i, be, nv: (jnp.minimum(i, nv[0] - 1), 0)),
                      wspec((d, de)), wspec((d, de)), wspec((de, d))],
            out_specs=pl.BlockSpec((MOE_ROWS * SUBLANES, LANES), lambda i, be, nv: (i, 0)),
            scratch_shapes=[pltpu.VMEM((d, de), BF16), pltpu.VMEM((d, de), BF16), pltpu.VMEM((de, d), BF16)]),
        out_shape=jax.ShapeDtypeStruct((n_slots * SUBLANES, LANES), F32),
        compiler_params=_cparams(("arbitrary",), 56),
        name="experts",
    )(block_e, n_valid, xs, w1, w3, w2)


def _tile_table(dest, rows):
    n_tiles = dest.shape[1] // rows
    return dest.reshape(TOP_K, n_tiles, rows).transpose(1, 0, 2).reshape(n_tiles, 1, TOP_K * rows)


def _route_tables(route, counts):
    expert = route[0:TOP_K].astype(jnp.int32)
    rank = route[4:4 + TOP_K].astype(jnp.int32)
    a = expert.size
    sizes = counts[:, 0].astype(jnp.int32)
    padded = (sizes + MOE_ROWS - 1) // MOE_ROWS * MOE_ROWS
    pend = jnp.cumsum(padded)
    pstarts = pend - padded
    first = jnp.sum(jnp.where(expert[..., None] == jnp.arange(N_EXPERTS, dtype=jnp.int32), pstarts, 0), axis=-1)
    dest = (first + rank).astype(jnp.int32)
    n_blocks = -(-a // MOE_ROWS) + N_EXPERTS
    n_slots = n_blocks * MOE_ROWS
    n_valid = (pend[-1] // MOE_ROWS).astype(jnp.int32)
    blk = jnp.minimum(jnp.arange(n_blocks, dtype=jnp.int32), n_valid - 1)
    block_e = jnp.minimum(jnp.sum(blk[:, None] * MOE_ROWS >= pend[None, :], axis=1),
                          N_EXPERTS - 1).astype(jnp.int32)
    gap_start = jnp.concatenate([pstarts + sizes, pend[-1:]]).astype(jnp.int32)
    gap_len = jnp.concatenate([padded - sizes, n_slots - pend[-1:]]).astype(jnp.int32)
    gap_end = jnp.cumsum(gap_len)
    j = jnp.arange(n_slots - a, dtype=jnp.int32)
    seg = jnp.sum(j[:, None] >= gap_end[None, :], axis=1)
    pad_slots = (gap_start[seg] + j - (gap_end - gap_len)[seg]).astype(jnp.int32)
    return dest, pad_slots, block_e, n_valid.reshape(1), n_slots


def _combine_kernel(dcur_ref, dnxt_ref, x_ref, gate_ref, gt2_ref, g_ref, y_hbm, o_ref, ybuf, sems, *, final):
    g = pl.program_id(0)
    n = pl.num_programs(0)
    rows = x_ref.shape[0]
    slot = g % 2

    def fetch(tbl_ref, s):
        def body(r, carry):
            for k in range(TOP_K):
                src = pl.multiple_of(tbl_ref[0, k * rows + r] * SUBLANES, SUBLANES)
                pltpu.make_async_copy(y_hbm.at[pl.ds(src, SUBLANES)],
                                      ybuf.at[s, k, pl.ds(pl.multiple_of(r * SUBLANES, SUBLANES), SUBLANES)],
                                      sems.at[s]).start(priority=k)
            return carry

        lax.fori_loop(0, rows, body, 0, unroll=8)

    @pl.when(g == 0)
    def _():
        fetch(dcur_ref, 0)

    @pl.when(g + 1 < n)
    def _():
        fetch(dnxt_ref, 1 - slot)

    for k in range(TOP_K):
        pltpu.make_async_copy(y_hbm.at[pl.ds(0, rows * SUBLANES)], ybuf.at[slot, k], sems.at[slot]).wait()

    gate = gate_ref[...]
    y = gate[:, 0:1] * _load_tiled(ybuf.at[slot, 0]) + gate[:, 1:2] * _load_tiled(ybuf.at[slot, 1])
    x2 = x_ref[...] + gt2_ref[...] * y
    o_ref[...] = _rms(x2, g_ref[...]) if final else x2


def _combine(x1, y_slots, dest, gates, gt2, g_final, n_ctx_tiles, has_ctx, final):
    b, lq, d = x1.shape
    nt = lq // ROW_TILE
    n_tiles = b * nt
    ctx_tiles = n_ctx_tiles if has_ctx else 0
    tok = lambda w: pl.BlockSpec((None, ROW_TILE, w), lambda g: (g // nt, g % nt, 0))
    tbl = lambda f: pl.BlockSpec((None, 1, TOP_K * ROW_TILE), lambda g: (f(g), 0, 0), memory_space=pltpu.SMEM)
    dest3 = _tile_table(dest, ROW_TILE)
    return pl.pallas_call(
        functools.partial(_combine_kernel, final=final),
        grid=(n_tiles,),
        in_specs=[tbl(lambda g: g), tbl(lambda g: jnp.minimum(g + 1, n_tiles - 1)),
                  tok(d), pl.BlockSpec((ROW_TILE, TOP_K), lambda g: (g, 0)),
                  pl.BlockSpec((None, 1, d), lambda g: (jnp.where(g % nt < ctx_tiles, b, g // nt), 0, 0)),
                  pl.BlockSpec((1, d), lambda g: (0, 0)),
                  pl.BlockSpec(memory_space=pl.ANY)],
        out_specs=tok(d),
        out_shape=jax.ShapeDtypeStruct((b, lq, d), F32),
        scratch_shapes=[pltpu.VMEM((2, TOP_K, ROW_TILE * SUBLANES, LANES), F32), pltpu.SemaphoreType.DMA((2,))],
        compiler_params=_cparams(("arbitrary",), 32),
        name="combine",
    )(dest3, dest3, x1, gates, gt2, g_final, y_slots)


def _rope_tables(n_ctx, seq):
    n_pairs = HEAD_DIM // 4
    inv = ROPE_BASE ** (-jnp.arange(n_pairs, dtype=F32) / n_pairs)
    rows = seq // GRID_W
    r = jnp.repeat(jnp.arange(rows, dtype=F32), GRID_W)
    col = jnp.tile(jnp.arange(GRID_W, dtype=F32), rows)
    ang = jnp.concatenate([r[:, None] * inv, col[:, None] * inv], axis=-1)
    cos = jnp.concatenate([jnp.ones((n_ctx, HEAD_DIM // 2), F32), jnp.cos(ang)], axis=0)
    sin = jnp.concatenate([jnp.zeros((n_ctx, HEAD_DIM // 2), F32), jnp.sin(ang)], axis=0)
    return jnp.tile(cos, (1, 4)), jnp.concatenate([-sin, -sin, sin, sin], axis=-1)


def _qk_perm():
    lane = jnp.arange(LANES)
    g, i = lane // 32, lane % 32
    within = (g % 2) * HEAD_DIM + 2 * i + g // 2
    return (jnp.arange(N_HEADS)[:, None] * LANES + within[None, :]).reshape(-1)


def kernel(x, c, ctx, c_ctx, w_mod, b_mod, g_norm1, g_norm2, w_in, conv_w, conv_b, lru_wa, lru_ba, lru_wx,
           lru_bx, lru_lambda, diff_lambda, g_subln, w_rnn_proj, w_attn_proj, w_o, w_router, b_router,
           w_e1, w_e3, w_e2, g_final):
    b, seq, d = x.shape
    n_ctx = ctx.shape[1]
    depth = w_mod.shape[0]
    n_ctx_tiles = n_ctx // ROW_TILE
    assert d == N_HEADS * V_DIM == SUBLANES * LANES and n_ctx % ROW_TILE == 0 and seq % ROW_TILE == 0

    cond = jnp.zeros((MOD_ROWS, d), F32).at[:b].set(c).at[b].set(c_ctx)
    mods = _adaln(cond, w_mod, b_mod).reshape(depth, MOD_ROWS, 6, 1, d)
    cos_t, sin_t = _rope_tables(n_ctx, seq)
    perm = _qk_perm()
    wrt = w_router.T
    wrt_hi = wrt.astype(BF16)
    wrt_lo = (wrt - wrt_hi.astype(F32)).astype(BF16)
    brt = jnp.broadcast_to(b_router.astype(F32)[:, None], (N_EXPERTS, LANES))

    xs, xs_ctx = x, ctx
    out = None
    for l in range(depth):
        last = l == depth - 1
        lam_init = 0.8 - 0.6 * math.exp(-0.3 * l)
        sh1, sc1, gt1, sh2, sc2, gt2 = [mods[l, :, j] for j in range(6)]
        wl = w_in[l]
        w_perm = jnp.concatenate(
            [wl[:, d:2 * d][:, perm], wl[:, 2 * d:3 * d], wl[:, 3 * d:4 * d][:, perm],
             wl[:, 0:d], wl[:, 4 * d:]], axis=1).astype(BF16)
        k, v, q, rx, rg, gr, ga = _inproj(xs, xs_ctx, sh1, sc1, g_norm1[l][None], w_perm, cos_t, sin_t,
                                          n_ctx_tiles)

        lq1, lk1, lq2, lk2 = diff_lambda[l].astype(F32)
        lam = (jnp.exp(jnp.sum(lq1 * lk1)) - jnp.exp(jnp.sum(lq2 * lk2)) + lam_init).reshape(1)
        attn, attn_ctx = _attention(lam, q, k, v, g_subln[l][None], n_ctx, not last, 1.0 - lam_init)

        cdec = LRU_C * jax.nn.softplus(-lru_lambda[l].astype(F32))
        tile_of = lambda vec: vec.reshape(vec.shape[:-1] + (SUBLANES, LANES))
        n_ctx_chunks = n_ctx // SCAN_CHUNK
        hf = None
        for direction in range(2):
            wg = (0.5 * jnp.concatenate([lru_wa[l, direction], lru_wx[l, direction]], axis=-1)).astype(BF16)
            bg = 0.5 * jnp.concatenate([lru_ba[l, direction].reshape(RNN_BLOCKS, 1, RNN_BW),
                                        lru_bx[l, direction].reshape(RNN_BLOCKS, 1, RNN_BW)], axis=-1)
            hf = _scan(rx, tile_of(conv_w[l]), tile_of(conv_b[l]), wg, bg, tile_of(-0.5 * cdec[direction]), hf,
                       n_ctx_chunks, reverse=direction == 1)
        y = hf

        x1, h2, route, counts = _postmix(xs, xs_ctx, attn, attn_ctx, y, rg, gr, ga, gt1, sh2, sc2,
                                         g_norm2[l][None],
                                         w_rnn_proj[l].astype(BF16), w_attn_proj[l].astype(BF16),
                                         w_o[l].astype(BF16), wrt_hi, wrt_lo, brt, n_ctx_tiles)
        lq = x1.shape[1]
        n_tok = b * lq
        dest, pad_slots, block_e, n_valid, n_slots = _route_tables(route, counts)
        xs_sorted = _dispatch(pad_slots, dest, h2, n_slots)
        y_slots = _experts(block_e, n_valid, xs_sorted, w_e1, w_e3, w_e2, l)
        res = _combine(x1, y_slots, dest, route[2:2 + TOP_K].T, gt2, g_final[None], n_ctx_tiles, not last, last)
        if last:
            out = res
        else:
            xs, xs_ctx = res, None
    return out
```

```python
import functools
import math

import jax
import jax.numpy as jnp
from jax import lax
from jax.experimental import pallas as pl
from jax.experimental.pallas import tpu as pltpu

F32 = jnp.float32
BF16 = jnp.bfloat16

N_HEADS = 8
HEAD_DIM = 64
V_DIM = 2 * HEAD_DIM
GRID_W = 64
EPS = 1e-6
RNN_BLOCKS = 8
RNN_BW = 128
CONV_W = 4
CONV_LEFT = 2
LRU_C = 8.0
ROPE_BASE = 10000.0
N_EXPERTS = 16
N_GROUPS = 4
EXPERTS_PER_GROUP = N_EXPERTS // N_GROUPS
TOP_K = 2

LANES = 128
SUBLANES = 8
ROW_TILE = 256
SCAN_CHUNK = 128
ATTN_TQ = 256
ATTN_CTX_TQ = 256
ATTN_TK = 256
ATTN_TK_PROB = 256
MOE_ROWS = 512
DISPATCH_ROWS = 1024
MOD_ROWS = 16
NEG_BIG = -1e30
LOG2_E = math.log2(math.e)


def _cparams(sem, vmem_mib):
    return pltpu.CompilerParams(dimension_semantics=sem, vmem_limit_bytes=vmem_mib * 1024 * 1024)


def _sigmoid(x):
    return 0.5 * jnp.tanh(0.5 * x) + 0.5


def _gelu_tanh(x):
    return 0.5 * x * (1.0 + jnp.tanh(math.sqrt(2.0 / math.pi) * (x + 0.044715 * (x * x * x))))


def _rms(x, g):
    return x * lax.rsqrt(jnp.mean(x * x, axis=-1, keepdims=True) + EPS) * g


def _store_tiled(ref, x):
    rows = x.shape[0]
    for j in range(SUBLANES):
        ref[pl.ds(j, rows, stride=SUBLANES), :] = x[:, j * LANES:(j + 1) * LANES]


def _load_tiled(ref):
    rows = ref.shape[0] // SUBLANES
    return jnp.concatenate([ref[pl.ds(j, rows, stride=SUBLANES), :] for j in range(SUBLANES)], axis=1)


def _adaln_kernel(c_ref, w_ref, b_ref, o_ref):
    c = c_ref[...]
    a = c * _sigmoid(c)
    o_ref[...] = jnp.dot(a, w_ref[...], preferred_element_type=F32,
                         precision=lax.Precision.HIGHEST) + b_ref[...]


def _adaln(cond, w_mod, b_mod):
    depth, d, d6 = w_mod.shape
    n = d6 // d
    return pl.pallas_call(
        _adaln_kernel,
        grid=(depth, n),
        in_specs=[pl.BlockSpec((MOD_ROWS, d), lambda l, j: (0, 0)),
                  pl.BlockSpec((None, d, d), lambda l, j: (l, 0, j)),
                  pl.BlockSpec((None, 1, d), lambda l, j: (l, 0, j))],
        out_specs=pl.BlockSpec((None, MOD_ROWS, d), lambda l, j: (l, 0, j)),
        out_shape=jax.ShapeDtypeStruct((depth, MOD_ROWS, d6), F32),
        compiler_params=_cparams(("arbitrary", "arbitrary"), 32),
        name="adaln",
    )(cond, w_mod, b_mod.reshape(depth, 1, d6))


def _stream_tile(x_ref, xc_ref, is_ctx):
    if xc_ref is None:
        return x_ref[...]
    return jnp.where(is_ctx, xc_ref[...], x_ref[...])


def _stream_specs(x, x_ctx, n_ctx_tiles, wrap=lambda f: f):
    d = x.shape[-1]
    if x_ctx is None:
        return [pl.BlockSpec((None, ROW_TILE, d), wrap(lambda bi, i: (bi, i, 0)))], [x]
    return ([pl.BlockSpec((None, ROW_TILE, d), wrap(lambda bi, i: (bi, jnp.maximum(i - n_ctx_tiles, 0), 0))),
             pl.BlockSpec((None, ROW_TILE, d), wrap(lambda bi, i: (bi, jnp.minimum(i, n_ctx_tiles - 1), 0)))],
            [x, x_ctx])


def _inproj_kernel(*refs, d, split, n_ctx_tiles):
    if split:
        (x_ref, xc_ref, sh_ref, sc_ref, g_ref, w_ref, cos_ref, sin_ref,
         k_ref, v_ref, q_ref, rx_ref, rg_ref, gr_ref, ga_ref) = refs
    else:
        (x_ref, sh_ref, sc_ref, g_ref, w_ref, cos_ref, sin_ref,
         k_ref, v_ref, q_ref, rx_ref, rg_ref, gr_ref, ga_ref) = refs
        xc_ref = None
    x = _stream_tile(x_ref, xc_ref, pl.program_id(1) < n_ctx_tiles)
    h = (_rms(x, g_ref[...]) * (1.0 + sc_ref[...]) + sh_ref[...]).astype(BF16)
    cosv = cos_ref[...]
    sinv = sin_ref[...]

    def proj(p):
        return jnp.dot(h, w_ref[:, p * d:(p + 1) * d], preferred_element_type=F32)

    def rope_store(res, ref, scale):
        for hh in range(N_HEADS):
            xs = res[:, hh * LANES:(hh + 1) * LANES]
            out = xs * cosv + pltpu.roll(xs, LANES // 2, 1) * sinv
            ref[:, hh * LANES:(hh + 1) * LANES] = (out * scale).astype(BF16)

    rope_store(proj(0), k_ref, 1.0)
    v_ref[...] = proj(1).T.astype(BF16)
    rope_store(proj(2), q_ref, HEAD_DIM ** -0.5 * LOG2_E)
    _store_tiled(rx_ref, proj(3))
    rg_ref[...] = proj(4)
    gr_ref[...] = proj(5)
    ga_ref[...] = proj(6)


def _inproj(x, x_ctx, sh, sc, g, w, cos_t, sin_t, n_ctx_tiles):
    b, _, d = x.shape
    lt = x.shape[1] + (0 if x_ctx is None else x_ctx.shape[1])
    nt = lt // ROW_TILE
    mod_map = lambda bi, i: (jnp.where(i < n_ctx_tiles, b, bi), 0, 0)
    tok = pl.BlockSpec((None, ROW_TILE, d), lambda bi, i: (bi, i, 0))
    x_specs, x_args = _stream_specs(x, x_ctx, n_ctx_tiles)
    out_bf = jax.ShapeDtypeStruct((b, lt, d), BF16)
    out_f = jax.ShapeDtypeStruct((b, lt, d), F32)
    return pl.pallas_call(
        functools.partial(_inproj_kernel, d=d, split=x_ctx is not None, n_ctx_tiles=n_ctx_tiles),
        grid=(b, nt),
        in_specs=x_specs + [
                  pl.BlockSpec((None, 1, d), mod_map),
                  pl.BlockSpec((None, 1, d), mod_map),
                  pl.BlockSpec((1, d), lambda bi, i: (0, 0)),
                  pl.BlockSpec(w.shape, lambda bi, i: (0, 0)),
                  pl.BlockSpec((ROW_TILE, LANES), lambda bi, i: (i, 0)),
                  pl.BlockSpec((ROW_TILE, LANES), lambda bi, i: (i, 0))],
        out_specs=[tok, pl.BlockSpec((None, d, ROW_TILE), lambda bi, i: (bi, 0, i)), tok,
                   pl.BlockSpec((None, ROW_TILE * SUBLANES, LANES), lambda bi, i: (bi, i, 0)), tok, tok, tok],
        out_shape=[out_bf, jax.ShapeDtypeStruct((b, d, lt), BF16), out_bf,
                   jax.ShapeDtypeStruct((b, lt * SUBLANES, LANES), F32), out_f, out_f, out_f],
        compiler_params=_cparams(("parallel", "arbitrary"), 56),
        name="inproj",
    )(*x_args, sh, sc, g, w, cos_t, sin_t)


def _scan_chunk(i, n_ctx, n_chunks, reverse):
    if not reverse:
        return i
    return jnp.where(i < n_ctx, n_ctx - 1 - i, n_chunks - 1 - (i - n_ctx))


def _scan_kernel(*refs, reverse, tt, n_ctx, n_chunks):
    if reverse:
        (rx_ref, prev_ref, next_ref, cw_ref, cb_ref, wg_ref, bg_ref, c_ref, hf_ref,
         out_ref, ext_scr, u_scr, a_scr, b_scr, h_scr) = refs
    else:
        (rx_ref, prev_ref, next_ref, cw_ref, cb_ref, wg_ref, bg_ref, c_ref,
         out_ref, ext_scr, u_scr, a_scr, b_scr, h_scr) = refs
        hf_ref = None
    i = pl.program_id(0)
    c = _scan_chunk(i, n_ctx, n_chunks, reverse)
    seq_start = jnp.logical_or(c == 0, c == n_ctx)
    seq_end = jnp.logical_or(c == n_ctx - 1, c == n_chunks - 1)
    nb = rx_ref.shape[0]
    tr = tt * SUBLANES
    left = CONV_LEFT * SUBLANES

    ext_scr[:, 0:left, :] = jnp.where(seq_start, 0.0, prev_ref[...])
    ext_scr[:, left:left + tr, :] = rx_ref[...]
    ext_scr[:, left + tr:left + tr + SUBLANES, :] = jnp.where(seq_end, 0.0, next_ref[...])
    u = cb_ref[...] + cw_ref[0] * ext_scr[:, 0:tr, :].reshape(nb, tt, SUBLANES, LANES)
    for j in range(1, CONV_W):
        u = u + cw_ref[j] * ext_scr[:, j * SUBLANES:j * SUBLANES + tr, :].reshape(nb, tt, SUBLANES, LANES)
    u_scr[...] = u.reshape(nb * tr, LANES)

    rows = nb * tt
    for n in range(RNN_BLOCKS):
        un = u_scr[pl.ds(n, rows, stride=SUBLANES), :]
        z = jnp.dot(un.astype(BF16), wg_ref[n], preferred_element_type=F32) + bg_ref[n]
        t = jnp.tanh(z)
        ig = 0.5 * t[:, RNN_BW:] + 0.5
        a = jnp.exp(c_ref[n:n + 1, :] * t[:, :RNN_BW] + c_ref[n:n + 1, :])
        a_scr[pl.ds(n, rows, stride=SUBLANES), :] = a
        b_scr[pl.ds(n, rows, stride=SUBLANES), :] = jnp.sqrt(1.0 - a * a) * ig * un

    @pl.when(i == 0)
    def _():
        h_scr[...] = jnp.zeros_like(h_scr)

    def step(s, hs):
        t = tt - 1 - s if reverse else s
        new = []
        for bi in range(nb):
            at = pl.ds(pl.multiple_of((bi * tt + t) * SUBLANES, SUBLANES), SUBLANES)
            h = a_scr[at, :] * hs[bi] + b_scr[at, :]
            ot = pl.ds(pl.multiple_of(t * SUBLANES, SUBLANES), SUBLANES)
            out_ref[bi, ot, :] = hf_ref[bi, ot, :] + h if reverse else h
            new.append(h)
        return tuple(new)

    hs = lax.fori_loop(0, tt, step, tuple(h_scr[bi] for bi in range(nb)), unroll=4)
    for bi in range(nb):
        h_scr[bi] = hs[bi]


def _scan(rx, cw, cb, wg, bg, cdec, hf, n_ctx, reverse):
    b, lt8, _ = rx.shape
    tt = SCAN_CHUNK
    n_chunks = lt8 // (tt * SUBLANES)
    chunk = lambda i: _scan_chunk(i, n_ctx, n_chunks, reverse)
    blk = pl.BlockSpec((b, tt * SUBLANES, LANES), lambda i: (0, chunk(i), 0))
    const2 = lambda i: (0, 0)
    const3 = lambda i: (0, 0, 0)
    in_specs = [blk,
                pl.BlockSpec((b, CONV_LEFT * SUBLANES, LANES),
                             lambda i: (0, jnp.maximum(chunk(i) * (tt // CONV_LEFT) - 1, 0), 0)),
                pl.BlockSpec((b, SUBLANES, LANES), lambda i: (0, jnp.minimum((chunk(i) + 1) * tt, n_chunks * tt - 1), 0)),
                pl.BlockSpec(cw.shape, const3),
                pl.BlockSpec(cb.shape, const2),
                pl.BlockSpec(wg.shape, const3),
                pl.BlockSpec(bg.shape, const3),
                pl.BlockSpec(cdec.shape, const2)]
    args = [rx, rx, rx, cw, cb, wg, bg, cdec]
    if reverse:
        in_specs.append(blk)
        args.append(hf)
    tr = tt * SUBLANES
    return pl.pallas_call(
        functools.partial(_scan_kernel, reverse=reverse, tt=tt, n_ctx=n_ctx, n_chunks=n_chunks),
        grid=(n_chunks,),
        in_specs=in_specs,
        out_specs=blk,
        out_shape=jax.ShapeDtypeStruct(rx.shape, F32),
        scratch_shapes=[pltpu.VMEM((b, tr + (CONV_W - 1) * SUBLANES, LANES), F32),
                        pltpu.VMEM((b * tr, LANES), F32),
                        pltpu.VMEM((b * tr, LANES), F32),
                        pltpu.VMEM((b * tr, LANES), F32),
                        pltpu.VMEM((b, SUBLANES, LANES), F32)],
        compiler_params=_cparams(("arbitrary",), 56),
        name="scan_bwd" if reverse else "scan_fwd",
    )(*args)


def _attn_passes(lam, q_refs, k_ref, vt_ref, g_ref, o_ref, *, write, read, tq, w_chunks, r_chunks, out_scale):
    n2 = 2 * tq
    if write is not None:
        s_w, m_w = write
        q = jnp.concatenate([r[...] for r in q_refs], axis=0) if len(q_refs) > 1 else q_refs[0][...]
        lane = lax.broadcasted_iota(jnp.int32, q.shape, 1)
        first = (lane // (HEAD_DIM // 2)) % 2 == 0
        zero = jnp.zeros_like(q)
        qs = jnp.concatenate([jnp.where(first, q, zero), jnp.where(first, zero, q)], axis=0)
        m8 = jnp.full((SUBLANES, n2), NEG_BIG, F32)
    if read is not None:
        s_r, m_r = read
        m = m_r[...]
        l8 = jnp.zeros((SUBLANES, n2), F32)
        acc = jnp.zeros((V_DIM, n2), F32)
    for j in range(max(len(w_chunks) if write is not None else 0, len(r_chunks) if read is not None else 0)):
        if read is not None and j < len(r_chunks):
            start, size = r_chunks[j]
            p = jnp.exp2(s_r[start:start + size, :] - m)
            l8 = l8 + jnp.sum(p.reshape(size // SUBLANES, SUBLANES, n2), axis=0)
            acc = acc + jnp.dot(vt_ref[:, start:start + size], p.astype(BF16), preferred_element_type=F32)
        if write is not None and j < len(w_chunks):
            start, size = w_chunks[j]
            st = lax.dot_general(k_ref[start:start + size, :], qs, (((1,), (1,)), ((), ())),
                                 preferred_element_type=F32)
            s_w[start:start + size, :] = st
            m8 = jnp.maximum(m8, jnp.max(st.reshape(size // SUBLANES, SUBLANES, n2), axis=0))
    if write is not None:
        m_w[...] = jnp.max(m8, axis=0, keepdims=True)
    if read is not None:
        on = acc * (1.0 / jnp.sum(l8, axis=0, keepdims=True))
        dt = on[:, :tq] - lam * on[:, tq:]
        yt = dt * lax.rsqrt(jnp.mean(dt * dt, axis=0, keepdims=True) + EPS)
        o_ref[...] = (yt.T * g_ref[...] * out_scale).astype(BF16)


def _attn_pipe_kernel(lam_ref, *refs, tq, n_tiles, n_q_refs, chunks, prob_chunks, out_scale):
    q_refs = refs[:n_q_refs]
    k_ref, vt_ref, g_ref, o_ref, s_a, s_b, m_a, m_b = refs[n_q_refs:]
    s = pl.program_id(0)
    run = functools.partial(_attn_passes, lam_ref[0], q_refs, k_ref, vt_ref, g_ref, o_ref, tq=tq,
                            w_chunks=chunks, r_chunks=prob_chunks, out_scale=out_scale)
    buf_a, buf_b = (s_a, m_a), (s_b, m_b)
    inner = jnp.logical_and(s > 0, s < n_tiles)
    even = s % 2 == 0

    @pl.when(s == 0)
    def _():
        run(write=buf_a, read=None)

    @pl.when(jnp.logical_and(inner, even))
    def _():
        run(write=buf_a, read=buf_b)

    @pl.when(jnp.logical_and(inner, jnp.logical_not(even)))
    def _():
        run(write=buf_b, read=buf_a)

    @pl.when(s == n_tiles)
    def _():
        run(write=None, read=buf_a if (n_tiles - 1) % 2 == 0 else buf_b)


def _attn_single_kernel(lam_ref, q_ref, k_ref, vt_ref, g_ref, o_ref, s_a, m_a, *, tq, chunks, out_scale):
    run = functools.partial(_attn_passes, lam_ref[0], (q_ref,), k_ref, vt_ref, g_ref, o_ref, tq=tq,
                            w_chunks=chunks, r_chunks=chunks, out_scale=out_scale)
    run(write=(s_a, m_a), read=None)
    run(write=None, read=(s_a, m_a))


def _key_chunks(n_ctx_rows, lt, tk):
    ctx = [(s, min(tk, n_ctx_rows - s)) for s in range(0, n_ctx_rows, tk)]
    lat = [(s, min(tk, lt - s)) for s in range(n_ctx_rows, lt, tk)]
    return tuple(ctx), tuple(ctx + lat)


def _attention(lam, q, k, vt, g_sub, n_ctx_rows, with_ctx_queries, out_scale):
    b, lt, d = q.shape
    tq, tqc = ATTN_TQ, ATTN_CTX_TQ
    ctx_q_tiles = n_ctx_rows // tqc
    n_sub = tq // tqc
    nq = (lt - n_ctx_rows) // tq
    ctx_chunks, all_chunks = _key_chunks(n_ctx_rows, lt, ATTN_TK)
    _, prob_chunks = _key_chunks(n_ctx_rows, lt, ATTN_TK_PROB)
    n2 = 2 * tq
    lam_in = pl.BlockSpec(memory_space=pltpu.SMEM)
    g_in = pl.BlockSpec((1, V_DIM), lambda bi, h, s: (0, 0))
    n_flat = b * N_HEADS * nq

    def tile_map(f, lag):
        def index_map(g):
            t = jnp.clip(g - lag, 0, n_flat - 1)
            bh = t // nq
            return f(bh // N_HEADS, bh % N_HEADS, t % nq)
        return index_map

    lat = pl.pallas_call(
        functools.partial(_attn_pipe_kernel, tq=tq, n_tiles=n_flat, n_q_refs=n_sub, chunks=all_chunks,
                          prob_chunks=prob_chunks,
                          out_scale=out_scale),
        grid=(n_flat + 1,),
        in_specs=[lam_in] + [
                  pl.BlockSpec((None, tqc, V_DIM),
                               tile_map(lambda bi, h, qi, j=j: (bi, qi * n_sub + ctx_q_tiles + j, h), 0))
                  for j in range(n_sub)] + [
                  pl.BlockSpec((None, lt, V_DIM), tile_map(lambda bi, h, qi: (bi, 0, h), 0)),
                  pl.BlockSpec((None, V_DIM, lt), tile_map(lambda bi, h, qi: (bi, h, 0), 1)),
                  pl.BlockSpec((1, V_DIM), lambda g: (0, 0))],
        out_specs=pl.BlockSpec((None, tq, V_DIM), tile_map(lambda bi, h, qi: (bi, qi, h), 1)),
        out_shape=jax.ShapeDtypeStruct((b, nq * tq, d), BF16),
        scratch_shapes=[pltpu.VMEM((lt, n2), F32), pltpu.VMEM((lt, n2), F32),
                        pltpu.VMEM((1, n2), F32), pltpu.VMEM((1, n2), F32)],
        compiler_params=_cparams(("arbitrary",), 56),
        name="diff_attn",
    )(lam, *([q] * n_sub), k, vt, g_sub)
    if not with_ctx_queries:
        return lat, None
    ctx = pl.pallas_call(
        functools.partial(_attn_single_kernel, tq=tqc, chunks=ctx_chunks, out_scale=out_scale),
        grid=(b, N_HEADS, ctx_q_tiles),
        in_specs=[lam_in,
                  pl.BlockSpec((None, tqc, V_DIM), lambda bi, h, s: (bi, s, h)),
                  pl.BlockSpec((None, n_ctx_rows, V_DIM), lambda bi, h, s: (bi, 0, h)),
                  pl.BlockSpec((None, V_DIM, n_ctx_rows), lambda bi, h, s: (bi, h, 0)),
                  g_in],
        out_specs=pl.BlockSpec((None, tqc, V_DIM), lambda bi, h, s: (bi, s, h)),
        out_shape=jax.ShapeDtypeStruct((b, n_ctx_rows, d), BF16),
        scratch_shapes=[pltpu.VMEM((n_ctx_rows, 2 * tqc), F32), pltpu.VMEM((1, 2 * tqc), F32)],
        compiler_params=_cparams(("parallel", "parallel", "arbitrary"), 32),
        name="diff_attn_ctx",
    )(lam, q, k, vt, g_sub)
    return lat, ctx


def _postmix_kernel(*refs, has_ctx, split, n_ctx_tiles, nt, n_tiles):
    refs = list(refs)
    x_ref = refs.pop(0)
    xc_ref = refs.pop(0) if split else None
    at_ref = refs.pop(0)
    atc_ref = refs.pop(0) if has_ctx else None
    (y_ref, rg_ref, gr_ref, ga_ref, gt1_ref, sh2_ref, sc2_ref, g2_ref,
     wr_ref, wa_ref, wo_ref, wrh_ref, wrl_ref, brt_ref, x1_ref, h2_ref, route_ref, cnt_ref,
     run_scr, hs_scr) = refs
    g = pl.program_id(0)

    @pl.when(g == 0)
    def _():
        run_scr[...] = jnp.zeros_like(run_scr)
        hs_scr[...] = jnp.zeros_like(hs_scr)

    h_hi = hs_scr[0]
    h_lo = hs_scr[1]
    nt_dims = (((1,), (1,)), ((), ()))
    lg = (lax.dot_general(wrh_ref[...], h_hi, nt_dims, preferred_element_type=F32)
          + lax.dot_general(wrh_ref[...], h_lo, nt_dims, preferred_element_type=F32)
          + lax.dot_general(wrl_ref[...], h_hi, nt_dims, preferred_element_type=F32)) + brt_ref[:, 0:1]

    is_ctx = jnp.minimum(g, n_tiles - 1) % nt < n_ctx_tiles
    x = _stream_tile(x_ref, xc_ref, is_ctx)
    at = _stream_tile(at_ref, atc_ref, is_ctx)
    yg = (_load_tiled(y_ref) * _gelu_tanh(rg_ref[...])).astype(BF16)
    o_r = jnp.dot(yg, wr_ref[...], preferred_element_type=F32)
    o_a = jnp.dot(at, wa_ref[...], preferred_element_type=F32)
    mix = (_sigmoid(gr_ref[...]) * o_r + _sigmoid(ga_ref[...]) * o_a).astype(BF16)
    x1 = x + gt1_ref[...] * jnp.dot(mix, wo_ref[...], preferred_element_type=F32)
    x1_ref[...] = x1
    h2 = _rms(x1, g2_ref[...]) * (1.0 + sc2_ref[...]) + sh2_ref[...]
    _store_tiled(h2_ref, h2)
    h2_hi = h2.astype(BF16)
    hs_scr[0] = h2_hi
    hs_scr[1] = (h2 - h2_hi.astype(F32)).astype(BF16)

    rows = h_hi.shape[0]
    ex = lax.broadcasted_iota(jnp.int32, lg.shape, 0)
    ex_f = ex.astype(F32)
    e = jnp.exp(lg - jnp.max(lg, axis=0, keepdims=True))
    p = e / jnp.sum(e, axis=0, keepdims=True)
    grp = ex // EXPERTS_PER_GROUP
    best = jnp.max(jnp.where(grp == 0, p, -1.0), axis=0, keepdims=True)
    best_g = jnp.zeros_like(best, dtype=jnp.int32)
    for gi in range(1, N_GROUPS):
        gm = jnp.max(jnp.where(grp == gi, p, -1.0), axis=0, keepdims=True)
        better = gm > best
        best_g = jnp.where(better, gi, best_g)
        best = jnp.where(better, gm, best)
    pg = jnp.where(grp == best_g, p, -1.0)
    big = float(N_EXPERTS)
    v1 = jnp.max(pg, axis=0, keepdims=True)
    i1 = jnp.min(jnp.where(pg == v1, ex_f, big), axis=0, keepdims=True)
    pg2 = jnp.where(ex_f == i1, -1.0, pg)
    v2 = jnp.max(pg2, axis=0, keepdims=True)
    i2 = jnp.min(jnp.where(pg2 == v2, ex_f, big), axis=0, keepdims=True)
    den = v1 + v2

    live = (g > 0).astype(F32)
    oh1 = (ex_f == i1).astype(F32)
    oh2 = (ex_f == i2).astype(F32)
    both = (oh1 + oh2) * live
    tri = (lax.broadcasted_iota(jnp.int32, (rows, rows), 0)
           < lax.broadcasted_iota(jnp.int32, (rows, rows), 1)).astype(BF16)
    before = jnp.dot(both.astype(BF16), tri, preferred_element_type=F32) + run_scr[:, 0:1]
    r1 = jnp.sum(oh1 * before, axis=0, keepdims=True)
    r2 = jnp.sum(oh2 * before, axis=0, keepdims=True)
    run = run_scr[...] + jnp.sum(both, axis=1, keepdims=True)
    run_scr[...] = run
    cnt_ref[...] = run
    row = lax.broadcasted_iota(jnp.int32, route_ref.shape, 0)
    route_ref[...] = jnp.where(row == 0, i1, jnp.where(row == 1, i2, jnp.where(row == 2, v1 / den, jnp.where(
        row == 3, v2 / den, jnp.where(row == 4, r1, r2)))))


def _postmix(x, x_ctx, attn, attn_ctx, y, rg, gr, ga, gt1, sh2, sc2, g2, wr, wa, wo, wrh, wrl, brt, n_ctx_tiles):
    b, lt, d = rg.shape
    has_ctx = attn_ctx is not None
    off = 0 if has_ctx else n_ctx_tiles
    lq = lt - off * ROW_TILE
    nt = lq // ROW_TILE
    n_tiles = b * nt

    def wrap(f):
        def index_map(g):
            t = jnp.minimum(g, n_tiles - 1)
            return f(t // nt, t % nt)
        return index_map

    full = pl.BlockSpec((None, ROW_TILE, d), wrap(lambda bi, i: (bi, i + off, 0)))
    part = pl.BlockSpec((None, ROW_TILE, d), wrap(lambda bi, i: (bi, i, 0)))
    tiled = pl.BlockSpec((None, ROW_TILE * SUBLANES, LANES), wrap(lambda bi, i: (bi, i + off, 0)))
    mod = pl.BlockSpec((None, 1, d), wrap(lambda bi, i: (jnp.where(i + off < n_ctx_tiles, b, bi), 0, 0)))
    const = lambda g: (0, 0)
    if x_ctx is None:
        x_specs, x_args = [full], [x]
    else:
        x_specs, x_args = _stream_specs(x, x_ctx, n_ctx_tiles, wrap)
    at_specs, at_args = _stream_specs(attn, attn_ctx, n_ctx_tiles, wrap)
    return pl.pallas_call(
        functools.partial(_postmix_kernel, has_ctx=has_ctx, split=x_ctx is not None, n_ctx_tiles=n_ctx_tiles,
                          nt=nt, n_tiles=n_tiles),
        grid=(n_tiles + 1,),
        in_specs=x_specs + at_specs + [tiled, full, full, full, mod, mod, mod,
                                      pl.BlockSpec((1, d), const),
                                      pl.BlockSpec((d, d), const), pl.BlockSpec((d, d), const),
                                      pl.BlockSpec((d, d), const),
                                      pl.BlockSpec((N_EXPERTS, d), const), pl.BlockSpec((N_EXPERTS, d), const),
                                      pl.BlockSpec((N_EXPERTS, LANES), const)],
        out_specs=[part, pl.BlockSpec((ROW_TILE * SUBLANES, LANES), wrap(lambda bi, i: (bi * nt + i, 0))),
                   pl.BlockSpec((SUBLANES, ROW_TILE), lambda g: (0, jnp.maximum(g - 1, 0))),
                   pl.BlockSpec((N_EXPERTS, LANES), const)],
        out_shape=[jax.ShapeDtypeStruct((b, lq, d), F32),
                   jax.ShapeDtypeStruct((b * lq * SUBLANES, LANES), F32),
                   jax.ShapeDtypeStruct((SUBLANES, b * lq), F32),
                   jax.ShapeDtypeStruct((N_EXPERTS, LANES), F32)],
        scratch_shapes=[pltpu.VMEM((N_EXPERTS, LANES), F32), pltpu.VMEM((2, ROW_TILE, d), BF16)],
        compiler_params=_cparams(("arbitrary",), 48),
        name="postmix",
    )(*x_args, *at_args, y, rg, gr, ga, gt1, sh2, sc2, g2, wr, wa, wo, wrh, wrl, brt)


def _dispatch_kernel(pad_ref, dest_ref, h_ref, xs_hbm, zrows, sem, *, n_pad):
    i = pl.program_id(0)
    rows = h_ref.shape[0] // SUBLANES

    def tile(ref, r):
        return ref.at[pl.ds(pl.multiple_of(r * SUBLANES, SUBLANES), SUBLANES)]

    @pl.when(i == 0)
    def _():
        zrows[...] = jnp.zeros_like(zrows)
        for c in range(n_pad // MOE_ROWS):
            def fill(r, carry):
                pltpu.make_async_copy(tile(zrows, r), tile(xs_hbm, pad_ref[c * MOE_ROWS + r]), sem).start()
                return carry

            lax.fori_loop(0, MOE_ROWS, fill, 0, unroll=8)
            pltpu.make_async_copy(zrows, xs_hbm.at[pl.ds(0, MOE_ROWS * SUBLANES)], sem).wait()

    def send(r, carry):
        for k in range(TOP_K):
            pltpu.make_async_copy(tile(h_ref, r), tile(xs_hbm, dest_ref[0, k * rows + r]), sem).start(priority=k)
        return carry

    lax.fori_loop(0, rows, send, 0, unroll=8)
    for k in range(TOP_K):
        pltpu.make_async_copy(h_ref, xs_hbm.at[pl.ds(0, rows * SUBLANES)], sem).wait()


def _dispatch(pad_slots, dest, h2, n_slots):
    t = h2.shape[0] // SUBLANES
    rows = DISPATCH_ROWS
    return pl.pallas_call(
        functools.partial(_dispatch_kernel, n_pad=pad_slots.shape[0]),
        grid_spec=pltpu.PrefetchScalarGridSpec(
            num_scalar_prefetch=1,
            grid=(t // rows,),
            in_specs=[pl.BlockSpec((None, 1, TOP_K * rows), lambda i, pad: (i, 0, 0), memory_space=pltpu.SMEM),
                      pl.BlockSpec((rows * SUBLANES, LANES), lambda i, pad: (i, 0))],
            out_specs=pl.BlockSpec(memory_space=pl.ANY),
            scratch_shapes=[pltpu.VMEM((MOE_ROWS * SUBLANES, LANES), F32), pltpu.SemaphoreType.DMA(())]),
        out_shape=jax.ShapeDtypeStruct((n_slots * SUBLANES, LANES), F32),
        compiler_params=_cparams(("arbitrary",), 32),
        name="dispatch",
    )(pad_slots, _tile_table(dest, rows), h2)


def _expert_kernel(be_ref, nv_ref, x_ref, w1_ref, w3_ref, w2_ref, y_ref, w1_b, w3_b, w2_b):
    i = pl.program_id(0)

    @pl.when(i < nv_ref[0])
    def _():
        @pl.when(jnp.logical_or(i == 0, be_ref[i] != be_ref[jnp.maximum(i - 1, 0)]))
        def _():
            w1_b[...] = w1_ref[...].astype(BF16)
            w3_b[...] = w3_ref[...].astype(BF16)
            w2_b[...] = w2_ref[...].astype(BF16)

        xb = _load_tiled(x_ref).astype(BF16)
        a1 = jnp.dot(xb, w1_b[...], preferred_element_type=F32)
        a3 = jnp.dot(xb, w3_b[...], preferred_element_type=F32)
        mid = (a1 * _sigmoid(a1) * a3).astype(BF16)
        _store_tiled(y_ref, jnp.dot(mid, w2_b[...], preferred_element_type=F32))

    @pl.when(i >= nv_ref[0])
    def _():
        y_ref[...] = jnp.zeros_like(y_ref)


def _experts(block_e, n_valid, xs, w1, w3, w2, layer):
    n_slots = xs.shape[0] // SUBLANES
    d, de = w1.shape[-2:]
    wspec = lambda shp: pl.BlockSpec((None, None) + shp, lambda i, be, nv: (layer, be[i], 0, 0))
    return pl.pallas_call(
        _expert_kernel,
        grid_spec=pltpu.PrefetchScalarGridSpec(
            num_scalar_prefetch=2,
            grid=(n_slots // MOE_ROWS,),
            in_specs=[pl.BlockSpec((MOE_ROWS * SUBLANES, LANES), lambda i, be, nv: (jnp.minimum(i, nv[0] - 1), 0)),
                      wspec((d, de)), wspec((d, de)), wspec((de, d))],
            out_specs=pl.BlockSpec((MOE_ROWS * SUBLANES, LANES), lambda i, be, nv: (i, 0)),
            scratch_shapes=[pltpu.VMEM((d, de), BF16), pltpu.VMEM((d, de), BF16), pltpu.VMEM((de, d), BF16)]),
        out_shape=jax.ShapeDtypeStruct((n_slots * SUBLANES, LANES), F32),
        compiler_params=_cparams(("arbitrary",), 56),
        name="experts",
    )(block_e, n_valid, xs, w1, w3, w2)


def _tile_table(dest, rows):
    n_tiles = dest.shape[1] // rows
    return dest.reshape(TOP_K, n_tiles, rows).transpose(1, 0, 2).reshape(n_tiles, 1, TOP_K * rows)


def _route_tables(route, counts):
    expert = route[0:TOP_K].astype(jnp.int32)
    rank = route[4:4 + TOP_K].astype(jnp.int32)
    a = expert.size
    sizes = counts[:, 0].astype(jnp.int32)
    padded = (sizes + MOE_ROWS - 1) // MOE_ROWS * MOE_ROWS
    pend = jnp.cumsum(padded)
    pstarts = pend - padded
    first = jnp.sum(jnp.where(expert[..., None] == jnp.arange(N_EXPERTS, dtype=jnp.int32), pstarts, 0), axis=-1)
    dest = (first + rank).astype(jnp.int32)
    n_blocks = -(-a // MOE_ROWS) + N_EXPERTS
    n_slots = n_blocks * MOE_ROWS
    n_valid = (pend[-1] // MOE_ROWS).astype(jnp.int32)
    blk = jnp.minimum(jnp.arange(n_blocks, dtype=jnp.int32), n_valid - 1)
    block_e = jnp.minimum(jnp.sum(blk[:, None] * MOE_ROWS >= pend[None, :], axis=1),
                          N_EXPERTS - 1).astype(jnp.int32)
    gap_start = jnp.concatenate([pstarts + sizes, pend[-1:]]).astype(jnp.int32)
    gap_len = jnp.concatenate([padded - sizes, n_slots - pend[-1:]]).astype(jnp.int32)
    gap_end = jnp.cumsum(gap_len)
    j = jnp.arange(n_slots - a, dtype=jnp.int32)
    seg = jnp.sum(j[:, None] >= gap_end[None, :], axis=1)
    pad_slots = (gap_start[seg] + j - (gap_end - gap_len)[seg]).astype(jnp.int32)
    return dest, pad_slots, block_e, n_valid.reshape(1), n_slots


def _combine_kernel(dcur_ref, dnxt_ref, x_ref, gate_ref, gt2_ref, g_ref, y_hbm, o_ref, ybuf, sems, *, final):
    g = pl.program_id(0)
    n = pl.num_programs(0)
    rows = x_ref.shape[0]
    slot = g % 2

    def fetch(tbl_ref, s):
        def body(r, carry):
            for k in range(TOP_K):
                src = pl.multiple_of(tbl_ref[0, k * rows + r] * SUBLANES, SUBLANES)
                pltpu.make_async_copy(y_hbm.at[pl.ds(src, SUBLANES)],
                                      ybuf.at[s, k, pl.ds(pl.multiple_of(r * SUBLANES, SUBLANES), SUBLANES)],
                                      sems.at[s]).start(priority=k)
            return carry

        lax.fori_loop(0, rows, body, 0, unroll=8)

    @pl.when(g == 0)
    def _():
        fetch(dcur_ref, 0)

    @pl.when(g + 1 < n)
    def _():
        fetch(dnxt_ref, 1 - slot)

    for k in range(TOP_K):
        pltpu.make_async_copy(y_hbm.at[pl.ds(0, rows * SUBLANES)], ybuf.at[slot, k], sems.at[slot]).wait()

    gate = gate_ref[...]
    y = gate[:, 0:1] * _load_tiled(ybuf.at[slot, 0]) + gate[:, 1:2] * _load_tiled(ybuf.at[slot, 1])
    x2 = x_ref[...] + gt2_ref[...] * y
    o_ref[...] = _rms(x2, g_ref[...]) if final else x2


def _combine(x1, y_slots, dest, gates, gt2, g_final, n_ctx_tiles, has_ctx, final):
    b, lq, d = x1.shape
    nt = lq // ROW_TILE
    n_tiles = b * nt
    ctx_tiles = n_ctx_tiles if has_ctx else 0
    tok = lambda w: pl.BlockSpec((None, ROW_TILE, w), lambda g: (g // nt, g % nt, 0))
    tbl = lambda f: pl.BlockSpec((None, 1, TOP_K * ROW_TILE), lambda g: (f(g), 0, 0), memory_space=pltpu.SMEM)
    dest3 = _tile_table(dest, ROW_TILE)
    return pl.pallas_call(
        functools.partial(_combine_kernel, final=final),
        grid=(n_tiles,),
        in_specs=[tbl(lambda g: g), tbl(lambda g: jnp.minimum(g + 1, n_tiles - 1)),
                  tok(d), pl.BlockSpec((ROW_TILE, TOP_K), lambda g: (g, 0)),
                  pl.BlockSpec((None, 1, d), lambda g: (jnp.where(g % nt < ctx_tiles, b, g // nt), 0, 0)),
                  pl.BlockSpec((1, d), lambda g: (0, 0)),
                  pl.BlockSpec(memory_space=pl.ANY)],
        out_specs=tok(d),
        out_shape=jax.ShapeDtypeStruct((b, lq, d), F32),
        scratch_shapes=[pltpu.VMEM((2, TOP_K, ROW_TILE * SUBLANES, LANES), F32), pltpu.SemaphoreType.DMA((2,))],
        compiler_params=_cparams(("arbitrary",), 32),
        name="combine",
    )(dest3, dest3, x1, gates, gt2, g_final, y_slots)


def _rope_tables(n_ctx, seq):
    n_pairs = HEAD_DIM // 4
    inv = ROPE_BASE ** (-jnp.arange(n_pairs, dtype=F32) / n_pairs)
    rows = seq // GRID_W
    r = jnp.repeat(jnp.arange(rows, dtype=F32), GRID_W)
    col = jnp.tile(jnp.arange(GRID_W, dtype=F32), rows)
    ang = jnp.concatenate([r[:, None] * inv, col[:, None] * inv], axis=-1)
    cos = jnp.concatenate([jnp.ones((n_ctx, HEAD_DIM // 2), F32), jnp.cos(ang)], axis=0)
    sin = jnp.concatenate([jnp.zeros((n_ctx, HEAD_DIM // 2), F32), jnp.sin(ang)], axis=0)
    return jnp.tile(cos, (1, 4)), jnp.concatenate([-sin, -sin, sin, sin], axis=-1)


def _qk_perm():
    lane = jnp.arange(LANES)
    g, i = lane // 32, lane % 32
    within = (g % 2) * HEAD_DIM + 2 * i + g // 2
    return (jnp.arange(N_HEADS)[:, None] * LANES + within[None, :]).reshape(-1)


def kernel(x, c, ctx, c_ctx, w_mod, b_mod, g_norm1, g_norm2, w_in, conv_w, conv_b, lru_wa, lru_ba, lru_wx,
           lru_bx, lru_lambda, diff_lambda, g_subln, w_rnn_proj, w_attn_proj, w_o, w_router, b_router,
           w_e1, w_e3, w_e2, g_final):
    b, seq, d = x.shape
    n_ctx = ctx.shape[1]
    depth = w_mod.shape[0]
    n_ctx_tiles = n_ctx // ROW_TILE
    assert d == N_HEADS * V_DIM == SUBLANES * LANES and n_ctx % ROW_TILE == 0 and seq % ROW_TILE == 0

    cond = jnp.zeros((MOD_ROWS, d), F32).at[:b].set(c).at[b].set(c_ctx)
    mods = _adaln(cond, w_mod, b_mod).reshape(depth, MOD_ROWS, 6, 1, d)
    cos_t, sin_t = _rope_tables(n_ctx, seq)
    perm = _qk_perm()
    wrt = w_router.T
    wrt_hi = wrt.astype(BF16)
    wrt_lo = (wrt - wrt_hi.astype(F32)).astype(BF16)
    brt = jnp.broadcast_to(b_router.astype(F32)[:, None], (N_EXPERTS, LANES))

    xs, xs_ctx = x, ctx
    out = None
    for l in range(depth):
        last = l == depth - 1
        lam_init = 0.8 - 0.6 * math.exp(-0.3 * l)
        sh1, sc1, gt1, sh2, sc2, gt2 = [mods[l, :, j] for j in range(6)]
        wl = w_in[l]
        w_perm = jnp.concatenate(
            [wl[:, d:2 * d][:, perm], wl[:, 2 * d:3 * d], wl[:, 3 * d:4 * d][:, perm],
             wl[:, 0:d], wl[:, 4 * d:]], axis=1).astype(BF16)
        k, v, q, rx, rg, gr, ga = _inproj(xs, xs_ctx, sh1, sc1, g_norm1[l][None], w_perm, cos_t, sin_t,
                                          n_ctx_tiles)

        lq1, lk1, lq2, lk2 = diff_lambda[l].astype(F32)
        lam = (jnp.exp(jnp.sum(lq1 * lk1)) - jnp.exp(jnp.sum(lq2 * lk2)) + lam_init).reshape(1)
        attn, attn_ctx = _attention(lam, q, k, v, g_subln[l][None], n_ctx, not last, 1.0 - lam_init)

        cdec = LRU_C * jax.nn.softplus(-lru_lambda[l].astype(F32))
        tile_of = lambda vec: vec.reshape(vec.shape[:-1] + (SUBLANES, LANES))
        n_ctx_chunks = n_ctx // SCAN_CHUNK
        hf = None
        for direction in range(2):
            wg = (0.5 * jnp.concatenate([lru_wa[l, direction], lru_wx[l, direction]], axis=-1)).astype(BF16)
            bg = 0.5 * jnp.concatenate([lru_ba[l, direction].reshape(RNN_BLOCKS, 1, RNN_BW),
                                        lru_bx[l, direction].reshape(RNN_BLOCKS, 1, RNN_BW)], axis=-1)
            hf = _scan(rx, tile_of(conv_w[l]), tile_of(conv_b[l]), wg, bg, tile_of(-0.5 * cdec[direction]), hf,
                       n_ctx_chunks, reverse=direction == 1)
        y = hf

        x1, h2, route, counts = _postmix(xs, xs_ctx, attn, attn_ctx, y, rg, gr, ga, gt1, sh2, sc2,
                                         g_norm2[l][None],
                                         w_rnn_proj[l].astype(BF16), w_attn_proj[l].astype(BF16),
                                         w_o[l].astype(BF16), wrt_hi, wrt_lo, brt, n_ctx_tiles)
        lq = x1.shape[1]
        n_tok = b * lq
        dest, pad_slots, block_e, n_valid, n_slots = _route_tables(route, counts)
        xs_sorted = _dispatch(pad_slots, dest, h2, n_slots)
        y_slots = _experts(block_e, n_valid, xs_sorted, w_e1, w_e3, w_e2, l)
        res = _combine(x1, y_slots, dest, route[2:2 + TOP_K].T, gt2, g_final[None], n_ctx_tiles, not last, last)
        if last:
            out = res
        else:
            xs, xs_ctx = res, None
    return out
```

```python
import functools
import math

import jax
import jax.numpy as jnp
from jax import lax
from jax.experimental import pallas as pl
from jax.experimental.pallas import tpu as pltpu

F32 = jnp.float32
BF16 = jnp.bfloat16

N_HEADS = 8
HEAD_DIM = 64
V_DIM = 2 * HEAD_DIM
GRID_W = 64
EPS = 1e-6
RNN_BLOCKS = 8
RNN_BW = 128
CONV_W = 4
CONV_LEFT = 2
LRU_C = 8.0
ROPE_BASE = 10000.0
N_EXPERTS = 16
N_GROUPS = 4
EXPERTS_PER_GROUP = N_EXPERTS // N_GROUPS
TOP_K = 2

LANES = 128
SUBLANES = 8
ROW_TILE = 256
SCAN_CHUNK = 128
ATTN_TQ = 512
ATTN_CTX_TQ = 256
ATTN_TK = 256
ATTN_TK_PROB = 256
MOE_ROWS = 512
DISPATCH_ROWS = 1024
MOD_ROWS = 16
NEG_BIG = -1e30
LOG2_E = math.log2(math.e)


def _cparams(sem, vmem_mib):
    return pltpu.CompilerParams(dimension_semantics=sem, vmem_limit_bytes=vmem_mib * 1024 * 1024)


def _sigmoid(x):
    return 0.5 * jnp.tanh(0.5 * x) + 0.5


def _gelu_tanh(x):
    return 0.5 * x * (1.0 + jnp.tanh(math.sqrt(2.0 / math.pi) * (x + 0.044715 * (x * x * x))))


def _rms(x, g):
    return x * lax.rsqrt(jnp.mean(x * x, axis=-1, keepdims=True) + EPS) * g


def _store_tiled(ref, x):
    rows = x.shape[0]
    for j in range(SUBLANES):
        ref[pl.ds(j, rows, stride=SUBLANES), :] = x[:, j * LANES:(j + 1) * LANES]


def _load_tiled(ref):
    rows = ref.shape[0] // SUBLANES
    return jnp.concatenate([ref[pl.ds(j, rows, stride=SUBLANES), :] for j in range(SUBLANES)], axis=1)


def _adaln_kernel(c_ref, w_ref, b_ref, o_ref):
    c = c_ref[...]
    a = c * _sigmoid(c)
    o_ref[...] = jnp.dot(a, w_ref[...], preferred_element_type=F32,
                         precision=lax.Precision.HIGHEST) + b_ref[...]


def _adaln(cond, w_mod, b_mod):
    depth, d, d6 = w_mod.shape
    n = d6 // d
    return pl.pallas_call(
        _adaln_kernel,
        grid=(depth, n),
        in_specs=[pl.BlockSpec((MOD_ROWS, d), lambda l, j: (0, 0)),
                  pl.BlockSpec((None, d, d), lambda l, j: (l, 0, j)),
                  pl.BlockSpec((None, 1, d), lambda l, j: (l, 0, j))],
        out_specs=pl.BlockSpec((None, MOD_ROWS, d), lambda l, j: (l, 0, j)),
        out_shape=jax.ShapeDtypeStruct((depth, MOD_ROWS, d6), F32),
        compiler_params=_cparams(("arbitrary", "arbitrary"), 32),
        name="adaln",
    )(cond, w_mod, b_mod.reshape(depth, 1, d6))


def _stream_tile(x_ref, xc_ref, is_ctx):
    if xc_ref is None:
        return x_ref[...]
    return jnp.where(is_ctx, xc_ref[...], x_ref[...])


def _stream_specs(x, x_ctx, n_ctx_tiles, wrap=lambda f: f):
    d = x.shape[-1]
    if x_ctx is None:
        return [pl.BlockSpec((None, ROW_TILE, d), wrap(lambda bi, i: (bi, i, 0)))], [x]
    return ([pl.BlockSpec((None, ROW_TILE, d), wrap(lambda bi, i: (bi, jnp.maximum(i - n_ctx_tiles, 0), 0))),
             pl.BlockSpec((None, ROW_TILE, d), wrap(lambda bi, i: (bi, jnp.minimum(i, n_ctx_tiles - 1), 0)))],
            [x, x_ctx])


def _inproj_kernel(*refs, d, split, n_ctx_tiles):
    if split:
        (x_ref, xc_ref, sh_ref, sc_ref, g_ref, w_ref, cos_ref, sin_ref,
         k_ref, v_ref, q_ref, rx_ref, rg_ref, gr_ref, ga_ref) = refs
    else:
        (x_ref, sh_ref, sc_ref, g_ref, w_ref, cos_ref, sin_ref,
         k_ref, v_ref, q_ref, rx_ref, rg_ref, gr_ref, ga_ref) = refs
        xc_ref = None
    x = _stream_tile(x_ref, xc_ref, pl.program_id(1) < n_ctx_tiles)
    h = (_rms(x, g_ref[...]) * (1.0 + sc_ref[...]) + sh_ref[...]).astype(BF16)
    cosv = cos_ref[...]
    sinv = sin_ref[...]

    def proj(p):
        return jnp.dot(h, w_ref[:, p * d:(p + 1) * d], preferred_element_type=F32)

    def rope_store(res, ref, scale):
        for hh in range(N_HEADS):
            xs = res[:, hh * LANES:(hh + 1) * LANES]
            out = xs * cosv + pltpu.roll(xs, LANES // 2, 1) * sinv
            ref[:, hh * LANES:(hh + 1) * LANES] = (out * scale).astype(BF16)

    rope_store(proj(0), k_ref, 1.0)
    v_ref[...] = proj(1).T.astype(BF16)
    rope_store(proj(2), q_ref, HEAD_DIM ** -0.5 * LOG2_E)
    _store_tiled(rx_ref, proj(3))
    rg_ref[...] = proj(4)
    gr_ref[...] = proj(5)
    ga_ref[...] = proj(6)


def _inproj(x, x_ctx, sh, sc, g, w, cos_t, sin_t, n_ctx_tiles):
    b, _, d = x.shape
    lt = x.shape[1] + (0 if x_ctx is None else x_ctx.shape[1])
    nt = lt // ROW_TILE
    mod_map = lambda bi, i: (jnp.where(i < n_ctx_tiles, b, bi), 0, 0)
    tok = pl.BlockSpec((None, ROW_TILE, d), lambda bi, i: (bi, i, 0))
    x_specs, x_args = _stream_specs(x, x_ctx, n_ctx_tiles)
    out_bf = jax.ShapeDtypeStruct((b, lt, d), BF16)
    out_f = jax.ShapeDtypeStruct((b, lt, d), F32)
    return pl.pallas_call(
        functools.partial(_inproj_kernel, d=d, split=x_ctx is not None, n_ctx_tiles=n_ctx_tiles),
        grid=(b, nt),
        in_specs=x_specs + [
                  pl.BlockSpec((None, 1, d), mod_map),
                  pl.BlockSpec((None, 1, d), mod_map),
                  pl.BlockSpec((1, d), lambda bi, i: (0, 0)),
                  pl.BlockSpec(w.shape, lambda bi, i: (0, 0)),
                  pl.BlockSpec((ROW_TILE, LANES), lambda bi, i: (i, 0)),
                  pl.BlockSpec((ROW_TILE, LANES), lambda bi, i: (i, 0))],
        out_specs=[tok, pl.BlockSpec((None, d, ROW_TILE), lambda bi, i: (bi, 0, i)), tok,
                   pl.BlockSpec((None, ROW_TILE * SUBLANES, LANES), lambda bi, i: (bi, i, 0)), tok, tok, tok],
        out_shape=[out_bf, jax.ShapeDtypeStruct((b, d, lt), BF16), out_bf,
                   jax.ShapeDtypeStruct((b, lt * SUBLANES, LANES), F32), out_f, out_f, out_f],
        compiler_params=_cparams(("parallel", "arbitrary"), 56),
        name="inproj",
    )(*x_args, sh, sc, g, w, cos_t, sin_t)


def _scan_chunk(i, n_ctx, n_chunks, reverse):
    if not reverse:
        return i
    return jnp.where(i < n_ctx, n_ctx - 1 - i, n_chunks - 1 - (i - n_ctx))


def _scan_kernel(*refs, reverse, tt, n_ctx, n_chunks):
    if reverse:
        (rx_ref, prev_ref, next_ref, cw_ref, cb_ref, wg_ref, bg_ref, c_ref, hf_ref,
         out_ref, ext_scr, u_scr, a_scr, b_scr, h_scr) = refs
    else:
        (rx_ref, prev_ref, next_ref, cw_ref, cb_ref, wg_ref, bg_ref, c_ref,
         out_ref, ext_scr, u_scr, a_scr, b_scr, h_scr) = refs
        hf_ref = None
    i = pl.program_id(0)
    c = _scan_chunk(i, n_ctx, n_chunks, reverse)
    seq_start = jnp.logical_or(c == 0, c == n_ctx)
    seq_end = jnp.logical_or(c == n_ctx - 1, c == n_chunks - 1)
    nb = rx_ref.shape[0]
    tr = tt * SUBLANES
    left = CONV_LEFT * SUBLANES

    ext_scr[:, 0:left, :] = jnp.where(seq_start, 0.0, prev_ref[...])
    ext_scr[:, left:left + tr, :] = rx_ref[...]
    ext_scr[:, left + tr:left + tr + SUBLANES, :] = jnp.where(seq_end, 0.0, next_ref[...])
    u = cb_ref[...] + cw_ref[0] * ext_scr[:, 0:tr, :].reshape(nb, tt, SUBLANES, LANES)
    for j in range(1, CONV_W):
        u = u + cw_ref[j] * ext_scr[:, j * SUBLANES:j * SUBLANES + tr, :].reshape(nb, tt, SUBLANES, LANES)
    u_scr[...] = u.reshape(nb * tr, LANES)

    rows = nb * tt
    for n in range(RNN_BLOCKS):
        un = u_scr[pl.ds(n, rows, stride=SUBLANES), :]
        z = jnp.dot(un.astype(BF16), wg_ref[n], preferred_element_type=F32) + bg_ref[n]
        t = jnp.tanh(z)
        ig = 0.5 * t[:, RNN_BW:] + 0.5
        a = jnp.exp(c_ref[n:n + 1, :] * t[:, :RNN_BW] + c_ref[n:n + 1, :])
        a_scr[pl.ds(n, rows, stride=SUBLANES), :] = a
        b_scr[pl.ds(n, rows, stride=SUBLANES), :] = jnp.sqrt(1.0 - a * a) * ig * un

    @pl.when(i == 0)
    def _():
        h_scr[...] = jnp.zeros_like(h_scr)

    def step(s, hs):
        t = tt - 1 - s if reverse else s
        new = []
        for bi in range(nb):
            at = pl.ds(pl.multiple_of((bi * tt + t) * SUBLANES, SUBLANES), SUBLANES)
            h = a_scr[at, :] * hs[bi] + b_scr[at, :]
            ot = pl.ds(pl.multiple_of(t * SUBLANES, SUBLANES), SUBLANES)
            out_ref[bi, ot, :] = hf_ref[bi, ot, :] + h if reverse else h
            new.append(h)
        return tuple(new)

    hs = lax.fori_loop(0, tt, step, tuple(h_scr[bi] for bi in range(nb)), unroll=4)
    for bi in range(nb):
        h_scr[bi] = hs[bi]


def _scan(rx, cw, cb, wg, bg, cdec, hf, n_ctx, reverse):
    b, lt8, _ = rx.shape
    tt = SCAN_CHUNK
    n_chunks = lt8 // (tt * SUBLANES)
    chunk = lambda i: _scan_chunk(i, n_ctx, n_chunks, reverse)
    blk = pl.BlockSpec((b, tt * SUBLANES, LANES), lambda i: (0, chunk(i), 0))
    const2 = lambda i: (0, 0)
    const3 = lambda i: (0, 0, 0)
    in_specs = [blk,
                pl.BlockSpec((b, CONV_LEFT * SUBLANES, LANES),
                             lambda i: (0, jnp.maximum(chunk(i) * (tt // CONV_LEFT) - 1, 0), 0)),
                pl.BlockSpec((b, SUBLANES, LANES), lambda i: (0, jnp.minimum((chunk(i) + 1) * tt, n_chunks * tt - 1), 0)),
                pl.BlockSpec(cw.shape, const3),
                pl.BlockSpec(cb.shape, const2),
                pl.BlockSpec(wg.shape, const3),
                pl.BlockSpec(bg.shape, const3),
                pl.BlockSpec(cdec.shape, const2)]
    args = [rx, rx, rx, cw, cb, wg, bg, cdec]
    if reverse:
        in_specs.append(blk)
        args.append(hf)
    tr = tt * SUBLANES
    return pl.pallas_call(
        functools.partial(_scan_kernel, reverse=reverse, tt=tt, n_ctx=n_ctx, n_chunks=n_chunks),
        grid=(n_chunks,),
        in_specs=in_specs,
        out_specs=blk,
        out_shape=jax.ShapeDtypeStruct(rx.shape, F32),
        scratch_shapes=[pltpu.VMEM((b, tr + (CONV_W - 1) * SUBLANES, LANES), F32),
                        pltpu.VMEM((b * tr, LANES), F32),
                        pltpu.VMEM((b * tr, LANES), F32),
                        pltpu.VMEM((b * tr, LANES), F32),
                        pltpu.VMEM((b, SUBLANES, LANES), F32)],
        compiler_params=_cparams(("arbitrary",), 56),
        name="scan_bwd" if reverse else "scan_fwd",
    )(*args)


def _attn_passes(lam, q_refs, k_ref, vt_ref, g_ref, o_ref, *, write, read, tq, w_chunks, r_chunks, out_scale):
    n2 = 2 * tq
    if write is not None:
        s_w, m_w = write
        q = jnp.concatenate([r[...] for r in q_refs], axis=0) if len(q_refs) > 1 else q_refs[0][...]
        lane = lax.broadcasted_iota(jnp.int32, q.shape, 1)
        first = (lane // (HEAD_DIM // 2)) % 2 == 0
        zero = jnp.zeros_like(q)
        qs = jnp.concatenate([jnp.where(first, q, zero), jnp.where(first, zero, q)], axis=0)
        m8 = jnp.full((SUBLANES, n2), NEG_BIG, F32)
    if read is not None:
        s_r, m_r = read
        m = m_r[...]
        l8 = jnp.zeros((SUBLANES, n2), F32)
        acc = jnp.zeros((V_DIM, n2), F32)
    for j in range(max(len(w_chunks) if write is not None else 0, len(r_chunks) if read is not None else 0)):
        if read is not None and j < len(r_chunks):
            start, size = r_chunks[j]
            p = jnp.exp2(s_r[start:start + size, :] - m)
            l8 = l8 + jnp.sum(p.reshape(size // SUBLANES, SUBLANES, n2), axis=0)
            acc = acc + jnp.dot(vt_ref[:, start:start + size], p.astype(BF16), preferred_element_type=F32)
        if write is not None and j < len(w_chunks):
            start, size = w_chunks[j]
            st = lax.dot_general(k_ref[start:start + size, :], qs, (((1,), (1,)), ((), ())),
                                 preferred_element_type=F32)
            s_w[start:start + size, :] = st
            m8 = jnp.maximum(m8, jnp.max(st.reshape(size // SUBLANES, SUBLANES, n2), axis=0))
    if write is not None:
        m_w[...] = jnp.max(m8, axis=0, keepdims=True)
    if read is not None:
        on = acc * (1.0 / jnp.sum(l8, axis=0, keepdims=True))
        dt = on[:, :tq] - lam * on[:, tq:]
        yt = dt * lax.rsqrt(jnp.mean(dt * dt, axis=0, keepdims=True) + EPS)
        o_ref[...] = (yt.T * g_ref[...] * out_scale).astype(BF16)


def _attn_pipe_kernel(lam_ref, *refs, tq, n_tiles, n_q_refs, chunks, prob_chunks, out_scale):
    q_refs = refs[:n_q_refs]
    k_ref, vt_ref, g_ref, o_ref, s_a, s_b, m_a, m_b = refs[n_q_refs:]
    s = pl.program_id(0)
    run = functools.partial(_attn_passes, lam_ref[0], q_refs, k_ref, vt_ref, g_ref, o_ref, tq=tq,
                            w_chunks=chunks, r_chunks=prob_chunks, out_scale=out_scale)
    buf_a, buf_b = (s_a, m_a), (s_b, m_b)
    inner = jnp.logical_and(s > 0, s < n_tiles)
    even = s % 2 == 0

    @pl.when(s == 0)
    def _():
        run(write=buf_a, read=None)

    @pl.when(jnp.logical_and(inner, even))
    def _():
        run(write=buf_a, read=buf_b)

    @pl.when(jnp.logical_and(inner, jnp.logical_not(even)))
    def _():
        run(write=buf_b, read=buf_a)

    @pl.when(s == n_tiles)
    def _():
        run(write=None, read=buf_a if (n_tiles - 1) % 2 == 0 else buf_b)


def _attn_single_kernel(lam_ref, q_ref, k_ref, vt_ref, g_ref, o_ref, s_a, m_a, *, tq, chunks, out_scale):
    run = functools.partial(_attn_passes, lam_ref[0], (q_ref,), k_ref, vt_ref, g_ref, o_ref, tq=tq,
                            w_chunks=chunks, r_chunks=chunks, out_scale=out_scale)
    run(write=(s_a, m_a), read=None)
    run(write=None, read=(s_a, m_a))


def _key_chunks(n_ctx_rows, lt, tk):
    ctx = [(s, min(tk, n_ctx_rows - s)) for s in range(0, n_ctx_rows, tk)]
    lat = [(s, min(tk, lt - s)) for s in range(n_ctx_rows, lt, tk)]
    return tuple(ctx), tuple(ctx + lat)


def _attention(lam, q, k, vt, g_sub, n_ctx_rows, with_ctx_queries, out_scale):
    b, lt, d = q.shape
    tq, tqc = ATTN_TQ, ATTN_CTX_TQ
    ctx_q_tiles = n_ctx_rows // tqc
    n_sub = tq // tqc
    nq = (lt - n_ctx_rows) // tq
    ctx_chunks, all_chunks = _key_chunks(n_ctx_rows, lt, ATTN_TK)
    _, prob_chunks = _key_chunks(n_ctx_rows, lt, ATTN_TK_PROB)
    n2 = 2 * tq
    lam_in = pl.BlockSpec(memory_space=pltpu.SMEM)
    g_in = pl.BlockSpec((1, V_DIM), lambda bi, h, s: (0, 0))
    n_flat = b * N_HEADS * nq

    def tile_map(f, lag):
        def index_map(g):
            t = jnp.clip(g - lag, 0, n_flat - 1)
            bh = t // nq
            return f(bh // N_HEADS, bh % N_HEADS, t % nq)
        return index_map

    lat = pl.pallas_call(
        functools.partial(_attn_pipe_kernel, tq=tq, n_tiles=n_flat, n_q_refs=n_sub, chunks=all_chunks,
                          prob_chunks=prob_chunks,
                          out_scale=out_scale),
        grid=(n_flat + 1,),
        in_specs=[lam_in] + [
                  pl.BlockSpec((None, tqc, V_DIM),
                               tile_map(lambda bi, h, qi, j=j: (bi, qi * n_sub + ctx_q_tiles + j, h), 0))
                  for j in range(n_sub)] + [
                  pl.BlockSpec((None, lt, V_DIM), tile_map(lambda bi, h, qi: (bi, 0, h), 0)),
                  pl.BlockSpec((None, V_DIM, lt), tile_map(lambda bi, h, qi: (bi, h, 0), 1)),
                  pl.BlockSpec((1, V_DIM), lambda g: (0, 0))],
        out_specs=pl.BlockSpec((None, tq, V_DIM), tile_map(lambda bi, h, qi: (bi, qi, h), 1)),
        out_shape=jax.ShapeDtypeStruct((b, nq * tq, d), BF16),
        scratch_shapes=[pltpu.VMEM((lt, n2), F32), pltpu.VMEM((lt, n2), F32),
                        pltpu.VMEM((1, n2), F32), pltpu.VMEM((1, n2), F32)],
        compiler_params=_cparams(("arbitrary",), 56),
        name="diff_attn",
    )(lam, *([q] * n_sub), k, vt, g_sub)
    if not with_ctx_queries:
        return lat, None
    ctx = pl.pallas_call(
        functools.partial(_attn_single_kernel, tq=tqc, chunks=ctx_chunks, out_scale=out_scale),
        grid=(b, N_HEADS, ctx_q_tiles),
        in_specs=[lam_in,
                  pl.BlockSpec((None, tqc, V_DIM), lambda bi, h, s: (bi, s, h)),
                  pl.BlockSpec((None, n_ctx_rows, V_DIM), lambda bi, h, s: (bi, 0, h)),
                  pl.BlockSpec((None, V_DIM, n_ctx_rows), lambda bi, h, s: (bi, h, 0)),
                  g_in],
        out_specs=pl.BlockSpec((None, tqc, V_DIM), lambda bi, h, s: (bi, s, h)),
        out_shape=jax.ShapeDtypeStruct((b, n_ctx_rows, d), BF16),
        scratch_shapes=[pltpu.VMEM((n_ctx_rows, 2 * tqc), F32), pltpu.VMEM((1, 2 * tqc), F32)],
        compiler_params=_cparams(("parallel", "parallel", "arbitrary"), 32),
        name="diff_attn_ctx",
    )(lam, q, k, vt, g_sub)
    return lat, ctx


def _postmix_kernel(*refs, has_ctx, split, n_ctx_tiles, nt, n_tiles):
    refs = list(refs)
    x_ref = refs.pop(0)
    xc_ref = refs.pop(0) if split else None
    at_ref = refs.pop(0)
    atc_ref = refs.pop(0) if has_ctx else None
    (y_ref, rg_ref, gr_ref, ga_ref, gt1_ref, sh2_ref, sc2_ref, g2_ref,
     wr_ref, wa_ref, wo_ref, wrh_ref, wrl_ref, brt_ref, x1_ref, h2_ref, route_ref, cnt_ref,
     run_scr, hs_scr) = refs
    g = pl.program_id(0)

    @pl.when(g == 0)
    def _():
        run_scr[...] = jnp.zeros_like(run_scr)
        hs_scr[...] = jnp.zeros_like(hs_scr)

    h_hi = hs_scr[0]
    h_lo = hs_scr[1]
    nt_dims = (((1,), (1,)), ((), ()))
    lg = (lax.dot_general(wrh_ref[...], h_hi, nt_dims, preferred_element_type=F32)
          + lax.dot_general(wrh_ref[...], h_lo, nt_dims, preferred_element_type=F32)
          + lax.dot_general(wrl_ref[...], h_hi, nt_dims, preferred_element_type=F32)) + brt_ref[:, 0:1]

    is_ctx = jnp.minimum(g, n_tiles - 1) % nt < n_ctx_tiles
    x = _stream_tile(x_ref, xc_ref, is_ctx)
    at = _stream_tile(at_ref, atc_ref, is_ctx)
    yg = (_load_tiled(y_ref) * _gelu_tanh(rg_ref[...])).astype(BF16)
    o_r = jnp.dot(yg, wr_ref[...], preferred_element_type=F32)
    o_a = jnp.dot(at, wa_ref[...], preferred_element_type=F32)
    mix = (_sigmoid(gr_ref[...]) * o_r + _sigmoid(ga_ref[...]) * o_a).astype(BF16)
    x1 = x + gt1_ref[...] * jnp.dot(mix, wo_ref[...], preferred_element_type=F32)
    x1_ref[...] = x1
    h2 = _rms(x1, g2_ref[...]) * (1.0 + sc2_ref[...]) + sh2_ref[...]
    _store_tiled(h2_ref, h2)
    h2_hi = h2.astype(BF16)
    hs_scr[0] = h2_hi
    hs_scr[1] = (h2 - h2_hi.astype(F32)).astype(BF16)

    rows = h_hi.shape[0]
    ex = lax.broadcasted_iota(jnp.int32, lg.shape, 0)
    ex_f = ex.astype(F32)
    e = jnp.exp(lg - jnp.max(lg, axis=0, keepdims=True))
    p = e / jnp.sum(e, axis=0, keepdims=True)
    grp = ex // EXPERTS_PER_GROUP
    best = jnp.max(jnp.where(grp == 0, p, -1.0), axis=0, keepdims=True)
    best_g = jnp.zeros_like(best, dtype=jnp.int32)
    for gi in range(1, N_GROUPS):
        gm = jnp.max(jnp.where(grp == gi, p, -1.0), axis=0, keepdims=True)
        better = gm > best
        best_g = jnp.where(better, gi, best_g)
        best = jnp.where(better, gm, best)
    pg = jnp.where(grp == best_g, p, -1.0)
    big = float(N_EXPERTS)
    v1 = jnp.max(pg, axis=0, keepdims=True)
    i1 = jnp.min(jnp.where(pg == v1, ex_f, big), axis=0, keepdims=True)
    pg2 = jnp.where(ex_f == i1, -1.0, pg)
    v2 = jnp.max(pg2, axis=0, keepdims=True)
    i2 = jnp.min(jnp.where(pg2 == v2, ex_f, big), axis=0, keepdims=True)
    den = v1 + v2

    live = (g > 0).astype(F32)
    oh1 = (ex_f == i1).astype(F32)
    oh2 = (ex_f == i2).astype(F32)
    both = (oh1 + oh2) * live
    tri = (lax.broadcasted_iota(jnp.int32, (rows, rows), 0)
           < lax.broadcasted_iota(jnp.int32, (rows, rows), 1)).astype(BF16)
    before = jnp.dot(both.astype(BF16), tri, preferred_element_type=F32) + run_scr[:, 0:1]
    r1 = jnp.sum(oh1 * before, axis=0, keepdims=True)
    r2 = jnp.sum(oh2 * before, axis=0, keepdims=True)
    run = run_scr[...] + jnp.sum(both, axis=1, keepdims=True)
    run_scr[...] = run
    cnt_ref[...] = run
    row = lax.broadcasted_iota(jnp.int32, route_ref.shape, 0)
    route_ref[...] = jnp.where(row == 0, i1, jnp.where(row == 1, i2, jnp.where(row == 2, v1 / den, jnp.where(
        row == 3, v2 / den, jnp.where(row == 4, r1, r2)))))


def _postmix(x, x_ctx, attn, attn_ctx, y, rg, gr, ga, gt1, sh2, sc2, g2, wr, wa, wo, wrh, wrl, brt, n_ctx_tiles):
    b, lt, d = rg.shape
    has_ctx = attn_ctx is not None
    off = 0 if has_ctx else n_ctx_tiles
    lq = lt - off * ROW_TILE
    nt = lq // ROW_TILE
    n_tiles = b * nt

    def wrap(f):
        def index_map(g):
            t = jnp.minimum(g, n_tiles - 1)
            return f(t // nt, t % nt)
        return index_map

    full = pl.BlockSpec((None, ROW_TILE, d), wrap(lambda bi, i: (bi, i + off, 0)))
    part = pl.BlockSpec((None, ROW_TILE, d), wrap(lambda bi, i: (bi, i, 0)))
    tiled = pl.BlockSpec((None, ROW_TILE * SUBLANES, LANES), wrap(lambda bi, i: (bi, i + off, 0)))
    mod = pl.BlockSpec((None, 1, d), wrap(lambda bi, i: (jnp.where(i + off < n_ctx_tiles, b, bi), 0, 0)))
    const = lambda g: (0, 0)
    if x_ctx is None:
        x_specs, x_args = [full], [x]
    else:
        x_specs, x_args = _stream_specs(x, x_ctx, n_ctx_tiles, wrap)
    at_specs, at_args = _stream_specs(attn, attn_ctx, n_ctx_tiles, wrap)
    return pl.pallas_call(
        functools.partial(_postmix_kernel, has_ctx=has_ctx, split=x_ctx is not None, n_ctx_tiles=n_ctx_tiles,
                          nt=nt, n_tiles=n_tiles),
        grid=(n_tiles + 1,),
        in_specs=x_specs + at_specs + [tiled, full, full, full, mod, mod, mod,
                                      pl.BlockSpec((1, d), const),
                                      pl.BlockSpec((d, d), const), pl.BlockSpec((d, d), const),
                                      pl.BlockSpec((d, d), const),
                                      pl.BlockSpec((N_EXPERTS, d), const), pl.BlockSpec((N_EXPERTS, d), const),
                                      pl.BlockSpec((N_EXPERTS, LANES), const)],
        out_specs=[part, pl.BlockSpec((ROW_TILE * SUBLANES, LANES), wrap(lambda bi, i: (bi * nt + i, 0))),
                   pl.BlockSpec((SUBLANES, ROW_TILE), lambda g: (0, jnp.maximum(g - 1, 0))),
                   pl.BlockSpec((N_EXPERTS, LANES), const)],
        out_shape=[jax.ShapeDtypeStruct((b, lq, d), F32),
                   jax.ShapeDtypeStruct((b * lq * SUBLANES, LANES), F32),
                   jax.ShapeDtypeStruct((SUBLANES, b * lq), F32),
                   jax.ShapeDtypeStruct((N_EXPERTS, LANES), F32)],
        scratch_shapes=[pltpu.VMEM((N_EXPERTS, LANES), F32), pltpu.VMEM((2, ROW_TILE, d), BF16)],
        compiler_params=_cparams(("arbitrary",), 48),
        name="postmix",
    )(*x_args, *at_args, y, rg, gr, ga, gt1, sh2, sc2, g2, wr, wa, wo, wrh, wrl, brt)


def _dispatch_kernel(pad_ref, dest_ref, h_ref, xs_hbm, zrows, sem, *, n_pad):
    i = pl.program_id(0)
    rows = h_ref.shape[0] // SUBLANES

    def tile(ref, r):
        return ref.at[pl.ds(pl.multiple_of(r * SUBLANES, SUBLANES), SUBLANES)]

    @pl.when(i == 0)
    def _():
        zrows[...] = jnp.zeros_like(zrows)
        for c in range(n_pad // MOE_ROWS):
            def fill(r, carry):
                pltpu.make_async_copy(tile(zrows, r), tile(xs_hbm, pad_ref[c * MOE_ROWS + r]), sem).start()
                return carry

            lax.fori_loop(0, MOE_ROWS, fill, 0, unroll=8)
            pltpu.make_async_copy(zrows, xs_hbm.at[pl.ds(0, MOE_ROWS * SUBLANES)], sem).wait()

    def send(r, carry):
        for k in range(TOP_K):
            pltpu.make_async_copy(tile(h_ref, r), tile(xs_hbm, dest_ref[0, k * rows + r]), sem).start(priority=k)
        return carry

    lax.fori_loop(0, rows, send, 0, unroll=8)
    for k in range(TOP_K):
        pltpu.make_async_copy(h_ref, xs_hbm.at[pl.ds(0, rows * SUBLANES)], sem).wait()


def _dispatch(pad_slots, dest, h2, n_slots):
    t = h2.shape[0] // SUBLANES
    rows = DISPATCH_ROWS
    return pl.pallas_call(
        functools.partial(_dispatch_kernel, n_pad=pad_slots.shape[0]),
        grid_spec=pltpu.PrefetchScalarGridSpec(
            num_scalar_prefetch=1,
            grid=(t // rows,),
            in_specs=[pl.BlockSpec((None, 1, TOP_K * rows), lambda i, pad: (i, 0, 0), memory_space=pltpu.SMEM),
                      pl.BlockSpec((rows * SUBLANES, LANES), lambda i, pad: (i, 0))],
            out_specs=pl.BlockSpec(memory_space=pl.ANY),
            scratch_shapes=[pltpu.VMEM((MOE_ROWS * SUBLANES, LANES), F32), pltpu.SemaphoreType.DMA(())]),
        out_shape=jax.ShapeDtypeStruct((n_slots * SUBLANES, LANES), F32),
        compiler_params=_cparams(("arbitrary",), 32),
        name="dispatch",
    )(pad_slots, _tile_table(dest, rows), h2)


def _expert_kernel(be_ref, nv_ref, x_ref, w1_ref, w3_ref, w2_ref, y_ref, w1_b, w3_b, w2_b):
    i = pl.program_id(0)

    @pl.when(i < nv_ref[0])
    def _():
        @pl.when(jnp.logical_or(i == 0, be_ref[i] != be_ref[jnp.maximum(i - 1, 0)]))
        def _():
            w1_b[...] = w1_ref[...].astype(BF16)
            w3_b[...] = w3_ref[...].astype(BF16)
            w2_b[...] = w2_ref[...].astype(BF16)

        xb = _load_tiled(x_ref).astype(BF16)
        a1 = jnp.dot(xb, w1_b[...], preferred_element_type=F32)
        a3 = jnp.dot(xb, w3_b[...], preferred_element_type=F32)
        mid = (a1 * _sigmoid(a1) * a3).astype(BF16)
        _store_tiled(y_ref, jnp.dot(mid, w2_b[...], preferred_element_type=F32))

    @pl.when(i >= nv_ref[0])
    def _():
        y_ref[...] = jnp.zeros_like(y_ref)


def _experts(block_e, n_valid, xs, w1, w3, w2, layer):
    n_slots = xs.shape[0] // SUBLANES
    d, de = w1.shape[-2:]
    wspec = lambda shp: pl.BlockSpec((None, None) + shp, lambda i, be, nv: (layer, be[i], 0, 0))
    return pl.pallas_call(
        _expert_kernel,
        grid_spec=pltpu.PrefetchScalarGridSpec(
            num_scalar_prefetch=2,
            grid=(n_slots // MOE_ROWS,),
            in_specs=[pl.BlockSpec((MOE_ROWS * SUBLANES, LANES), lambda i, be, nv: (jnp.minimum(i, nv[0] - 1), 0)),
                      wspec((d, de)), wspec((d, de)), wspec((de, d))],
            out_specs=pl.BlockSpec((MOE_ROWS * SUBLANES, LANES), lambda i, be, nv: (i, 0)),
            scratch_shapes=[pltpu.VMEM((d, de), BF16), pltpu.VMEM((d, de), BF16), pltpu.VMEM((de, d), BF16)]),
        out_shape=jax.ShapeDtypeStruct((n_slots * SUBLANES, LANES), F32),
        compiler_params=_cparams(("arbitrary",), 56),
        name="experts",
    )(block_e, n_valid, xs, w1, w3, w2)


def _tile_table(dest, rows):
    n_tiles = dest.shape[1] // rows
    return dest.reshape(TOP_K, n_tiles, rows).transpose(1, 0, 2).reshape(n_tiles, 1, TOP_K * rows)


def _route_tables(route, counts):
    expert = route[0:TOP_K].astype(jnp.int32)
    rank = route[4:4 + TOP_K].astype(jnp.int32)
    a = expert.size
    sizes = counts[:, 0].astype(jnp.int32)
    padded = (sizes + MOE_ROWS - 1) // MOE_ROWS * MOE_ROWS
    pend = jnp.cumsum(padded)
    pstarts = pend - padded
    first = jnp.sum(jnp.where(expert[..., None] == jnp.arange(N_EXPERTS, dtype=jnp.int32), pstarts, 0), axis=-1)
    dest = (first + rank).astype(jnp.int32)
    n_blocks = -(-a // MOE_ROWS) + N_EXPERTS
    n_slots = n_blocks * MOE_ROWS
    n_valid = (pend[-1] // MOE_ROWS).astype(jnp.int32)
    blk = jnp.minimum(jnp.arange(n_blocks, dtype=jnp.int32), n_valid - 1)
    block_e = jnp.minimum(jnp.sum(blk[:, None] * MOE_ROWS >= pend[None, :], axis=1),
                          N_EXPERTS - 1).astype(jnp.int32)
    gap_start = jnp.concatenate([pstarts + sizes, pend[-1:]]).astype(jnp.int32)
    gap_len = jnp.concatenate([padded - sizes, n_slots - pend[-1:]]).astype(jnp.int32)
    gap_end = jnp.cumsum(gap_len)
    j = jnp.arange(n_slots - a, dtype=jnp.int32)
    seg = jnp.sum(j[:, None] >= gap_end[None, :], axis=1)
    pad_slots = (gap_start[seg] + j - (gap_end - gap_len)[seg]).astype(jnp.int32)
    return dest, pad_slots, block_e, n_valid.reshape(1), n_slots


def _combine_kernel(dcur_ref, dnxt_ref, x_ref, gate_ref, gt2_ref, g_ref, y_hbm, o_ref, ybuf, sems, *, final):
    g = pl.program_id(0)
    n = pl.num_programs(0)
    rows = x_ref.shape[0]
    slot = g % 2

    def fetch(tbl_ref, s):
        def body(r, carry):
            for k in range(TOP_K):
                src = pl.multiple_of(tbl_ref[0, k * rows + r] * SUBLANES, SUBLANES)
                pltpu.make_async_copy(y_hbm.at[pl.ds(src, SUBLANES)],
                                      ybuf.at[s, k, pl.ds(pl.multiple_of(r * SUBLANES, SUBLANES), SUBLANES)],
                                      sems.at[s]).start(priority=k)
            return carry

        lax.fori_loop(0, rows, body, 0, unroll=8)

    @pl.when(g == 0)
    def _():
        fetch(dcur_ref, 0)

    @pl.when(g + 1 < n)
    def _():
        fetch(dnxt_ref, 1 - slot)

    for k in range(TOP_K):
        pltpu.make_async_copy(y_hbm.at[pl.ds(0, rows * SUBLANES)], ybuf.at[slot, k], sems.at[slot]).wait()

    gate = gate_ref[...]
    y = gate[:, 0:1] * _load_tiled(ybuf.at[slot, 0]) + gate[:, 1:2] * _load_tiled(ybuf.at[slot, 1])
    x2 = x_ref[...] + gt2_ref[...] * y
    o_ref[...] = _rms(x2, g_ref[...]) if final else x2


def _combine(x1, y_slots, dest, gates, gt2, g_final, n_ctx_tiles, has_ctx, final):
    b, lq, d = x1.shape
    nt = lq // ROW_TILE
    n_tiles = b * nt
    ctx_tiles = n_ctx_tiles if has_ctx else 0
    tok = lambda w: pl.BlockSpec((None, ROW_TILE, w), lambda g: (g // nt, g % nt, 0))
    tbl = lambda f: pl.BlockSpec((None, 1, TOP_K * ROW_TILE), lambda g: (f(g), 0, 0), memory_space=pltpu.SMEM)
    dest3 = _tile_table(dest, ROW_TILE)
    return pl.pallas_call(
        functools.partial(_combine_kernel, final=final),
        grid=(n_tiles,),
        in_specs=[tbl(lambda g: g), tbl(lambda g: jnp.minimum(g + 1, n_tiles - 1)),
                  tok(d), pl.BlockSpec((ROW_TILE, TOP_K), lambda g: (g, 0)),
                  pl.BlockSpec((None, 1, d), lambda g: (jnp.where(g % nt < ctx_tiles, b, g // nt), 0, 0)),
                  pl.BlockSpec((1, d), lambda g: (0, 0)),
                  pl.BlockSpec(memory_space=pl.ANY)],
        out_specs=tok(d),
        out_shape=jax.ShapeDtypeStruct((b, lq, d), F32),
        scratch_shapes=[pltpu.VMEM((2, TOP_K, ROW_TILE * SUBLANES, LANES), F32), pltpu.SemaphoreType.DMA((2,))],
        compiler_params=_cparams(("arbitrary",), 32),
        name="combine",
    )(dest3, dest3, x1, gates, gt2, g_final, y_slots)


def _rope_tables(n_ctx, seq):
    n_pairs = HEAD_DIM // 4
    inv = ROPE_BASE ** (-jnp.arange(n_pairs, dtype=F32) / n_pairs)
    rows = seq // GRID_W
    r = jnp.repeat(jnp.arange(rows, dtype=F32), GRID_W)
    col = jnp.tile(jnp.arange(GRID_W, dtype=F32), rows)
    ang = jnp.concatenate([r[:, None] * inv, col[:, None] * inv], axis=-1)
    cos = jnp.concatenate([jnp.ones((n_ctx, HEAD_DIM // 2), F32), jnp.cos(ang)], axis=0)
    sin = jnp.concatenate([jnp.zeros((n_ctx, HEAD_DIM // 2), F32), jnp.sin(ang)], axis=0)
    return jnp.tile(cos, (1, 4)), jnp.concatenate([-sin, -sin, sin, sin], axis=-1)


def _qk_perm():
    lane = jnp.arange(LANES)
    g, i = lane // 32, lane % 32
    within = (g % 2) * HEAD_DIM + 2 * i + g // 2
    return (jnp.arange(N_HEADS)[:, None] * LANES + within[None, :]).reshape(-1)


def kernel(x, c, ctx, c_ctx, w_mod, b_mod, g_norm1, g_norm2, w_in, conv_w, conv_b, lru_wa, lru_ba, lru_wx,
           lru_bx, lru_lambda, diff_lambda, g_subln, w_rnn_proj, w_attn_proj, w_o, w_router, b_router,
           w_e1, w_e3, w_e2, g_final):
    b, seq, d = x.shape
    n_ctx = ctx.shape[1]
    depth = w_mod.shape[0]
    n_ctx_tiles = n_ctx // ROW_TILE
    assert d == N_HEADS * V_DIM == SUBLANES * LANES and n_ctx % ROW_TILE == 0 and seq % ROW_TILE == 0

    cond = jnp.zeros((MOD_ROWS, d), F32).at[:b].set(c).at[b].set(c_ctx)
    mods = _adaln(cond, w_mod, b_mod).reshape(depth, MOD_ROWS, 6, 1, d)
    cos_t, sin_t = _rope_tables(n_ctx, seq)
    perm = _qk_perm()
    wrt = w_router.T
    wrt_hi = wrt.astype(BF16)
    wrt_lo = (wrt - wrt_hi.astype(F32)).astype(BF16)
    brt = jnp.broadcast_to(b_router.astype(F32)[:, None], (N_EXPERTS, LANES))

    xs, xs_ctx = x, ctx
    out = None
    for l in range(depth):
        last = l == depth - 1
        lam_init = 0.8 - 0.6 * math.exp(-0.3 * l)
        sh1, sc1, gt1, sh2, sc2, gt2 = [mods[l, :, j] for j in range(6)]
        wl = w_in[l]
        w_perm = jnp.concatenate(
            [wl[:, d:2 * d][:, perm], wl[:, 2 * d:3 * d], wl[:, 3 * d:4 * d][:, perm],
             wl[:, 0:d], wl[:, 4 * d:]], axis=1).astype(BF16)
        k, v, q, rx, rg, gr, ga = _inproj(xs, xs_ctx, sh1, sc1, g_norm1[l][None], w_perm, cos_t, sin_t,
                                          n_ctx_tiles)

        lq1, lk1, lq2, lk2 = diff_lambda[l].astype(F32)
        lam = (jnp.exp(jnp.sum(lq1 * lk1)) - jnp.exp(jnp.sum(lq2 * lk2)) + lam_init).reshape(1)
        attn, attn_ctx = _attention(lam, q, k, v, g_subln[l][None], n_ctx, not last, 1.0 - lam_init)

        cdec = LRU_C * jax.nn.softplus(-lru_lambda[l].astype(F32))
        tile_of = lambda vec: vec.reshape(vec.shape[:-1] + (SUBLANES, LANES))
        n_ctx_chunks = n_ctx // SCAN_CHUNK
        hf = None
        for direction in range(2):
            wg = (0.5 * jnp.concatenate([lru_wa[l, direction], lru_wx[l, direction]], axis=-1)).astype(BF16)
            bg = 0.5 * jnp.concatenate([lru_ba[l, direction].reshape(RNN_BLOCKS, 1, RNN_BW),
                                        lru_bx[l, direction].reshape(RNN_BLOCKS, 1, RNN_BW)], axis=-1)
            hf = _scan(rx, tile_of(conv_w[l]), tile_of(conv_b[l]), wg, bg, tile_of(-0.5 * cdec[direction]), hf,
                       n_ctx_chunks, reverse=direction == 1)
        y = hf

        x1, h2, route, counts = _postmix(xs, xs_ctx, attn, attn_ctx, y, rg, gr, ga, gt1, sh2, sc2,
                                         g_norm2[l][None],
                                         w_rnn_proj[l].astype(BF16), w_attn_proj[l].astype(BF16),
                                         w_o[l].astype(BF16), wrt_hi, wrt_lo, brt, n_ctx_tiles)
        lq = x1.shape[1]
        n_tok = b * lq
        dest, pad_slots, block_e, n_valid, n_slots = _route_tables(route, counts)
        xs_sorted = _dispatch(pad_slots, dest, h2, n_slots)
        y_slots = _experts(block_e, n_valid, xs_sorted, w_e1, w_e3, w_e2, l)
        res = _combine(x1, y_slots, dest, route[2:2 + TOP_K].T, gt2, g_final[None], n_ctx_tiles, not last, last)
        if last:
            out = res
        else:
            xs, xs_ctx = res, None
    return out
```

```python
import functools
import math

import jax
import jax.numpy as jnp
from jax import lax
from jax.experimental import pallas as pl
from jax.experimental.pallas import tpu as pltpu

F32 = jnp.float32
BF16 = jnp.bfloat16

N_HEADS = 8
HEAD_DIM = 64
V_DIM = 2 * HEAD_DIM
GRID_W = 64
EPS = 1e-6
RNN_BLOCKS = 8
RNN_BW = 128
CONV_W = 4
CONV_LEFT = 2
LRU_C = 8.0
ROPE_BASE = 10000.0
N_EXPERTS = 16
N_GROUPS = 4
EXPERTS_PER_GROUP = N_EXPERTS // N_GROUPS
TOP_K = 2

LANES = 128
SUBLANES = 8
ROW_TILE = 256
SCAN_CHUNK = 128
ATTN_TQ = 512
ATTN_CTX_TQ = 256
ATTN_TK = 256
MOE_ROWS = 512
DISPATCH_ROWS = 1024
MOD_ROWS = 16
NEG_BIG = -1e30
LOG2_E = math.log2(math.e)


def _cparams(sem, vmem_mib):
    return pltpu.CompilerParams(dimension_semantics=sem, vmem_limit_bytes=vmem_mib * 1024 * 1024)


def _sigmoid(x):
    return 0.5 * jnp.tanh(0.5 * x) + 0.5


def _gelu_tanh(x):
    return 0.5 * x * (1.0 + jnp.tanh(math.sqrt(2.0 / math.pi) * (x + 0.044715 * (x * x * x))))


def _rms(x, g):
    return x * lax.rsqrt(jnp.mean(x * x, axis=-1, keepdims=True) + EPS) * g


def _store_tiled(ref, x):
    rows = x.shape[0]
    for j in range(SUBLANES):
        ref[pl.ds(j, rows, stride=SUBLANES), :] = x[:, j * LANES:(j + 1) * LANES]


def _load_tiled(ref):
    rows = ref.shape[0] // SUBLANES
    return jnp.concatenate([ref[pl.ds(j, rows, stride=SUBLANES), :] for j in range(SUBLANES)], axis=1)


def _adaln_kernel(c_ref, w_ref, b_ref, o_ref):
    c = c_ref[...]
    a = c * _sigmoid(c)
    o_ref[...] = jnp.dot(a, w_ref[...], preferred_element_type=F32,
                         precision=lax.Precision.HIGHEST) + b_ref[...]


def _adaln(cond, w_mod, b_mod):
    depth, d, d6 = w_mod.shape
    n = d6 // d
    return pl.pallas_call(
        _adaln_kernel,
        grid=(depth, n),
        in_specs=[pl.BlockSpec((MOD_ROWS, d), lambda l, j: (0, 0)),
                  pl.BlockSpec((None, d, d), lambda l, j: (l, 0, j)),
                  pl.BlockSpec((None, 1, d), lambda l, j: (l, 0, j))],
        out_specs=pl.BlockSpec((None, MOD_ROWS, d), lambda l, j: (l, 0, j)),
        out_shape=jax.ShapeDtypeStruct((depth, MOD_ROWS, d6), F32),
        compiler_params=_cparams(("arbitrary", "arbitrary"), 32),
        name="adaln",
    )(cond, w_mod, b_mod.reshape(depth, 1, d6))


def _stream_tile(x_ref, xc_ref, is_ctx):
    if xc_ref is None:
        return x_ref[...]
    return jnp.where(is_ctx, xc_ref[...], x_ref[...])


def _stream_specs(x, x_ctx, n_ctx_tiles, wrap=lambda f: f):
    d = x.shape[-1]
    if x_ctx is None:
        return [pl.BlockSpec((None, ROW_TILE, d), wrap(lambda bi, i: (bi, i, 0)))], [x]
    return ([pl.BlockSpec((None, ROW_TILE, d), wrap(lambda bi, i: (bi, jnp.maximum(i - n_ctx_tiles, 0), 0))),
             pl.BlockSpec((None, ROW_TILE, d), wrap(lambda bi, i: (bi, jnp.minimum(i, n_ctx_tiles - 1), 0)))],
            [x, x_ctx])


def _inproj_kernel(*refs, d, split, n_ctx_tiles):
    if split:
        (x_ref, xc_ref, sh_ref, sc_ref, g_ref, w_ref, cos_ref, sin_ref,
         k_ref, v_ref, q_ref, rx_ref, rg_ref, gr_ref, ga_ref) = refs
    else:
        (x_ref, sh_ref, sc_ref, g_ref, w_ref, cos_ref, sin_ref,
         k_ref, v_ref, q_ref, rx_ref, rg_ref, gr_ref, ga_ref) = refs
        xc_ref = None
    x = _stream_tile(x_ref, xc_ref, pl.program_id(1) < n_ctx_tiles)
    h = (_rms(x, g_ref[...]) * (1.0 + sc_ref[...]) + sh_ref[...]).astype(BF16)
    cosv = cos_ref[...]
    sinv = sin_ref[...]

    def proj(p):
        return jnp.dot(h, w_ref[:, p * d:(p + 1) * d], preferred_element_type=F32)

    def rope_store(res, ref, scale):
        for hh in range(N_HEADS):
            xs = res[:, hh * LANES:(hh + 1) * LANES]
            out = xs * cosv + pltpu.roll(xs, LANES // 2, 1) * sinv
            ref[:, hh * LANES:(hh + 1) * LANES] = (out * scale).astype(BF16)

    rope_store(proj(0), k_ref, 1.0)
    v_ref[...] = proj(1).T.astype(BF16)
    rope_store(proj(2), q_ref, HEAD_DIM ** -0.5 * LOG2_E)
    _store_tiled(rx_ref, proj(3))
    rg_ref[...] = proj(4)
    gr_ref[...] = proj(5)
    ga_ref[...] = proj(6)


def _inproj(x, x_ctx, sh, sc, g, w, cos_t, sin_t, n_ctx_tiles):
    b, _, d = x.shape
    lt = x.shape[1] + (0 if x_ctx is None else x_ctx.shape[1])
    nt = lt // ROW_TILE
    mod_map = lambda bi, i: (jnp.where(i < n_ctx_tiles, b, bi), 0, 0)
    tok = pl.BlockSpec((None, ROW_TILE, d), lambda bi, i: (bi, i, 0))
    x_specs, x_args = _stream_specs(x, x_ctx, n_ctx_tiles)
    out_bf = jax.ShapeDtypeStruct((b, lt, d), BF16)
    out_f = jax.ShapeDtypeStruct((b, lt, d), F32)
    return pl.pallas_call(
        functools.partial(_inproj_kernel, d=d, split=x_ctx is not None, n_ctx_tiles=n_ctx_tiles),
        grid=(b, nt),
        in_specs=x_specs + [
                  pl.BlockSpec((None, 1, d), mod_map),
                  pl.BlockSpec((None, 1, d), mod_map),
                  pl.BlockSpec((1, d), lambda bi, i: (0, 0)),
                  pl.BlockSpec(w.shape, lambda bi, i: (0, 0)),
                  pl.BlockSpec((ROW_TILE, LANES), lambda bi, i: (i, 0)),
                  pl.BlockSpec((ROW_TILE, LANES), lambda bi, i: (i, 0))],
        out_specs=[tok, pl.BlockSpec((None, d, ROW_TILE), lambda bi, i: (bi, 0, i)), tok,
                   pl.BlockSpec((None, ROW_TILE * SUBLANES, LANES), lambda bi, i: (bi, i, 0)), tok, tok, tok],
        out_shape=[out_bf, jax.ShapeDtypeStruct((b, d, lt), BF16), out_bf,
                   jax.ShapeDtypeStruct((b, lt * SUBLANES, LANES), F32), out_f, out_f, out_f],
        compiler_params=_cparams(("parallel", "arbitrary"), 56),
        name="inproj",
    )(*x_args, sh, sc, g, w, cos_t, sin_t)


def _scan_chunk(i, n_ctx, n_chunks, reverse):
    if not reverse:
        return i
    return jnp.where(i < n_ctx, n_ctx - 1 - i, n_chunks - 1 - (i - n_ctx))


def _scan_kernel(*refs, reverse, tt, n_ctx, n_chunks):
    if reverse:
        (rx_ref, prev_ref, next_ref, cw_ref, cb_ref, wg_ref, bg_ref, c_ref, hf_ref,
         out_ref, ext_scr, u_scr, a_scr, b_scr, h_scr) = refs
    else:
        (rx_ref, prev_ref, next_ref, cw_ref, cb_ref, wg_ref, bg_ref, c_ref,
         out_ref, ext_scr, u_scr, a_scr, b_scr, h_scr) = refs
        hf_ref = None
    i = pl.program_id(0)
    c = _scan_chunk(i, n_ctx, n_chunks, reverse)
    seq_start = jnp.logical_or(c == 0, c == n_ctx)
    seq_end = jnp.logical_or(c == n_ctx - 1, c == n_chunks - 1)
    nb = rx_ref.shape[0]
    tr = tt * SUBLANES
    left = CONV_LEFT * SUBLANES

    ext_scr[:, 0:left, :] = jnp.where(seq_start, 0.0, prev_ref[...])
    ext_scr[:, left:left + tr, :] = rx_ref[...]
    ext_scr[:, left + tr:left + tr + SUBLANES, :] = jnp.where(seq_end, 0.0, next_ref[...])
    u = cb_ref[...] + cw_ref[0] * ext_scr[:, 0:tr, :].reshape(nb, tt, SUBLANES, LANES)
    for j in range(1, CONV_W):
        u = u + cw_ref[j] * ext_scr[:, j * SUBLANES:j * SUBLANES + tr, :].reshape(nb, tt, SUBLANES, LANES)
    u_scr[...] = u.reshape(nb * tr, LANES)

    rows = nb * tt
    for n in range(RNN_BLOCKS):
        un = u_scr[pl.ds(n, rows, stride=SUBLANES), :]
        z = jnp.dot(un.astype(BF16), wg_ref[n], preferred_element_type=F32) + bg_ref[n]
        t = jnp.tanh(z)
        ig = 0.5 * t[:, RNN_BW:] + 0.5
        a = jnp.exp(c_ref[n:n + 1, :] * t[:, :RNN_BW] + c_ref[n:n + 1, :])
        a_scr[pl.ds(n, rows, stride=SUBLANES), :] = a
        b_scr[pl.ds(n, rows, stride=SUBLANES), :] = jnp.sqrt(1.0 - a * a) * ig * un

    @pl.when(i == 0)
    def _():
        h_scr[...] = jnp.zeros_like(h_scr)

    def step(s, hs):
        t = tt - 1 - s if reverse else s
        new = []
        for bi in range(nb):
            at = pl.ds(pl.multiple_of((bi * tt + t) * SUBLANES, SUBLANES), SUBLANES)
            h = a_scr[at, :] * hs[bi] + b_scr[at, :]
            ot = pl.ds(pl.multiple_of(t * SUBLANES, SUBLANES), SUBLANES)
            out_ref[bi, ot, :] = hf_ref[bi, ot, :] + h if reverse else h
            new.append(h)
        return tuple(new)

    hs = lax.fori_loop(0, tt, step, tuple(h_scr[bi] for bi in range(nb)), unroll=4)
    for bi in range(nb):
        h_scr[bi] = hs[bi]


def _scan(rx, cw, cb, wg, bg, cdec, hf, n_ctx, reverse):
    b, lt8, _ = rx.shape
    tt = SCAN_CHUNK
    n_chunks = lt8 // (tt * SUBLANES)
    chunk = lambda i: _scan_chunk(i, n_ctx, n_chunks, reverse)
    blk = pl.BlockSpec((b, tt * SUBLANES, LANES), lambda i: (0, chunk(i), 0))
    const2 = lambda i: (0, 0)
    const3 = lambda i: (0, 0, 0)
    in_specs = [blk,
                pl.BlockSpec((b, CONV_LEFT * SUBLANES, LANES),
                             lambda i: (0, jnp.maximum(chunk(i) * (tt // CONV_LEFT) - 1, 0), 0)),
                pl.BlockSpec((b, SUBLANES, LANES), lambda i: (0, jnp.minimum((chunk(i) + 1) * tt, n_chunks * tt - 1), 0)),
                pl.BlockSpec(cw.shape, const3),
                pl.BlockSpec(cb.shape, const2),
                pl.BlockSpec(wg.shape, const3),
                pl.BlockSpec(bg.shape, const3),
                pl.BlockSpec(cdec.shape, const2)]
    args = [rx, rx, rx, cw, cb, wg, bg, cdec]
    if reverse:
        in_specs.append(blk)
        args.append(hf)
    tr = tt * SUBLANES
    return pl.pallas_call(
        functools.partial(_scan_kernel, reverse=reverse, tt=tt, n_ctx=n_ctx, n_chunks=n_chunks),
        grid=(n_chunks,),
        in_specs=in_specs,
        out_specs=blk,
        out_shape=jax.ShapeDtypeStruct(rx.shape, F32),
        scratch_shapes=[pltpu.VMEM((b, tr + (CONV_W - 1) * SUBLANES, LANES), F32),
                        pltpu.VMEM((b * tr, LANES), F32),
                        pltpu.VMEM((b * tr, LANES), F32),
                        pltpu.VMEM((b * tr, LANES), F32),
                        pltpu.VMEM((b, SUBLANES, LANES), F32)],
        compiler_params=_cparams(("arbitrary",), 56),
        name="scan_bwd" if reverse else "scan_fwd",
    )(*args)


def _attn_passes(lam, q_refs, k_ref, vt_ref, g_ref, o_ref, *, write, read, tq, w_chunks, r_chunks, out_scale):
    n2 = 2 * tq
    if write is not None:
        s_w, m_w = write
        q = jnp.concatenate([r[...] for r in q_refs], axis=0) if len(q_refs) > 1 else q_refs[0][...]
        lane = lax.broadcasted_iota(jnp.int32, q.shape, 1)
        first = (lane // (HEAD_DIM // 2)) % 2 == 0
        zero = jnp.zeros_like(q)
        qs = jnp.concatenate([jnp.where(first, q, zero), jnp.where(first, zero, q)], axis=0)
        m8 = jnp.full((SUBLANES, n2), NEG_BIG, F32)
    if read is not None:
        s_r, m_r = read
        m = m_r[...]
        l8 = jnp.zeros((SUBLANES, n2), F32)
        acc = jnp.zeros((V_DIM, n2), F32)
    for j in range(max(len(w_chunks) if write is not None else 0, len(r_chunks) if read is not None else 0)):
        if read is not None and j < len(r_chunks):
            start, size = r_chunks[j]
            p = jnp.exp2(s_r[start:start + size, :] - m)
            l8 = l8 + jnp.sum(p.reshape(size // SUBLANES, SUBLANES, n2), axis=0)
            acc = acc + jnp.dot(vt_ref[:, start:start + size], p.astype(BF16), preferred_element_type=F32)
        if write is not None and j < len(w_chunks):
            start, size = w_chunks[j]
            st = lax.dot_general(k_ref[start:start + size, :], qs, (((1,), (1,)), ((), ())),
                                 preferred_element_type=F32)
            s_w[start:start + size, :] = st
            m8 = jnp.maximum(m8, jnp.max(st.reshape(size // SUBLANES, SUBLANES, n2), axis=0))
    if write is not None:
        m_w[...] = jnp.max(m8, axis=0, keepdims=True)
    if read is not None:
        on = acc * (1.0 / jnp.sum(l8, axis=0, keepdims=True))
        dt = on[:, :tq] - lam * on[:, tq:]
        yt = dt * lax.rsqrt(jnp.mean(dt * dt, axis=0, keepdims=True) + EPS)
        o_ref[...] = (yt.T * g_ref[...] * out_scale).astype(BF16)


def _attn_pipe_kernel(lam_ref, *refs, tq, n_tiles, n_q_refs, chunks, out_scale):
    q_refs = refs[:n_q_refs]
    k_ref, vt_ref, g_ref, o_ref, s_a, s_b, m_a, m_b = refs[n_q_refs:]
    s = pl.program_id(0)
    run = functools.partial(_attn_passes, lam_ref[0], q_refs, k_ref, vt_ref, g_ref, o_ref, tq=tq,
                            w_chunks=chunks, r_chunks=chunks, out_scale=out_scale)
    buf_a, buf_b = (s_a, m_a), (s_b, m_b)
    inner = jnp.logical_and(s > 0, s < n_tiles)
    even = s % 2 == 0

    @pl.when(s == 0)
    def _():
        run(write=buf_a, read=None)

    @pl.when(jnp.logical_and(inner, even))
    def _():
        run(write=buf_a, read=buf_b)

    @pl.when(jnp.logical_and(inner, jnp.logical_not(even)))
    def _():
        run(write=buf_b, read=buf_a)

    @pl.when(s == n_tiles)
    def _():
        run(write=None, read=buf_a if (n_tiles - 1) % 2 == 0 else buf_b)


def _attn_single_kernel(lam_ref, q_ref, k_ref, vt_ref, g_ref, o_ref, s_a, m_a, *, tq, chunks, out_scale):
    run = functools.partial(_attn_passes, lam_ref[0], (q_ref,), k_ref, vt_ref, g_ref, o_ref, tq=tq,
                            w_chunks=chunks, r_chunks=chunks, out_scale=out_scale)
    run(write=(s_a, m_a), read=None)
    run(write=None, read=(s_a, m_a))


def _key_chunks(n_ctx_rows, lt, tk):
    ctx = [(s, min(tk, n_ctx_rows - s)) for s in range(0, n_ctx_rows, tk)]
    lat = [(s, min(tk, lt - s)) for s in range(n_ctx_rows, lt, tk)]
    return tuple(ctx), tuple(ctx + lat)


def _attention(lam, q, k, vt, g_sub, n_ctx_rows, with_ctx_queries, out_scale):
    b, lt, d = q.shape
    tq, tqc = ATTN_TQ, ATTN_CTX_TQ
    ctx_q_tiles = n_ctx_rows // tqc
    n_sub = tq // tqc
    nq = (lt - n_ctx_rows) // tq
    ctx_chunks, all_chunks = _key_chunks(n_ctx_rows, lt, ATTN_TK)
    n2 = 2 * tq
    lam_in = pl.BlockSpec(memory_space=pltpu.SMEM)
    g_in = pl.BlockSpec((1, V_DIM), lambda bi, h, s: (0, 0))
    n_flat = b * N_HEADS * nq

    def tile_map(f, lag):
        def index_map(g):
            t = jnp.clip(g - lag, 0, n_flat - 1)
            bh = t // nq
            return f(bh // N_HEADS, bh % N_HEADS, t % nq)
        return index_map

    lat = pl.pallas_call(
        functools.partial(_attn_pipe_kernel, tq=tq, n_tiles=n_flat, n_q_refs=n_sub, chunks=all_chunks,
                          out_scale=out_scale),
        grid=(n_flat + 1,),
        in_specs=[lam_in] + [
                  pl.BlockSpec((None, tqc, V_DIM),
                               tile_map(lambda bi, h, qi, j=j: (bi, qi * n_sub + ctx_q_tiles + j, h), 0))
                  for j in range(n_sub)] + [
                  pl.BlockSpec((None, lt, V_DIM), tile_map(lambda bi, h, qi: (bi, 0, h), 0)),
                  pl.BlockSpec((None, V_DIM, lt), tile_map(lambda bi, h, qi: (bi, h, 0), 1)),
                  pl.BlockSpec((1, V_DIM), lambda g: (0, 0))],
        out_specs=pl.BlockSpec((None, tq, V_DIM), tile_map(lambda bi, h, qi: (bi, qi, h), 1)),
        out_shape=jax.ShapeDtypeStruct((b, nq * tq, d), BF16),
        scratch_shapes=[pltpu.VMEM((lt, n2), F32), pltpu.VMEM((lt, n2), F32),
                        pltpu.VMEM((1, n2), F32), pltpu.VMEM((1, n2), F32)],
        compiler_params=_cparams(("arbitrary",), 56),
        name="diff_attn",
    )(lam, *([q] * n_sub), k, vt, g_sub)
    if not with_ctx_queries:
        return lat, None
    ctx = pl.pallas_call(
        functools.partial(_attn_single_kernel, tq=tqc, chunks=ctx_chunks, out_scale=out_scale),
        grid=(b, N_HEADS, ctx_q_tiles),
        in_specs=[lam_in,
                  pl.BlockSpec((None, tqc, V_DIM), lambda bi, h, s: (bi, s, h)),
                  pl.BlockSpec((None, n_ctx_rows, V_DIM), lambda bi, h, s: (bi, 0, h)),
                  pl.BlockSpec((None, V_DIM, n_ctx_rows), lambda bi, h, s: (bi, h, 0)),
                  g_in],
        out_specs=pl.BlockSpec((None, tqc, V_DIM), lambda bi, h, s: (bi, s, h)),
        out_shape=jax.ShapeDtypeStruct((b, n_ctx_rows, d), BF16),
        scratch_shapes=[pltpu.VMEM((n_ctx_rows, 2 * tqc), F32), pltpu.VMEM((1, 2 * tqc), F32)],
        compiler_params=_cparams(("parallel", "parallel", "arbitrary"), 32),
        name="diff_attn_ctx",
    )(lam, q, k, vt, g_sub)
    return lat, ctx


def _postmix_kernel(*refs, has_ctx, split, n_ctx_tiles, nt, n_tiles):
    refs = list(refs)
    x_ref = refs.pop(0)
    xc_ref = refs.pop(0) if split else None
    at_ref = refs.pop(0)
    atc_ref = refs.pop(0) if has_ctx else None
    (y_ref, rg_ref, gr_ref, ga_ref, gt1_ref, sh2_ref, sc2_ref, g2_ref,
     wr_ref, wa_ref, wo_ref, wrh_ref, wrl_ref, brt_ref, x1_ref, h2_ref, route_ref, cnt_ref,
     run_scr, hs_scr) = refs
    g = pl.program_id(0)

    @pl.when(g == 0)
    def _():
        run_scr[...] = jnp.zeros_like(run_scr)
        hs_scr[...] = jnp.zeros_like(hs_scr)

    h_hi = hs_scr[0]
    h_lo = hs_scr[1]
    nt_dims = (((1,), (1,)), ((), ()))
    lg = (lax.dot_general(wrh_ref[...], h_hi, nt_dims, preferred_element_type=F32)
          + lax.dot_general(wrh_ref[...], h_lo, nt_dims, preferred_element_type=F32)
          + lax.dot_general(wrl_ref[...], h_hi, nt_dims, preferred_element_type=F32)) + brt_ref[:, 0:1]

    is_ctx = jnp.minimum(g, n_tiles - 1) % nt < n_ctx_tiles
    x = _stream_tile(x_ref, xc_ref, is_ctx)
    at = _stream_tile(at_ref, atc_ref, is_ctx)
    yg = (_load_tiled(y_ref) * _gelu_tanh(rg_ref[...])).astype(BF16)
    o_r = jnp.dot(yg, wr_ref[...], preferred_element_type=F32)
    o_a = jnp.dot(at, wa_ref[...], preferred_element_type=F32)
    mix = (_sigmoid(gr_ref[...]) * o_r + _sigmoid(ga_ref[...]) * o_a).astype(BF16)
    x1 = x + gt1_ref[...] * jnp.dot(mix, wo_ref[...], preferred_element_type=F32)
    x1_ref[...] = x1
    h2 = _rms(x1, g2_ref[...]) * (1.0 + sc2_ref[...]) + sh2_ref[...]
    _store_tiled(h2_ref, h2)
    h2_hi = h2.astype(BF16)
    hs_scr[0] = h2_hi
    hs_scr[1] = (h2 - h2_hi.astype(F32)).astype(BF16)

    rows = h_hi.shape[0]
    ex = lax.broadcasted_iota(jnp.int32, lg.shape, 0)
    ex_f = ex.astype(F32)
    e = jnp.exp(lg - jnp.max(lg, axis=0, keepdims=True))
    p = e / jnp.sum(e, axis=0, keepdims=True)
    grp = ex // EXPERTS_PER_GROUP
    best = jnp.max(jnp.where(grp == 0, p, -1.0), axis=0, keepdims=True)
    best_g = jnp.zeros_like(best, dtype=jnp.int32)
    for gi in range(1, N_GROUPS):
        gm = jnp.max(jnp.where(grp == gi, p, -1.0), axis=0, keepdims=True)
        better = gm > best
        best_g = jnp.where(better, gi, best_g)
        best = jnp.where(better, gm, best)
    pg = jnp.where(grp == best_g, p, -1.0)
    big = float(N_EXPERTS)
    v1 = jnp.max(pg, axis=0, keepdims=True)
    i1 = jnp.min(jnp.where(pg == v1, ex_f, big), axis=0, keepdims=True)
    pg2 = jnp.where(ex_f == i1, -1.0, pg)
    v2 = jnp.max(pg2, axis=0, keepdims=True)
    i2 = jnp.min(jnp.where(pg2 == v2, ex_f, big), axis=0, keepdims=True)
    den = v1 + v2

    live = (g > 0).astype(F32)
    oh1 = (ex_f == i1).astype(F32)
    oh2 = (ex_f == i2).astype(F32)
    both = (oh1 + oh2) * live
    tri = (lax.broadcasted_iota(jnp.int32, (rows, rows), 0)
           < lax.broadcasted_iota(jnp.int32, (rows, rows), 1)).astype(BF16)
    before = jnp.dot(both.astype(BF16), tri, preferred_element_type=F32) + run_scr[:, 0:1]
    r1 = jnp.sum(oh1 * before, axis=0, keepdims=True)
    r2 = jnp.sum(oh2 * before, axis=0, keepdims=True)
    run = run_scr[...] + jnp.sum(both, axis=1, keepdims=True)
    run_scr[...] = run
    cnt_ref[...] = run
    row = lax.broadcasted_iota(jnp.int32, route_ref.shape, 0)
    route_ref[...] = jnp.where(row == 0, i1, jnp.where(row == 1, i2, jnp.where(row == 2, v1 / den, jnp.where(
        row == 3, v2 / den, jnp.where(row == 4, r1, r2)))))


def _postmix(x, x_ctx, attn, attn_ctx, y, rg, gr, ga, gt1, sh2, sc2, g2, wr, wa, wo, wrh, wrl, brt, n_ctx_tiles):
    b, lt, d = rg.shape
    has_ctx = attn_ctx is not None
    off = 0 if has_ctx else n_ctx_tiles
    lq = lt - off * ROW_TILE
    nt = lq // ROW_TILE
    n_tiles = b * nt

    def wrap(f):
        def index_map(g):
            t = jnp.minimum(g, n_tiles - 1)
            return f(t // nt, t % nt)
        return index_map

    full = pl.BlockSpec((None, ROW_TILE, d), wrap(lambda bi, i: (bi, i + off, 0)))
    part = pl.BlockSpec((None, ROW_TILE, d), wrap(lambda bi, i: (bi, i, 0)))
    tiled = pl.BlockSpec((None, ROW_TILE * SUBLANES, LANES), wrap(lambda bi, i: (bi, i + off, 0)))
    mod = pl.BlockSpec((None, 1, d), wrap(lambda bi, i: (jnp.where(i + off < n_ctx_tiles, b, bi), 0, 0)))
    const = lambda g: (0, 0)
    if x_ctx is None:
        x_specs, x_args = [full], [x]
    else:
        x_specs, x_args = _stream_specs(x, x_ctx, n_ctx_tiles, wrap)
    at_specs, at_args = _stream_specs(attn, attn_ctx, n_ctx_tiles, wrap)
    return pl.pallas_call(
        functools.partial(_postmix_kernel, has_ctx=has_ctx, split=x_ctx is not None, n_ctx_tiles=n_ctx_tiles,
                          nt=nt, n_tiles=n_tiles),
        grid=(n_tiles + 1,),
        in_specs=x_specs + at_specs + [tiled, full, full, full, mod, mod, mod,
                                      pl.BlockSpec((1, d), const),
                                      pl.BlockSpec((d, d), const), pl.BlockSpec((d, d), const),
                                      pl.BlockSpec((d, d), const),
                                      pl.BlockSpec((N_EXPERTS, d), const), pl.BlockSpec((N_EXPERTS, d), const),
                                      pl.BlockSpec((N_EXPERTS, LANES), const)],
        out_specs=[part, pl.BlockSpec((ROW_TILE * SUBLANES, LANES), wrap(lambda bi, i: (bi * nt + i, 0))),
                   pl.BlockSpec((SUBLANES, ROW_TILE), lambda g: (0, jnp.maximum(g - 1, 0))),
                   pl.BlockSpec((N_EXPERTS, LANES), const)],
        out_shape=[jax.ShapeDtypeStruct((b, lq, d), F32),
                   jax.ShapeDtypeStruct((b * lq * SUBLANES, LANES), F32),
                   jax.ShapeDtypeStruct((SUBLANES, b * lq), F32),
                   jax.ShapeDtypeStruct((N_EXPERTS, LANES), F32)],
        scratch_shapes=[pltpu.VMEM((N_EXPERTS, LANES), F32), pltpu.VMEM((2, ROW_TILE, d), BF16)],
        compiler_params=_cparams(("arbitrary",), 48),
        name="postmix",
    )(*x_args, *at_args, y, rg, gr, ga, gt1, sh2, sc2, g2, wr, wa, wo, wrh, wrl, brt)


def _dispatch_kernel(pad_ref, dest_ref, h_ref, xs_hbm, zrows, sem, *, n_pad):
    i = pl.program_id(0)
    rows = h_ref.shape[0] // SUBLANES

    def tile(ref, r):
        return ref.at[pl.ds(pl.multiple_of(r * SUBLANES, SUBLANES), SUBLANES)]

    @pl.when(i == 0)
    def _():
        zrows[...] = jnp.zeros_like(zrows)
        for c in range(n_pad // MOE_ROWS):
            def fill(r2, carry):
                for k in range(2):
                    r = 2 * r2 + k
                    pltpu.make_async_copy(tile(zrows, r), tile(xs_hbm, pad_ref[c * MOE_ROWS + r]),
                                          sem).start(priority=k)
                return carry

            lax.fori_loop(0, MOE_ROWS // 2, fill, 0, unroll=4)
            pltpu.make_async_copy(zrows, xs_hbm.at[pl.ds(0, MOE_ROWS * SUBLANES)], sem).wait()

    def send(r, carry):
        for k in range(TOP_K):
            pltpu.make_async_copy(tile(h_ref, r), tile(xs_hbm, dest_ref[0, k * rows + r]), sem).start(priority=k)
        return carry

    lax.fori_loop(0, rows, send, 0, unroll=8)
    for k in range(TOP_K):
        pltpu.make_async_copy(h_ref, xs_hbm.at[pl.ds(0, rows * SUBLANES)], sem).wait()


def _dispatch(pad_slots, dest, h2, n_slots):
    t = h2.shape[0] // SUBLANES
    rows = DISPATCH_ROWS
    return pl.pallas_call(
        functools.partial(_dispatch_kernel, n_pad=pad_slots.shape[0]),
        grid_spec=pltpu.PrefetchScalarGridSpec(
            num_scalar_prefetch=1,
            grid=(t // rows,),
            in_specs=[pl.BlockSpec((None, 1, TOP_K * rows), lambda i, pad: (i, 0, 0), memory_space=pltpu.SMEM),
                      pl.BlockSpec((rows * SUBLANES, LANES), lambda i, pad: (i, 0))],
            out_specs=pl.BlockSpec(memory_space=pl.ANY),
            scratch_shapes=[pltpu.VMEM((MOE_ROWS * SUBLANES, LANES), F32), pltpu.SemaphoreType.DMA(())]),
        out_shape=jax.ShapeDtypeStruct((n_slots * SUBLANES, LANES), F32),
        compiler_params=_cparams(("arbitrary",), 32),
        name="dispatch",
    )(pad_slots, _tile_table(dest, rows), h2)


def _expert_kernel(be_ref, nv_ref, x_ref, w1_ref, w3_ref, w2_ref, y_ref, w1_b, w3_b, w2_b):
    i = pl.program_id(0)

    @pl.when(i < nv_ref[0])
    def _():
        @pl.when(jnp.logical_or(i == 0, be_ref[i] != be_ref[jnp.maximum(i - 1, 0)]))
        def _():
            w1_b[...] = w1_ref[...].astype(BF16)
            w3_b[...] = w3_ref[...].astype(BF16)
            w2_b[...] = w2_ref[...].astype(BF16)

        xb = _load_tiled(x_ref).astype(BF16)
        a1 = jnp.dot(xb, w1_b[...], preferred_element_type=F32)
        a3 = jnp.dot(xb, w3_b[...], preferred_element_type=F32)
        mid = (a1 * _sigmoid(a1) * a3).astype(BF16)
        _store_tiled(y_ref, jnp.dot(mid, w2_b[...], preferred_element_type=F32))

    @pl.when(i >= nv_ref[0])
    def _():
        y_ref[...] = jnp.zeros_like(y_ref)


def _experts(block_e, n_valid, xs, w1, w3, w2, layer):
    n_slots = xs.shape[0] // SUBLANES
    d, de = w1.shape[-2:]
    wspec = lambda shp: pl.BlockSpec((None, None) + shp, lambda i, be, nv: (layer, be[i], 0, 0))
    return pl.pallas_call(
        _expert_kernel,
        grid_spec=pltpu.PrefetchScalarGridSpec(
            num_scalar_prefetch=2,
            grid=(n_slots // MOE_ROWS,),
            in_specs=[pl.BlockSpec((MOE_ROWS * SUBLANES, LANES), lambda i, be, nv: (jnp.minimum(i, nv[0] - 1), 0)),
                      wspec((d, de)), wspec((d, de)), wspec((de, d))],
            out_specs=pl.BlockSpec((MOE_ROWS * SUBLANES, LANES), lambda i, be, nv: (i, 0)),
            scratch_shapes=[pltpu.VMEM((d, de), BF16), pltpu.VMEM((d, de), BF16), pltpu.VMEM((de, d), BF16)]),
        out_shape=jax.ShapeDtypeStruct((n_slots * SUBLANES, LANES), F32),
        compiler_params=_cparams(("arbitrary",), 56),
        name="experts",
    )(block_e, n_valid, xs, w1, w3, w2)


def _tile_table(dest, rows):
    n_tiles = dest.shape[1] // rows
    return dest.reshape(TOP_K, n_tiles, rows).transpose(1, 0, 2).reshape(n_tiles, 1, TOP_K * rows)


def _route_tables(route, counts):
    expert = route[0:TOP_K].astype(jnp.int32)
    rank = route[4:4 + TOP_K].astype(jnp.int32)
    a = expert.size
    sizes = counts[:, 0].astype(jnp.int32)
    padded = (sizes + MOE_ROWS - 1) // MOE_ROWS * MOE_ROWS
    pend = jnp.cumsum(padded)
    pstarts = pend - padded
    first = jnp.sum(jnp.where(expert[..., None] == jnp.arange(N_EXPERTS, dtype=jnp.int32), pstarts, 0), axis=-1)
    dest = (first + rank).astype(jnp.int32)
    n_blocks = -(-a // MOE_ROWS) + N_EXPERTS
    n_slots = n_blocks * MOE_ROWS
    n_valid = (pend[-1] // MOE_ROWS).astype(jnp.int32)
    blk = jnp.minimum(jnp.arange(n_blocks, dtype=jnp.int32), n_valid - 1)
    block_e = jnp.minimum(jnp.sum(blk[:, None] * MOE_ROWS >= pend[None, :], axis=1),
                          N_EXPERTS - 1).astype(jnp.int32)
    gap_start = jnp.concatenate([pstarts + sizes, pend[-1:]]).astype(jnp.int32)
    gap_len = jnp.concatenate([padded - sizes, n_slots - pend[-1:]]).astype(jnp.int32)
    gap_end = jnp.cumsum(gap_len)
    j = jnp.arange(n_slots - a, dtype=jnp.int32)
    seg = jnp.sum(j[:, None] >= gap_end[None, :], axis=1)
    pad_slots = (gap_start[seg] + j - (gap_end - gap_len)[seg]).astype(jnp.int32)
    return dest, pad_slots, block_e, n_valid.reshape(1), n_slots


def _combine_kernel(dcur_ref, dnxt_ref, x_ref, gate_ref, gt2_ref, g_ref, y_hbm, o_ref, ybuf, sems, *, final):
    g = pl.program_id(0)
    n = pl.num_programs(0)
    rows = x_ref.shape[0]
    slot = g % 2

    def fetch(tbl_ref, s):
        def body(r, carry):
            for k in range(TOP_K):
                src = pl.multiple_of(tbl_ref[0, k * rows + r] * SUBLANES, SUBLANES)
                pltpu.make_async_copy(y_hbm.at[pl.ds(src, SUBLANES)],
                                      ybuf.at[s, k, pl.ds(pl.multiple_of(r * SUBLANES, SUBLANES), SUBLANES)],
                                      sems.at[s]).start(priority=k)
            return carry

        lax.fori_loop(0, rows, body, 0, unroll=8)

    @pl.when(g == 0)
    def _():
        fetch(dcur_ref, 0)

    @pl.when(g + 1 < n)
    def _():
        fetch(dnxt_ref, 1 - slot)

    for k in range(TOP_K):
        pltpu.make_async_copy(y_hbm.at[pl.ds(0, rows * SUBLANES)], ybuf.at[slot, k], sems.at[slot]).wait()

    gate = gate_ref[...]
    y = gate[:, 0:1] * _load_tiled(ybuf.at[slot, 0]) + gate[:, 1:2] * _load_tiled(ybuf.at[slot, 1])
    x2 = x_ref[...] + gt2_ref[...] * y
    o_ref[...] = _rms(x2, g_ref[...]) if final else x2


def _combine(x1, y_slots, dest, gates, gt2, g_final, n_ctx_tiles, has_ctx, final):
    b, lq, d = x1.shape
    nt = lq // ROW_TILE
    n_tiles = b * nt
    ctx_tiles = n_ctx_tiles if has_ctx else 0
    tok = lambda w: pl.BlockSpec((None, ROW_TILE, w), lambda g: (g // nt, g % nt, 0))
    tbl = lambda f: pl.BlockSpec((None, 1, TOP_K * ROW_TILE), lambda g: (f(g), 0, 0), memory_space=pltpu.SMEM)
    dest3 = _tile_table(dest, ROW_TILE)
    return pl.pallas_call(
        functools.partial(_combine_kernel, final=final),
        grid=(n_tiles,),
        in_specs=[tbl(lambda g: g), tbl(lambda g: jnp.minimum(g + 1, n_tiles - 1)),
                  tok(d), pl.BlockSpec((ROW_TILE, TOP_K), lambda g: (g, 0)),
                  pl.BlockSpec((None, 1, d), lambda g: (jnp.where(g % nt < ctx_tiles, b, g // nt), 0, 0)),
                  pl.BlockSpec((1, d), lambda g: (0, 0)),
                  pl.BlockSpec(memory_space=pl.ANY)],
        out_specs=tok(d),
        out_shape=jax.ShapeDtypeStruct((b, lq, d), F32),
        scratch_shapes=[pltpu.VMEM((2, TOP_K, ROW_TILE * SUBLANES, LANES), F32), pltpu.SemaphoreType.DMA((2,))],
        compiler_params=_cparams(("arbitrary",), 32),
        name="combine",
    )(dest3, dest3, x1, gates, gt2, g_final, y_slots)


def _rope_tables(n_ctx, seq):
    n_pairs = HEAD_DIM // 4
    inv = ROPE_BASE ** (-jnp.arange(n_pairs, dtype=F32) / n_pairs)
    rows = seq // GRID_W
    r = jnp.repeat(jnp.arange(rows, dtype=F32), GRID_W)
    col = jnp.tile(jnp.arange(GRID_W, dtype=F32), rows)
    ang = jnp.concatenate([r[:, None] * inv, col[:, None] * inv], axis=-1)
    cos = jnp.concatenate([jnp.ones((n_ctx, HEAD_DIM // 2), F32), jnp.cos(ang)], axis=0)
    sin = jnp.concatenate([jnp.zeros((n_ctx, HEAD_DIM // 2), F32), jnp.sin(ang)], axis=0)
    return jnp.tile(cos, (1, 4)), jnp.concatenate([-sin, -sin, sin, sin], axis=-1)


def _qk_perm():
    lane = jnp.arange(LANES)
    g, i = lane // 32, lane % 32
    within = (g % 2) * HEAD_DIM + 2 * i + g // 2
    return (jnp.arange(N_HEADS)[:, None] * LANES + within[None, :]).reshape(-1)


def kernel(x, c, ctx, c_ctx, w_mod, b_mod, g_norm1, g_norm2, w_in, conv_w, conv_b, lru_wa, lru_ba, lru_wx,
           lru_bx, lru_lambda, diff_lambda, g_subln, w_rnn_proj, w_attn_proj, w_o, w_router, b_router,
           w_e1, w_e3, w_e2, g_final):
    b, seq, d = x.shape
    n_ctx = ctx.shape[1]
    depth = w_mod.shape[0]
    n_ctx_tiles = n_ctx // ROW_TILE
    assert d == N_HEADS * V_DIM == SUBLANES * LANES and n_ctx % ROW_TILE == 0 and seq % ROW_TILE == 0

    cond = jnp.zeros((MOD_ROWS, d), F32).at[:b].set(c).at[b].set(c_ctx)
    mods = _adaln(cond, w_mod, b_mod).reshape(depth, MOD_ROWS, 6, 1, d)
    cos_t, sin_t = _rope_tables(n_ctx, seq)
    perm = _qk_perm()
    wrt = w_router.T
    wrt_hi = wrt.astype(BF16)
    wrt_lo = (wrt - wrt_hi.astype(F32)).astype(BF16)
    brt = jnp.broadcast_to(b_router.astype(F32)[:, None], (N_EXPERTS, LANES))

    xs, xs_ctx = x, ctx
    out = None
    for l in range(depth):
        last = l == depth - 1
        lam_init = 0.8 - 0.6 * math.exp(-0.3 * l)
        sh1, sc1, gt1, sh2, sc2, gt2 = [mods[l, :, j] for j in range(6)]
        wl = w_in[l]
        w_perm = jnp.concatenate(
            [wl[:, d:2 * d][:, perm], wl[:, 2 * d:3 * d], wl[:, 3 * d:4 * d][:, perm],
             wl[:, 0:d], wl[:, 4 * d:]], axis=1).astype(BF16)
        k, v, q, rx, rg, gr, ga = _inproj(xs, xs_ctx, sh1, sc1, g_norm1[l][None], w_perm, cos_t, sin_t,
                                          n_ctx_tiles)

        lq1, lk1, lq2, lk2 = diff_lambda[l].astype(F32)
        lam = (jnp.exp(jnp.sum(lq1 * lk1)) - jnp.exp(jnp.sum(lq2 * lk2)) + lam_init).reshape(1)
        attn, attn_ctx = _attention(lam, q, k, v, g_subln[l][None], n_ctx, not last, 1.0 - lam_init)

        cdec = LRU_C * jax.nn.softplus(-lru_lambda[l].astype(F32))
        tile_of = lambda vec: vec.reshape(vec.shape[:-1] + (SUBLANES, LANES))
        n_ctx_chunks = n_ctx // SCAN_CHUNK
        hf = None
        for direction in range(2):
            wg = (0.5 * jnp.concatenate([lru_wa[l, direction], lru_wx[l, direction]], axis=-1)).astype(BF16)
            bg = 0.5 * jnp.concatenate([lru_ba[l, direction].reshape(RNN_BLOCKS, 1, RNN_BW),
                                        lru_bx[l, direction].reshape(RNN_BLOCKS, 1, RNN_BW)], axis=-1)
            hf = _scan(rx, tile_of(conv_w[l]), tile_of(conv_b[l]), wg, bg, tile_of(-0.5 * cdec[direction]), hf,
                       n_ctx_chunks, reverse=direction == 1)
        y = hf

        x1, h2, route, counts = _postmix(xs, xs_ctx, attn, attn_ctx, y, rg, gr, ga, gt1, sh2, sc2,
                                         g_norm2[l][None],
                                         w_rnn_proj[l].astype(BF16), w_attn_proj[l].astype(BF16),
                                         w_o[l].astype(BF16), wrt_hi, wrt_lo, brt, n_ctx_tiles)
        lq = x1.shape[1]
        n_tok = b * lq
        dest, pad_slots, block_e, n_valid, n_slots = _route_tables(route, counts)
        xs_sorted = _dispatch(pad_slots, dest, h2, n_slots)
        y_slots = _experts(block_e, n_valid, xs_sorted, w_e1, w_e3, w_e2, l)
        res = _combine(x1, y_slots, dest, route[2:2 + TOP_K].T, gt2, g_final[None], n_ctx_tiles, not last, last)
        if last:
            out = res
        else:
            xs, xs_ctx = res, None
    return out
```
